```python
import jax, jax.numpy as jnp
from jax import lax
import numpy as np

D_MODEL = 2048
BATCH = 1
SEQ = 8192
DEPTH = 2

GRID_W = 64
CTX_LEN = 256
N_EVEN = (DEPTH + 1) // 2
N_ODD = DEPTH // 2

HG_HEADS = 8
HG_DK = 128
HG_DV = 128
HG_WIDTH = HG_HEADS * HG_DK
SG_GROUPS = 8
SG_CH = 128
SG_WIDTH = SG_GROUPS * SG_CH
SG_CHUNK = 128
FT_GROUPS = 4
IN_SIZES = (HG_WIDTH, HG_WIDTH, HG_WIDTH, HG_WIDTH, HG_WIDTH, SG_WIDTH, SG_WIDTH)
IN_WIDTH = sum(IN_SIZES)
IN_SPLITS = tuple(int(s) for s in np.cumsum(IN_SIZES)[:-1])
HG_STATE_COLS = 3 * HG_WIDTH
MIX_WIDTH = HG_WIDTH + SG_WIDTH
D_FF = 4 * D_MODEL
N_MOD = 6
EPS = 1e-6

kernel_name = "hybrid_hgrn2_gmlp_fnet_flow_block"


def rmsnorm(x, g):
    x32 = x.astype(jnp.float32)
    y = x32 * lax.rsqrt(jnp.mean(jnp.square(x32), axis=-1, keepdims=True) + EPS)
    return (y * g).astype(x.dtype)


def layernorm(x, g, b):
    x32 = x.astype(jnp.float32)
    mu = jnp.mean(x32, axis=-1, keepdims=True)
    xc = x32 - mu
    y = xc * lax.rsqrt(jnp.mean(jnp.square(xc), axis=-1, keepdims=True) + EPS)
    return (y * g + b).astype(x.dtype)


def modulate(x, gain, shift, scale):
    return rmsnorm(x, gain) * (1 + scale) + shift


def to_heads(t):
    return t.astype(jnp.float32).reshape(t.shape[0], t.shape[1], HG_HEADS, -1)


def flip(t):
    return jnp.flip(t, axis=1)


def hgrn_gates(f_raw, lb):
    f = lb + (1.0 - lb) * jax.nn.sigmoid(f_raw)
    return jnp.log(f), 1.0 - f


def gla_scan(q, k, v, logf, s0):
    b_, L, H, _ = q.shape
    dv = v.shape[-1]
    rows = L // GRID_W

    def blocks(t):
        return t.reshape(b_, rows, GRID_W, H, t.shape[-1]).transpose(1, 0, 3, 2, 4)

    mask = jnp.tril(jnp.ones((GRID_W, GRID_W), dtype=bool))[:, :, None]

    def step(S, blk):
        qc, kc, vc, gc = blk
        bcum = jnp.cumsum(gc, axis=2)
        diff = bcum[:, :, :, None, :] - bcum[:, :, None, :, :]
        decay = jnp.exp(jnp.where(mask, diff, -jnp.inf))
        scores = jnp.einsum('bhtsk,bhsk->bhts', qc[:, :, :, None, :] * decay, kc)
        o = (jnp.einsum('bhts,bhsv->bhtv', scores, vc)
             + jnp.einsum('bhtk,bhkv->bhtv', qc * jnp.exp(bcum), S))
        blast = bcum[:, :, -1:, :]
        S = (jnp.exp(blast[:, :, 0, :])[..., None] * S
             + jnp.einsum('bhsk,bhsv->bhkv', kc * jnp.exp(blast - bcum), vc))
        return S, o

    S, o = lax.scan(step, s0, (blocks(q), blocks(k), blocks(v), blocks(logf)))
    o = o.transpose(1, 0, 3, 2, 4).reshape(b_, L, H, dv)
    return o, S


def gla_final_state(k, v, logf):
    bcum = jnp.cumsum(logf, axis=1)
    w = jnp.exp(bcum[:, -1:] - bcum)
    return jnp.einsum('blhk,blhv->bhkv', k * w, v)


def context_states(hc, lb_l, w_in_e):
    z = hc @ w_in_e[:, :HG_STATE_COLS]
    f_fr, f_br, i_r = jnp.split(z, 3, axis=-1)
    v = to_heads(i_r)
    lf_f, k_f = hgrn_gates(to_heads(f_fr), lb_l[0].reshape(HG_HEADS, HG_DK))
    lf_b, k_b = hgrn_gates(to_heads(f_br), lb_l[1].reshape(HG_HEADS, HG_DK))
    sf = gla_final_state(k_f, v, lf_f)
    sb = gla_final_state(flip(k_b), flip(v), flip(lf_b))
    return sf, sb


def spatial_gate(u_r, v_r, w_s, b_s, ln_g, ln_b):
    B, L, _ = u_r.shape
    u = jax.nn.gelu(u_r)
    v = layernorm(jax.nn.gelu(v_r), ln_g, ln_b)
    v = v.reshape(B, L // SG_CHUNK, SG_CHUNK, SG_GROUPS, SG_CH)
    mixed = jnp.einsum('gts,bnsgc->bntgc', w_s, v) + b_s.T[:, :, None]
    return u * mixed.reshape(B, L, SG_WIDTH)


def even_mixer(h, lb_l, w_in_e, w_out_e, hg_gain_e, sg_w_e, sg_b_e, sg_ln_g_e, sg_ln_b_e, s0f, s0b):
    B, L, _ = h.shape
    z = h @ w_in_e
    f_fr, f_br, i_r, q_r, g_r, u_r, v_r = jnp.split(z, IN_SPLITS, axis=-1)
    lf_f, k_f = hgrn_gates(to_heads(f_fr), lb_l[0].reshape(HG_HEADS, HG_DK))
    lf_b, k_b = hgrn_gates(to_heads(f_br), lb_l[1].reshape(HG_HEADS, HG_DK))
    q = to_heads(jax.nn.silu(q_r)) * (HG_DK ** -0.5)
    i = to_heads(i_r)
    o_f, sf = gla_scan(q, k_f, i, lf_f, s0f)
    o_b, sb = gla_scan(flip(q), flip(k_b), flip(i), flip(lf_b), s0b)
    o = rmsnorm(o_f + flip(o_b), hg_gain_e.reshape(HG_HEADS, HG_DV))
    o = o.reshape(B, L, HG_WIDTH).astype(h.dtype) * jax.nn.silu(g_r)
    s = spatial_gate(u_r, v_r, sg_w_e, sg_b_e, sg_ln_g_e, sg_ln_b_e)
    out = jnp.concatenate([o, s], axis=-1) @ w_out_e
    return out, sf, sb


def fourier_mixer(h, w_o):
    B, L, D = h.shape
    hg = h.astype(jnp.float32).reshape(B, L, FT_GROUPS, D // FT_GROUPS)
    y = jnp.fft.fft2(hg, axes=(1, 3), norm="ortho").real
    return y.reshape(B, L, D).astype(h.dtype) @ w_o


def sq_relu_mlp(h, w1, w2):
    return jnp.square(jax.nn.relu(h @ w1)) @ w2


def setup_inputs(seed: int = 0) -> dict:
    key = jax.random.key(seed)
    ks = jax.random.split(key, 20)
    f32 = jnp.float32
    nrm = lambda k, shape, s: jax.random.normal(k, shape, f32) * s
    return {
        "x": nrm(ks[0], (BATCH, SEQ, D_MODEL), 1.0),
        "c": nrm(ks[1], (BATCH, D_MODEL), 1.0),
        "ctx": nrm(ks[2], (BATCH, CTX_LEN, D_MODEL), 1.0),
        "c_ctx": nrm(ks[3], (D_MODEL,), 1.0),
        "w_ada": nrm(ks[4], (DEPTH, D_MODEL, N_MOD * D_MODEL), 0.5 * D_MODEL ** -0.5),
        "b_ada": nrm(ks[5], (DEPTH, N_MOD * D_MODEL), 0.02),
        "norm_gain": 1.0 + nrm(ks[6], (DEPTH, 4, D_MODEL), 0.02),
        "w_in": nrm(ks[7], (N_EVEN, D_MODEL, IN_WIDTH), D_MODEL ** -0.5),
        "w_out": nrm(ks[8], (N_EVEN, MIX_WIDTH, D_MODEL), MIX_WIDTH ** -0.5),
        "lb_raw": nrm(ks[9], (2, DEPTH + 1, HG_WIDTH), 0.1),
        "hg_norm_gain": 1.0 + nrm(ks[10], (N_EVEN, HG_WIDTH), 0.02),
        "sg_w": nrm(ks[11], (N_EVEN, SG_GROUPS, SG_CHUNK, SG_CHUNK), SG_CHUNK ** -0.5),
        "sg_b": 1.0 + nrm(ks[12], (N_EVEN, SG_GROUPS, SG_CHUNK), 0.02),
        "sg_ln_gain": 1.0 + nrm(ks[13], (N_EVEN, SG_WIDTH), 0.02),
        "sg_ln_bias": nrm(ks[14], (N_EVEN, SG_WIDTH), 0.02),
        "w_fourier": nrm(ks[15], (N_ODD, D_MODEL, D_MODEL), D_MODEL ** -0.5),
        "w_mlp_in": nrm(ks[16], (DEPTH, D_MODEL, D_FF), D_MODEL ** -0.5),
        "w_mlp_out": nrm(ks[17], (DEPTH, D_FF, D_MODEL), D_FF ** -0.5),
    }


def reference(x, c, ctx, c_ctx, w_ada, b_ada, norm_gain, w_in, w_out, lb_raw, hg_norm_gain,
              sg_w, sg_b, sg_ln_gain, sg_ln_bias, w_fourier, w_mlp_in, w_mlp_out):
    lb_all = jnp.cumsum(jax.nn.softmax(lb_raw.astype(jnp.float32), axis=1), axis=1)
    h_ctx = ctx
    for l in range(DEPTH):
        g_pre_m, g_post_m, g_pre_f, g_post_f = norm_gain[l]
        mx = [m[:, None, :] for m in jnp.split(jax.nn.silu(c) @ w_ada[l] + b_ada[l], N_MOD, axis=-1)]
        ctx_later = any(j % 2 == 0 for j in range(l + 1, DEPTH))
        need_ctx_here = (l % 2 == 0) or ctx_later
        if need_ctx_here:
            mc = jnp.split(jax.nn.silu(c_ctx) @ w_ada[l] + b_ada[l], N_MOD, axis=-1)
            hc = modulate(h_ctx, g_pre_m, mc[0], mc[1])
        hx = modulate(x, g_pre_m, mx[0], mx[1])
        if l % 2 == 0:
            e = l // 2
            lb_l = lb_all[:, l]
            params = (w_in[e], w_out[e], hg_norm_gain[e], sg_w[e], sg_b[e], sg_ln_gain[e], sg_ln_bias[e])
            if ctx_later:
                zero = jnp.zeros((h_ctx.shape[0], HG_HEADS, HG_DK, HG_DV), jnp.float32)
                mix_c, sf, sb = even_mixer(hc, lb_l, *params, zero, zero)
            else:
                sf, sb = context_states(hc, lb_l, w_in[e])
            mix_x, _, _ = even_mixer(hx, lb_l, *params, sf, sb)
        else:
            o = l // 2
            mix_x = fourier_mixer(hx, w_fourier[o])
            if ctx_later:
                mix_c = fourier_mixer(hc, w_fourier[o])
        x = x + mx[2] * rmsnorm(mix_x, g_post_m)
        x = x + mx[5] * rmsnorm(sq_relu_mlp(modulate(x, g_pre_f, mx[3], mx[4]), w_mlp_in[l], w_mlp_out[l]), g_post_f)
        if ctx_later:
            h_ctx = h_ctx + mc[2] * rmsnorm(mix_c, g_post_m)
            h_ctx = h_ctx + mc[5] * rmsnorm(
                sq_relu_mlp(modulate(h_ctx, g_pre_f, mc[3], mc[4]), w_mlp_in[l], w_mlp_out[l]), g_post_f)
    return x
```

```python
import functools

import numpy as np
import jax
import jax.numpy as jnp
from jax import lax
from jax.experimental import pallas as pl
from jax.experimental.pallas import tpu as pltpu

D_MODEL = 2048
SEQ = 8192
DEPTH = 2
CTX_LEN = 256
CHUNK = 64
SUB = 16
HG_HEADS = 8
HG_DK = 128
HG_WIDTH = HG_HEADS * HG_DK
SG_GROUPS = 8
SG_CH = 128
SG_WIDTH = SG_GROUPS * SG_CH
SG_CHUNK = 128
FT_GROUPS = 4
FT_CH = D_MODEL // FT_GROUPS
FFT_L1 = 64
FFT_L2 = 128
IN_WIDTH = 5 * HG_WIDTH + 2 * SG_WIDTH
D_FF = 4 * D_MODEL
N_MOD = 6
EPS = 1e-6
EXP_CLAMP = 80.0
LANES = 128

F32 = jnp.float32
BF16 = jnp.bfloat16
MIB = 1024 * 1024


def _params(semantics, vmem_mib):
    return pltpu.CompilerParams(dimension_semantics=semantics, vmem_limit_bytes=vmem_mib * MIB)


def _rms(x):
    return x * lax.rsqrt(jnp.mean(x * x, axis=-1, keepdims=True) + EPS)


def _silu(x):
    return x * jax.nn.sigmoid(x)


def _gelu_tanh(x):
    cdf = 0.5 * (1.0 + jnp.tanh(float(np.sqrt(2.0 / np.pi)) * (x + 0.044715 * (x * x * x))))
    return x * cdf


def _ada_kernel(c_ref, w_ref, b_ref, o_ref):
    tn = o_ref.shape[-1]
    for r in range(2):
        s = _silu(c_ref[r])
        cols = [jnp.sum(w_ref[0, :, j * LANES:(j + 1) * LANES] * s, axis=0, keepdims=True)
                for j in range(tn // LANES)]
        o_ref[0, r:r + 1, :] = jnp.concatenate(cols, axis=1) + b_ref[0]


def _ada_mods(c2, w_ada, b_ada):
    tn = 512
    n = N_MOD * D_MODEL
    cb = jnp.broadcast_to(c2[:, :, None], (2, D_MODEL, LANES))
    return pl.pallas_call(
        _ada_kernel,
        grid=(DEPTH, n // tn),
        in_specs=[pl.BlockSpec((2, D_MODEL, LANES), lambda l, j: (0, 0, 0)),
                  pl.BlockSpec((1, D_MODEL, tn), lambda l, j: (l, 0, j)),
                  pl.BlockSpec((1, 1, tn), lambda l, j: (l, 0, j))],
        out_specs=pl.BlockSpec((1, 2, tn), lambda l, j: (l, 0, j)),
        out_shape=jax.ShapeDtypeStruct((DEPTH, 2, n), F32),
        compiler_params=_params(("parallel", "parallel"), 32),
        name="ada_mods",
    )(cb, w_ada, b_ada.reshape(DEPTH, 1, n))


def _norm_mm_kernel(x_ref, vec_ref, w_ref, o_ref, h_ref):
    @pl.when(pl.program_id(1) == 0)
    def _():
        h = _rms(x_ref[...]) * vec_ref[0:1] * (1.0 + vec_ref[2:3]) + vec_ref[1:2]
        h_ref[...] = h.astype(BF16)

    o_ref[...] = jnp.dot(h_ref[...], w_ref[...], preferred_element_type=F32).astype(o_ref.dtype)


def _norm_mm(x, vec, w, n_cols, tm, tn):
    m = x.shape[0]
    return pl.pallas_call(
        _norm_mm_kernel,
        grid=(m // tm, n_cols // tn),
        in_specs=[pl.BlockSpec((tm, D_MODEL), lambda i, j: (i, 0)),
                  pl.BlockSpec((8, D_MODEL), lambda i, j: (0, 0)),
                  pl.BlockSpec((D_MODEL, tn), lambda i, j: (0, j))],
        out_specs=pl.BlockSpec((tm, tn), lambda i, j: (i, j)),
        out_shape=jax.ShapeDtypeStruct((m, n_cols), BF16),
        scratch_shapes=[pltpu.VMEM((tm, D_MODEL), BF16)],
        compiler_params=_params(("parallel", "arbitrary"), 48),
        name="norm_mm",
    )(x, vec, w)


def _seg_cumsum(g, row_in_chunk):
    p = g
    s = 1
    while s < CHUNK:
        p = p + jnp.where(row_in_chunk >= s, pltpu.roll(p, s, axis=0), 0.0)
        s *= 2
    return p


def _gla_state_kernel(f_ref, i_ref, lb_ref, s0_ref, senter_ref, sfin_ref,
                      s_acc, p_buf, g_buf, k_buf, *, nchunk):
    d = pl.program_id(0)

    @pl.when(pl.program_id(2) == 0)
    def _():
        s_acc[...] = s0_ref[0, 0]

    rows = f_ref.shape[0]
    lb = lb_ref[0, 0]
    f = lb + (1.0 - lb) * jax.nn.sigmoid(f_ref[...].astype(F32))
    g = jnp.log(f)
    ric = lax.broadcasted_iota(jnp.int32, (rows, LANES), 0) & (CHUNK - 1)
    p_buf[...] = _seg_cumsum(g, ric)
    g_buf[...] = g
    k_buf[...] = 1.0 - f

    def body(j, carry):
        c = jnp.where(d == 0, j, nchunk - 1 - j)
        base = pl.multiple_of(c * CHUNK, CHUNK)
        sl = pl.ds(base, CHUNK)
        p = p_buf[sl, :]
        tot = p_buf[pl.ds(base + CHUNK - 1, 1), :]
        ex = jnp.where(d == 0, tot - p, p - g_buf[sl, :])
        kt = (k_buf[sl, :] * jnp.exp(ex)).astype(BF16)
        u = lax.dot_general(i_ref[sl, :], kt, (((0,), (0,)), ((), ())), preferred_element_type=F32)
        s = s_acc[...]
        senter_ref[0, 0, c] = s.astype(BF16)
        s_acc[...] = s * jnp.exp(tot) + u
        return carry

    lax.fori_loop(0, nchunk, body, 0)
    sfin_ref[0, 0] = s_acc[...]


def _gla_states(z, lb4, s0, rows_per_block):
    m = z.shape[0]
    nb = m // rows_per_block
    nchunk = rows_per_block // CHUNK

    def rb(d, h, b):
        return b + d * (nb - 1 - 2 * b)

    return pl.pallas_call(
        functools.partial(_gla_state_kernel, nchunk=nchunk),
        grid=(2, HG_HEADS, nb),
        in_specs=[pl.BlockSpec((rows_per_block, HG_DK), lambda d, h, b: (rb(d, h, b), d * HG_HEADS + h)),
                  pl.BlockSpec((rows_per_block, HG_DK), lambda d, h, b: (rb(d, h, b), 2 * HG_HEADS + h)),
                  pl.BlockSpec((1, 1, 1, HG_DK), lambda d, h, b: (d, h, 0, 0)),
                  pl.BlockSpec((1, 1, HG_DK, HG_DK), lambda d, h, b: (d, h, 0, 0))],
        out_specs=[pl.BlockSpec((1, 1, nchunk, HG_DK, HG_DK), lambda d, h, b: (d, h, rb(d, h, b), 0, 0)),
                   pl.BlockSpec((1, 1, HG_DK, HG_DK), lambda d, h, b: (d, h, 0, 0))],
        out_shape=[jax.ShapeDtypeStruct((2, HG_HEADS, m // CHUNK, HG_DK, HG_DK), BF16),
                   jax.ShapeDtypeStruct((2, HG_HEADS, HG_DK, HG_DK), F32)],
        scratch_shapes=[pltpu.VMEM((HG_DK, HG_DK), F32),
                        pltpu.VMEM((rows_per_block, HG_DK), F32),
                        pltpu.VMEM((rows_per_block, HG_DK), F32),
                        pltpu.VMEM((rows_per_block, HG_DK), F32)],
        compiler_params=_params(("parallel", "parallel", "arbitrary"), 32),
        name="gla_states",
    )(z, z, lb4, s0)


def _gla_out_kernel(ff_ref, fb_ref, i_ref, q_ref, gt_ref, lb_ref, gain_ref, sf_ref, sb_ref, o_ref,
                    pf_buf, pb_buf, gb_buf, kf_buf, kb_buf, q_buf, *, nchunk):
    rows = ff_ref.shape[0]
    ric = lax.broadcasted_iota(jnp.int32, (rows, LANES), 0) & (CHUNK - 1)
    lbf = lb_ref[0, 0]
    lbb = lb_ref[1, 0]
    ff = lbf + (1.0 - lbf) * jax.nn.sigmoid(ff_ref[...].astype(F32))
    pf_buf[...] = _seg_cumsum(jnp.log(ff), ric)
    kf_buf[...] = 1.0 - ff
    fb = lbb + (1.0 - lbb) * jax.nn.sigmoid(fb_ref[...].astype(F32))
    gb = jnp.log(fb)
    pb_buf[...] = _seg_cumsum(gb, ric)
    gb_buf[...] = gb
    kb_buf[...] = 1.0 - fb
    q_buf[...] = _silu(q_ref[...].astype(F32)) * (HG_DK ** -0.5)

    blk = lax.broadcasted_iota(jnp.int32, (CHUNK, LANES), 0) // SUB
    r2 = lax.broadcasted_iota(jnp.int32, (CHUNK, CHUNK), 0)
    c2 = lax.broadcasted_iota(jnp.int32, (CHUNK, CHUNK), 1)
    nt = (((1,), (1,)), ((), ()))
    gain = gain_ref[0]

    def body(c, carry):
        base = pl.multiple_of(c * CHUNK, CHUNK)
        sl = pl.ds(base, CHUNK)
        pf = pf_buf[sl, :]
        pb = pb_buf[sl, :]
        gbc = gb_buf[sl, :]
        kf = kf_buf[sl, :]
        kb = kb_buf[sl, :]
        q = q_buf[sl, :]
        totb = pb_buf[pl.ds(base + CHUNK - 1, 1), :]
        suf = totb - pb + gbc
        qf_seg, kf_seg, qb_seg, kb_seg = [], [], [], []
        for i in range(CHUNK // SUB):
            mid = pf_buf[pl.ds(base + SUB * i + SUB // 2 - 1, 1), :]
            qf_seg.append(jnp.where(blk == i, q * jnp.exp(jnp.minimum(pf - mid, EXP_CLAMP)), 0.0))
            kf_seg.append(jnp.where(blk <= i, kf * jnp.exp(jnp.minimum(mid - pf, EXP_CLAMP)), 0.0))
            rmid = pl.ds(base + SUB * i + SUB // 2, 1)
            nid = totb - pb_buf[rmid, :] + gb_buf[rmid, :]
            qb_seg.append(jnp.where(blk == i, q * jnp.exp(jnp.minimum(suf - nid, EXP_CLAMP)), 0.0))
            kb_seg.append(jnp.where(blk >= i, kb * jnp.exp(jnp.minimum(nid - suf, EXP_CLAMP)), 0.0))
        qf_big = jnp.concatenate(qf_seg, axis=1).astype(BF16)
        kf_big = jnp.concatenate(kf_seg, axis=1).astype(BF16)
        qb_big = jnp.concatenate(qb_seg, axis=1).astype(BF16)
        kb_big = jnp.concatenate(kb_seg, axis=1).astype(BF16)
        sc_f = lax.dot_general(qf_big, kf_big, nt, preferred_element_type=F32)
        sc_b = lax.dot_general(qb_big, kb_big, nt, preferred_element_type=F32)
        a = (jnp.where(c2 <= r2, sc_f, 0.0) + jnp.where(c2 >= r2, sc_b, 0.0)).astype(BF16)
        o = jnp.dot(a, i_ref[sl, :], preferred_element_type=F32)
        qcat = jnp.concatenate([q * jnp.exp(pf), q * jnp.exp(suf)], axis=1).astype(BF16)
        scat = jnp.concatenate([sf_ref[0, 0, c], sb_ref[0, 0, c]], axis=1)
        o = o + lax.dot_general(qcat, scat, nt, preferred_element_type=F32)
        o = _rms(o) * gain
        o_ref[sl, :] = (o * _silu(gt_ref[sl, :].astype(F32))).astype(BF16)
        return carry

    lax.fori_loop(0, nchunk, body, 0)


def _gla_out(z, lb4, hg_gain, senter, rows_per_block):
    m = z.shape[0]
    nchunk = rows_per_block // CHUNK

    def col(k):
        return pl.BlockSpec((rows_per_block, HG_DK), lambda h, b, k=k: (b, k * HG_HEADS + h))

    def st(d):
        return pl.BlockSpec((1, 1, nchunk, HG_DK, HG_DK), lambda h, b, d=d: (d, h, b, 0, 0))

    return pl.pallas_call(
        functools.partial(_gla_out_kernel, nchunk=nchunk),
        grid=(HG_HEADS, m // rows_per_block),
        in_specs=[col(0), col(1), col(2), col(3), col(4),
                  pl.BlockSpec((2, 1, 1, HG_DK), lambda h, b: (0, h, 0, 0)),
                  pl.BlockSpec((1, 1, HG_DK), lambda h, b: (h, 0, 0)),
                  st(0), st(1)],
        out_specs=pl.BlockSpec((rows_per_block, HG_DK), lambda h, b: (b, h)),
        out_shape=jax.ShapeDtypeStruct((m, HG_WIDTH), BF16),
        scratch_shapes=[pltpu.VMEM((rows_per_block, HG_DK), F32) for _ in range(6)],
        compiler_params=_params(("parallel", "parallel"), 32),
        name="gla_out",
    )(z, z, z, z, z, lb4, hg_gain.reshape(HG_HEADS, 1, HG_DK), senter, senter)


def _spatial_kernel(u_ref, v_ref, w_ref, bias_ref, lng_ref, lnb_ref, o_ref, y_buf):
    v = _gelu_tanh(v_ref[...].astype(F32))
    xc = v - jnp.mean(v, axis=-1, keepdims=True)
    y = xc * lax.rsqrt(jnp.mean(xc * xc, axis=-1, keepdims=True) + EPS) * lng_ref[...] + lnb_ref[...]
    y_buf[...] = y.astype(BF16)
    for n in range(u_ref.shape[0] // SG_CHUNK):
        rs = slice(n * SG_CHUNK, (n + 1) * SG_CHUNK)
        for g in range(SG_GROUPS):
            cs = slice(g * SG_CH, (g + 1) * SG_CH)
            mixed = jnp.dot(w_ref[g], y_buf[rs, cs], preferred_element_type=F32) + bias_ref[:, cs]
            o_ref[rs, cs] = (_gelu_tanh(u_ref[rs, cs].astype(F32)) * mixed).astype(BF16)


def _spatial(z, sg_w, bias_full, ln_g, ln_b, rows_per_block):
    m = z.shape[0]
    return pl.pallas_call(
        _spatial_kernel,
        grid=(m // rows_per_block,),
        in_specs=[pl.BlockSpec((rows_per_block, SG_WIDTH), lambda b: (b, 5)),
                  pl.BlockSpec((rows_per_block, SG_WIDTH), lambda b: (b, 6)),
                  pl.BlockSpec((SG_GROUPS, SG_CHUNK, SG_CHUNK), lambda b: (0, 0, 0)),
                  pl.BlockSpec((SG_CHUNK, SG_WIDTH), lambda b: (0, 0)),
                  pl.BlockSpec((1, SG_WIDTH), lambda b: (0, 0)),
                  pl.BlockSpec((1, SG_WIDTH), lambda b: (0, 0))],
        out_specs=pl.BlockSpec((rows_per_block, SG_WIDTH), lambda b: (b, 0)),
        out_shape=jax.ShapeDtypeStruct((m, SG_WIDTH), BF16),
        scratch_shapes=[pltpu.VMEM((rows_per_block, SG_WIDTH), BF16)],
        compiler_params=_params(("parallel",), 32),
        name="spatial_gate",
    )(z, z, sg_w, bias_full, ln_g, ln_b)


def _residual_epilogue(mix, x, vec_ref, xo_ref, ho_ref):
    xn = x + vec_ref[1:2] * (_rms(mix) * vec_ref[0:1])
    xo_ref[...] = xn
    if ho_ref is not None:
        ho_ref[...] = (_rms(xn) * vec_ref[2:3] * (1.0 + vec_ref[4:5]) + vec_ref[3:4]).astype(BF16)


def _mm_res_kernel(*refs, n_a, emit_h):
    a_refs, w_refs = refs[:n_a], refs[n_a:2 * n_a]
    x_ref, vec_ref, xo_ref = refs[2 * n_a], refs[2 * n_a + 1], refs[2 * n_a + 2]
    ho_ref = refs[2 * n_a + 3] if emit_h else None
    mix = jnp.dot(a_refs[0][...], w_refs[0][...], preferred_element_type=F32)
    for a_ref, w_ref in zip(a_refs[1:], w_refs[1:]):
        mix = mix + jnp.dot(a_ref[...], w_ref[...], preferred_element_type=F32)
    _residual_epilogue(mix, x_ref[...], vec_ref, xo_ref, ho_ref)


def _mm_res(a_list, w_list, x, vec, tm, emit_h=True):
    m = x.shape[0]
    n_a = len(a_list)
    in_specs = ([pl.BlockSpec((tm, a.shape[1]), lambda i: (i, 0)) for a in a_list]
                + [pl.BlockSpec(w.shape, lambda i: (0, 0)) for w in w_list]
                + [pl.BlockSpec((tm, D_MODEL), lambda i: (i, 0)),
                   pl.BlockSpec((8, D_MODEL), lambda i: (0, 0))])
    out_specs = [pl.BlockSpec((tm, D_MODEL), lambda i: (i, 0))]
    out_shape = [jax.ShapeDtypeStruct((m, D_MODEL), F32)]
    if emit_h:
        out_specs.append(pl.BlockSpec((tm, D_MODEL), lambda i: (i, 0)))
        out_shape.append(jax.ShapeDtypeStruct((m, D_MODEL), BF16))
    return pl.pallas_call(
        functools.partial(_mm_res_kernel, n_a=n_a, emit_h=emit_h),
        grid=(m // tm,),
        in_specs=in_specs, out_specs=out_specs, out_shape=out_shape,
        compiler_params=_params(("parallel",), 48),
        name="mm_residual",
    )(*a_list, *w_list, x, vec)


def _mlp_kernel(*refs, emit_h):
    h_ref, w1_ref, w2_ref, x_ref, vec_ref, xo_ref = refs[:6]
    ho_ref = refs[6] if emit_h else None
    acc_ref = refs[-1]
    j = pl.program_id(1)
    a = jnp.dot(h_ref[...], w1_ref[...], preferred_element_type=F32)
    a = jnp.square(jnp.maximum(a, 0.0)).astype(BF16)
    p = jnp.dot(a, w2_ref[...], preferred_element_type=F32)

    @pl.when(j == 0)
    def _():
        acc_ref[...] = p

    @pl.when(j > 0)
    def _():
        acc_ref[...] += p

    @pl.when(j == pl.num_programs(1) - 1)
    def _():
        _residual_epilogue(acc_ref[...], x_ref[...], vec_ref, xo_ref, ho_ref)


def _mlp(h, w1, w2, x, vec, tm, tf, emit_h):
    m = x.shape[0]
    out_specs = [pl.BlockSpec((tm, D_MODEL), lambda i, j: (i, 0))]
    out_shape = [jax.ShapeDtypeStruct((m, D_MODEL), F32)]
    if emit_h:
        out_specs.append(pl.BlockSpec((tm, D_MODEL), lambda i, j: (i, 0)))
        out_shape.append(jax.ShapeDtypeStruct((m, D_MODEL), BF16))
    return pl.pallas_call(
        functools.partial(_mlp_kernel, emit_h=emit_h),
        grid=(m // tm, D_FF // tf),
        in_specs=[pl.BlockSpec((tm, D_MODEL), lambda i, j: (i, 0)),
                  pl.BlockSpec((D_MODEL, tf), lambda i, j: (0, j)),
                  pl.BlockSpec((tf, D_MODEL), lambda i, j: (j, 0)),
                  pl.BlockSpec((tm, D_MODEL), lambda i, j: (i, 0)),
                  pl.BlockSpec((8, D_MODEL), lambda i, j: (0, 0))],
        out_specs=out_specs, out_shape=out_shape,
        scratch_shapes=[pltpu.VMEM((tm, D_MODEL), F32)],
        compiler_params=_params(("parallel", "arbitrary"), 52),
        name="mlp",
    )(h, w1, w2, x, vec)


def _dft_constants():
    def cs(n):
        k = np.arange(n)
        ang = 2.0 * np.pi * ((k[:, None] * k[None, :]) % n) / n
        return np.cos(ang) / np.sqrt(n), np.sin(ang) / np.sqrt(n)

    cc, sc = cs(FT_CH)
    w_ch = np.concatenate([cc, sc], axis=1)
    c1, s1 = cs(FFT_L1)
    m1a = np.concatenate([c1, -s1], axis=0)
    m1b = np.concatenate([-s1, -c1], axis=0)
    c2, s2 = cs(FFT_L2)
    u1 = np.arange(FFT_L1)[:, None]
    t2 = np.arange(FFT_L2)[None, :]
    ang = 2.0 * np.pi * ((u1 * t2) % SEQ) / SEQ
    tw_c = np.cos(ang).reshape(SEQ, 1)
    tw_s = np.sin(ang).reshape(SEQ, 1)
    f = lambda a: jnp.asarray(a, F32)
    return f(w_ch), f(m1a), f(m1b), f(c2), f(s2), f(tw_c), f(tw_s)


def _chan_dft_kernel(x_ref, w_ref, a_ref, b_ref):
    ab = jnp.dot(x_ref[...], w_ref[...], preferred_element_type=F32)
    a_ref[...] = ab[:, :FT_CH].astype(BF16)
    b_ref[...] = ab[:, FT_CH:].astype(BF16)


def _chan_dft(h, w_ch, tm):
    m = h.shape[0]
    blk = pl.BlockSpec((tm, FT_CH), lambda i, g: (i, g))
    return pl.pallas_call(
        _chan_dft_kernel,
        grid=(m // tm, FT_GROUPS),
        in_specs=[blk, pl.BlockSpec((FT_CH, 2 * FT_CH), lambda i, g: (0, 0))],
        out_specs=[blk, blk],
        out_shape=[jax.ShapeDtypeStruct((m, D_MODEL), BF16)] * 2,
        compiler_params=_params(("parallel", "parallel"), 32),
        name="chan_dft",
    )(h, w_ch)


def _pos_dft1_kernel(a_ref, b_ref, ma_ref, mb_ref, zr_ref, zi_ref):
    z = (jnp.dot(ma_ref[...], a_ref[...], preferred_element_type=F32)
         + jnp.dot(mb_ref[...], b_ref[...], preferred_element_type=F32))
    zr_ref[...] = z[:FFT_L1].astype(BF16)
    zi_ref[...] = z[FFT_L1:].astype(BF16)


def _pos_dft1(a2, b2, m1a, m1b, tn):
    n = a2.shape[1]
    blk = pl.BlockSpec((FFT_L1, tn), lambda j: (0, j))
    mspec = pl.BlockSpec((2 * FFT_L1, FFT_L1), lambda j: (0, 0))
    return pl.pallas_call(
        _pos_dft1_kernel,
        grid=(n // tn,),
        in_specs=[blk, blk, mspec, mspec],
        out_specs=[blk, blk],
        out_shape=[jax.ShapeDtypeStruct((FFT_L1, n), BF16)] * 2,
        compiler_params=_params(("parallel",), 32),
        name="pos_dft1",
    )(a2, b2, m1a, m1b)


def _pos_dft2_kernel(zr_ref, zi_ref, c_ref, s_ref, m2c_ref, m2s_ref, y_ref):
    zr = zr_ref[...].astype(F32)
    zi = zi_ref[...].astype(F32)
    c = c_ref[...]
    s = s_ref[...]
    tr = (zr * c + zi * s).astype(BF16)
    ti = (zi * c - zr * s).astype(BF16)
    y = (jnp.dot(m2c_ref[...], tr, preferred_element_type=F32)
         + jnp.dot(m2s_ref[...], ti, preferred_element_type=F32))
    y_ref[...] = y.astype(BF16)


def _pos_dft2(zr, zi, tw_c, tw_s, m2c, m2s, tc):
    blk = pl.BlockSpec((FFT_L2, tc), lambda u, j: (u, j))
    tw = pl.BlockSpec((FFT_L2, 1), lambda u, j: (u, 0))
    mspec = pl.BlockSpec((FFT_L2, FFT_L2), lambda u, j: (0, 0))
    return pl.pallas_call(
        _pos_dft2_kernel,
        grid=(FFT_L1, D_MODEL // tc),
        in_specs=[blk, blk, tw, tw, mspec, mspec],
        out_specs=blk,
        out_shape=jax.ShapeDtypeStruct((SEQ, D_MODEL), BF16),
        compiler_params=_params(("parallel", "parallel"), 32),
        name="pos_dft2",
    )(zr, zi, tw_c, tw_s, m2c, m2s)


def _fourier_mix(h):
    w_ch, m1a, m1b, m2c, m2s, tw_c, tw_s = _dft_constants()
    a, b = _chan_dft(h, w_ch.astype(BF16), 512)
    a2 = a.reshape(FFT_L1, FFT_L2 * D_MODEL)
    b2 = b.reshape(FFT_L1, FFT_L2 * D_MODEL)
    zr2, zi2 = _pos_dft1(a2, b2, m1a.astype(BF16), m1b.astype(BF16), 8192)
    zr = zr2.reshape(SEQ, D_MODEL)
    zi = zi2.reshape(SEQ, D_MODEL)
    yp = _pos_dft2(zr, zi, tw_c, tw_s, m2c.astype(BF16), m2s.astype(BF16), D_MODEL)
    return yp.reshape(FFT_L1, FFT_L2, D_MODEL).transpose(1, 0, 2).reshape(SEQ, D_MODEL)


def _pack_rows(*rows):
    rows = [r.reshape(1, D_MODEL).astype(F32) for r in rows]
    rows += [jnp.zeros((1, D_MODEL), F32)] * (8 - len(rows))
    return jnp.concatenate(rows, axis=0)


def kernel(x, c, ctx, c_ctx, w_ada, b_ada, norm_gain, w_in, w_out, lb_raw, hg_norm_gain,
           sg_w, sg_b, sg_ln_gain, sg_ln_bias, w_fourier, w_mlp_in, w_mlp_out):
    assert x.shape == (1, SEQ, D_MODEL) and ctx.shape == (1, CTX_LEN, D_MODEL)
    xs = x.reshape(SEQ, D_MODEL)
    cs = ctx.reshape(CTX_LEN, D_MODEL)

    mods = _ada_mods(jnp.concatenate([c.reshape(1, D_MODEL), c_ctx.reshape(1, D_MODEL)], axis=0), w_ada, b_ada)
    mod = lambda l, r, j: mods[l, r, j * D_MODEL:(j + 1) * D_MODEL]

    lb = jnp.cumsum(jax.nn.softmax(lb_raw.astype(F32), axis=1), axis=1)[:, 0]
    lb4 = lb.reshape(2, HG_HEADS, 1, HG_DK)

    w_in0 = w_in[0].astype(BF16)
    vec_x = _pack_rows(norm_gain[0, 0], mod(0, 0, 0), mod(0, 0, 1))
    vec_c = _pack_rows(norm_gain[0, 0], mod(0, 1, 0), mod(0, 1, 1))
    z = _norm_mm(xs, vec_x, w_in0, IN_WIDTH, 1024, 1024)
    z_ctx = _norm_mm(cs, vec_c, w_in0, 3 * HG_WIDTH, CTX_LEN, 1024)
    zero_state = jnp.zeros((2, HG_HEADS, HG_DK, HG_DK), F32)
    _, s_ctx = _gla_states(z_ctx, lb4, zero_state, CTX_LEN)
    senter, _ = _gla_states(z, lb4, s_ctx, 512)
    o = _gla_out(z, lb4, hg_norm_gain[0], senter, 512)
    bias_full = jnp.repeat(sg_b[0].T.astype(F32), SG_CH, axis=1)
    s = _spatial(z, sg_w[0].astype(BF16), bias_full, sg_ln_gain[0].reshape(1, SG_WIDTH),
                 sg_ln_bias[0].reshape(1, SG_WIDTH), 512)
    w_out0 = w_out[0].astype(BF16)
    vec = _pack_rows(norm_gain[0, 1], mod(0, 0, 2), norm_gain[0, 2], mod(0, 0, 3), mod(0, 0, 4))
    x1, h = _mm_res([o, s], [w_out0[:HG_WIDTH], w_out0[HG_WIDTH:]], xs, vec, 256)
    vec = _pack_rows(norm_gain[0, 3], mod(0, 0, 5), norm_gain[1, 0], mod(1, 0, 0), mod(1, 0, 1))
    x2, h = _mlp(h, w_mlp_in[0].astype(BF16), w_mlp_out[0].astype(BF16), x1, vec, 512, 512, True)

    y = _fourier_mix(h)
    vec = _pack_rows(norm_gain[1, 1], mod(1, 0, 2), norm_gain[1, 2], mod(1, 0, 3), mod(1, 0, 4))
    x3, h = _mm_res([y], [w_fourier[0].astype(BF16)], x2, vec, 256)
    vec = _pack_rows(norm_gain[1, 3], mod(1, 0, 5))
    (x4,) = _mlp(h, w_mlp_in[1].astype(BF16), w_mlp_out[1].astype(BF16), x3, vec, 512, 512, False)
    return x4.reshape(1, SEQ, D_MODEL)
```

```python
import functools

import numpy as np
import jax
import jax.numpy as jnp
from jax import lax
from jax.experimental import pallas as pl
from jax.experimental.pallas import tpu as pltpu

D_MODEL = 2048
SEQ = 8192
DEPTH = 2
CTX_LEN = 256
CHUNK = 64
SUB = 16
HG_HEADS = 8
HG_DK = 128
HG_WIDTH = HG_HEADS * HG_DK
SG_GROUPS = 8
SG_CH = 128
SG_WIDTH = SG_GROUPS * SG_CH
SG_CHUNK = 128
FT_GROUPS = 4
FT_CH = D_MODEL // FT_GROUPS
FFT_L1 = 64
FFT_L2 = 128
IN_WIDTH = 5 * HG_WIDTH + 2 * SG_WIDTH
D_FF = 4 * D_MODEL
N_MOD = 6
EPS = 1e-6
EXP_CLAMP = 80.0
LANES = 128

F32 = jnp.float32
BF16 = jnp.bfloat16
MIB = 1024 * 1024


def _params(semantics, vmem_mib):
    return pltpu.CompilerParams(dimension_semantics=semantics, vmem_limit_bytes=vmem_mib * MIB)


def _rms(x):
    return x * lax.rsqrt(jnp.mean(x * x, axis=-1, keepdims=True) + EPS)


def _silu(x):
    return x * jax.nn.sigmoid(x)


def _gelu_tanh(x):
    cdf = 0.5 * (1.0 + jnp.tanh(float(np.sqrt(2.0 / np.pi)) * (x + 0.044715 * (x * x * x))))
    return x * cdf


def _ada_kernel(c_ref, w_ref, b_ref, o_ref):
    tn = o_ref.shape[-1]
    for r in range(2):
        s = _silu(c_ref[r])
        cols = [jnp.sum(w_ref[0, :, j * LANES:(j + 1) * LANES] * s, axis=0, keepdims=True)
                for j in range(tn // LANES)]
        o_ref[0, r:r + 1, :] = jnp.concatenate(cols, axis=1) + b_ref[0]


def _ada_mods(c2, w_ada, b_ada):
    tn = 512
    n = N_MOD * D_MODEL
    cb = jnp.broadcast_to(c2[:, :, None], (2, D_MODEL, LANES))
    return pl.pallas_call(
        _ada_kernel,
        grid=(DEPTH, n // tn),
        in_specs=[pl.BlockSpec((2, D_MODEL, LANES), lambda l, j: (0, 0, 0)),
                  pl.BlockSpec((1, D_MODEL, tn), lambda l, j: (l, 0, j)),
                  pl.BlockSpec((1, 1, tn), lambda l, j: (l, 0, j))],
        out_specs=pl.BlockSpec((1, 2, tn), lambda l, j: (l, 0, j)),
        out_shape=jax.ShapeDtypeStruct((DEPTH, 2, n), F32),
        compiler_params=_params(("parallel", "parallel"), 32),
        name="ada_mods",
    )(cb, w_ada, b_ada.reshape(DEPTH, 1, n))


def _norm_mm_kernel(x_ref, vec_ref, w_ref, o_ref, h_ref):
    @pl.when(pl.program_id(1) == 0)
    def _():
        h = _rms(x_ref[...]) * vec_ref[0:1] * (1.0 + vec_ref[2:3]) + vec_ref[1:2]
        h_ref[...] = h.astype(BF16)

    o_ref[...] = jnp.dot(h_ref[...], w_ref[...], preferred_element_type=F32).astype(o_ref.dtype)


def _norm_mm(x, vec, w, n_cols, tm, tn):
    m = x.shape[0]
    return pl.pallas_call(
        _norm_mm_kernel,
        grid=(m // tm, n_cols // tn),
        in_specs=[pl.BlockSpec((tm, D_MODEL), lambda i, j: (i, 0)),
                  pl.BlockSpec((8, D_MODEL), lambda i, j: (0, 0)),
                  pl.BlockSpec((D_MODEL, tn), lambda i, j: (0, j))],
        out_specs=pl.BlockSpec((tm, tn), lambda i, j: (i, j)),
        out_shape=jax.ShapeDtypeStruct((m, n_cols), BF16),
        scratch_shapes=[pltpu.VMEM((tm, D_MODEL), BF16)],
        compiler_params=_params(("parallel", "arbitrary"), 48),
        name="norm_mm",
    )(x, vec, w)


def _seg_cumsum(g, row_in_chunk):
    p = g
    s = 1
    while s < CHUNK:
        p = p + jnp.where(row_in_chunk >= s, pltpu.roll(p, s, axis=0), 0.0)
        s *= 2
    return p


def _gla_state_kernel(f_ref, i_ref, lb_ref, s0_ref, senter_ref, sfin_ref,
                      s_acc, p_buf, g_buf, k_buf, *, nchunk):
    d = pl.program_id(0)

    @pl.when(pl.program_id(2) == 0)
    def _():
        s_acc[...] = s0_ref[0, 0]

    rows = f_ref.shape[0]
    lb = lb_ref[0, 0]
    f = lb + (1.0 - lb) * jax.nn.sigmoid(f_ref[...].astype(F32))
    g = jnp.log(f)
    ric = lax.broadcasted_iota(jnp.int32, (rows, LANES), 0) & (CHUNK - 1)
    p_buf[...] = _seg_cumsum(g, ric)
    g_buf[...] = g
    k_buf[...] = 1.0 - f

    s = s_acc[...]
    for j in range(nchunk):
        c = jnp.where(d == 0, j, nchunk - 1 - j)
        base = pl.multiple_of(c * CHUNK, CHUNK)
        sl = pl.ds(base, CHUNK)
        p = p_buf[sl, :]
        tot = p_buf[pl.ds(base + CHUNK - 1, 1), :]
        ex = jnp.where(d == 0, tot - p, p - g_buf[sl, :])
        kt = (k_buf[sl, :] * jnp.exp(ex)).astype(BF16)
        u = lax.dot_general(i_ref[sl, :], kt, (((0,), (0,)), ((), ())), preferred_element_type=F32)
        senter_ref[0, 0, c] = s.astype(BF16)
        s = s * jnp.exp(tot) + u
    s_acc[...] = s
    sfin_ref[0, 0] = s


def _gla_states(z, lb4, s0, rows_per_block):
    m = z.shape[0]
    nb = m // rows_per_block
    nchunk = rows_per_block // CHUNK

    def rb(d, h, b):
        return b + d * (nb - 1 - 2 * b)

    return pl.pallas_call(
        functools.partial(_gla_state_kernel, nchunk=nchunk),
        grid=(2, HG_HEADS, nb),
        in_specs=[pl.BlockSpec((rows_per_block, HG_DK), lambda d, h, b: (rb(d, h, b), d * HG_HEADS + h)),
                  pl.BlockSpec((rows_per_block, HG_DK), lambda d, h, b: (rb(d, h, b), 2 * HG_HEADS + h)),
                  pl.BlockSpec((1, 1, 1, HG_DK), lambda d, h, b: (d, h, 0, 0)),
                  pl.BlockSpec((1, 1, HG_DK, HG_DK), lambda d, h, b: (d, h, 0, 0))],
        out_specs=[pl.BlockSpec((1, 1, nchunk, HG_DK, HG_DK), lambda d, h, b: (d, h, rb(d, h, b), 0, 0)),
                   pl.BlockSpec((1, 1, HG_DK, HG_DK), lambda d, h, b: (d, h, 0, 0))],
        out_shape=[jax.ShapeDtypeStruct((2, HG_HEADS, m // CHUNK, HG_DK, HG_DK), BF16),
                   jax.ShapeDtypeStruct((2, HG_HEADS, HG_DK, HG_DK), F32)],
        scratch_shapes=[pltpu.VMEM((HG_DK, HG_DK), F32),
                        pltpu.VMEM((rows_per_block, HG_DK), F32),
                        pltpu.VMEM((rows_per_block, HG_DK), F32),
                        pltpu.VMEM((rows_per_block, HG_DK), F32)],
        compiler_params=_params(("parallel", "parallel", "arbitrary"), 32),
        name="gla_states",
    )(z, z, lb4, s0)


def _gla_out_kernel(ff_ref, fb_ref, i_ref, q_ref, gt_ref, lb_ref, gain_ref, sf_ref, sb_ref, o_ref,
                    pf_buf, pb_buf, gb_buf, kf_buf, kb_buf, q_buf, *, nchunk):
    rows = ff_ref.shape[0]
    ric = lax.broadcasted_iota(jnp.int32, (rows, LANES), 0) & (CHUNK - 1)
    lbf = lb_ref[0, 0]
    lbb = lb_ref[1, 0]
    ff = lbf + (1.0 - lbf) * jax.nn.sigmoid(ff_ref[...].astype(F32))
    pf_buf[...] = _seg_cumsum(jnp.log(ff), ric)
    kf_buf[...] = 1.0 - ff
    fb = lbb + (1.0 - lbb) * jax.nn.sigmoid(fb_ref[...].astype(F32))
    gb = jnp.log(fb)
    pb_buf[...] = _seg_cumsum(gb, ric)
    gb_buf[...] = gb
    kb_buf[...] = 1.0 - fb
    q_buf[...] = _silu(q_ref[...].astype(F32)) * (HG_DK ** -0.5)

    blk = lax.broadcasted_iota(jnp.int32, (CHUNK, LANES), 0) // SUB
    r2 = lax.broadcasted_iota(jnp.int32, (CHUNK, CHUNK), 0)
    c2 = lax.broadcasted_iota(jnp.int32, (CHUNK, CHUNK), 1)
    nt = (((1,), (1,)), ((), ()))
    gain = gain_ref[0]

    for c in range(nchunk):
        base = c * CHUNK
        sl = pl.ds(base, CHUNK)
        pf = pf_buf[sl, :]
        pb = pb_buf[sl, :]
        gbc = gb_buf[sl, :]
        kf = kf_buf[sl, :]
        kb = kb_buf[sl, :]
        q = q_buf[sl, :]
        totb = pb_buf[pl.ds(base + CHUNK - 1, 1), :]
        suf = totb - pb + gbc
        qf_seg, kf_seg, qb_seg, kb_seg = [], [], [], []
        for i in range(CHUNK // SUB):
            mid = pf_buf[pl.ds(base + SUB * i + SUB // 2 - 1, 1), :]
            qf_seg.append(jnp.where(blk == i, q * jnp.exp(jnp.minimum(pf - mid, EXP_CLAMP)), 0.0))
            kf_seg.append(jnp.where(blk <= i, kf * jnp.exp(jnp.minimum(mid - pf, EXP_CLAMP)), 0.0))
            rmid = pl.ds(base + SUB * i + SUB // 2, 1)
            nid = totb - pb_buf[rmid, :] + gb_buf[rmid, :]
            qb_seg.append(jnp.where(blk == i, q * jnp.exp(jnp.minimum(suf - nid, EXP_CLAMP)), 0.0))
            kb_seg.append(jnp.where(blk >= i, kb * jnp.exp(jnp.minimum(nid - suf, EXP_CLAMP)), 0.0))
        qf_big = jnp.concatenate(qf_seg, axis=1).astype(BF16)
        kf_big = jnp.concatenate(kf_seg, axis=1).astype(BF16)
        qb_big = jnp.concatenate(qb_seg, axis=1).astype(BF16)
        kb_big = jnp.concatenate(kb_seg, axis=1).astype(BF16)
        sc_f = lax.dot_general(qf_big, kf_big, nt, preferred_element_type=F32)
        sc_b = lax.dot_general(qb_big, kb_big, nt, preferred_element_type=F32)
        a = (jnp.where(c2 <= r2, sc_f, 0.0) + jnp.where(c2 >= r2, sc_b, 0.0)).astype(BF16)
        o = jnp.dot(a, i_ref[sl, :], preferred_element_type=F32)
        qcat = jnp.concatenate([q * jnp.exp(pf), q * jnp.exp(suf)], axis=1).astype(BF16)
        scat = jnp.concatenate([sf_ref[0, 0, c], sb_ref[0, 0, c]], axis=1)
        o = o + lax.dot_general(qcat, scat, nt, preferred_element_type=F32)
        o = _rms(o) * gain
        o_ref[sl, :] = (o * _silu(gt_ref[sl, :].astype(F32))).astype(BF16)


def _gla_out(z, lb4, hg_gain, senter, rows_per_block):
    m = z.shape[0]
    nchunk = rows_per_block // CHUNK

    def col(k):
        return pl.BlockSpec((rows_per_block, HG_DK), lambda h, b, k=k: (b, k * HG_HEADS + h))

    def st(d):
        return pl.BlockSpec((1, 1, nchunk, HG_DK, HG_DK), lambda h, b, d=d: (d, h, b, 0, 0))

    return pl.pallas_call(
        functools.partial(_gla_out_kernel, nchunk=nchunk),
        grid=(HG_HEADS, m // rows_per_block),
        in_specs=[col(0), col(1), col(2), col(3), col(4),
                  pl.BlockSpec((2, 1, 1, HG_DK), lambda h, b: (0, h, 0, 0)),
                  pl.BlockSpec((1, 1, HG_DK), lambda h, b: (h, 0, 0)),
                  st(0), st(1)],
        out_specs=pl.BlockSpec((rows_per_block, HG_DK), lambda h, b: (b, h)),
        out_shape=jax.ShapeDtypeStruct((m, HG_WIDTH), BF16),
        scratch_shapes=[pltpu.VMEM((rows_per_block, HG_DK), F32) for _ in range(6)],
        compiler_params=_params(("parallel", "parallel"), 32),
        name="gla_out",
    )(z, z, z, z, z, lb4, hg_gain.reshape(HG_HEADS, 1, HG_DK), senter, senter)


def _spatial_kernel(u_ref, v_ref, w_ref, bias_ref, lng_ref, lnb_ref, o_ref, y_buf):
    v = _gelu_tanh(v_ref[...].astype(F32))
    xc = v - jnp.mean(v, axis=-1, keepdims=True)
    y = xc * lax.rsqrt(jnp.mean(xc * xc, axis=-1, keepdims=True) + EPS) * lng_ref[...] + lnb_ref[...]
    y_buf[...] = y.astype(BF16)
    for n in range(u_ref.shape[0] // SG_CHUNK):
        rs = slice(n * SG_CHUNK, (n + 1) * SG_CHUNK)
        for g in range(SG_GROUPS):
            cs = slice(g * SG_CH, (g + 1) * SG_CH)
            mixed = jnp.dot(w_ref[g], y_buf[rs, cs], preferred_element_type=F32) + bias_ref[:, cs]
            o_ref[rs, cs] = (_gelu_tanh(u_ref[rs, cs].astype(F32)) * mixed).astype(BF16)


def _spatial(z, sg_w, bias_full, ln_g, ln_b, rows_per_block):
    m = z.shape[0]
    return pl.pallas_call(
        _spatial_kernel,
        grid=(m // rows_per_block,),
        in_specs=[pl.BlockSpec((rows_per_block, SG_WIDTH), lambda b: (b, 5)),
                  pl.BlockSpec((rows_per_block, SG_WIDTH), lambda b: (b, 6)),
                  pl.BlockSpec((SG_GROUPS, SG_CHUNK, SG_CHUNK), lambda b: (0, 0, 0)),
                  pl.BlockSpec((SG_CHUNK, SG_WIDTH), lambda b: (0, 0)),
                  pl.BlockSpec((1, SG_WIDTH), lambda b: (0, 0)),
                  pl.BlockSpec((1, SG_WIDTH), lambda b: (0, 0))],
        out_specs=pl.BlockSpec((rows_per_block, SG_WIDTH), lambda b: (b, 0)),
        out_shape=jax.ShapeDtypeStruct((m, SG_WIDTH), BF16),
        scratch_shapes=[pltpu.VMEM((rows_per_block, SG_WIDTH), BF16)],
        compiler_params=_params(("parallel",), 32),
        name="spatial_gate",
    )(z, z, sg_w, bias_full, ln_g, ln_b)


def _residual_epilogue(mix, x, vec_ref, xo_ref, ho_ref):
    xn = x + vec_ref[1:2] * (_rms(mix) * vec_ref[0:1])
    xo_ref[...] = xn
    if ho_ref is not None:
        ho_ref[...] = (_rms(xn) * vec_ref[2:3] * (1.0 + vec_ref[4:5]) + vec_ref[3:4]).astype(BF16)


def _mm_res_kernel(*refs, n_a, emit_h):
    a_refs, w_refs = refs[:n_a], refs[n_a:2 * n_a]
    x_ref, vec_ref, xo_ref = refs[2 * n_a], refs[2 * n_a + 1], refs[2 * n_a + 2]
    ho_ref = refs[2 * n_a + 3] if emit_h else None
    mix = jnp.dot(a_refs[0][...], w_refs[0][...], preferred_element_type=F32)
    for a_ref, w_ref in zip(a_refs[1:], w_refs[1:]):
        mix = mix + jnp.dot(a_ref[...], w_ref[...], preferred_element_type=F32)
    _residual_epilogue(mix, x_ref[...], vec_ref, xo_ref, ho_ref)


def _mm_res(a_list, w_list, x, vec, tm, emit_h=True):
    m = x.shape[0]
    n_a = len(a_list)
    in_specs = ([pl.BlockSpec((tm, a.shape[1]), lambda i: (i, 0)) for a in a_list]
                + [pl.BlockSpec(w.shape, lambda i: (0, 0)) for w in w_list]
                + [pl.BlockSpec((tm, D_MODEL), lambda i: (i, 0)),
                   pl.BlockSpec((8, D_MODEL), lambda i: (0, 0))])
    out_specs = [pl.BlockSpec((tm, D_MODEL), lambda i: (i, 0))]
    out_shape = [jax.ShapeDtypeStruct((m, D_MODEL), F32)]
    if emit_h:
        out_specs.append(pl.BlockSpec((tm, D_MODEL), lambda i: (i, 0)))
        out_shape.append(jax.ShapeDtypeStruct((m, D_MODEL), BF16))
    return pl.pallas_call(
        functools.partial(_mm_res_kernel, n_a=n_a, emit_h=emit_h),
        grid=(m // tm,),
        in_specs=in_specs, out_specs=out_specs, out_shape=out_shape,
        compiler_params=_params(("parallel",), 48),
        name="mm_residual",
    )(*a_list, *w_list, x, vec)


def _mlp_kernel(*refs, emit_h):
    h_ref, w1_ref, w2_ref, x_ref, vec_ref, xo_ref = refs[:6]
    ho_ref = refs[6] if emit_h else None
    acc_ref = refs[-1]
    j = pl.program_id(1)
    a = jnp.dot(h_ref[...], w1_ref[...], preferred_element_type=F32)
    a = jnp.square(jnp.maximum(a, 0.0)).astype(BF16)
    p = jnp.dot(a, w2_ref[...], preferred_element_type=F32)

    @pl.when(j == 0)
    def _():
        acc_ref[...] = p

    @pl.when(j > 0)
    def _():
        acc_ref[...] += p

    @pl.when(j == pl.num_programs(1) - 1)
    def _():
        _residual_epilogue(acc_ref[...], x_ref[...], vec_ref, xo_ref, ho_ref)


def _mlp(h, w1, w2, x, vec, tm, tf, emit_h):
    m = x.shape[0]
    out_specs = [pl.BlockSpec((tm, D_MODEL), lambda i, j: (i, 0))]
    out_shape = [jax.ShapeDtypeStruct((m, D_MODEL), F32)]
    if emit_h:
        out_specs.append(pl.BlockSpec((tm, D_MODEL), lambda i, j: (i, 0)))
        out_shape.append(jax.ShapeDtypeStruct((m, D_MODEL), BF16))
    return pl.pallas_call(
        functools.partial(_mlp_kernel, emit_h=emit_h),
        grid=(m // tm, D_FF // tf),
        in_specs=[pl.BlockSpec((tm, D_MODEL), lambda i, j: (i, 0)),
                  pl.BlockSpec((D_MODEL, tf), lambda i, j: (0, j)),
                  pl.BlockSpec((tf, D_MODEL), lambda i, j: (j, 0)),
                  pl.BlockSpec((tm, D_MODEL), lambda i, j: (i, 0)),
                  pl.BlockSpec((8, D_MODEL), lambda i, j: (0, 0))],
        out_specs=out_specs, out_shape=out_shape,
        scratch_shapes=[pltpu.VMEM((tm, D_MODEL), F32)],
        compiler_params=_params(("parallel", "arbitrary"), 52),
        name="mlp",
    )(h, w1, w2, x, vec)


def _dft_constants():
    def cs(n):
        k = np.arange(n)
        ang = 2.0 * np.pi * ((k[:, None] * k[None, :]) % n) / n
        return np.cos(ang) / np.sqrt(n), np.sin(ang) / np.sqrt(n)

    cc, sc = cs(FT_CH)
    c1, s1 = cs(FFT_L1)
    m1 = np.concatenate([c1, -s1], axis=0)
    c2, s2 = cs(FFT_L2)
    m2 = np.block([[c2, s2], [-s2, c2]])
    u1 = np.arange(FFT_L1)[:, None]
    t2 = np.arange(FFT_L2)[None, :]
    ang = 2.0 * np.pi * ((u1 * t2) % SEQ) / SEQ
    tw_c = np.cos(ang).reshape(SEQ, 1)
    tw_s = np.sin(ang).reshape(SEQ, 1)
    f = lambda a: jnp.asarray(a, F32)
    return f(cc), f(sc), f(m1), f(m2), f(tw_c), f(tw_s)


def _swap_row_factors(a, n_outer, n_inner):
    return a.reshape(n_outer, n_inner, a.shape[1]).transpose(1, 0, 2).reshape(a.shape)


def _pos_dft1_kernel(x_ref, m_ref, zr_ref, zi_ref):
    for k in range(x_ref.shape[0] // FFT_L1):
        sl = slice(k * FFT_L1, (k + 1) * FFT_L1)
        z = jnp.dot(m_ref[...], x_ref[sl, :], preferred_element_type=F32)
        zr_ref[sl, :] = z[:FFT_L1].astype(BF16)
        zi_ref[sl, :] = z[FFT_L1:].astype(BF16)


def _pos_dft1(xt, m1, slabs):
    rows = slabs * FFT_L1
    blk = pl.BlockSpec((rows, D_MODEL), lambda j: (j, 0))
    return pl.pallas_call(
        _pos_dft1_kernel,
        grid=(SEQ // rows,),
        in_specs=[blk, pl.BlockSpec((2 * FFT_L1, FFT_L1), lambda j: (0, 0))],
        out_specs=[blk, blk],
        out_shape=[jax.ShapeDtypeStruct((SEQ, D_MODEL), BF16)] * 2,
        compiler_params=_params(("parallel",), 32),
        name="pos_dft1",
    )(xt, m1)


def _pos_dft2_kernel(zr_ref, zi_ref, c_ref, s_ref, m_ref, wc_ref, ws_ref, y_ref, yr_buf, yi_buf):
    for k in range(zr_ref.shape[0] // FFT_L2):
        sl = slice(k * FFT_L2, (k + 1) * FFT_L2)
        zr = zr_ref[sl, :].astype(F32)
        zi = zi_ref[sl, :].astype(F32)
        c = c_ref[sl, :]
        s = s_ref[sl, :]
        t = jnp.concatenate([(zr * c + zi * s).astype(BF16), (zi * c - zr * s).astype(BF16)], axis=0)
        y = jnp.dot(m_ref[...], t, preferred_element_type=F32)
        yr_buf[sl, :] = y[:FFT_L2].astype(BF16)
        yi_buf[sl, :] = y[FFT_L2:].astype(BF16)
    for g in range(FT_GROUPS):
        cs = slice(g * FT_CH, (g + 1) * FT_CH)
        y_ref[:, cs] = (jnp.dot(yr_buf[:, cs], wc_ref[...], preferred_element_type=F32)
                        + jnp.dot(yi_buf[:, cs], ws_ref[...], preferred_element_type=F32)).astype(BF16)


def _pos_dft2(zr, zi, tw_c, tw_s, m2, wc, ws, slabs):
    rows = slabs * FFT_L2
    blk = pl.BlockSpec((rows, D_MODEL), lambda j: (j, 0))
    tw = pl.BlockSpec((rows, 1), lambda j: (j, 0))
    const = lambda a: pl.BlockSpec(a.shape, lambda j: (0, 0))
    return pl.pallas_call(
        _pos_dft2_kernel,
        grid=(SEQ // rows,),
        in_specs=[blk, blk, tw, tw, const(m2), const(wc), const(ws)],
        out_specs=blk,
        out_shape=jax.ShapeDtypeStruct((SEQ, D_MODEL), BF16),
        scratch_shapes=[pltpu.VMEM((rows, D_MODEL), BF16), pltpu.VMEM((rows, D_MODEL), BF16)],
        compiler_params=_params(("parallel",), 40),
        name="pos_dft2",
    )(zr, zi, tw_c, tw_s, m2, wc, ws)


def _fourier_mix(h):
    cc, sc, m1, m2, tw_c, tw_s = _dft_constants()
    ht = _swap_row_factors(h, FFT_L1, FFT_L2)
    zr, zi = _pos_dft1(ht, m1.astype(BF16), 8)
    zr = _swap_row_factors(zr, FFT_L2, FFT_L1)
    zi = _swap_row_factors(zi, FFT_L2, FFT_L1)
    yp = _pos_dft2(zr, zi, tw_c, tw_s, m2.astype(BF16), cc.astype(BF16), sc.astype(BF16), 4)
    return _swap_row_factors(yp, FFT_L1, FFT_L2)


def _pack_rows(*rows):
    rows = [r.reshape(1, D_MODEL).astype(F32) for r in rows]
    rows += [jnp.zeros((1, D_MODEL), F32)] * (8 - len(rows))
    return jnp.concatenate(rows, axis=0)


def kernel(x, c, ctx, c_ctx, w_ada, b_ada, norm_gain, w_in, w_out, lb_raw, hg_norm_gain,
           sg_w, sg_b, sg_ln_gain, sg_ln_bias, w_fourier, w_mlp_in, w_mlp_out):
    assert x.shape == (1, SEQ, D_MODEL) and ctx.shape == (1, CTX_LEN, D_MODEL)
    xs = x.reshape(SEQ, D_MODEL)
    cs = ctx.reshape(CTX_LEN, D_MODEL)

    mods = _ada_mods(jnp.concatenate([c.reshape(1, D_MODEL), c_ctx.reshape(1, D_MODEL)], axis=0), w_ada, b_ada)
    mod = lambda l, r, j: mods[l, r, j * D_MODEL:(j + 1) * D_MODEL]

    lb = jnp.cumsum(jax.nn.softmax(lb_raw.astype(F32), axis=1), axis=1)[:, 0]
    lb4 = lb.reshape(2, HG_HEADS, 1, HG_DK)

    w_in0 = w_in[0].astype(BF16)
    vec_x = _pack_rows(norm_gain[0, 0], mod(0, 0, 0), mod(0, 0, 1))
    vec_c = _pack_rows(norm_gain[0, 0], mod(0, 1, 0), mod(0, 1, 1))
    z = _norm_mm(xs, vec_x, w_in0, IN_WIDTH, 1024, 1024)
    z_ctx = _norm_mm(cs, vec_c, w_in0, 3 * HG_WIDTH, CTX_LEN, 1024)
    zero_state = jnp.zeros((2, HG_HEADS, HG_DK, HG_DK), F32)
    _, s_ctx = _gla_states(z_ctx, lb4, zero_state, CTX_LEN)
    senter, _ = _gla_states(z, lb4, s_ctx, 512)
    o = _gla_out(z, lb4, hg_norm_gain[0], senter, 512)
    bias_full = jnp.repeat(sg_b[0].T.astype(F32), SG_CH, axis=1)
    s = _spatial(z, sg_w[0].astype(BF16), bias_full, sg_ln_gain[0].reshape(1, SG_WIDTH),
                 sg_ln_bias[0].reshape(1, SG_WIDTH), 512)
    w_out0 = w_out[0].astype(BF16)
    vec = _pack_rows(norm_gain[0, 1], mod(0, 0, 2), norm_gain[0, 2], mod(0, 0, 3), mod(0, 0, 4))
    x1, h = _mm_res([o, s], [w_out0[:HG_WIDTH], w_out0[HG_WIDTH:]], xs, vec, 256)
    vec = _pack_rows(norm_gain[0, 3], mod(0, 0, 5), norm_gain[1, 0], mod(1, 0, 0), mod(1, 0, 1))
    x2, h = _mlp(h, w_mlp_in[0].astype(BF16), w_mlp_out[0].astype(BF16), x1, vec, 512, 1024, True)

    y = _fourier_mix(h)
    vec = _pack_rows(norm_gain[1, 1], mod(1, 0, 2), norm_gain[1, 2], mod(1, 0, 3), mod(1, 0, 4))
    x3, h = _mm_res([y], [w_fourier[0].astype(BF16)], x2, vec, 256)
    vec = _pack_rows(norm_gain[1, 3], mod(1, 0, 5))
    (x4,) = _mlp(h, w_mlp_in[1].astype(BF16), w_mlp_out[1].astype(BF16), x3, vec, 512, 1024, False)
    return x4.reshape(1, SEQ, D_MODEL)
```

```python
import functools

import numpy as np
import jax
import jax.numpy as jnp
from jax import lax
from jax.experimental import pallas as pl
from jax.experimental.pallas import tpu as pltpu

D_MODEL = 2048
SEQ = 8192
DEPTH = 2
CTX_LEN = 256
CHUNK = 64
SUB = 16
GLA_ROWS = 1024
HG_HEADS = 8
HG_DK = 128
HG_WIDTH = HG_HEADS * HG_DK
SG_GROUPS = 8
SG_CH = 128
SG_WIDTH = SG_GROUPS * SG_CH
SG_CHUNK = 128
FT_GROUPS = 4
FT_CH = D_MODEL // FT_GROUPS
FFT_L1 = 64
FFT_L2 = 128
IN_WIDTH = 5 * HG_WIDTH + 2 * SG_WIDTH
D_FF = 4 * D_MODEL
N_MOD = 6
EPS = 1e-6
EXP_CLAMP = 80.0
LANES = 128

F32 = jnp.float32
BF16 = jnp.bfloat16
MIB = 1024 * 1024


def _params(semantics, vmem_mib):
    return pltpu.CompilerParams(dimension_semantics=semantics, vmem_limit_bytes=vmem_mib * MIB)


def _rms(x):
    return x * lax.rsqrt(jnp.mean(x * x, axis=-1, keepdims=True) + EPS)


def _silu(x):
    return x * jax.nn.sigmoid(x)


def _gelu_tanh(x):
    cdf = 0.5 * (1.0 + jnp.tanh(float(np.sqrt(2.0 / np.pi)) * (x + 0.044715 * (x * x * x))))
    return x * cdf


def _ada_kernel(c_ref, w_ref, b_ref, o_ref):
    tn = o_ref.shape[-1]
    for r in range(2):
        s = _silu(c_ref[r])
        cols = [jnp.sum(w_ref[0, :, j * LANES:(j + 1) * LANES] * s, axis=0, keepdims=True)
                for j in range(tn // LANES)]
        o_ref[0, r:r + 1, :] = jnp.concatenate(cols, axis=1) + b_ref[0]


def _ada_mods(c2, w_ada, b_ada):
    tn = 1024
    n = N_MOD * D_MODEL
    cb = jnp.broadcast_to(c2[:, :, None], (2, D_MODEL, LANES))
    return pl.pallas_call(
        _ada_kernel,
        grid=(DEPTH, n // tn),
        in_specs=[pl.BlockSpec((2, D_MODEL, LANES), lambda l, j: (0, 0, 0)),
                  pl.BlockSpec((1, D_MODEL, tn), lambda l, j: (l, 0, j)),
                  pl.BlockSpec((1, 1, tn), lambda l, j: (l, 0, j))],
        out_specs=pl.BlockSpec((1, 2, tn), lambda l, j: (l, 0, j)),
        out_shape=jax.ShapeDtypeStruct((DEPTH, 2, n), F32),
        compiler_params=_params(("parallel", "parallel"), 32),
        name="ada_mods",
    )(cb, w_ada, b_ada.reshape(DEPTH, 1, n))


def _norm_mm_kernel(x_ref, vec_ref, w_ref, o_ref, h_ref):
    @pl.when(pl.program_id(1) == 0)
    def _():
        h = _rms(x_ref[...]) * vec_ref[0:1] * (1.0 + vec_ref[2:3]) + vec_ref[1:2]
        h_ref[...] = h.astype(BF16)

    o_ref[...] = jnp.dot(h_ref[...], w_ref[0].astype(BF16), preferred_element_type=F32).astype(o_ref.dtype)


def _norm_mm(x, vec, w, n_cols, tm, tn):
    m = x.shape[0]
    return pl.pallas_call(
        _norm_mm_kernel,
        grid=(m // tm, n_cols // tn),
        in_specs=[pl.BlockSpec((tm, D_MODEL), lambda i, j: (i, 0)),
                  pl.BlockSpec((8, D_MODEL), lambda i, j: (0, 0)),
                  pl.BlockSpec((1, D_MODEL, tn), lambda i, j: (0, 0, j))],
        out_specs=pl.BlockSpec((tm, tn), lambda i, j: (i, j)),
        out_shape=jax.ShapeDtypeStruct((m, n_cols), BF16),
        scratch_shapes=[pltpu.VMEM((tm, D_MODEL), BF16)],
        compiler_params=_params(("parallel", "arbitrary"), 52),
        name="norm_mm",
    )(x, vec, w)


def _seg_cumsum(g, row_in_chunk):
    p = g
    s = 1
    while s < CHUNK:
        p = p + jnp.where(row_in_chunk >= s, pltpu.roll(p, s, axis=0), 0.0)
        s *= 2
    return p


def _side_cast_specs(layered_weights, n_steps, step_of):
    in_specs, out_specs, out_shapes = [], [], []
    for w, layer in layered_weights:
        rows, n = w.shape[1] // n_steps, w.shape[2]
        in_specs.append(pl.BlockSpec((1, rows, n), lambda *g, layer=layer: (layer, step_of(*g), 0)))
        out_specs.append(pl.BlockSpec((rows, n), lambda *g: (step_of(*g), 0)))
        out_shapes.append(jax.ShapeDtypeStruct((w.shape[1], n), BF16))
    return in_specs, out_specs, out_shapes


def _side_cast(side_in, side_out):
    for wi, wo in zip(side_in, side_out):
        wo[...] = wi[0].astype(BF16)


def _gla_state_kernel(*refs, nchunk, n_side):
    f_ref, i_ref, lb_ref, s0_ref = refs[:4]
    side_in = refs[4:4 + n_side]
    senter_ref, sfin_ref = refs[4 + n_side:6 + n_side]
    side_out = refs[6 + n_side:6 + 2 * n_side]
    s_acc, p_buf, g_buf, k_buf = refs[6 + 2 * n_side:]
    _side_cast(side_in, side_out)
    d = pl.program_id(0)

    @pl.when(pl.program_id(2) == 0)
    def _():
        s_acc[...] = s0_ref[0, 0]

    rows = f_ref.shape[0]
    lb = lb_ref[0, 0]
    f = lb + (1.0 - lb) * jax.nn.sigmoid(f_ref[...].astype(F32))
    g = jnp.log(f)
    ric = lax.broadcasted_iota(jnp.int32, (rows, LANES), 0) & (CHUNK - 1)
    p_buf[...] = _seg_cumsum(g, ric)
    g_buf[...] = g
    k_buf[...] = 1.0 - f

    s = s_acc[...]
    for j in range(nchunk):
        c = jnp.where(d == 0, j, nchunk - 1 - j)
        base = pl.multiple_of(c * CHUNK, CHUNK)
        sl = pl.ds(base, CHUNK)
        p = p_buf[sl, :]
        tot = p_buf[pl.ds(base + CHUNK - 1, 1), :]
        ex = jnp.where(d == 0, tot - p, p - g_buf[sl, :])
        kt = (k_buf[sl, :] * jnp.exp(ex)).astype(BF16)
        u = lax.dot_general(i_ref[sl, :], kt, (((0,), (0,)), ((), ())), preferred_element_type=F32)
        senter_ref[0, 0, c] = s.astype(BF16)
        s = s * jnp.exp(tot) + u
    s_acc[...] = s
    sfin_ref[0, 0] = s


def _gla_states(z, lb4, s0, rows_per_block, side=()):
    m = z.shape[0]
    nb = m // rows_per_block
    nchunk = rows_per_block // CHUNK

    def rb(d, h, b):
        return b + d * (nb - 1 - 2 * b)

    side_in, side_out, side_shapes = _side_cast_specs(
        side, 2 * HG_HEADS * nb, lambda d, h, b: (d * HG_HEADS + h) * nb + b)
    return pl.pallas_call(
        functools.partial(_gla_state_kernel, nchunk=nchunk, n_side=len(side)),
        grid=(2, HG_HEADS, nb),
        in_specs=[pl.BlockSpec((rows_per_block, HG_DK), lambda d, h, b: (rb(d, h, b), d * HG_HEADS + h)),
                  pl.BlockSpec((rows_per_block, HG_DK), lambda d, h, b: (rb(d, h, b), 2 * HG_HEADS + h)),
                  pl.BlockSpec((1, 1, 1, HG_DK), lambda d, h, b: (d, h, 0, 0)),
                  pl.BlockSpec((1, 1, HG_DK, HG_DK), lambda d, h, b: (d, h, 0, 0))] + side_in,
        out_specs=[pl.BlockSpec((1, 1, nchunk, HG_DK, HG_DK), lambda d, h, b: (d, h, rb(d, h, b), 0, 0)),
                   pl.BlockSpec((1, 1, HG_DK, HG_DK), lambda d, h, b: (d, h, 0, 0))] + side_out,
        out_shape=[jax.ShapeDtypeStruct((2, HG_HEADS, m // CHUNK, HG_DK, HG_DK), BF16),
                   jax.ShapeDtypeStruct((2, HG_HEADS, HG_DK, HG_DK), F32)] + side_shapes,
        scratch_shapes=[pltpu.VMEM((HG_DK, HG_DK), F32),
                        pltpu.VMEM((rows_per_block, HG_DK), F32),
                        pltpu.VMEM((rows_per_block, HG_DK), F32),
                        pltpu.VMEM((rows_per_block, HG_DK), F32)],
        compiler_params=_params(("parallel", "parallel", "arbitrary"), 32),
        name="gla_states",
    )(z, z, lb4, s0, *[w for w, _ in side])


def _gla_out_kernel(*refs, nchunk, n_side):
    ff_ref, fb_ref, i_ref, q_ref, gt_ref, lb_ref, gain_ref, sf_ref, sb_ref = refs[:9]
    side_in = refs[9:9 + n_side]
    o_ref = refs[9 + n_side]
    side_out = refs[10 + n_side:10 + 2 * n_side]
    pf_buf, pb_buf, gb_buf, kf_buf, kb_buf, q_buf = refs[10 + 2 * n_side:]
    _side_cast(side_in, side_out)
    rows = ff_ref.shape[0]
    ric = lax.broadcasted_iota(jnp.int32, (rows, LANES), 0) & (CHUNK - 1)
    lbf = lb_ref[0, 0]
    lbb = lb_ref[1, 0]
    ff = lbf + (1.0 - lbf) * jax.nn.sigmoid(ff_ref[...].astype(F32))
    pf_buf[...] = _seg_cumsum(jnp.log(ff), ric)
    kf_buf[...] = 1.0 - ff
    fb = lbb + (1.0 - lbb) * jax.nn.sigmoid(fb_ref[...].astype(F32))
    gb = jnp.log(fb)
    pb_buf[...] = _seg_cumsum(gb, ric)
    gb_buf[...] = gb
    kb_buf[...] = 1.0 - fb
    q_buf[...] = _silu(q_ref[...].astype(F32)) * (HG_DK ** -0.5)

    blk = lax.broadcasted_iota(jnp.int32, (CHUNK, LANES), 0) // SUB
    r2 = lax.broadcasted_iota(jnp.int32, (CHUNK, CHUNK), 0)
    c2 = lax.broadcasted_iota(jnp.int32, (CHUNK, CHUNK), 1)
    nt = (((1,), (1,)), ((), ()))
    gain = gain_ref[0]

    for c in range(nchunk):
        base = c * CHUNK
        sl = pl.ds(base, CHUNK)
        pf = pf_buf[sl, :]
        pb = pb_buf[sl, :]
        gbc = gb_buf[sl, :]
        kf = kf_buf[sl, :]
        kb = kb_buf[sl, :]
        q = q_buf[sl, :]
        totb = pb_buf[pl.ds(base + CHUNK - 1, 1), :]
        suf = totb - pb + gbc
        qf_seg, kf_seg, qb_seg, kb_seg = [], [], [], []
        for i in range(CHUNK // SUB):
            mid = pf_buf[pl.ds(base + SUB * i + SUB // 2 - 1, 1), :]
            qf_seg.append(jnp.where(blk == i, q * jnp.exp(jnp.minimum(pf - mid, EXP_CLAMP)), 0.0))
            kf_seg.append(jnp.where(blk <= i, kf * jnp.exp(jnp.minimum(mid - pf, EXP_CLAMP)), 0.0))
            rmid = pl.ds(base + SUB * i + SUB // 2, 1)
            nid = totb - pb_buf[rmid, :] + gb_buf[rmid, :]
            qb_seg.append(jnp.where(blk == i, q * jnp.exp(jnp.minimum(suf - nid, EXP_CLAMP)), 0.0))
            kb_seg.append(jnp.where(blk >= i, kb * jnp.exp(jnp.minimum(nid - suf, EXP_CLAMP)), 0.0))
        qf_big = jnp.concatenate(qf_seg, axis=1).astype(BF16)
        kf_big = jnp.concatenate(kf_seg, axis=1).astype(BF16)
        qb_big = jnp.concatenate(qb_seg, axis=1).astype(BF16)
        kb_big = jnp.concatenate(kb_seg, axis=1).astype(BF16)
        sc_f = lax.dot_general(qf_big, kf_big, nt, preferred_element_type=F32)
        sc_b = lax.dot_general(qb_big, kb_big, nt, preferred_element_type=F32)
        a = (jnp.where(c2 <= r2, sc_f, 0.0) + jnp.where(c2 >= r2, sc_b, 0.0)).astype(BF16)
        o = jnp.dot(a, i_ref[sl, :], preferred_element_type=F32)
        qcat = jnp.concatenate([q * jnp.exp(pf), q * jnp.exp(suf)], axis=1).astype(BF16)
        scat = jnp.concatenate([sf_ref[0, 0, c], sb_ref[0, 0, c]], axis=1)
        o = o + lax.dot_general(qcat, scat, nt, preferred_element_type=F32)
        o = _rms(o) * gain
        o_ref[sl, :] = (o * _silu(gt_ref[sl, :].astype(F32))).astype(BF16)


def _gla_out(z, lb4, hg_gain, senter, rows_per_block, side=()):
    m = z.shape[0]
    nb = m // rows_per_block
    nchunk = rows_per_block // CHUNK

    def col(k):
        return pl.BlockSpec((rows_per_block, HG_DK), lambda h, b, k=k: (b, k * HG_HEADS + h))

    def st(d):
        return pl.BlockSpec((1, 1, nchunk, HG_DK, HG_DK), lambda h, b, d=d: (d, h, b, 0, 0))

    side_in, side_out, side_shapes = _side_cast_specs(side, HG_HEADS * nb, lambda h, b: h * nb + b)
    return pl.pallas_call(
        functools.partial(_gla_out_kernel, nchunk=nchunk, n_side=len(side)),
        grid=(HG_HEADS, nb),
        in_specs=[col(0), col(1), col(2), col(3), col(4),
                  pl.BlockSpec((2, 1, 1, HG_DK), lambda h, b: (0, h, 0, 0)),
                  pl.BlockSpec((1, 1, HG_DK), lambda h, b: (h, 0, 0)),
                  st(0), st(1)] + side_in,
        out_specs=[pl.BlockSpec((rows_per_block, HG_DK), lambda h, b: (b, h))] + side_out,
        out_shape=[jax.ShapeDtypeStruct((m, HG_WIDTH), BF16)] + side_shapes,
        scratch_shapes=[pltpu.VMEM((rows_per_block, HG_DK), F32) for _ in range(6)],
        compiler_params=_params(("parallel", "parallel"), 32),
        name="gla_out",
    )(z, z, z, z, z, lb4, hg_gain.reshape(HG_HEADS, 1, HG_DK), senter, senter, *[w for w, _ in side])


def _spatial_kernel(u_ref, v_ref, w_ref, bias_ref, lng_ref, lnb_ref, o_ref, y_buf):
    v = _gelu_tanh(v_ref[...].astype(F32))
    xc = v - jnp.mean(v, axis=-1, keepdims=True)
    y = xc * lax.rsqrt(jnp.mean(xc * xc, axis=-1, keepdims=True) + EPS) * lng_ref[...] + lnb_ref[...]
    y_buf[...] = y.astype(BF16)
    for n in range(u_ref.shape[0] // SG_CHUNK):
        rs = slice(n * SG_CHUNK, (n + 1) * SG_CHUNK)
        for g in range(SG_GROUPS):
            cs = slice(g * SG_CH, (g + 1) * SG_CH)
            mixed = jnp.dot(w_ref[g], y_buf[rs, cs], preferred_element_type=F32) + bias_ref[:, cs]
            o_ref[rs, cs] = (_gelu_tanh(u_ref[rs, cs].astype(F32)) * mixed).astype(BF16)


def _spatial(z, sg_w, bias_full, ln_g, ln_b, rows_per_block):
    m = z.shape[0]
    return pl.pallas_call(
        _spatial_kernel,
        grid=(m // rows_per_block,),
        in_specs=[pl.BlockSpec((rows_per_block, SG_WIDTH), lambda b: (b, 5)),
                  pl.BlockSpec((rows_per_block, SG_WIDTH), lambda b: (b, 6)),
                  pl.BlockSpec((SG_GROUPS, SG_CHUNK, SG_CHUNK), lambda b: (0, 0, 0)),
                  pl.BlockSpec((SG_CHUNK, SG_WIDTH), lambda b: (0, 0)),
                  pl.BlockSpec((1, SG_WIDTH), lambda b: (0, 0)),
                  pl.BlockSpec((1, SG_WIDTH), lambda b: (0, 0))],
        out_specs=pl.BlockSpec((rows_per_block, SG_WIDTH), lambda b: (b, 0)),
        out_shape=jax.ShapeDtypeStruct((m, SG_WIDTH), BF16),
        scratch_shapes=[pltpu.VMEM((rows_per_block, SG_WIDTH), BF16)],
        compiler_params=_params(("parallel",), 32),
        name="spatial_gate",
    )(z, z, sg_w, bias_full, ln_g, ln_b)


def _residual_epilogue(mix, x, vec_ref, xo_ref, ho_ref):
    xn = x + vec_ref[1:2] * (_rms(mix) * vec_ref[0:1])
    xo_ref[...] = xn
    if ho_ref is not None:
        ho_ref[...] = (_rms(xn) * vec_ref[2:3] * (1.0 + vec_ref[4:5]) + vec_ref[3:4]).astype(BF16)


def _mm_res_kernel(*refs, n_a, emit_h):
    a_refs, w_refs = refs[:n_a], refs[n_a:2 * n_a]
    x_ref, vec_ref, xo_ref = refs[2 * n_a], refs[2 * n_a + 1], refs[2 * n_a + 2]
    ho_ref = refs[2 * n_a + 3] if emit_h else None
    mix = jnp.dot(a_refs[0][...], w_refs[0][...], preferred_element_type=F32)
    for a_ref, w_ref in zip(a_refs[1:], w_refs[1:]):
        mix = mix + jnp.dot(a_ref[...], w_ref[...], preferred_element_type=F32)
    _residual_epilogue(mix, x_ref[...], vec_ref, xo_ref, ho_ref)


def _mm_res(a_list, w, x, vec, tm, emit_h=True):
    m = x.shape[0]
    n_a = len(a_list)
    w_list = [w] * n_a
    offs = np.cumsum([0] + [a.shape[1] for a in a_list])
    in_specs = ([pl.BlockSpec((tm, a.shape[1]), lambda i: (i, 0)) for a in a_list]
                + [pl.BlockSpec((a.shape[1], D_MODEL), lambda i, k=int(o) // a.shape[1]: (k, 0))
                   for a, o in zip(a_list, offs)]
                + [pl.BlockSpec((tm, D_MODEL), lambda i: (i, 0)),
                   pl.BlockSpec((8, D_MODEL), lambda i: (0, 0))])
    out_specs = [pl.BlockSpec((tm, D_MODEL), lambda i: (i, 0))]
    out_shape = [jax.ShapeDtypeStruct((m, D_MODEL), F32)]
    if emit_h:
        out_specs.append(pl.BlockSpec((tm, D_MODEL), lambda i: (i, 0)))
        out_shape.append(jax.ShapeDtypeStruct((m, D_MODEL), BF16))
    return pl.pallas_call(
        functools.partial(_mm_res_kernel, n_a=n_a, emit_h=emit_h),
        grid=(m // tm,),
        in_specs=in_specs, out_specs=out_specs, out_shape=out_shape,
        compiler_params=_params(("parallel",), 56),
        name="mm_residual",
    )(*a_list, *w_list, x, vec)


def _mlp_kernel(*refs, emit_h):
    h_ref, w1_ref, w2_ref, x_ref, vec_ref, xo_ref = refs[:6]
    ho_ref = refs[6] if emit_h else None
    acc_ref = refs[-1]
    j = pl.program_id(1)
    a = jnp.dot(h_ref[...], w1_ref[...], preferred_element_type=F32)
    a = jnp.square(jnp.maximum(a, 0.0)).astype(BF16)
    p = jnp.dot(a, w2_ref[...], preferred_element_type=F32)

    @pl.when(j == 0)
    def _():
        acc_ref[...] = p

    @pl.when(j > 0)
    def _():
        acc_ref[...] += p

    @pl.when(j == pl.num_programs(1) - 1)
    def _():
        _residual_epilogue(acc_ref[...], x_ref[...], vec_ref, xo_ref, ho_ref)


def _mlp(h, w1, w2, x, vec, tm, tf, emit_h):
    m = x.shape[0]
    out_specs = [pl.BlockSpec((tm, D_MODEL), lambda i, j: (i, 0))]
    out_shape = [jax.ShapeDtypeStruct((m, D_MODEL), F32)]
    if emit_h:
        out_specs.append(pl.BlockSpec((tm, D_MODEL), lambda i, j: (i, 0)))
        out_shape.append(jax.ShapeDtypeStruct((m, D_MODEL), BF16))
    return pl.pallas_call(
        functools.partial(_mlp_kernel, emit_h=emit_h),
        grid=(m // tm, D_FF // tf),
        in_specs=[pl.BlockSpec((tm, D_MODEL), lambda i, j: (i, 0)),
                  pl.BlockSpec((D_MODEL, tf), lambda i, j: (0, j)),
                  pl.BlockSpec((tf, D_MODEL), lambda i, j: (j, 0)),
                  pl.BlockSpec((tm, D_MODEL), lambda i, j: (i, 0)),
                  pl.BlockSpec((8, D_MODEL), lambda i, j: (0, 0))],
        out_specs=out_specs, out_shape=out_shape,
        scratch_shapes=[pltpu.VMEM((tm, D_MODEL), F32)],
        compiler_params=_params(("parallel", "arbitrary"), 52),
        name="mlp",
    )(h, w1, w2, x, vec)


def _dft_constants():
    def cs(n):
        k = np.arange(n)
        ang = 2.0 * np.pi * ((k[:, None] * k[None, :]) % n) / n
        return np.cos(ang) / np.sqrt(n), np.sin(ang) / np.sqrt(n)

    cc, sc = cs(FT_CH)
    c1, s1 = cs(FFT_L1)
    m1 = np.concatenate([c1, -s1], axis=0)
    c2, s2 = cs(FFT_L2)
    m2 = np.block([[c2, s2], [-s2, c2]])
    u1 = np.arange(FFT_L1)[:, None]
    t2 = np.arange(FFT_L2)[None, :]
    ang = 2.0 * np.pi * ((u1 * t2) % SEQ) / SEQ
    tw_c = np.cos(ang).reshape(SEQ, 1)
    tw_s = np.sin(ang).reshape(SEQ, 1)
    f = lambda a: jnp.asarray(a, F32)
    return f(cc), f(sc), f(m1), f(m2), f(tw_c), f(tw_s)


def _swap_row_factors(a, n_outer, n_inner):
    return a.reshape(n_outer, n_inner, a.shape[1]).transpose(1, 0, 2).reshape(a.shape)


def _pos_dft1_kernel(x_ref, m_ref, zr_ref, zi_ref):
    for k in range(x_ref.shape[0] // FFT_L1):
        sl = slice(k * FFT_L1, (k + 1) * FFT_L1)
        z = jnp.dot(m_ref[...], x_ref[sl, :], preferred_element_type=F32)
        zr_ref[sl, :] = z[:FFT_L1].astype(BF16)
        zi_ref[sl, :] = z[FFT_L1:].astype(BF16)


def _pos_dft1(xt, m1, slabs):
    rows = slabs * FFT_L1
    blk = pl.BlockSpec((rows, D_MODEL), lambda j: (j, 0))
    return pl.pallas_call(
        _pos_dft1_kernel,
        grid=(SEQ // rows,),
        in_specs=[blk, pl.BlockSpec((2 * FFT_L1, FFT_L1), lambda j: (0, 0))],
        out_specs=[blk, blk],
        out_shape=[jax.ShapeDtypeStruct((SEQ, D_MODEL), BF16)] * 2,
        compiler_params=_params(("parallel",), 32),
        name="pos_dft1",
    )(xt, m1)


def _pos_dft2_kernel(zr_ref, zi_ref, c_ref, s_ref, m_ref, wc_ref, ws_ref, y_ref, yr_buf, yi_buf):
    for k in range(zr_ref.shape[0] // FFT_L2):
        sl = slice(k * FFT_L2, (k + 1) * FFT_L2)
        zr = zr_ref[sl, :].astype(F32)
        zi = zi_ref[sl, :].astype(F32)
        c = c_ref[sl, :]
        s = s_ref[sl, :]
        t = jnp.concatenate([(zr * c + zi * s).astype(BF16), (zi * c - zr * s).astype(BF16)], axis=0)
        y = jnp.dot(m_ref[...], t, preferred_element_type=F32)
        yr_buf[sl, :] = y[:FFT_L2].astype(BF16)
        yi_buf[sl, :] = y[FFT_L2:].astype(BF16)
    for g in range(FT_GROUPS):
        cs = slice(g * FT_CH, (g + 1) * FT_CH)
        y_ref[:, cs] = (jnp.dot(yr_buf[:, cs], wc_ref[...], preferred_element_type=F32)
                        + jnp.dot(yi_buf[:, cs], ws_ref[...], preferred_element_type=F32)).astype(BF16)


def _pos_dft2(zr, zi, tw_c, tw_s, m2, wc, ws, slabs):
    rows = slabs * FFT_L2
    blk = pl.BlockSpec((rows, D_MODEL), lambda j: (j, 0))
    tw = pl.BlockSpec((rows, 1), lambda j: (j, 0))
    const = lambda a: pl.BlockSpec(a.shape, lambda j: (0, 0))
    return pl.pallas_call(
        _pos_dft2_kernel,
        grid=(SEQ // rows,),
        in_specs=[blk, blk, tw, tw, const(m2), const(wc), const(ws)],
        out_specs=blk,
        out_shape=jax.ShapeDtypeStruct((SEQ, D_MODEL), BF16),
        scratch_shapes=[pltpu.VMEM((rows, D_MODEL), BF16), pltpu.VMEM((rows, D_MODEL), BF16)],
        compiler_params=_params(("parallel",), 40),
        name="pos_dft2",
    )(zr, zi, tw_c, tw_s, m2, wc, ws)


def _fourier_mix(h):
    cc, sc, m1, m2, tw_c, tw_s = _dft_constants()
    ht = _swap_row_factors(h, FFT_L1, FFT_L2)
    zr, zi = _pos_dft1(ht, m1.astype(BF16), 8)
    zr = _swap_row_factors(zr, FFT_L2, FFT_L1)
    zi = _swap_row_factors(zi, FFT_L2, FFT_L1)
    yp = _pos_dft2(zr, zi, tw_c, tw_s, m2.astype(BF16), cc.astype(BF16), sc.astype(BF16), 4)
    return _swap_row_factors(yp, FFT_L1, FFT_L2)


def _pack_rows(*rows):
    rows = [r.reshape(1, D_MODEL).astype(F32) for r in rows]
    rows += [jnp.zeros((1, D_MODEL), F32)] * (8 - len(rows))
    return jnp.concatenate(rows, axis=0)


def kernel(x, c, ctx, c_ctx, w_ada, b_ada, norm_gain, w_in, w_out, lb_raw, hg_norm_gain,
           sg_w, sg_b, sg_ln_gain, sg_ln_bias, w_fourier, w_mlp_in, w_mlp_out):
    assert x.shape == (1, SEQ, D_MODEL) and ctx.shape == (1, CTX_LEN, D_MODEL)
    xs = x.reshape(SEQ, D_MODEL)
    cs = ctx.reshape(CTX_LEN, D_MODEL)

    mods = _ada_mods(jnp.concatenate([c.reshape(1, D_MODEL), c_ctx.reshape(1, D_MODEL)], axis=0), w_ada, b_ada)
    mod = lambda l, r, j: mods[l, r, j * D_MODEL:(j + 1) * D_MODEL]

    lb = jnp.cumsum(jax.nn.softmax(lb_raw.astype(F32), axis=1), axis=1)[:, 0]
    lb4 = lb.reshape(2, HG_HEADS, 1, HG_DK)

    vec_x = _pack_rows(norm_gain[0, 0], mod(0, 0, 0), mod(0, 0, 1))
    vec_c = _pack_rows(norm_gain[0, 0], mod(0, 1, 0), mod(0, 1, 1))
    z = _norm_mm(xs, vec_x, w_in, IN_WIDTH, 1024, 1024)
    z_ctx = _norm_mm(cs, vec_c, w_in, 3 * HG_WIDTH, CTX_LEN, 1024)
    zero_state = jnp.zeros((2, HG_HEADS, HG_DK, HG_DK), F32)
    _, s_ctx = _gla_states(z_ctx, lb4, zero_state, CTX_LEN)
    senter, _, w_out0, w_fou, w1_0, w2_0 = _gla_states(
        z, lb4, s_ctx, GLA_ROWS, side=[(w_out, 0), (w_fourier, 0), (w_mlp_in, 0), (w_mlp_out, 0)])
    o, w1_1, w2_1 = _gla_out(z, lb4, hg_norm_gain[0], senter, GLA_ROWS, side=[(w_mlp_in, 1), (w_mlp_out, 1)])
    bias_full = jnp.repeat(sg_b[0].T.astype(F32), SG_CH, axis=1)
    s = _spatial(z, sg_w[0].astype(BF16), bias_full, sg_ln_gain[0].reshape(1, SG_WIDTH),
                 sg_ln_bias[0].reshape(1, SG_WIDTH), 512)
    vec = _pack_rows(norm_gain[0, 1], mod(0, 0, 2), norm_gain[0, 2], mod(0, 0, 3), mod(0, 0, 4))
    x1, h = _mm_res([o, s], w_out0, xs, vec, 512)
    vec = _pack_rows(norm_gain[0, 3], mod(0, 0, 5), norm_gain[1, 0], mod(1, 0, 0), mod(1, 0, 1))
    x2, h = _mlp(h, w1_0, w2_0, x1, vec, 512, 1024, True)

    y = _fourier_mix(h)
    vec = _pack_rows(norm_gain[1, 1], mod(1, 0, 2), norm_gain[1, 2], mod(1, 0, 3), mod(1, 0, 4))
    x3, h = _mm_res([y], w_fou, x2, vec, 512)
    vec = _pack_rows(norm_gain[1, 3], mod(1, 0, 5))
    (x4,) = _mlp(h, w1_1, w2_1, x3, vec, 512, 1024, False)
    return x4.reshape(1, SEQ, D_MODEL)
```

```python
import functools

import numpy as np
import jax
import jax.numpy as jnp
from jax import lax
from jax.experimental import pallas as pl
from jax.experimental.pallas import tpu as pltpu

D_MODEL = 2048
SEQ = 8192
DEPTH = 2
CTX_LEN = 256
CHUNK = 64
SUB = 16
GLA_ROWS = 1024
HG_HEADS = 8
HG_DK = 128
HG_WIDTH = HG_HEADS * HG_DK
SG_GROUPS = 8
SG_CH = 128
SG_WIDTH = SG_GROUPS * SG_CH
SG_CHUNK = 128
FT_GROUPS = 4
FT_CH = D_MODEL // FT_GROUPS
FFT_L1 = 64
FFT_L2 = 128
IN_WIDTH = 5 * HG_WIDTH + 2 * SG_WIDTH
D_FF = 4 * D_MODEL
N_MOD = 6
EPS = 1e-6
EXP_CLAMP = 115.0
LOG2E = 1.0 / float(np.log(2.0))
LANES = 128

F32 = jnp.float32
BF16 = jnp.bfloat16
MIB = 1024 * 1024


def _params(semantics, vmem_mib):
    return pltpu.CompilerParams(dimension_semantics=semantics, vmem_limit_bytes=vmem_mib * MIB)


def _rms(x):
    return x * lax.rsqrt(jnp.mean(x * x, axis=-1, keepdims=True) + EPS)


def _silu(x):
    return x * jax.nn.sigmoid(x)


def _gelu_tanh(x):
    cdf = 0.5 * (1.0 + jnp.tanh(float(np.sqrt(2.0 / np.pi)) * (x + 0.044715 * (x * x * x))))
    return x * cdf


def _ada_kernel(c_ref, w_ref, b_ref, o_ref):
    tn = o_ref.shape[-1]
    for r in range(2):
        s = _silu(c_ref[r])
        cols = [jnp.sum(w_ref[0, :, j * LANES:(j + 1) * LANES] * s, axis=0, keepdims=True)
                for j in range(tn // LANES)]
        o_ref[0, r:r + 1, :] = jnp.concatenate(cols, axis=1) + b_ref[0]


def _ada_mods(c2, w_ada, b_ada):
    tn = 1024
    n = N_MOD * D_MODEL
    cb = jnp.broadcast_to(c2[:, :, None], (2, D_MODEL, LANES))
    return pl.pallas_call(
        _ada_kernel,
        grid=(DEPTH, n // tn),
        in_specs=[pl.BlockSpec((2, D_MODEL, LANES), lambda l, j: (0, 0, 0)),
                  pl.BlockSpec((1, D_MODEL, tn), lambda l, j: (l, 0, j)),
                  pl.BlockSpec((1, 1, tn), lambda l, j: (l, 0, j))],
        out_specs=pl.BlockSpec((1, 2, tn), lambda l, j: (l, 0, j)),
        out_shape=jax.ShapeDtypeStruct((DEPTH, 2, n), F32),
        compiler_params=_params(("parallel", "parallel"), 32),
        name="ada_mods",
    )(cb, w_ada, b_ada.reshape(DEPTH, 1, n))


def _norm_mm_kernel(x_ref, vec_ref, w_ref, o_ref, h_ref):
    @pl.when(pl.program_id(1) == 0)
    def _():
        h_ref[...] = (_rms(x_ref[...]) * vec_ref[0:1] + vec_ref[1:2]).astype(BF16)

    o_ref[...] = jnp.dot(h_ref[...], w_ref[0].astype(BF16), preferred_element_type=F32).astype(o_ref.dtype)


def _norm_mm(x, vec, w, n_cols, tm, tn):
    m = x.shape[0]
    return pl.pallas_call(
        _norm_mm_kernel,
        grid=(m // tm, n_cols // tn),
        in_specs=[pl.BlockSpec((tm, D_MODEL), lambda i, j: (i, 0)),
                  pl.BlockSpec((8, D_MODEL), lambda i, j: (0, 0)),
                  pl.BlockSpec((1, D_MODEL, tn), lambda i, j: (0, 0, j))],
        out_specs=pl.BlockSpec((tm, tn), lambda i, j: (i, j)),
        out_shape=jax.ShapeDtypeStruct((m, n_cols), BF16),
        scratch_shapes=[pltpu.VMEM((tm, D_MODEL), BF16)],
        compiler_params=_params(("parallel", "arbitrary"), 52),
        name="norm_mm",
    )(x, vec, w)


def _seg_cumsum(g, row_in_chunk):
    p = g
    s = 1
    while s < CHUNK:
        p = p + jnp.where(row_in_chunk >= s, pltpu.roll(p, s, axis=0), 0.0)
        s *= 2
    return p


def _side_cast_specs(layered_weights, n_steps, step_of):
    in_specs, out_specs, out_shapes = [], [], []
    for w, layer in layered_weights:
        rows, n = w.shape[1] // n_steps, w.shape[2]
        in_specs.append(pl.BlockSpec((1, rows, n), lambda *g, layer=layer: (layer, step_of(*g), 0)))
        out_specs.append(pl.BlockSpec((rows, n), lambda *g: (step_of(*g), 0)))
        out_shapes.append(jax.ShapeDtypeStruct((w.shape[1], n), BF16))
    return in_specs, out_specs, out_shapes


def _side_cast(side_in, side_out):
    for wi, wo in zip(side_in, side_out):
        wo[...] = wi[0].astype(BF16)


def _gla_state_kernel(*refs, nchunk, n_side):
    f_ref, i_ref, lb_ref, s0_ref = refs[:4]
    side_in = refs[4:4 + n_side]
    senter_ref, sfin_ref = refs[4 + n_side:6 + n_side]
    side_out = refs[6 + n_side:6 + 2 * n_side]
    s_acc, p_buf, g_buf, k_buf = refs[6 + 2 * n_side:]
    _side_cast(side_in, side_out)
    d = pl.program_id(0)

    @pl.when(pl.program_id(2) == 0)
    def _():
        s_acc[...] = s0_ref[0, 0]

    rows = f_ref.shape[0]
    lb = lb_ref[0, 0]
    f = lb + (1.0 - lb) * jax.nn.sigmoid(f_ref[...].astype(F32))
    g = jnp.log(f) * LOG2E
    ric = lax.broadcasted_iota(jnp.int32, (rows, LANES), 0) & (CHUNK - 1)
    p_buf[...] = _seg_cumsum(g, ric)
    g_buf[...] = g
    k_buf[...] = 1.0 - f

    s = s_acc[...]
    for j in range(nchunk):
        c = jnp.where(d == 0, j, nchunk - 1 - j)
        base = pl.multiple_of(c * CHUNK, CHUNK)
        sl = pl.ds(base, CHUNK)
        p = p_buf[sl, :]
        tot = p_buf[pl.ds(base + CHUNK - 1, 1), :]
        ex = jnp.where(d == 0, tot - p, p - g_buf[sl, :])
        kt = (k_buf[sl, :] * jnp.exp2(ex)).astype(BF16)
        u = lax.dot_general(i_ref[sl, :], kt, (((0,), (0,)), ((), ())), preferred_element_type=F32)
        senter_ref[0, 0, c] = s.astype(BF16)
        s = s * jnp.exp2(tot) + u
    s_acc[...] = s
    sfin_ref[0, 0] = s


def _gla_states(z, lb4, s0, rows_per_block, side=()):
    m = z.shape[0]
    nb = m // rows_per_block
    nchunk = rows_per_block // CHUNK

    def rb(d, h, b):
        return b + d * (nb - 1 - 2 * b)

    side_in, side_out, side_shapes = _side_cast_specs(
        side, 2 * HG_HEADS * nb, lambda d, h, b: (d * HG_HEADS + h) * nb + b)
    return pl.pallas_call(
        functools.partial(_gla_state_kernel, nchunk=nchunk, n_side=len(side)),
        grid=(2, HG_HEADS, nb),
        in_specs=[pl.BlockSpec((rows_per_block, HG_DK), lambda d, h, b: (rb(d, h, b), d * HG_HEADS + h)),
                  pl.BlockSpec((rows_per_block, HG_DK), lambda d, h, b: (rb(d, h, b), 2 * HG_HEADS + h)),
                  pl.BlockSpec((1, 1, 1, HG_DK), lambda d, h, b: (d, h, 0, 0)),
                  pl.BlockSpec((1, 1, HG_DK, HG_DK), lambda d, h, b: (d, h, 0, 0))] + side_in,
        out_specs=[pl.BlockSpec((1, 1, nchunk, HG_DK, HG_DK), lambda d, h, b: (d, h, rb(d, h, b), 0, 0)),
                   pl.BlockSpec((1, 1, HG_DK, HG_DK), lambda d, h, b: (d, h, 0, 0))] + side_out,
        out_shape=[jax.ShapeDtypeStruct((2, HG_HEADS, m // CHUNK, HG_DK, HG_DK), BF16),
                   jax.ShapeDtypeStruct((2, HG_HEADS, HG_DK, HG_DK), F32)] + side_shapes,
        scratch_shapes=[pltpu.VMEM((HG_DK, HG_DK), F32),
                        pltpu.VMEM((rows_per_block, HG_DK), F32),
                        pltpu.VMEM((rows_per_block, HG_DK), F32),
                        pltpu.VMEM((rows_per_block, HG_DK), F32)],
        compiler_params=_params(("parallel", "parallel", "arbitrary"), 32),
        name="gla_states",
    )(z, z, lb4, s0, *[w for w, _ in side])


def _gla_out_kernel(*refs, nchunk, n_side):
    ff_ref, fb_ref, i_ref, q_ref, gt_ref, lb_ref, gain_ref, sf_ref, sb_ref = refs[:9]
    side_in = refs[9:9 + n_side]
    o_ref = refs[9 + n_side]
    side_out = refs[10 + n_side:10 + 2 * n_side]
    pf_buf, pb_buf, gb_buf, kf_buf, kb_buf, q_buf = refs[10 + 2 * n_side:]
    _side_cast(side_in, side_out)
    rows = ff_ref.shape[0]
    ric = lax.broadcasted_iota(jnp.int32, (rows, LANES), 0) & (CHUNK - 1)
    lbf = lb_ref[0, 0]
    lbb = lb_ref[1, 0]
    ff = lbf + (1.0 - lbf) * jax.nn.sigmoid(ff_ref[...].astype(F32))
    pf_buf[...] = _seg_cumsum(jnp.log(ff) * LOG2E, ric)
    kf_buf[...] = 1.0 - ff
    fb = lbb + (1.0 - lbb) * jax.nn.sigmoid(fb_ref[...].astype(F32))
    gb = jnp.log(fb) * LOG2E
    pb_buf[...] = _seg_cumsum(gb, ric)
    gb_buf[...] = gb
    kb_buf[...] = 1.0 - fb
    q_buf[...] = _silu(q_ref[...].astype(F32)) * (HG_DK ** -0.5)

    blk = lax.broadcasted_iota(jnp.int32, (CHUNK, LANES), 0) // SUB
    r2 = lax.broadcasted_iota(jnp.int32, (CHUNK, CHUNK), 0)
    c2 = lax.broadcasted_iota(jnp.int32, (CHUNK, CHUNK), 1)
    nt = (((1,), (1,)), ((), ()))
    gain = gain_ref[0]
    nsub = CHUNK // SUB

    for c in range(nchunk):
        base = c * CHUNK
        sl = pl.ds(base, CHUNK)
        pf = pf_buf[sl, :]
        pb = pb_buf[sl, :]
        gbc = gb_buf[sl, :]
        kf = kf_buf[sl, :]
        kb = kb_buf[sl, :]
        q = q_buf[sl, :]
        totb = pb_buf[pl.ds(base + CHUNK - 1, 1), :]
        suf = totb - pb + gbc
        mids, nids = [], []
        for i in range(nsub):
            mids.append(pf_buf[pl.ds(base + SUB * i + SUB // 2 - 1, 1), :])
            rmid = pl.ds(base + SUB * i + SUB // 2, 1)
            nids.append(totb - pb_buf[rmid, :] + gb_buf[rmid, :])
        mid_rows = jnp.concatenate([jnp.broadcast_to(m, (SUB, LANES)) for m in mids], axis=0)
        nid_rows = jnp.concatenate([jnp.broadcast_to(m, (SUB, LANES)) for m in nids], axis=0)
        qf_all = q * jnp.exp2(jnp.minimum(pf - mid_rows, EXP_CLAMP))
        qb_all = q * jnp.exp2(jnp.minimum(suf - nid_rows, EXP_CLAMP))
        qf_seg, kf_seg, qb_seg, kb_seg = [], [], [], []
        for i in range(nsub):
            qf_seg.append(jnp.where(blk == i, qf_all, 0.0))
            qb_seg.append(jnp.where(blk == i, qb_all, 0.0))
            hi = SUB * (i + 1)
            top = kf[:hi] * jnp.exp2(jnp.minimum(mids[i] - pf[:hi], EXP_CLAMP))
            kf_seg.append(top if hi == CHUNK else
                          jnp.concatenate([top, jnp.zeros((CHUNK - hi, LANES), F32)], axis=0))
            lo = SUB * i
            bot = kb[lo:] * jnp.exp2(jnp.minimum(nids[i] - suf[lo:], EXP_CLAMP))
            kb_seg.append(bot if lo == 0 else
                          jnp.concatenate([jnp.zeros((lo, LANES), F32), bot], axis=0))
        qf_big = jnp.concatenate(qf_seg, axis=1).astype(BF16)
        kf_big = jnp.concatenate(kf_seg, axis=1).astype(BF16)
        qb_big = jnp.concatenate(qb_seg, axis=1).astype(BF16)
        kb_big = jnp.concatenate(kb_seg, axis=1).astype(BF16)
        sc_f = lax.dot_general(qf_big, kf_big, nt, preferred_element_type=F32)
        sc_b = lax.dot_general(qb_big, kb_big, nt, preferred_element_type=F32)
        a = (jnp.where(c2 <= r2, sc_f, 0.0) + jnp.where(c2 >= r2, sc_b, 0.0)).astype(BF16)
        o = jnp.dot(a, i_ref[sl, :], preferred_element_type=F32)
        qcat = jnp.concatenate([q * jnp.exp2(pf), q * jnp.exp2(suf)], axis=1).astype(BF16)
        scat = jnp.concatenate([sf_ref[0, 0, c], sb_ref[0, 0, c]], axis=1)
        o = o + lax.dot_general(qcat, scat, nt, preferred_element_type=F32)
        o = _rms(o) * gain
        o_ref[sl, :] = (o * _silu(gt_ref[sl, :].astype(F32))).astype(BF16)


def _gla_out(z, lb4, hg_gain, senter, rows_per_block, side=()):
    m = z.shape[0]
    nb = m // rows_per_block
    nchunk = rows_per_block // CHUNK

    def col(k):
        return pl.BlockSpec((rows_per_block, HG_DK), lambda h, b, k=k: (b, k * HG_HEADS + h))

    def st(d):
        return pl.BlockSpec((1, 1, nchunk, HG_DK, HG_DK), lambda h, b, d=d: (d, h, b, 0, 0))

    side_in, side_out, side_shapes = _side_cast_specs(side, HG_HEADS * nb, lambda h, b: h * nb + b)
    return pl.pallas_call(
        functools.partial(_gla_out_kernel, nchunk=nchunk, n_side=len(side)),
        grid=(HG_HEADS, nb),
        in_specs=[col(0), col(1), col(2), col(3), col(4),
                  pl.BlockSpec((2, 1, 1, HG_DK), lambda h, b: (0, h, 0, 0)),
                  pl.BlockSpec((1, 1, HG_DK), lambda h, b: (h, 0, 0)),
                  st(0), st(1)] + side_in,
        out_specs=[pl.BlockSpec((rows_per_block, HG_DK), lambda h, b: (b, h))] + side_out,
        out_shape=[jax.ShapeDtypeStruct((m, HG_WIDTH), BF16)] + side_shapes,
        scratch_shapes=[pltpu.VMEM((rows_per_block, HG_DK), F32) for _ in range(6)],
        compiler_params=_params(("parallel", "parallel"), 32),
        name="gla_out",
    )(z, z, z, z, z, lb4, hg_gain.reshape(HG_HEADS, 1, HG_DK), senter, senter, *[w for w, _ in side])


def _spatial_kernel(u_ref, v_ref, w_ref, bias_ref, lng_ref, lnb_ref, o_ref, y_buf):
    v = _gelu_tanh(v_ref[...].astype(F32))
    xc = v - jnp.mean(v, axis=-1, keepdims=True)
    y = xc * lax.rsqrt(jnp.mean(xc * xc, axis=-1, keepdims=True) + EPS) * lng_ref[...] + lnb_ref[...]
    y_buf[...] = y.astype(BF16)
    for n in range(u_ref.shape[0] // SG_CHUNK):
        rs = slice(n * SG_CHUNK, (n + 1) * SG_CHUNK)
        for g in range(SG_GROUPS):
            cs = slice(g * SG_CH, (g + 1) * SG_CH)
            mixed = jnp.dot(w_ref[g], y_buf[rs, cs], preferred_element_type=F32) + bias_ref[:, cs]
            o_ref[rs, cs] = (_gelu_tanh(u_ref[rs, cs].astype(F32)) * mixed).astype(BF16)


def _spatial(z, sg_w, bias_full, ln_g, ln_b, rows_per_block):
    m = z.shape[0]
    return pl.pallas_call(
        _spatial_kernel,
        grid=(m // rows_per_block,),
        in_specs=[pl.BlockSpec((rows_per_block, SG_WIDTH), lambda b: (b, 5)),
                  pl.BlockSpec((rows_per_block, SG_WIDTH), lambda b: (b, 6)),
                  pl.BlockSpec((SG_GROUPS, SG_CHUNK, SG_CHUNK), lambda b: (0, 0, 0)),
                  pl.BlockSpec((SG_CHUNK, SG_WIDTH), lambda b: (0, 0)),
                  pl.BlockSpec((1, SG_WIDTH), lambda b: (0, 0)),
                  pl.BlockSpec((1, SG_WIDTH), lambda b: (0, 0))],
        out_specs=pl.BlockSpec((rows_per_block, SG_WIDTH), lambda b: (b, 0)),
        out_shape=jax.ShapeDtypeStruct((m, SG_WIDTH), BF16),
        scratch_shapes=[pltpu.VMEM((rows_per_block, SG_WIDTH), BF16)],
        compiler_params=_params(("parallel",), 32),
        name="spatial_gate",
    )(z, z, sg_w, bias_full, ln_g, ln_b)


def _residual_epilogue(mix, x, vec_ref, xo_ref, ho_ref):
    xn = x + _rms(mix) * vec_ref[0:1]
    xo_ref[...] = xn
    if ho_ref is not None:
        ho_ref[...] = (_rms(xn) * vec_ref[1:2] + vec_ref[2:3]).astype(BF16)


def _mm_res_kernel(*refs, n_a, emit_h):
    a_refs, w_refs = refs[:n_a], refs[n_a:2 * n_a]
    x_ref, vec_ref, xo_ref = refs[2 * n_a], refs[2 * n_a + 1], refs[2 * n_a + 2]
    ho_ref = refs[2 * n_a + 3] if emit_h else None
    mix = jnp.dot(a_refs[0][...], w_refs[0][...], preferred_element_type=F32)
    for a_ref, w_ref in zip(a_refs[1:], w_refs[1:]):
        mix = mix + jnp.dot(a_ref[...], w_ref[...], preferred_element_type=F32)
    _residual_epilogue(mix, x_ref[...], vec_ref, xo_ref, ho_ref)


def _mm_res(a_list, w, x, vec, tm, emit_h=True):
    m = x.shape[0]
    n_a = len(a_list)
    w_list = [w] * n_a
    offs = np.cumsum([0] + [a.shape[1] for a in a_list])
    in_specs = ([pl.BlockSpec((tm, a.shape[1]), lambda i: (i, 0)) for a in a_list]
                + [pl.BlockSpec((a.shape[1], D_MODEL), lambda i, k=int(o) // a.shape[1]: (k, 0))
                   for a, o in zip(a_list, offs)]
                + [pl.BlockSpec((tm, D_MODEL), lambda i: (i, 0)),
                   pl.BlockSpec((8, D_MODEL), lambda i: (0, 0))])
    out_specs = [pl.BlockSpec((tm, D_MODEL), lambda i: (i, 0))]
    out_shape = [jax.ShapeDtypeStruct((m, D_MODEL), F32)]
    if emit_h:
        out_specs.append(pl.BlockSpec((tm, D_MODEL), lambda i: (i, 0)))
        out_shape.append(jax.ShapeDtypeStruct((m, D_MODEL), BF16))
    return pl.pallas_call(
        functools.partial(_mm_res_kernel, n_a=n_a, emit_h=emit_h),
        grid=(m // tm,),
        in_specs=in_specs, out_specs=out_specs, out_shape=out_shape,
        compiler_params=_params(("parallel",), 56),
        name="mm_residual",
    )(*a_list, *w_list, x, vec)


def _mlp_kernel(*refs, emit_h, n_side):
    h_ref, w1_ref, w2_ref, x_ref, vec_ref = refs[:5]
    side_in = refs[5:5 + n_side]
    xo_ref = refs[5 + n_side]
    ho_ref = refs[6 + n_side] if emit_h else None
    n_out = 2 if emit_h else 1
    side_out = refs[5 + n_side + n_out:5 + 2 * n_side + n_out]
    acc_ref = refs[-1]
    _side_cast(side_in, side_out)
    j = pl.program_id(1)

    @pl.when(j == 0)
    def _():
        acc_ref[...] = jnp.zeros_like(acc_ref)

    a = jnp.dot(h_ref[...], w1_ref[...], preferred_element_type=F32)
    a = jnp.square(jnp.maximum(a, 0.0)).astype(BF16)
    acc_ref[...] += jnp.dot(a, w2_ref[...], preferred_element_type=F32)

    @pl.when(j == pl.num_programs(1) - 1)
    def _():
        _residual_epilogue(acc_ref[...], x_ref[...], vec_ref, xo_ref, ho_ref)


def _mlp(h, w1, w2, x, vec, tm, tf, emit_h, side=()):
    m = x.shape[0]
    nj = D_FF // tf
    out_specs = [pl.BlockSpec((tm, D_MODEL), lambda i, j: (i, 0))]
    out_shape = [jax.ShapeDtypeStruct((m, D_MODEL), F32)]
    if emit_h:
        out_specs.append(pl.BlockSpec((tm, D_MODEL), lambda i, j: (i, 0)))
        out_shape.append(jax.ShapeDtypeStruct((m, D_MODEL), BF16))
    side_in, side_out, side_shapes = _side_cast_specs(side, (m // tm) * nj, lambda i, j: i * nj + j)
    return pl.pallas_call(
        functools.partial(_mlp_kernel, emit_h=emit_h, n_side=len(side)),
        grid=(m // tm, nj),
        in_specs=[pl.BlockSpec((tm, D_MODEL), lambda i, j: (i, 0)),
                  pl.BlockSpec((D_MODEL, tf), lambda i, j: (0, j)),
                  pl.BlockSpec((tf, D_MODEL), lambda i, j: (j, 0)),
                  pl.BlockSpec((tm, D_MODEL), lambda i, j: (i, 0)),
                  pl.BlockSpec((8, D_MODEL), lambda i, j: (0, 0))] + side_in,
        out_specs=out_specs + side_out, out_shape=out_shape + side_shapes,
        scratch_shapes=[pltpu.VMEM((tm, D_MODEL), F32)],
        compiler_params=_params(("parallel", "arbitrary"), 56),
        name="mlp",
    )(h, w1, w2, x, vec, *[w for w, _ in side])


def _dft_constants():
    def cs(n):
        k = np.arange(n)
        ang = 2.0 * np.pi * ((k[:, None] * k[None, :]) % n) / n
        return np.cos(ang) / np.sqrt(n), np.sin(ang) / np.sqrt(n)

    cc, sc = cs(FT_CH)
    c1, s1 = cs(FFT_L1)
    m1 = np.concatenate([c1, -s1], axis=0)
    c2, s2 = cs(FFT_L2)
    m2 = np.block([[c2, s2], [-s2, c2]])
    u1 = np.arange(FFT_L1)[:, None]
    t2 = np.arange(FFT_L2)[None, :]
    ang = 2.0 * np.pi * ((u1 * t2) % SEQ) / SEQ
    tw_c = np.cos(ang).reshape(SEQ, 1)
    tw_s = np.sin(ang).reshape(SEQ, 1)
    f = lambda a: jnp.asarray(a, F32)
    return f(cc), f(sc), f(m1), f(m2), f(tw_c), f(tw_s)


def _swap_row_factors(a, n_outer, n_inner):
    return a.reshape(n_outer, n_inner, a.shape[1]).transpose(1, 0, 2).reshape(a.shape)


def _pos_dft1_kernel(x_ref, m_ref, zr_ref, zi_ref):
    for k in range(x_ref.shape[0] // FFT_L1):
        sl = slice(k * FFT_L1, (k + 1) * FFT_L1)
        z = jnp.dot(m_ref[...], x_ref[sl, :], preferred_element_type=F32)
        zr_ref[sl, :] = z[:FFT_L1].astype(BF16)
        zi_ref[sl, :] = z[FFT_L1:].astype(BF16)


def _pos_dft1(xt, m1, slabs):
    rows = slabs * FFT_L1
    blk = pl.BlockSpec((rows, D_MODEL), lambda j: (j, 0))
    return pl.pallas_call(
        _pos_dft1_kernel,
        grid=(SEQ // rows,),
        in_specs=[blk, pl.BlockSpec((2 * FFT_L1, FFT_L1), lambda j: (0, 0))],
        out_specs=[blk, blk],
        out_shape=[jax.ShapeDtypeStruct((SEQ, D_MODEL), BF16)] * 2,
        compiler_params=_params(("parallel",), 32),
        name="pos_dft1",
    )(xt, m1)


def _pos_dft2_kernel(zr_ref, zi_ref, c_ref, s_ref, m_ref, wc_ref, ws_ref, y_ref, yr_buf, yi_buf):
    for k in range(zr_ref.shape[0] // FFT_L2):
        sl = slice(k * FFT_L2, (k + 1) * FFT_L2)
        zr = zr_ref[sl, :].astype(F32)
        zi = zi_ref[sl, :].astype(F32)
        c = c_ref[sl, :]
        s = s_ref[sl, :]
        t = jnp.concatenate([(zr * c + zi * s).astype(BF16), (zi * c - zr * s).astype(BF16)], axis=0)
        y = jnp.dot(m_ref[...], t, preferred_element_type=F32)
        yr_buf[sl, :] = y[:FFT_L2].astype(BF16)
        yi_buf[sl, :] = y[FFT_L2:].astype(BF16)
    for g in range(FT_GROUPS):
        cs = slice(g * FT_CH, (g + 1) * FT_CH)
        y_ref[:, cs] = (jnp.dot(yr_buf[:, cs], wc_ref[...], preferred_element_type=F32)
                        + jnp.dot(yi_buf[:, cs], ws_ref[...], preferred_element_type=F32)).astype(BF16)


def _pos_dft2(zr, zi, tw_c, tw_s, m2, wc, ws, slabs):
    rows = slabs * FFT_L2
    blk = pl.BlockSpec((rows, D_MODEL), lambda j: (j, 0))
    tw = pl.BlockSpec((rows, 1), lambda j: (j, 0))
    const = lambda a: pl.BlockSpec(a.shape, lambda j: (0, 0))
    return pl.pallas_call(
        _pos_dft2_kernel,
        grid=(SEQ // rows,),
        in_specs=[blk, blk, tw, tw, const(m2), const(wc), const(ws)],
        out_specs=blk,
        out_shape=jax.ShapeDtypeStruct((SEQ, D_MODEL), BF16),
        scratch_shapes=[pltpu.VMEM((rows, D_MODEL), BF16), pltpu.VMEM((rows, D_MODEL), BF16)],
        compiler_params=_params(("parallel",), 40),
        name="pos_dft2",
    )(zr, zi, tw_c, tw_s, m2, wc, ws)


def _fourier_mix(h):
    cc, sc, m1, m2, tw_c, tw_s = _dft_constants()
    ht = _swap_row_factors(h, FFT_L1, FFT_L2)
    zr, zi = _pos_dft1(ht, m1.astype(BF16), 8)
    zr = _swap_row_factors(zr, FFT_L2, FFT_L1)
    zi = _swap_row_factors(zi, FFT_L2, FFT_L1)
    yp = _pos_dft2(zr, zi, tw_c, tw_s, m2.astype(BF16), cc.astype(BF16), sc.astype(BF16), 4)
    return _swap_row_factors(yp, FFT_L1, FFT_L2)


def _pack_rows(*rows):
    rows = [r.reshape(1, D_MODEL).astype(F32) for r in rows]
    rows += [jnp.zeros((1, D_MODEL), F32)] * (8 - len(rows))
    return jnp.concatenate(rows, axis=0)


def kernel(x, c, ctx, c_ctx, w_ada, b_ada, norm_gain, w_in, w_out, lb_raw, hg_norm_gain,
           sg_w, sg_b, sg_ln_gain, sg_ln_bias, w_fourier, w_mlp_in, w_mlp_out):
    assert x.shape == (1, SEQ, D_MODEL) and ctx.shape == (1, CTX_LEN, D_MODEL)
    xs = x.reshape(SEQ, D_MODEL)
    cs = ctx.reshape(CTX_LEN, D_MODEL)

    mods = _ada_mods(jnp.concatenate([c.reshape(1, D_MODEL), c_ctx.reshape(1, D_MODEL)], axis=0), w_ada, b_ada)
    mod = lambda l, r, j: mods[l, r, j * D_MODEL:(j + 1) * D_MODEL]

    lb = jnp.cumsum(jax.nn.softmax(lb_raw.astype(F32), axis=1), axis=1)[:, 0]
    lb4 = lb.reshape(2, HG_HEADS, 1, HG_DK)

    pre = lambda gain, l, r, j: _pack_rows(gain * (1.0 + mod(l, r, j + 1)), mod(l, r, j))
    post = lambda gain, l, j, *nxt: _pack_rows(mod(l, 0, j) * gain, *nxt)

    z = _norm_mm(xs, pre(norm_gain[0, 0], 0, 0, 0), w_in, IN_WIDTH, 1024, 1024)
    z_ctx = _norm_mm(cs, pre(norm_gain[0, 0], 0, 1, 0), w_in, 3 * HG_WIDTH, CTX_LEN, 1024)
    zero_state = jnp.zeros((2, HG_HEADS, HG_DK, HG_DK), F32)
    _, s_ctx = _gla_states(z_ctx, lb4, zero_state, CTX_LEN)
    senter, _, w_out0, w1_0 = _gla_states(z, lb4, s_ctx, GLA_ROWS, side=[(w_out, 0), (w_mlp_in, 0)])
    o, w2_0, w_fou = _gla_out(z, lb4, hg_norm_gain[0], senter, GLA_ROWS, side=[(w_mlp_out, 0), (w_fourier, 0)])
    bias_full = jnp.repeat(sg_b[0].T.astype(F32), SG_CH, axis=1)
    s = _spatial(z, sg_w[0].astype(BF16), bias_full, sg_ln_gain[0].reshape(1, SG_WIDTH),
                 sg_ln_bias[0].reshape(1, SG_WIDTH), 512)
    vec = post(norm_gain[0, 1], 0, 2, *pre(norm_gain[0, 2], 0, 0, 3)[:2])
    x1, h = _mm_res([o, s], w_out0, xs, vec, 512)
    vec = post(norm_gain[0, 3], 0, 5, *pre(norm_gain[1, 0], 1, 0, 0)[:2])
    x2, h, w1_1, w2_1 = _mlp(h, w1_0, w2_0, x1, vec, 512, 1024, True, side=[(w_mlp_in, 1), (w_mlp_out, 1)])

    y = _fourier_mix(h)
    vec = post(norm_gain[1, 1], 1, 2, *pre(norm_gain[1, 2], 1, 0, 3)[:2])
    x3, h = _mm_res([y], w_fou, x2, vec, 512)
    (x4,) = _mlp(h, w1_1, w2_1, x3, post(norm_gain[1, 3], 1, 5), 512, 1024, False)
    return x4.reshape(1, SEQ, D_MODEL)
```

```python
import functools

import numpy as np
import jax
import jax.numpy as jnp
from jax import lax
from jax.experimental import pallas as pl
from jax.experimental.pallas import tpu as pltpu

D_MODEL = 2048
SEQ = 8192
DEPTH = 2
CTX_LEN = 256
CHUNK = 64
SUB = 16
GLA_ROWS = 1024
SUBROWS = 256
HG_HEADS = 8
HG_DK = 128
HG_WIDTH = HG_HEADS * HG_DK
SG_GROUPS = 8
SG_CH = 128
SG_WIDTH = SG_GROUPS * SG_CH
SG_CHUNK = 128
FT_GROUPS = 4
FT_CH = D_MODEL // FT_GROUPS
FFT_L1 = 64
FFT_L2 = 128
IN_WIDTH = 5 * HG_WIDTH + 2 * SG_WIDTH
D_FF = 4 * D_MODEL
N_MOD = 6
EPS = 1e-6
EXP_CLAMP = 115.0
LOG2E = 1.0 / float(np.log(2.0))
LANES = 128

F32 = jnp.float32
BF16 = jnp.bfloat16
MIB = 1024 * 1024


def _params(semantics, vmem_mib):
    return pltpu.CompilerParams(dimension_semantics=semantics, vmem_limit_bytes=vmem_mib * MIB)


def _rms(x):
    return x * lax.rsqrt(jnp.mean(x * x, axis=-1, keepdims=True) + EPS)


def _silu(x):
    return x * jax.nn.sigmoid(x)


def _row_subblocks(rows, step=SUBROWS):
    step = min(rows, step)
    return [slice(s, s + step) for s in range(0, rows, step)]


def _gelu_tanh(x):
    cdf = 0.5 * (1.0 + jnp.tanh(float(np.sqrt(2.0 / np.pi)) * (x + 0.044715 * (x * x * x))))
    return x * cdf


def _ada_kernel(c_ref, w_ref, b_ref, o_ref):
    tn = o_ref.shape[-1]
    for r in range(2):
        s = _silu(c_ref[r])
        cols = [jnp.sum(w_ref[0, :, j * LANES:(j + 1) * LANES] * s, axis=0, keepdims=True)
                for j in range(tn // LANES)]
        o_ref[0, r:r + 1, :] = jnp.concatenate(cols, axis=1) + b_ref[0]


def _ada_mods(c2, w_ada, b_ada):
    tn = 1024
    n = N_MOD * D_MODEL
    cb = jnp.broadcast_to(c2[:, :, None], (2, D_MODEL, LANES))
    return pl.pallas_call(
        _ada_kernel,
        grid=(DEPTH, n // tn),
        in_specs=[pl.BlockSpec((2, D_MODEL, LANES), lambda l, j: (0, 0, 0)),
                  pl.BlockSpec((1, D_MODEL, tn), lambda l, j: (l, 0, j)),
                  pl.BlockSpec((1, 1, tn), lambda l, j: (l, 0, j))],
        out_specs=pl.BlockSpec((1, 2, tn), lambda l, j: (l, 0, j)),
        out_shape=jax.ShapeDtypeStruct((DEPTH, 2, n), F32),
        compiler_params=_params(("parallel", "parallel"), 32),
        name="ada_mods",
    )(cb, w_ada, b_ada.reshape(DEPTH, 1, n))


def _norm_mm_kernel(x_ref, vec_ref, w_ref, o_ref, h_ref):
    w = w_ref[0].astype(BF16)

    @pl.when(pl.program_id(1) == 0)
    def _():
        for rs in _row_subblocks(x_ref.shape[0]):
            h = (_rms(x_ref[rs, :]) * vec_ref[0:1] + vec_ref[1:2]).astype(BF16)
            h_ref[rs, :] = h
            o_ref[rs, :] = jnp.dot(h, w, preferred_element_type=F32).astype(o_ref.dtype)

    @pl.when(pl.program_id(1) > 0)
    def _():
        o_ref[...] = jnp.dot(h_ref[...], w, preferred_element_type=F32).astype(o_ref.dtype)


def _norm_mm(x, vec, w, n_cols, tm, tn):
    m = x.shape[0]
    return pl.pallas_call(
        _norm_mm_kernel,
        grid=(m // tm, n_cols // tn),
        in_specs=[pl.BlockSpec((tm, D_MODEL), lambda i, j: (i, 0)),
                  pl.BlockSpec((8, D_MODEL), lambda i, j: (0, 0)),
                  pl.BlockSpec((1, D_MODEL, tn), lambda i, j: (0, 0, j))],
        out_specs=pl.BlockSpec((tm, tn), lambda i, j: (i, j)),
        out_shape=jax.ShapeDtypeStruct((m, n_cols), BF16),
        scratch_shapes=[pltpu.VMEM((tm, D_MODEL), BF16)],
        compiler_params=_params(("parallel", "arbitrary"), 52),
        name="norm_mm",
    )(x, vec, w)


def _seg_cumsum(g, row_in_chunk):
    p = g
    s = 1
    while s < CHUNK:
        p = p + jnp.where(row_in_chunk >= s, pltpu.roll(p, s, axis=0), 0.0)
        s *= 2
    return p


def _side_cast_specs(layered_weights, n_steps, step_of):
    in_specs, out_specs, out_shapes = [], [], []
    for w, layer in layered_weights:
        rows, n = w.shape[1] // n_steps, w.shape[2]
        in_specs.append(pl.BlockSpec((1, rows, n), lambda *g, layer=layer: (layer, step_of(*g), 0)))
        out_specs.append(pl.BlockSpec((rows, n), lambda *g: (step_of(*g), 0)))
        out_shapes.append(jax.ShapeDtypeStruct((w.shape[1], n), BF16))
    return in_specs, out_specs, out_shapes


def _side_cast(side_in, side_out):
    for wi, wo in zip(side_in, side_out):
        wo[...] = wi[0].astype(BF16)


def _gla_state_kernel(*refs, nchunk, n_side, d0):
    f_ref, i_ref, lb_ref, s0_ref = refs[:4]
    side_in = refs[4:4 + n_side]
    senter_ref, sfin_ref = refs[4 + n_side:6 + n_side]
    side_out = refs[6 + n_side:6 + 2 * n_side]
    s_acc, p_buf, g_buf, k_buf = refs[6 + 2 * n_side:]
    _side_cast(side_in, side_out)
    d = pl.program_id(0) + d0

    @pl.when(pl.program_id(2) == 0)
    def _():
        s_acc[...] = s0_ref[0, 0]

    rows = f_ref.shape[0]
    lb = lb_ref[0, 0]
    f = lb + (1.0 - lb) * jax.nn.sigmoid(f_ref[...].astype(F32))
    g = jnp.log(f) * LOG2E
    ric = lax.broadcasted_iota(jnp.int32, (rows, LANES), 0) & (CHUNK - 1)
    p_buf[...] = _seg_cumsum(g, ric)
    g_buf[...] = g
    k_buf[...] = 1.0 - f

    s = s_acc[...]
    for j in range(nchunk):
        c = jnp.where(d == 0, j, nchunk - 1 - j)
        base = pl.multiple_of(c * CHUNK, CHUNK)
        sl = pl.ds(base, CHUNK)
        p = p_buf[sl, :]
        tot = p_buf[pl.ds(base + CHUNK - 1, 1), :]
        ex = jnp.where(d == 0, tot - p, p - g_buf[sl, :])
        kt = (k_buf[sl, :] * jnp.exp2(ex)).astype(BF16)
        u = lax.dot_general(i_ref[sl, :], kt, (((0,), (0,)), ((), ())), preferred_element_type=F32)
        senter_ref[0, 0, c] = s.astype(BF16)
        s = s * jnp.exp2(tot) + u
    s_acc[...] = s
    sfin_ref[0, 0] = s


def _gla_states(z, lb4, s0, rows_per_block, side=(), d0=0, nd=2):
    m = z.shape[0]
    nb = m // rows_per_block
    nchunk = rows_per_block // CHUNK

    def rb(d, h, b):
        return b + (d + d0) * (nb - 1 - 2 * b)

    side_in, side_out, side_shapes = _side_cast_specs(
        side, nd * HG_HEADS * nb, lambda d, h, b: (d * HG_HEADS + h) * nb + b)
    return pl.pallas_call(
        functools.partial(_gla_state_kernel, nchunk=nchunk, n_side=len(side), d0=d0),
        grid=(nd, HG_HEADS, nb),
        in_specs=[pl.BlockSpec((rows_per_block, HG_DK), lambda d, h, b: (rb(d, h, b), (d + d0) * HG_HEADS + h)),
                  pl.BlockSpec((rows_per_block, HG_DK), lambda d, h, b: (rb(d, h, b), 2 * HG_HEADS + h)),
                  pl.BlockSpec((1, 1, 1, HG_DK), lambda d, h, b: (d + d0, h, 0, 0)),
                  pl.BlockSpec((1, 1, HG_DK, HG_DK), lambda d, h, b: (d + d0, h, 0, 0))] + side_in,
        out_specs=[pl.BlockSpec((1, 1, nchunk, HG_DK, HG_DK), lambda d, h, b: (d, h, rb(d, h, b), 0, 0)),
                   pl.BlockSpec((1, 1, HG_DK, HG_DK), lambda d, h, b: (d, h, 0, 0))] + side_out,
        out_shape=[jax.ShapeDtypeStruct((nd, HG_HEADS, m // CHUNK, HG_DK, HG_DK), BF16),
                   jax.ShapeDtypeStruct((nd, HG_HEADS, HG_DK, HG_DK), F32)] + side_shapes,
        scratch_shapes=[pltpu.VMEM((HG_DK, HG_DK), F32),
                        pltpu.VMEM((rows_per_block, HG_DK), F32),
                        pltpu.VMEM((rows_per_block, HG_DK), F32),
                        pltpu.VMEM((rows_per_block, HG_DK), F32)],
        compiler_params=_params(("parallel", "parallel", "arbitrary"), 32),
        name="gla_states",
    )(z, z, lb4, s0, *[w for w, _ in side])


def _gla_out_kernel(*refs, nchunk, n_side):
    ff_ref, fb_ref, i_ref, q_ref, gt_ref, lb_ref, gain_ref, sf_ref, sb_ref = refs[:9]
    side_in = refs[9:9 + n_side]
    o_ref = refs[9 + n_side]
    side_out = refs[10 + n_side:10 + 2 * n_side]
    pf_buf, pb_buf, gb_buf, kf_buf, kb_buf, q_buf = refs[10 + 2 * n_side:]
    _side_cast(side_in, side_out)
    rows = ff_ref.shape[0]
    ric = lax.broadcasted_iota(jnp.int32, (rows, LANES), 0) & (CHUNK - 1)
    lbf = lb_ref[0, 0]
    lbb = lb_ref[1, 0]
    ff = lbf + (1.0 - lbf) * jax.nn.sigmoid(ff_ref[...].astype(F32))
    pf_buf[...] = _seg_cumsum(jnp.log(ff) * LOG2E, ric)
    kf_buf[...] = 1.0 - ff
    fb = lbb + (1.0 - lbb) * jax.nn.sigmoid(fb_ref[...].astype(F32))
    gb = jnp.log(fb) * LOG2E
    pb_buf[...] = _seg_cumsum(gb, ric)
    gb_buf[...] = gb
    kb_buf[...] = 1.0 - fb
    q_buf[...] = _silu(q_ref[...].astype(F32)) * (HG_DK ** -0.5)

    blk = lax.broadcasted_iota(jnp.int32, (CHUNK, LANES), 0) // SUB
    r2 = lax.broadcasted_iota(jnp.int32, (CHUNK, CHUNK), 0)
    c2 = lax.broadcasted_iota(jnp.int32, (CHUNK, CHUNK), 1)
    nt = (((1,), (1,)), ((), ()))
    gain = gain_ref[0]
    nsub = CHUNK // SUB

    for c in range(nchunk):
        base = c * CHUNK
        sl = pl.ds(base, CHUNK)
        pf = pf_buf[sl, :]
        pb = pb_buf[sl, :]
        gbc = gb_buf[sl, :]
        kf = kf_buf[sl, :]
        kb = kb_buf[sl, :]
        q = q_buf[sl, :]
        totb = pb_buf[pl.ds(base + CHUNK - 1, 1), :]
        suf = totb - pb + gbc
        mids, nids = [], []
        for i in range(nsub):
            mids.append(pf_buf[pl.ds(base + SUB * i + SUB // 2 - 1, 1), :])
            rmid = pl.ds(base + SUB * i + SUB // 2, 1)
            nids.append(totb - pb_buf[rmid, :] + gb_buf[rmid, :])
        mid_rows = jnp.concatenate([jnp.broadcast_to(m, (SUB, LANES)) for m in mids], axis=0)
        nid_rows = jnp.concatenate([jnp.broadcast_to(m, (SUB, LANES)) for m in nids], axis=0)
        qf_all = q * jnp.exp2(jnp.minimum(pf - mid_rows, EXP_CLAMP))
        qb_all = q * jnp.exp2(jnp.minimum(suf - nid_rows, EXP_CLAMP))
        qf_seg, kf_seg, qb_seg, kb_seg = [], [], [], []
        for i in range(nsub):
            qf_seg.append(jnp.where(blk == i, qf_all, 0.0))
            qb_seg.append(jnp.where(blk == i, qb_all, 0.0))
            hi = SUB * (i + 1)
            top = kf[:hi] * jnp.exp2(jnp.minimum(mids[i] - pf[:hi], EXP_CLAMP))
            kf_seg.append(top if hi == CHUNK else
                          jnp.concatenate([top, jnp.zeros((CHUNK - hi, LANES), F32)], axis=0))
            lo = SUB * i
            bot = kb[lo:] * jnp.exp2(jnp.minimum(nids[i] - suf[lo:], EXP_CLAMP))
            kb_seg.append(bot if lo == 0 else
                          jnp.concatenate([jnp.zeros((lo, LANES), F32), bot], axis=0))
        qf_big = jnp.concatenate(qf_seg, axis=1).astype(BF16)
        kf_big = jnp.concatenate(kf_seg, axis=1).astype(BF16)
        qb_big = jnp.concatenate(qb_seg, axis=1).astype(BF16)
        kb_big = jnp.concatenate(kb_seg, axis=1).astype(BF16)
        sc_f = lax.dot_general(qf_big, kf_big, nt, preferred_element_type=F32)
        sc_b = lax.dot_general(qb_big, kb_big, nt, preferred_element_type=F32)
        a = (jnp.where(c2 <= r2, sc_f, 0.0) + jnp.where(c2 >= r2, sc_b, 0.0)).astype(BF16)
        o = jnp.dot(a, i_ref[sl, :], preferred_element_type=F32)
        qcat = jnp.concatenate([q * jnp.exp2(pf), q * jnp.exp2(suf)], axis=1).astype(BF16)
        scat = jnp.concatenate([sf_ref[0, 0, c], sb_ref[0, 0, c]], axis=1)
        o = o + lax.dot_general(qcat, scat, nt, preferred_element_type=F32)
        o = _rms(o) * gain
        o_ref[sl, :] = (o * _silu(gt_ref[sl, :].astype(F32))).astype(BF16)


def _gla_out(z, lb4, hg_gain, senter, rows_per_block, side=()):
    m = z.shape[0]
    nb = m // rows_per_block
    nchunk = rows_per_block // CHUNK

    def col(k):
        return pl.BlockSpec((rows_per_block, HG_DK), lambda h, b, k=k: (b, k * HG_HEADS + h))

    def st(d):
        return pl.BlockSpec((1, 1, nchunk, HG_DK, HG_DK), lambda h, b, d=d: (d, h, b, 0, 0))

    side_in, side_out, side_shapes = _side_cast_specs(side, HG_HEADS * nb, lambda h, b: h * nb + b)
    return pl.pallas_call(
        functools.partial(_gla_out_kernel, nchunk=nchunk, n_side=len(side)),
        grid=(HG_HEADS, nb),
        in_specs=[col(0), col(1), col(2), col(3), col(4),
                  pl.BlockSpec((2, 1, 1, HG_DK), lambda h, b: (0, h, 0, 0)),
                  pl.BlockSpec((1, 1, HG_DK), lambda h, b: (h, 0, 0)),
                  st(0), st(1)] + side_in,
        out_specs=[pl.BlockSpec((rows_per_block, HG_DK), lambda h, b: (b, h))] + side_out,
        out_shape=[jax.ShapeDtypeStruct((m, HG_WIDTH), BF16)] + side_shapes,
        scratch_shapes=[pltpu.VMEM((rows_per_block, HG_DK), F32) for _ in range(6)],
        compiler_params=_params(("parallel", "parallel"), 32),
        name="gla_out",
    )(z, z, z, z, z, lb4, hg_gain.reshape(HG_HEADS, 1, HG_DK), senter, senter, *[w for w, _ in side])


def _spatial_kernel(u_ref, v_ref, w_ref, bias_ref, lng_ref, lnb_ref, o_ref, y_buf):
    v = _gelu_tanh(v_ref[...].astype(F32))
    xc = v - jnp.mean(v, axis=-1, keepdims=True)
    y = xc * lax.rsqrt(jnp.mean(xc * xc, axis=-1, keepdims=True) + EPS) * lng_ref[...] + lnb_ref[...]
    y_buf[...] = y.astype(BF16)
    for n in range(u_ref.shape[0] // SG_CHUNK):
        rs = slice(n * SG_CHUNK, (n + 1) * SG_CHUNK)
        for g in range(SG_GROUPS):
            cs = slice(g * SG_CH, (g + 1) * SG_CH)
            mixed = jnp.dot(w_ref[g], y_buf[rs, cs], preferred_element_type=F32) + bias_ref[:, cs]
            o_ref[rs, cs] = (_gelu_tanh(u_ref[rs, cs].astype(F32)) * mixed).astype(BF16)


def _spatial(z, sg_w, bias_full, ln_g, ln_b, rows_per_block):
    m = z.shape[0]
    return pl.pallas_call(
        _spatial_kernel,
        grid=(m // rows_per_block,),
        in_specs=[pl.BlockSpec((rows_per_block, SG_WIDTH), lambda b: (b, 5)),
                  pl.BlockSpec((rows_per_block, SG_WIDTH), lambda b: (b, 6)),
                  pl.BlockSpec((SG_GROUPS, SG_CHUNK, SG_CHUNK), lambda b: (0, 0, 0)),
                  pl.BlockSpec((SG_CHUNK, SG_WIDTH), lambda b: (0, 0)),
                  pl.BlockSpec((1, SG_WIDTH), lambda b: (0, 0)),
                  pl.BlockSpec((1, SG_WIDTH), lambda b: (0, 0))],
        out_specs=pl.BlockSpec((rows_per_block, SG_WIDTH), lambda b: (b, 0)),
        out_shape=jax.ShapeDtypeStruct((m, SG_WIDTH), BF16),
        scratch_shapes=[pltpu.VMEM((rows_per_block, SG_WIDTH), BF16)],
        compiler_params=_params(("parallel",), 32),
        name="spatial_gate",
    )(z, z, sg_w, bias_full, ln_g, ln_b)


def _residual_epilogue(mix, x_ref, vec_ref, xo_ref, ho_ref, rs):
    xn = x_ref[rs, :] + _rms(mix) * vec_ref[0:1]
    xo_ref[rs, :] = xn
    if ho_ref is not None:
        hn = (_rms(xn) * vec_ref[1:2] + vec_ref[2:3]).astype(BF16)
        outs = ho_ref if isinstance(ho_ref, (tuple, list)) else (ho_ref,)
        width = hn.shape[1] // len(outs)
        for k, out in enumerate(outs):
            out[rs, :] = hn[:, k * width:(k + 1) * width]


def _mm_res_kernel(*refs, n_a, emit_h):
    a_refs, w_refs = refs[:n_a], refs[n_a:2 * n_a]
    x_ref, vec_ref, xo_ref = refs[2 * n_a], refs[2 * n_a + 1], refs[2 * n_a + 2]
    ho_ref = refs[2 * n_a + 3] if emit_h else None
    for rs in _row_subblocks(x_ref.shape[0], 128):
        mix = jnp.dot(a_refs[0][rs, :], w_refs[0][...], preferred_element_type=F32)
        for a_ref, w_ref in zip(a_refs[1:], w_refs[1:]):
            mix = mix + jnp.dot(a_ref[rs, :], w_ref[...], preferred_element_type=F32)
        _residual_epilogue(mix, x_ref, vec_ref, xo_ref, ho_ref, rs)


def _mm_res(a_list, w, x, vec, tm, emit_h=True):
    m = x.shape[0]
    n_a = len(a_list)
    w_list = [w] * n_a
    offs = np.cumsum([0] + [a.shape[1] for a in a_list])
    in_specs = ([pl.BlockSpec((tm, a.shape[1]), lambda i: (i, 0)) for a in a_list]
                + [pl.BlockSpec((a.shape[1], D_MODEL), lambda i, k=int(o) // a.shape[1]: (k, 0))
                   for a, o in zip(a_list, offs)]
                + [pl.BlockSpec((tm, D_MODEL), lambda i: (i, 0)),
                   pl.BlockSpec((8, D_MODEL), lambda i: (0, 0))])
    out_specs = [pl.BlockSpec((tm, D_MODEL), lambda i: (i, 0))]
    out_shape = [jax.ShapeDtypeStruct((m, D_MODEL), F32)]
    if emit_h:
        out_specs.append(pl.BlockSpec((tm, D_MODEL), lambda i: (i, 0)))
        out_shape.append(jax.ShapeDtypeStruct((m, D_MODEL), BF16))
    return pl.pallas_call(
        functools.partial(_mm_res_kernel, n_a=n_a, emit_h=emit_h),
        grid=(m // tm,),
        in_specs=in_specs, out_specs=out_specs, out_shape=out_shape,
        compiler_params=_params(("parallel",), 56),
        name="mm_residual",
    )(*a_list, *w_list, x, vec)


def _mlp_kernel(*refs, n_h, n_side):
    h_ref, w1_ref, w2_ref, x_ref, vec_ref = refs[:5]
    side_in = refs[5:5 + n_side]
    xo_ref = refs[5 + n_side]
    ho_ref = tuple(refs[6 + n_side:6 + n_side + n_h]) or None
    n_out = 1 + n_h
    side_out = refs[5 + n_side + n_out:5 + 2 * n_side + n_out]
    acc_ref = refs[-1]
    _side_cast(side_in, side_out)
    j = pl.program_id(1)

    last = pl.num_programs(1) - 1

    def partial_product(rs):
        a = jnp.dot(h_ref[rs, :], w1_ref[...], preferred_element_type=F32)
        a = jnp.square(jnp.maximum(a, 0.0)).astype(BF16)
        return jnp.dot(a, w2_ref[...], preferred_element_type=F32)

    @pl.when(j == 0)
    def _():
        for rs in _row_subblocks(h_ref.shape[0]):
            acc_ref[rs, :] = partial_product(rs)

    @pl.when(jnp.logical_and(j > 0, j < last))
    def _():
        for rs in _row_subblocks(h_ref.shape[0]):
            acc_ref[rs, :] += partial_product(rs)

    @pl.when(j == last)
    def _():
        for rs in _row_subblocks(h_ref.shape[0]):
            _residual_epilogue(acc_ref[rs, :] + partial_product(rs), x_ref, vec_ref, xo_ref, ho_ref, rs)


def _mlp(h, w1, w2, x, vec, tm, tf, n_h, side=()):
    m = x.shape[0]
    nj = D_FF // tf
    out_specs = [pl.BlockSpec((tm, D_MODEL), lambda i, j: (i, 0))]
    out_shape = [jax.ShapeDtypeStruct((m, D_MODEL), F32)]
    for _ in range(n_h):
        out_specs.append(pl.BlockSpec((tm, D_MODEL // n_h), lambda i, j: (i, 0)))
        out_shape.append(jax.ShapeDtypeStruct((m, D_MODEL // n_h), BF16))
    side_in, side_out, side_shapes = _side_cast_specs(side, (m // tm) * nj, lambda i, j: i * nj + j)
    return pl.pallas_call(
        functools.partial(_mlp_kernel, n_h=n_h, n_side=len(side)),
        grid=(m // tm, nj),
        in_specs=[pl.BlockSpec((tm, D_MODEL), lambda i, j: (i, 0)),
                  pl.BlockSpec((D_MODEL, tf), lambda i, j: (0, j)),
                  pl.BlockSpec((tf, D_MODEL), lambda i, j: (j, 0)),
                  pl.BlockSpec((tm, D_MODEL), lambda i, j: (i, 0)),
                  pl.BlockSpec((8, D_MODEL), lambda i, j: (0, 0))] + side_in,
        out_specs=out_specs + side_out, out_shape=out_shape + side_shapes,
        scratch_shapes=[pltpu.VMEM((tm, D_MODEL), F32)],
        compiler_params=_params(("parallel", "arbitrary"), 56),
        name="mlp",
    )(h, w1, w2, x, vec, *[w for w, _ in side])


def _dft_constants():
    def cs(n):
        k = np.arange(n)
        ang = 2.0 * np.pi * ((k[:, None] * k[None, :]) % n) / n
        return np.cos(ang) / np.sqrt(n), np.sin(ang) / np.sqrt(n)

    cc, sc = cs(FT_CH)
    c1, s1 = cs(FFT_L1)
    m1 = np.concatenate([c1, -s1], axis=0)
    c2, s2 = cs(FFT_L2)
    m2 = np.block([[c2, s2], [-s2, c2]])
    u1 = np.arange(FFT_L1)[:, None]
    t2 = np.arange(FFT_L2)[None, :]
    ang = 2.0 * np.pi * ((u1 * t2) % SEQ) / SEQ
    tw_c = np.cos(ang).reshape(SEQ, 1)
    tw_s = np.sin(ang).reshape(SEQ, 1)
    f = lambda a: jnp.asarray(a, F32)
    return f(cc), f(sc), f(m1), f(m2), f(tw_c), f(tw_s)


def _swap_row_factors(a, n_outer, n_inner):
    return a.reshape(n_outer, n_inner, a.shape[1]).transpose(1, 0, 2).reshape(a.shape)


def _pos_dft1_kernel(x_ref, m_ref, zr_ref, zi_ref):
    for k in range(x_ref.shape[0] // FFT_L1):
        sl = slice(k * FFT_L1, (k + 1) * FFT_L1)
        z = jnp.dot(m_ref[...], x_ref[sl, :], preferred_element_type=F32)
        zr_ref[sl, :] = z[:FFT_L1].astype(BF16)
        zi_ref[sl, :] = z[FFT_L1:].astype(BF16)


def _pos_dft1(xt, m1, slabs):
    rows = slabs * FFT_L1
    blk = pl.BlockSpec((rows, xt.shape[1]), lambda j: (j, 0))
    return pl.pallas_call(
        _pos_dft1_kernel,
        grid=(SEQ // rows,),
        in_specs=[blk, pl.BlockSpec((2 * FFT_L1, FFT_L1), lambda j: (0, 0))],
        out_specs=[blk, blk],
        out_shape=[jax.ShapeDtypeStruct(xt.shape, BF16)] * 2,
        compiler_params=_params(("parallel",), 32),
        name="pos_dft1",
    )(xt, m1)


def _pos_dft2_kernel(zr_ref, zi_ref, c_ref, s_ref, m_ref, wc_ref, ws_ref, y_ref, yr_buf, yi_buf):
    for k in range(zr_ref.shape[0] // FFT_L2):
        sl = slice(k * FFT_L2, (k + 1) * FFT_L2)
        zr = zr_ref[sl, :].astype(F32)
        zi = zi_ref[sl, :].astype(F32)
        c = c_ref[sl, :]
        s = s_ref[sl, :]
        t = jnp.concatenate([(zr * c + zi * s).astype(BF16), (zi * c - zr * s).astype(BF16)], axis=0)
        y = jnp.dot(m_ref[...], t, preferred_element_type=F32)
        yr_buf[sl, :] = y[:FFT_L2].astype(BF16)
        yi_buf[sl, :] = y[FFT_L2:].astype(BF16)
    for g in range(zr_ref.shape[1] // FT_CH):
        cs = slice(g * FT_CH, (g + 1) * FT_CH)
        y_ref[:, cs] = (jnp.dot(yr_buf[:, cs], wc_ref[...], preferred_element_type=F32)
                        + jnp.dot(yi_buf[:, cs], ws_ref[...], preferred_element_type=F32)).astype(BF16)


def _pos_dft2(zr, zi, tw_c, tw_s, m2, wc, ws, slabs):
    rows = slabs * FFT_L2
    width = zr.shape[1]
    blk = pl.BlockSpec((rows, width), lambda j: (j, 0))
    tw = pl.BlockSpec((rows, 1), lambda j: (j, 0))
    const = lambda a: pl.BlockSpec(a.shape, lambda j: (0, 0))
    return pl.pallas_call(
        _pos_dft2_kernel,
        grid=(SEQ // rows,),
        in_specs=[blk, blk, tw, tw, const(m2), const(wc), const(ws)],
        out_specs=blk,
        out_shape=jax.ShapeDtypeStruct(zr.shape, BF16),
        scratch_shapes=[pltpu.VMEM((rows, width), BF16), pltpu.VMEM((rows, width), BF16)],
        compiler_params=_params(("parallel",), 40),
        name="pos_dft2",
    )(zr, zi, tw_c, tw_s, m2, wc, ws)


def _fourier_mix(h_parts):
    cc, sc, m1, m2, tw_c, tw_s = _dft_constants()
    m1, m2, cc, sc = m1.astype(BF16), m2.astype(BF16), cc.astype(BF16), sc.astype(BF16)
    hts = [_swap_row_factors(h, FFT_L1, FFT_L2) for h in h_parts]
    zs = [_pos_dft1(ht, m1, 8) for ht in hts]
    zs = [[_swap_row_factors(z, FFT_L2, FFT_L1) for z in zri] for zri in zs]
    yps = [_pos_dft2(zr, zi, tw_c, tw_s, m2, cc, sc, 4) for zr, zi in zs]
    return [_swap_row_factors(yp, FFT_L1, FFT_L2) for yp in yps]


def _pack_rows(*rows):
    rows = [r.reshape(1, D_MODEL).astype(F32) for r in rows]
    rows += [jnp.zeros((1, D_MODEL), F32)] * (8 - len(rows))
    return jnp.concatenate(rows, axis=0)


def kernel(x, c, ctx, c_ctx, w_ada, b_ada, norm_gain, w_in, w_out, lb_raw, hg_norm_gain,
           sg_w, sg_b, sg_ln_gain, sg_ln_bias, w_fourier, w_mlp_in, w_mlp_out):
    assert x.shape == (1, SEQ, D_MODEL) and ctx.shape == (1, CTX_LEN, D_MODEL)
    xs = x.reshape(SEQ, D_MODEL)
    cs = ctx.reshape(CTX_LEN, D_MODEL)

    mods = _ada_mods(jnp.concatenate([c.reshape(1, D_MODEL), c_ctx.reshape(1, D_MODEL)], axis=0), w_ada, b_ada)
    mod = lambda l, r, j: mods[l, r, j * D_MODEL:(j + 1) * D_MODEL]

    lb = jnp.cumsum(jax.nn.softmax(lb_raw.astype(F32), axis=1), axis=1)[:, 0]
    lb4 = lb.reshape(2, HG_HEADS, 1, HG_DK)

    pre = lambda gain, l, r, j: _pack_rows(gain * (1.0 + mod(l, r, j + 1)), mod(l, r, j))
    post = lambda gain, l, j, *nxt: _pack_rows(mod(l, 0, j) * gain, *nxt)

    z = _norm_mm(xs, pre(norm_gain[0, 0], 0, 0, 0), w_in, IN_WIDTH, 1024, 1024)
    z_ctx = _norm_mm(cs, pre(norm_gain[0, 0], 0, 1, 0), w_in, 3 * HG_WIDTH, CTX_LEN, 1024)
    zero_state = jnp.zeros((2, HG_HEADS, HG_DK, HG_DK), F32)
    _, s_ctx = _gla_states(z_ctx, lb4, zero_state, CTX_LEN)
    senter, _, w_out0, w1_0 = _gla_states(z, lb4, s_ctx, GLA_ROWS, side=[(w_out, 0), (w_mlp_in, 0)])
    o, w2_0, w_fou = _gla_out(z, lb4, hg_norm_gain[0], senter, GLA_ROWS, side=[(w_mlp_out, 0), (w_fourier, 0)])
    bias_full = jnp.repeat(sg_b[0].T.astype(F32), SG_CH, axis=1)
    s = _spatial(z, sg_w[0].astype(BF16), bias_full, sg_ln_gain[0].reshape(1, SG_WIDTH),
                 sg_ln_bias[0].reshape(1, SG_WIDTH), 512)
    vec = post(norm_gain[0, 1], 0, 2, *pre(norm_gain[0, 2], 0, 0, 3)[:2])
    x1, h = _mm_res([o, s], w_out0, xs, vec, 512)
    vec = post(norm_gain[0, 3], 0, 5, *pre(norm_gain[1, 0], 1, 0, 0)[:2])
    x2, h_a, h_b, w1_1, w2_1 = _mlp(h, w1_0, w2_0, x1, vec, 512, 1024, 2,
                                    side=[(w_mlp_in, 1), (w_mlp_out, 1)])

    y_parts = _fourier_mix([h_a, h_b])
    vec = post(norm_gain[1, 1], 1, 2, *pre(norm_gain[1, 2], 1, 0, 3)[:2])
    x3, h = _mm_res(y_parts, w_fou, x2, vec, 512)
    (x4,) = _mlp(h, w1_1, w2_1, x3, post(norm_gain[1, 3], 1, 5), 512, 1024, 0)
    return x4.reshape(1, SEQ, D_MODEL)
```

```python
import functools

import numpy as np
import jax
import jax.numpy as jnp
from jax import lax
from jax.experimental import pallas as pl
from jax.experimental.pallas import tpu as pltpu

D_MODEL = 2048
SEQ = 8192
DEPTH = 2
CTX_LEN = 256
CHUNK = 64
SUB = 16
GLA_ROWS = 1024
SUBROWS = 256
HG_HEADS = 8
HG_DK = 128
HG_WIDTH = HG_HEADS * HG_DK
SG_GROUPS = 8
SG_CH = 128
SG_WIDTH = SG_GROUPS * SG_CH
SG_CHUNK = 128
FT_GROUPS = 4
FT_CH = D_MODEL // FT_GROUPS
FFT_L1 = 64
FFT_L2 = 128
IN_WIDTH = 5 * HG_WIDTH + 2 * SG_WIDTH
D_FF = 4 * D_MODEL
N_MOD = 6
EPS = 1e-6
EXP_CLAMP = 115.0
LOG2E = 1.0 / float(np.log(2.0))
LANES = 128
SUBLANES = 8
MXU_COLS = 256
PROJ_ROWS = 256

F32 = jnp.float32
BF16 = jnp.bfloat16
MIB = 1024 * 1024


def _params(semantics, vmem_mib):
    return pltpu.CompilerParams(dimension_semantics=semantics, vmem_limit_bytes=vmem_mib * MIB)


def _rms(x):
    return x * lax.rsqrt(jnp.mean(x * x, axis=-1, keepdims=True) + EPS)


def _silu(x):
    return x * jax.nn.sigmoid(x)


def _row_subblocks(rows, step=SUBROWS):
    step = min(rows, step)
    return [slice(s, s + step) for s in range(0, rows, step)]


def _gelu_tanh(x):
    cdf = 0.5 * (1.0 + jnp.tanh(float(np.sqrt(2.0 / np.pi)) * (x + 0.044715 * (x * x * x))))
    return x * cdf


def _ada_kernel(c_ref, w_ref, b_ref, o_ref):
    tn = o_ref.shape[-1]
    for r in range(2):
        s = _silu(c_ref[r])
        cols = [jnp.sum(w_ref[0, :, j * LANES:(j + 1) * LANES] * s, axis=0, keepdims=True)
                for j in range(tn // LANES)]
        o_ref[0, r:r + 1, :] = jnp.concatenate(cols, axis=1) + b_ref[0]


def _ada_mods(c2, w_ada, b_ada):
    tn = 1024
    n = N_MOD * D_MODEL
    cb = jnp.broadcast_to(c2[:, :, None], (2, D_MODEL, LANES))
    return pl.pallas_call(
        _ada_kernel,
        grid=(DEPTH, n // tn),
        in_specs=[pl.BlockSpec((2, D_MODEL, LANES), lambda l, j: (0, 0, 0)),
                  pl.BlockSpec((1, D_MODEL, tn), lambda l, j: (l, 0, j)),
                  pl.BlockSpec((1, 1, tn), lambda l, j: (l, 0, j))],
        out_specs=pl.BlockSpec((1, 2, tn), lambda l, j: (l, 0, j)),
        out_shape=jax.ShapeDtypeStruct((DEPTH, 2, n), F32),
        compiler_params=_params(("parallel", "parallel"), 32),
        name="ada_mods",
    )(cb, w_ada, b_ada.reshape(DEPTH, 1, n))


def _side_cast_specs(weights, n_steps, step_of):
    in_specs, out_specs, out_shapes = [], [], []
    for w, layer, *cols in weights:
        width, cidx = cols if cols else (w.shape[2], 0)
        rows = w.shape[1] // n_steps
        in_specs.append(pl.BlockSpec((1, rows, width),
                                     lambda *g, layer=layer, cidx=cidx: (layer, step_of(*g), cidx)))
        out_specs.append(pl.BlockSpec((rows, width), lambda *g: (step_of(*g), 0)))
        out_shapes.append(jax.ShapeDtypeStruct((w.shape[1], width), BF16))
    return in_specs, out_specs, out_shapes


def _side_cast(side_in, side_out):
    for wi, wo in zip(side_in, side_out):
        wo[...] = wi[0].astype(BF16)


def _norm_mm_kernel(*refs, emit_h, n_side, keep_w):
    x_ref, vec_ref, w_ref = refs[:3]
    side_in = refs[3:3 + n_side]
    o_ref = refs[3 + n_side]
    n_out = 2 if emit_h else 1
    side_out = refs[3 + n_side + n_out:3 + 2 * n_side + n_out]
    scratch = refs[3 + 2 * n_side + n_out:]
    h_ref = refs[3 + n_side + 1] if emit_h else scratch[0]
    _side_cast(side_in, side_out)
    if keep_w:
        wb_ref = scratch[-1]

        @pl.when(pl.program_id(0) == 0)
        def _():
            wb_ref[pl.program_id(1)] = w_ref[0].astype(BF16)

        w = wb_ref[pl.program_id(1)]
    else:
        w = w_ref[0].astype(BF16)

    @pl.when(pl.program_id(1) == 0)
    def _():
        for rs in _row_subblocks(x_ref.shape[0]):
            h = (_rms(x_ref[rs, :]) * vec_ref[0:1] + vec_ref[1:2]).astype(BF16)
            h_ref[rs, :] = h
            o_ref[rs, :] = jnp.dot(h, w, preferred_element_type=F32).astype(o_ref.dtype)

    @pl.when(pl.program_id(1) > 0)
    def _():
        o_ref[...] = jnp.dot(h_ref[...], w, preferred_element_type=F32).astype(o_ref.dtype)


def _norm_mm(x, vec, w, n_cols, tm, tn, emit_h=False, side=(), keep_w=False):
    m = x.shape[0]
    nj = n_cols // tn
    out_specs = [pl.BlockSpec((tm, tn), lambda i, j: (i, j))]
    out_shape = [jax.ShapeDtypeStruct((m, n_cols), BF16)]
    scratch = []
    if emit_h:
        out_specs.append(pl.BlockSpec((tm, D_MODEL), lambda i, j: (i, 0)))
        out_shape.append(jax.ShapeDtypeStruct((m, D_MODEL), BF16))
    else:
        scratch.append(pltpu.VMEM((tm, D_MODEL), BF16))
    if keep_w:
        scratch.append(pltpu.VMEM((nj, D_MODEL, tn), BF16))
        w_spec = pl.BlockSpec((1, D_MODEL, tn), lambda i, j: (0, 0, jnp.where(i == 0, j, nj - 1)))
    else:
        w_spec = pl.BlockSpec((1, D_MODEL, tn), lambda i, j: (0, 0, j))
    side_in, side_out, side_shapes = _side_cast_specs(side, (m // tm) * nj, lambda i, j: i * nj + j)
    return pl.pallas_call(
        functools.partial(_norm_mm_kernel, emit_h=emit_h, n_side=len(side), keep_w=keep_w),
        grid=(m // tm, nj),
        in_specs=[pl.BlockSpec((tm, D_MODEL), lambda i, j: (i, 0)),
                  pl.BlockSpec((8, D_MODEL), lambda i, j: (0, 0)), w_spec] + side_in,
        out_specs=out_specs + side_out, out_shape=out_shape + side_shapes,
        scratch_shapes=scratch,
        compiler_params=_params(("arbitrary", "arbitrary"), 56),
        name="norm_mm",
    )(x, vec, w, *[s[0] for s in side])


def _seg_cumsum(g, row_in_chunk):
    p = g
    s = 1
    while s < CHUNK:
        p = p + jnp.where(row_in_chunk >= s, pltpu.roll(p, s, axis=0), 0.0)
        s *= 2
    return p


def _gla_state_kernel(*refs, nchunk, n_side, d0):
    f_ref, i_ref, lb_ref, s0_ref = refs[:4]
    side_in = refs[4:4 + n_side]
    senter_ref, sfin_ref = refs[4 + n_side:6 + n_side]
    side_out = refs[6 + n_side:6 + 2 * n_side]
    s_acc, p_buf, g_buf, k_buf = refs[6 + 2 * n_side:]
    _side_cast(side_in, side_out)
    d = pl.program_id(0) + d0

    @pl.when(pl.program_id(2) == 0)
    def _():
        s_acc[...] = s0_ref[0, 0]

    rows = f_ref.shape[0]
    lb = lb_ref[0, 0]
    f = lb + (1.0 - lb) * jax.nn.sigmoid(f_ref[...].astype(F32))
    g = jnp.log(f) * LOG2E
    ric = lax.broadcasted_iota(jnp.int32, (rows, LANES), 0) & (CHUNK - 1)
    p_buf[...] = _seg_cumsum(g, ric)
    g_buf[...] = g
    k_buf[...] = 1.0 - f

    s = s_acc[...]
    for j in range(nchunk):
        c = jnp.where(d == 0, j, nchunk - 1 - j)
        base = pl.multiple_of(c * CHUNK, CHUNK)
        sl = pl.ds(base, CHUNK)
        p = p_buf[sl, :]
        tot = p_buf[pl.ds(base + CHUNK - 1, 1), :]
        ex = jnp.where(d == 0, tot - p, p - g_buf[sl, :])
        kt = (k_buf[sl, :] * jnp.exp2(ex)).astype(BF16)
        u = lax.dot_general(i_ref[sl, :], kt, (((0,), (0,)), ((), ())), preferred_element_type=F32)
        senter_ref[0, 0, c] = s.astype(BF16)
        s = s * jnp.exp2(tot) + u
    s_acc[...] = s
    sfin_ref[0, 0] = s


def _gla_states(z, lb4, s0, rows_per_block, side=(), d0=0, nd=2):
    m = z.shape[0]
    nb = m // rows_per_block
    nchunk = rows_per_block // CHUNK

    def rb(d, h, b):
        return b + (d + d0) * (nb - 1 - 2 * b)

    side_in, side_out, side_shapes = _side_cast_specs(
        side, nd * HG_HEADS * nb, lambda d, h, b: (d * HG_HEADS + h) * nb + b)
    return pl.pallas_call(
        functools.partial(_gla_state_kernel, nchunk=nchunk, n_side=len(side), d0=d0),
        grid=(nd, HG_HEADS, nb),
        in_specs=[pl.BlockSpec((rows_per_block, HG_DK), lambda d, h, b: (rb(d, h, b), (d + d0) * HG_HEADS + h)),
                  pl.BlockSpec((rows_per_block, HG_DK), lambda d, h, b: (rb(d, h, b), 2 * HG_HEADS + h)),
                  pl.BlockSpec((1, 1, 1, HG_DK), lambda d, h, b: (d + d0, h, 0, 0)),
                  pl.BlockSpec((1, 1, HG_DK, HG_DK), lambda d, h, b: (d + d0, h, 0, 0))] + side_in,
        out_specs=[pl.BlockSpec((1, 1, nchunk, HG_DK, HG_DK), lambda d, h, b: (d, h, rb(d, h, b), 0, 0)),
                   pl.BlockSpec((1, 1, HG_DK, HG_DK), lambda d, h, b: (d, h, 0, 0))] + side_out,
        out_shape=[jax.ShapeDtypeStruct((nd, HG_HEADS, m // CHUNK, HG_DK, HG_DK), BF16),
                   jax.ShapeDtypeStruct((nd, HG_HEADS, HG_DK, HG_DK), F32)] + side_shapes,
        scratch_shapes=[pltpu.VMEM((HG_DK, HG_DK), F32),
                        pltpu.VMEM((rows_per_block, HG_DK), F32),
                        pltpu.VMEM((rows_per_block, HG_DK), F32),
                        pltpu.VMEM((rows_per_block, HG_DK), F32)],
        compiler_params=_params(("parallel", "parallel", "arbitrary"), 32),
        name="gla_states",
    )(z, z, lb4, s0, *[w for w, _ in side])


def _gla_local_kernel(*refs, n_side):
    h_ref, wgu_ref, wv_ref, ff_ref, fb_ref, i_ref, q_ref, lb_ref = refs[:8]
    side_in = refs[8:8 + n_side]
    zb_ref, oi_ref, qe_ref, u_ref, d_ref = refs[8 + n_side:13 + n_side]
    side_out = refs[13 + n_side:13 + 2 * n_side]
    bufs = refs[13 + 2 * n_side:]
    _side_cast(side_in, side_out)
    rows = h_ref.shape[0]
    nchunk = rows // CHUNK
    row_parts = rows // PROJ_ROWS

    def project(piece):
        lo = (piece // row_parts) * MXU_COLS
        rs = slice((piece % row_parts) * PROJ_ROWS, (piece % row_parts + 1) * PROJ_ROWS)
        w_ref, off = (wgu_ref, lo) if lo < 2 * HG_WIDTH else (wv_ref, lo - 2 * HG_WIDTH)
        zb_ref[rs, lo:lo + MXU_COLS] = jnp.dot(h_ref[rs, :], w_ref[:, off:off + MXU_COLS],
                                               preferred_element_type=F32).astype(BF16)

    n_pieces = row_parts * 3 * HG_WIDTH // MXU_COLS
    pieces_before = [(k + 1) * n_pieces // (HG_HEADS * nchunk) - k * n_pieces // (HG_HEADS * nchunk)
                     for k in range(HG_HEADS * nchunk)]

    ric = lax.broadcasted_iota(jnp.int32, (rows, LANES), 0) & (CHUNK - 1)
    blk = lax.broadcasted_iota(jnp.int32, (CHUNK, LANES), 0) // SUB
    r2 = lax.broadcasted_iota(jnp.int32, (CHUNK, CHUNK), 0)
    c2 = lax.broadcasted_iota(jnp.int32, (CHUNK, CHUNK), 1)
    nt = (((1,), (1,)), ((), ()))
    tn_dims = (((0,), (0,)), ((), ()))
    nsub = CHUNK // SUB
    emitted = [0]

    def before_chunk(k):
        for _ in range(pieces_before[k]):
            project(emitted[0])
            emitted[0] += 1

    for hd in range(HG_HEADS):
        _gla_local_head(hd, nchunk, ric, blk, r2, c2, nt, tn_dims, nsub, ff_ref, fb_ref, i_ref, q_ref, lb_ref,
                        oi_ref, qe_ref, u_ref, d_ref, bufs[6 * (hd % 2):6 * (hd % 2) + 6], before_chunk)
    assert emitted[0] == n_pieces


def _gla_local_head(hd, nchunk, ric, blk, r2, c2, nt, tn_dims, nsub, ff_ref, fb_ref, i_ref, q_ref, lb_ref,
                    oi_ref, qe_ref, u_ref, d_ref, bufs, before_chunk):
    pf_buf, pb_buf, gb_buf, kf_buf, kb_buf, q_buf = bufs
    cs = slice(hd * HG_DK, (hd + 1) * HG_DK)
    lbf = lb_ref[0, hd]
    lbb = lb_ref[1, hd]
    ff = lbf + (1.0 - lbf) * jax.nn.sigmoid(ff_ref[:, cs].astype(F32))
    pf_buf[...] = _seg_cumsum(jnp.log(ff) * LOG2E, ric)
    kf_buf[...] = 1.0 - ff
    fb = lbb + (1.0 - lbb) * jax.nn.sigmoid(fb_ref[:, cs].astype(F32))
    gb = jnp.log(fb) * LOG2E
    pb_buf[...] = _seg_cumsum(gb, ric)
    gb_buf[...] = gb
    kb_buf[...] = 1.0 - fb
    q_buf[...] = _silu(q_ref[:, cs].astype(F32)) * (HG_DK ** -0.5)

    for c in range(nchunk):
        before_chunk(hd * nchunk + c)
        base = c * CHUNK
        sl = pl.ds(base, CHUNK)
        pf = pf_buf[sl, :]
        pb = pb_buf[sl, :]
        gbc = gb_buf[sl, :]
        kf = kf_buf[sl, :]
        kb = kb_buf[sl, :]
        q = q_buf[sl, :]
        totb = pb_buf[pl.ds(base + CHUNK - 1, 1), :]
        suf = totb - pb + gbc
        mids, nids = [], []
        for i in range(nsub):
            mids.append(pf_buf[pl.ds(base + SUB * i + SUB // 2 - 1, 1), :])
            rmid = pl.ds(base + SUB * i + SUB // 2, 1)
            nids.append(totb - pb_buf[rmid, :] + gb_buf[rmid, :])
        mid_rows = jnp.concatenate([jnp.broadcast_to(m, (SUB, LANES)) for m in mids], axis=0)
        nid_rows = jnp.concatenate([jnp.broadcast_to(m, (SUB, LANES)) for m in nids], axis=0)
        qf_all = q * jnp.exp2(jnp.minimum(pf - mid_rows, EXP_CLAMP))
        qb_all = q * jnp.exp2(jnp.minimum(suf - nid_rows, EXP_CLAMP))
        qf_seg, kf_seg, qb_seg, kb_seg = [], [], [], []
        for i in range(nsub):
            qf_seg.append(jnp.where(blk == i, qf_all, 0.0))
            qb_seg.append(jnp.where(blk == i, qb_all, 0.0))
            hi = SUB * (i + 1)
            top = kf[:hi] * jnp.exp2(jnp.minimum(mids[i] - pf[:hi], EXP_CLAMP))
            kf_seg.append(top if hi == CHUNK else
                          jnp.concatenate([top, jnp.zeros((CHUNK - hi, LANES), F32)], axis=0))
            lo = SUB * i
            bot = kb[lo:] * jnp.exp2(jnp.minimum(nids[i] - suf[lo:], EXP_CLAMP))
            kb_seg.append(bot if lo == 0 else
                          jnp.concatenate([jnp.zeros((lo, LANES), F32), bot], axis=0))
        qf_big = jnp.concatenate(qf_seg, axis=1).astype(BF16)
        kf_big = jnp.concatenate(kf_seg, axis=1).astype(BF16)
        qb_big = jnp.concatenate(qb_seg, axis=1).astype(BF16)
        kb_big = jnp.concatenate(kb_seg, axis=1).astype(BF16)
        sc_f = lax.dot_general(qf_big, kf_big, nt, preferred_element_type=F32)
        sc_b = lax.dot_general(qb_big, kb_big, nt, preferred_element_type=F32)
        a = (jnp.where(c2 <= r2, sc_f, 0.0) + jnp.where(c2 >= r2, sc_b, 0.0)).astype(BF16)
        v = i_ref[sl, cs]
        oi_ref[sl, cs] = jnp.dot(a, v, preferred_element_type=F32).astype(BF16)
        qe_ref[sl, 2 * hd * HG_DK:2 * (hd + 1) * HG_DK] = jnp.concatenate(
            [q * jnp.exp2(pf), q * jnp.exp2(suf)], axis=1).astype(BF16)
        totf = pf_buf[pl.ds(base + CHUNK - 1, 1), :]
        kt = jnp.concatenate([kf * jnp.exp2(totf - pf), kb * jnp.exp2(pb - gbc)], axis=1).astype(BF16)
        u = lax.dot_general(v, kt, tn_dims, preferred_element_type=F32).astype(BF16)
        u_ref[0, hd, c] = u[:, :HG_DK]
        u_ref[1, hd, c] = u[:, HG_DK:]
        d_ref[0, hd, c * SUBLANES:(c + 1) * SUBLANES, :] = jnp.broadcast_to(jnp.exp2(totf), (SUBLANES, HG_DK))
        d_ref[1, hd, c * SUBLANES:(c + 1) * SUBLANES, :] = jnp.broadcast_to(jnp.exp2(totb), (SUBLANES, HG_DK))


def _gla_local(h, w_gu, w_v, z_a, lb4, rows_per_block, side=()):
    m = h.shape[0]
    nb = m // rows_per_block
    nchunk = rows_per_block // CHUNK
    n_chunks = m // CHUNK
    resident = lambda a: pl.BlockSpec(a.shape, lambda b: (0,) * a.ndim, pipeline_mode=pl.Buffered(1))
    col = lambda k: pl.BlockSpec((rows_per_block, HG_WIDTH), lambda b, k=k: (b, k))
    side_in, side_out, side_shapes = _side_cast_specs(side, nb, lambda b: b)
    return pl.pallas_call(
        functools.partial(_gla_local_kernel, n_side=len(side)),
        grid=(nb,),
        in_specs=[pl.BlockSpec((rows_per_block, D_MODEL), lambda b: (b, 0)),
                  resident(w_gu), resident(w_v), col(0), col(1), col(2), col(3), resident(lb4)] + side_in,
        out_specs=[pl.BlockSpec((rows_per_block, 3 * HG_WIDTH), lambda b: (b, 0)),
                   pl.BlockSpec((rows_per_block, HG_WIDTH), lambda b: (b, 0)),
                   pl.BlockSpec((rows_per_block, 2 * HG_WIDTH), lambda b: (b, 0)),
                   pl.BlockSpec((2, HG_HEADS, nchunk, HG_DK, HG_DK), lambda b: (0, 0, b, 0, 0)),
                   pl.BlockSpec((2, HG_HEADS, nchunk * SUBLANES, HG_DK), lambda b: (0, 0, b, 0))] + side_out,
        out_shape=[jax.ShapeDtypeStruct((m, 3 * HG_WIDTH), BF16),
                   jax.ShapeDtypeStruct((m, HG_WIDTH), BF16),
                   jax.ShapeDtypeStruct((m, 2 * HG_WIDTH), BF16),
                   jax.ShapeDtypeStruct((2, HG_HEADS, n_chunks, HG_DK, HG_DK), BF16),
                   jax.ShapeDtypeStruct((2, HG_HEADS, n_chunks * SUBLANES, HG_DK), F32)] + side_shapes,
        scratch_shapes=[pltpu.VMEM((rows_per_block, HG_DK), F32) for _ in range(12)],
        compiler_params=_params(("parallel",), 56),
        name="gla_local",
    )(h, w_gu, w_v, z_a, z_a, z_a, z_a, lb4, *[s[0] for s in side])


def _gla_scan_kernel(u_ref, d_ref, s0_ref, se_ref, s_acc, *, nchunk):
    d = pl.program_id(0)

    @pl.when(pl.program_id(2) == 0)
    def _():
        s_acc[...] = s0_ref[0, 0]

    s = s_acc[...]
    for j in range(nchunk):
        c = jnp.where(d == 0, j, nchunk - 1 - j)
        se_ref[0, 0, c] = s.astype(BF16)
        s = s * d_ref[0, 0, pl.ds(pl.multiple_of(c * SUBLANES, SUBLANES), 1), :] + u_ref[0, 0, c].astype(F32)
    s_acc[...] = s


def _gla_scan(u, dec, s0, nchunk):
    n_chunks = u.shape[2]
    nb = n_chunks // nchunk

    def rb(d, h, b):
        return b + d * (nb - 1 - 2 * b)

    return pl.pallas_call(
        functools.partial(_gla_scan_kernel, nchunk=nchunk),
        grid=(2, HG_HEADS, nb),
        in_specs=[pl.BlockSpec((1, 1, nchunk, HG_DK, HG_DK), lambda d, h, b: (d, h, rb(d, h, b), 0, 0)),
                  pl.BlockSpec((1, 1, nchunk * SUBLANES, HG_DK), lambda d, h, b: (d, h, rb(d, h, b), 0)),
                  pl.BlockSpec((1, 1, HG_DK, HG_DK), lambda d, h, b: (d, h, 0, 0))],
        out_specs=pl.BlockSpec((1, 1, nchunk, HG_DK, HG_DK), lambda d, h, b: (d, h, rb(d, h, b), 0, 0)),
        out_shape=jax.ShapeDtypeStruct(u.shape, BF16),
        scratch_shapes=[pltpu.VMEM((HG_DK, HG_DK), F32)],
        compiler_params=_params(("parallel", "parallel", "arbitrary"), 32),
        name="gla_scan",
    )(u, dec, s0)


def _gla_finish_kernel(*refs, nchunk, n_chunks, n_side):
    oi_ref, qe_ref, gt_ref, gain_ref, u_ref, d_ref, s0_ref = refs[:7]
    side_in = refs[7:7 + n_side]
    o_ref = refs[7 + n_side]
    side_out = refs[8 + n_side:8 + 2 * n_side]
    s_all = refs[-1]
    _side_cast(side_in, side_out)

    @pl.when(pl.program_id(1) == 0)
    def _():
        def step(c, carry):
            sf, sb = carry
            cb = n_chunks - 1 - c
            s_all[0, c] = sf.astype(BF16)
            s_all[1, cb] = sb.astype(BF16)
            df = d_ref[0, 0, pl.ds(pl.multiple_of(c * SUBLANES, SUBLANES), 1), :]
            db = d_ref[1, 0, pl.ds(pl.multiple_of(cb * SUBLANES, SUBLANES), 1), :]
            return (sf * df + u_ref[0, 0, c].astype(F32), sb * db + u_ref[1, 0, cb].astype(F32))

        lax.fori_loop(0, n_chunks, step, (s0_ref[0, 0], s0_ref[1, 0]), unroll=4)

    nt = (((1,), (1,)), ((), ()))
    gain = gain_ref[0]
    first = pl.program_id(1) * nchunk
    for c in range(nchunk):
        sl = slice(c * CHUNK, (c + 1) * CHUNK)
        scat = jnp.concatenate([s_all[0, first + c], s_all[1, first + c]], axis=1)
        o = oi_ref[sl, :].astype(F32) + lax.dot_general(qe_ref[sl, :], scat, nt, preferred_element_type=F32)
        o = _rms(o) * gain
        o_ref[sl, :] = (o * _silu(gt_ref[sl, :].astype(F32))).astype(BF16)


def _gla_finish(oi, qe, z_b, hg_gain, s_inc, s_dec, s0, rows_per_block, side=()):
    m = oi.shape[0]
    nb = m // rows_per_block
    nchunk = rows_per_block // CHUNK
    n_chunks = m // CHUNK
    side_in, side_out, side_shapes = _side_cast_specs(side, HG_HEADS * nb, lambda h, b: h * nb + b)
    return pl.pallas_call(
        functools.partial(_gla_finish_kernel, nchunk=nchunk, n_chunks=n_chunks, n_side=len(side)),
        grid=(HG_HEADS, nb),
        in_specs=[pl.BlockSpec((rows_per_block, HG_DK), lambda h, b: (b, h)),
                  pl.BlockSpec((rows_per_block, 2 * HG_DK), lambda h, b: (b, h)),
                  pl.BlockSpec((rows_per_block, HG_DK), lambda h, b: (b, h)),
                  pl.BlockSpec((1, 1, HG_DK), lambda h, b: (h, 0, 0)),
                  pl.BlockSpec((2, 1, n_chunks, HG_DK, HG_DK), lambda h, b: (0, h, 0, 0, 0)),
                  pl.BlockSpec((2, 1, n_chunks * SUBLANES, HG_DK), lambda h, b: (0, h, 0, 0)),
                  pl.BlockSpec((2, 1, HG_DK, HG_DK), lambda h, b: (0, h, 0, 0))] + side_in,
        out_specs=[pl.BlockSpec((rows_per_block, HG_DK), lambda h, b: (b, h))] + side_out,
        out_shape=[jax.ShapeDtypeStruct((m, HG_WIDTH), BF16)] + side_shapes,
        scratch_shapes=[pltpu.VMEM((2, n_chunks, HG_DK, HG_DK), BF16)],
        compiler_params=_params(("parallel", "arbitrary"), 48),
        name="gla_finish",
    )(oi, qe, z_b, hg_gain.reshape(HG_HEADS, 1, HG_DK), s_inc, s_dec, s0, *[s[0] for s in side])


def _spatial_kernel(*refs, n_side):
    u_ref, v_ref, w_ref, bias_ref, lng_ref, lnb_ref = refs[:6]
    side_in = refs[6:6 + n_side]
    o_ref = refs[6 + n_side]
    side_out = refs[7 + n_side:7 + 2 * n_side]
    y_buf = refs[-1]
    _side_cast(side_in, side_out)
    v = _gelu_tanh(v_ref[...].astype(F32))
    xc = v - jnp.mean(v, axis=-1, keepdims=True)
    y = xc * lax.rsqrt(jnp.mean(xc * xc, axis=-1, keepdims=True) + EPS) * lng_ref[...] + lnb_ref[...]
    y_buf[...] = y.astype(BF16)
    for n in range(u_ref.shape[0] // SG_CHUNK):
        rs = slice(n * SG_CHUNK, (n + 1) * SG_CHUNK)
        for g in range(SG_GROUPS):
            cs = slice(g * SG_CH, (g + 1) * SG_CH)
            mixed = jnp.dot(w_ref[g], y_buf[rs, cs], preferred_element_type=F32) + bias_ref[:, cs]
            o_ref[rs, cs] = (_gelu_tanh(u_ref[rs, cs].astype(F32)) * mixed).astype(BF16)


def _spatial(z_b, sg_w, bias_full, ln_g, ln_b, rows_per_block, side=()):
    m = z_b.shape[0]
    nb = m // rows_per_block
    side_in, side_out, side_shapes = _side_cast_specs(side, nb, lambda b: b)
    return pl.pallas_call(
        functools.partial(_spatial_kernel, n_side=len(side)),
        grid=(nb,),
        in_specs=[pl.BlockSpec((rows_per_block, SG_WIDTH), lambda b: (b, 1)),
                  pl.BlockSpec((rows_per_block, SG_WIDTH), lambda b: (b, 2)),
                  pl.BlockSpec((SG_GROUPS, SG_CHUNK, SG_CHUNK), lambda b: (0, 0, 0)),
                  pl.BlockSpec((SG_CHUNK, SG_WIDTH), lambda b: (0, 0)),
                  pl.BlockSpec((1, SG_WIDTH), lambda b: (0, 0)),
                  pl.BlockSpec((1, SG_WIDTH), lambda b: (0, 0))] + side_in,
        out_specs=[pl.BlockSpec((rows_per_block, SG_WIDTH), lambda b: (b, 0))] + side_out,
        out_shape=[jax.ShapeDtypeStruct((m, SG_WIDTH), BF16)] + side_shapes,
        scratch_shapes=[pltpu.VMEM((rows_per_block, SG_WIDTH), BF16)],
        compiler_params=_params(("parallel",), 48),
        name="spatial_gate",
    )(z_b, z_b, sg_w, bias_full, ln_g, ln_b, *[s[0] for s in side])


def _residual_epilogue(mix, x_ref, vec_ref, xo_ref, ho_ref, rs):
    xn = x_ref[rs, :] + _rms(mix) * vec_ref[0:1]
    xo_ref[rs, :] = xn
    if ho_ref is not None:
        hn = (_rms(xn) * vec_ref[1:2] + vec_ref[2:3]).astype(BF16)
        outs = ho_ref if isinstance(ho_ref, (tuple, list)) else (ho_ref,)
        width = hn.shape[1] // len(outs)
        for k, out in enumerate(outs):
            out[rs, :] = hn[:, k * width:(k + 1) * width]


def _mm_res_kernel(*refs, n_a, emit_h):
    a_refs, w_refs = refs[:n_a], refs[n_a:2 * n_a]
    x_ref, vec_ref, xo_ref = refs[2 * n_a], refs[2 * n_a + 1], refs[2 * n_a + 2]
    ho_ref = refs[2 * n_a + 3] if emit_h else None
    for rs in _row_subblocks(x_ref.shape[0], 128):
        mix = jnp.dot(a_refs[0][rs, :], w_refs[0][...], preferred_element_type=F32)
        for a_ref, w_ref in zip(a_refs[1:], w_refs[1:]):
            mix = mix + jnp.dot(a_ref[rs, :], w_ref[...], preferred_element_type=F32)
        _residual_epilogue(mix, x_ref, vec_ref, xo_ref, ho_ref, rs)


def _mm_res(a_list, w, x, vec, tm, emit_h=True):
    m = x.shape[0]
    n_a = len(a_list)
    w_list = [w] * n_a
    offs = np.cumsum([0] + [a.shape[1] for a in a_list])
    in_specs = ([pl.BlockSpec((tm, a.shape[1]), lambda i: (i, 0)) for a in a_list]
                + [pl.BlockSpec((a.shape[1], D_MODEL), lambda i, k=int(o) // a.shape[1]: (k, 0))
                   for a, o in zip(a_list, offs)]
                + [pl.BlockSpec((tm, D_MODEL), lambda i: (i, 0)),
                   pl.BlockSpec((8, D_MODEL), lambda i: (0, 0))])
    out_specs = [pl.BlockSpec((tm, D_MODEL), lambda i: (i, 0))]
    out_shape = [jax.ShapeDtypeStruct((m, D_MODEL), F32)]
    if emit_h:
        out_specs.append(pl.BlockSpec((tm, D_MODEL), lambda i: (i, 0)))
        out_shape.append(jax.ShapeDtypeStruct((m, D_MODEL), BF16))
    return pl.pallas_call(
        functools.partial(_mm_res_kernel, n_a=n_a, emit_h=emit_h),
        grid=(m // tm,),
        in_specs=in_specs, out_specs=out_specs, out_shape=out_shape,
        compiler_params=_params(("parallel",), 56),
        name="mm_residual",
    )(*a_list, *w_list, x, vec)


def _mlp_kernel(*refs, n_h, n_side):
    h_ref, w1_ref, w2_ref, x_ref, vec_ref = refs[:5]
    side_in = refs[5:5 + n_side]
    xo_ref = refs[5 + n_side]
    ho_ref = tuple(refs[6 + n_side:6 + n_side + n_h]) or None
    n_out = 1 + n_h
    side_out = refs[5 + n_side + n_out:5 + 2 * n_side + n_out]
    acc_ref = refs[-1]
    _side_cast(side_in, side_out)
    j = pl.program_id(1)

    last = pl.num_programs(1) - 1

    def partial_product(rs):
        a = jnp.dot(h_ref[rs, :], w1_ref[...], preferred_element_type=F32)
        a = jnp.square(jnp.maximum(a, 0.0)).astype(BF16)
        return jnp.dot(a, w2_ref[...], preferred_element_type=F32)

    @pl.when(j == 0)
    def _():
        for rs in _row_subblocks(h_ref.shape[0]):
            acc_ref[rs, :] = partial_product(rs)

    @pl.when(jnp.logical_and(j > 0, j < last))
    def _():
        for rs in _row_subblocks(h_ref.shape[0]):
            acc_ref[rs, :] += partial_product(rs)

    @pl.when(j == last)
    def _():
        for rs in _row_subblocks(h_ref.shape[0]):
            _residual_epilogue(acc_ref[rs, :] + partial_product(rs), x_ref, vec_ref, xo_ref, ho_ref, rs)


def _mlp(h, w1, w2, x, vec, tm, tf, n_h, side=()):
    m = x.shape[0]
    nj = D_FF // tf
    out_specs = [pl.BlockSpec((tm, D_MODEL), lambda i, j: (i, 0))]
    out_shape = [jax.ShapeDtypeStruct((m, D_MODEL), F32)]
    for _ in range(n_h):
        out_specs.append(pl.BlockSpec((tm, D_MODEL // n_h), lambda i, j: (i, 0)))
        out_shape.append(jax.ShapeDtypeStruct((m, D_MODEL // n_h), BF16))
    side_in, side_out, side_shapes = _side_cast_specs(side, (m // tm) * nj, lambda i, j: i * nj + j)
    return pl.pallas_call(
        functools.partial(_mlp_kernel, n_h=n_h, n_side=len(side)),
        grid=(m // tm, nj),
        in_specs=[pl.BlockSpec((tm, D_MODEL), lambda i, j: (i, 0)),
                  pl.BlockSpec((D_MODEL, tf), lambda i, j: (0, j)),
                  pl.BlockSpec((tf, D_MODEL), lambda i, j: (j, 0)),
                  pl.BlockSpec((tm, D_MODEL), lambda i, j: (i, 0)),
                  pl.BlockSpec((8, D_MODEL), lambda i, j: (0, 0))] + side_in,
        out_specs=out_specs + side_out, out_shape=out_shape + side_shapes,
        scratch_shapes=[pltpu.VMEM((tm, D_MODEL), F32)],
        compiler_params=_params(("parallel", "arbitrary"), 56),
        name="mlp",
    )(h, w1, w2, x, vec, *[w for w, _ in side])


def _dft_constants():
    def cs(n):
        k = np.arange(n)
        ang = 2.0 * np.pi * ((k[:, None] * k[None, :]) % n) / n
        return np.cos(ang) / np.sqrt(n), np.sin(ang) / np.sqrt(n)

    cc, sc = cs(FT_CH)
    c1, s1 = cs(FFT_L1)
    m1 = np.concatenate([c1, -s1], axis=0)
    c2, s2 = cs(FFT_L2)
    m2 = np.block([[c2, s2], [-s2, c2]])
    u1 = np.arange(FFT_L1)[:, None]
    t2 = np.arange(FFT_L2)[None, :]
    ang = 2.0 * np.pi * ((u1 * t2) % SEQ) / SEQ
    tw_c = np.cos(ang).reshape(SEQ, 1)
    tw_s = np.sin(ang).reshape(SEQ, 1)
    f = lambda a: jnp.asarray(a, F32)
    return f(cc), f(sc), f(m1), f(m2), f(tw_c), f(tw_s)


def _swap_row_factors(a, n_outer, n_inner):
    return a.reshape(n_outer, n_inner, a.shape[1]).transpose(1, 0, 2).reshape(a.shape)


def _pos_dft1_kernel(x_ref, m_ref, zr_ref, zi_ref):
    for k in range(x_ref.shape[0] // FFT_L1):
        sl = slice(k * FFT_L1, (k + 1) * FFT_L1)
        z = jnp.dot(m_ref[...], x_ref[sl, :], preferred_element_type=F32)
        zr_ref[sl, :] = z[:FFT_L1].astype(BF16)
        zi_ref[sl, :] = z[FFT_L1:].astype(BF16)


def _pos_dft1(xt, m1, slabs):
    rows = slabs * FFT_L1
    blk = pl.BlockSpec((rows, xt.shape[1]), lambda j: (j, 0))
    return pl.pallas_call(
        _pos_dft1_kernel,
        grid=(SEQ // rows,),
        in_specs=[blk, pl.BlockSpec((2 * FFT_L1, FFT_L1), lambda j: (0, 0))],
        out_specs=[blk, blk],
        out_shape=[jax.ShapeDtypeStruct(xt.shape, BF16)] * 2,
        compiler_params=_params(("parallel",), 32),
        name="pos_dft1",
    )(xt, m1)


def _pos_dft2_kernel(zr_ref, zi_ref, c_ref, s_ref, m_ref, wc_ref, ws_ref, y_ref, yr_buf, yi_buf):
    for k in range(zr_ref.shape[0] // FFT_L2):
        sl = slice(k * FFT_L2, (k + 1) * FFT_L2)
        zr = zr_ref[sl, :].astype(F32)
        zi = zi_ref[sl, :].astype(F32)
        c = c_ref[sl, :]
        s = s_ref[sl, :]
        t = jnp.concatenate([(zr * c + zi * s).astype(BF16), (zi * c - zr * s).astype(BF16)], axis=0)
        y = jnp.dot(m_ref[...], t, preferred_element_type=F32)
        yr_buf[sl, :] = y[:FFT_L2].astype(BF16)
        yi_buf[sl, :] = y[FFT_L2:].astype(BF16)
    for g in range(zr_ref.shape[1] // FT_CH):
        cs = slice(g * FT_CH, (g + 1) * FT_CH)
        y_ref[:, cs] = (jnp.dot(yr_buf[:, cs], wc_ref[...], preferred_element_type=F32)
                        + jnp.dot(yi_buf[:, cs], ws_ref[...], preferred_element_type=F32)).astype(BF16)


def _pos_dft2(zr, zi, tw_c, tw_s, m2, wc, ws, slabs):
    rows = slabs * FFT_L2
    width = zr.shape[1]
    blk = pl.BlockSpec((rows, width), lambda j: (j, 0))
    tw = pl.BlockSpec((rows, 1), lambda j: (j, 0))
    const = lambda a: pl.BlockSpec(a.shape, lambda j: (0, 0))
    return pl.pallas_call(
        _pos_dft2_kernel,
        grid=(SEQ // rows,),
        in_specs=[blk, blk, tw, tw, const(m2), const(wc), const(ws)],
        out_specs=blk,
        out_shape=jax.ShapeDtypeStruct(zr.shape, BF16),
        scratch_shapes=[pltpu.VMEM((rows, width), BF16), pltpu.VMEM((rows, width), BF16)],
        compiler_params=_params(("parallel",), 40),
        name="pos_dft2",
    )(zr, zi, tw_c, tw_s, m2, wc, ws)


def _fourier_mix(h_parts):
    cc, sc, m1, m2, tw_c, tw_s = _dft_constants()
    m1, m2, cc, sc = m1.astype(BF16), m2.astype(BF16), cc.astype(BF16), sc.astype(BF16)
    hts = [_swap_row_factors(h, FFT_L1, FFT_L2) for h in h_parts]
    zs = [_pos_dft1(ht, m1, 16) for ht in hts]
    zs = [[_swap_row_factors(z, FFT_L2, FFT_L1) for z in zri] for zri in zs]
    yps = [_pos_dft2(zr, zi, tw_c, tw_s, m2, cc, sc, 8) for zr, zi in zs]
    return [_swap_row_factors(yp, FFT_L1, FFT_L2) for yp in yps]


def _pack_rows(*rows):
    rows = [r.reshape(1, D_MODEL).astype(F32) for r in rows]
    rows += [jnp.zeros((1, D_MODEL), F32)] * (8 - len(rows))
    return jnp.concatenate(rows, axis=0)


def kernel(x, c, ctx, c_ctx, w_ada, b_ada, norm_gain, w_in, w_out, lb_raw, hg_norm_gain,
           sg_w, sg_b, sg_ln_gain, sg_ln_bias, w_fourier, w_mlp_in, w_mlp_out):
    assert x.shape == (1, SEQ, D_MODEL) and ctx.shape == (1, CTX_LEN, D_MODEL)
    xs = x.reshape(SEQ, D_MODEL)
    cs = ctx.reshape(CTX_LEN, D_MODEL)

    mods = _ada_mods(jnp.concatenate([c.reshape(1, D_MODEL), c_ctx.reshape(1, D_MODEL)], axis=0), w_ada, b_ada)
    mod = lambda l, r, j: mods[l, r, j * D_MODEL:(j + 1) * D_MODEL]

    lb = jnp.cumsum(jax.nn.softmax(lb_raw.astype(F32), axis=1), axis=1)[:, 0]
    lb4 = lb.reshape(2, HG_HEADS, 1, HG_DK)

    pre = lambda gain, l, r, j: _pack_rows(gain * (1.0 + mod(l, r, j + 1)), mod(l, r, j))
    post = lambda gain, l, j, *nxt: _pack_rows(mod(l, 0, j) * gain, *nxt)

    z_a, h0, w_gu, w_v = _norm_mm(xs, pre(norm_gain[0, 0], 0, 0, 0), w_in, 4 * HG_WIDTH, 512, 1024, emit_h=True,
                                  side=[(w_in, 0, 2 * HG_WIDTH, 2), (w_in, 0, HG_WIDTH, 6)], keep_w=True)
    (z_ctx,) = _norm_mm(cs, pre(norm_gain[0, 0], 0, 1, 0), w_in, 3 * HG_WIDTH, CTX_LEN, 1024)
    zero_state = jnp.zeros((2, HG_HEADS, HG_DK, HG_DK), F32)
    _, s_ctx = _gla_states(z_ctx, lb4, zero_state, CTX_LEN)
    z_b, o_intra, q_dec, s_inc, s_dec, w_out0, w1_0, w2_0 = _gla_local(
        h0, w_gu, w_v, z_a, lb4, 256, side=[(w_out, 0), (w_mlp_in, 0), (w_mlp_out, 0)])
    o, w_fou = _gla_finish(o_intra, q_dec, z_b, hg_norm_gain[0], s_inc, s_dec, s_ctx, 4 * GLA_ROWS,
                           side=[(w_fourier, 0)])
    bias_full = jnp.repeat(sg_b[0].T.astype(F32), SG_CH, axis=1)
    (s,) = _spatial(z_b, sg_w[0].astype(BF16), bias_full, sg_ln_gain[0].reshape(1, SG_WIDTH),
                    sg_ln_bias[0].reshape(1, SG_WIDTH), 512)
    vec = post(norm_gain[0, 1], 0, 2, *pre(norm_gain[0, 2], 0, 0, 3)[:2])
    x1, h = _mm_res([o, s], w_out0, xs, vec, 512)
    vec = post(norm_gain[0, 3], 0, 5, *pre(norm_gain[1, 0], 1, 0, 0)[:2])
    x2, h_a, h_b, w1_1, w2_1 = _mlp(h, w1_0, w2_0, x1, vec, 512, 1024, 2,
                                    side=[(w_mlp_in, 1), (w_mlp_out, 1)])

    y_parts = _fourier_mix([h_a, h_b])
    vec = post(norm_gain[1, 1], 1, 2, *pre(norm_gain[1, 2], 1, 0, 3)[:2])
    x3, h = _mm_res(y_parts, w_fou, x2, vec, 512)
    (x4,) = _mlp(h, w1_1, w2_1, x3, post(norm_gain[1, 3], 1, 5), 512, 1024, 0)
    return x4.reshape(1, SEQ, D_MODEL)
```

```python
import functools

import numpy as np
import jax
import jax.numpy as jnp
from jax import lax
from jax.experimental import pallas as pl
from jax.experimental.pallas import tpu as pltpu

D_MODEL = 2048
SEQ = 8192
DEPTH = 2
CTX_LEN = 256
CHUNK = 64
SUB = 16
MM_ROWS = 512
MM_COLS = 1024
MLP_FF_COLS = 1024
SUBROWS = 256
GLA_LOCAL_ROWS = 256
GLA_FINISH_ROWS = 4096
DFT_ROWS = 1024
HG_HEADS = 8
HG_DK = 128
HG_WIDTH = HG_HEADS * HG_DK
SG_GROUPS = 8
SG_CH = 128
SG_WIDTH = SG_GROUPS * SG_CH
SG_CHUNK = 128
FT_GROUPS = 4
FT_CH = D_MODEL // FT_GROUPS
FFT_L1 = 64
FFT_L2 = 128
IN_WIDTH = 5 * HG_WIDTH + 2 * SG_WIDTH
D_FF = 4 * D_MODEL
N_MOD = 6
EPS = 1e-6
EXP_CLAMP = 115.0
LOG2E = 1.0 / float(np.log(2.0))
LANES = 128
SUBLANES = 8
MXU_COLS = 256
PROJ_ROWS = 256

F32 = jnp.float32
BF16 = jnp.bfloat16
MIB = 1024 * 1024


def _params(semantics, vmem_mib):
    return pltpu.CompilerParams(dimension_semantics=semantics, vmem_limit_bytes=vmem_mib * MIB)


def _rms(x):
    return x * lax.rsqrt(jnp.mean(x * x, axis=-1, keepdims=True) + EPS)


def _silu(x):
    return x * jax.nn.sigmoid(x)


def _row_subblocks(rows, step=SUBROWS):
    step = min(rows, step)
    return [slice(s, s + step) for s in range(0, rows, step)]


def _gelu_tanh(x):
    cdf = 0.5 * (1.0 + jnp.tanh(float(np.sqrt(2.0 / np.pi)) * (x + 0.044715 * (x * x * x))))
    return x * cdf


def _ada_kernel(c_ref, w_ref, b_ref, o_ref):
    tn = o_ref.shape[-1]
    for r in range(2):
        s = _silu(c_ref[r])
        cols = [jnp.sum(w_ref[0, :, j * LANES:(j + 1) * LANES] * s, axis=0, keepdims=True)
                for j in range(tn // LANES)]
        o_ref[0, r:r + 1, :] = jnp.concatenate(cols, axis=1) + b_ref[0]


def _ada_mods(c2, w_ada, b_ada):
    tn = 1024
    n = N_MOD * D_MODEL
    cb = jnp.broadcast_to(c2[:, :, None], (2, D_MODEL, LANES))
    return pl.pallas_call(
        _ada_kernel,
        grid=(DEPTH, n // tn),
        in_specs=[pl.BlockSpec((2, D_MODEL, LANES), lambda l, j: (0, 0, 0)),
                  pl.BlockSpec((1, D_MODEL, tn), lambda l, j: (l, 0, j)),
                  pl.BlockSpec((1, 1, tn), lambda l, j: (l, 0, j))],
        out_specs=pl.BlockSpec((1, 2, tn), lambda l, j: (l, 0, j)),
        out_shape=jax.ShapeDtypeStruct((DEPTH, 2, n), F32),
        compiler_params=_params(("parallel", "parallel"), 32),
        name="ada_mods",
    )(cb, w_ada, b_ada.reshape(DEPTH, 1, n))


def _side_cast_specs(weights, n_steps, step_of):
    in_specs, out_specs, out_shapes = [], [], []
    for w, layer, *cols in weights:
        width, cidx = cols if cols else (w.shape[2], 0)
        rows = w.shape[1] // n_steps
        in_specs.append(pl.BlockSpec((1, rows, width),
                                     lambda *g, layer=layer, cidx=cidx: (layer, step_of(*g), cidx)))
        out_specs.append(pl.BlockSpec((rows, width), lambda *g: (step_of(*g), 0)))
        out_shapes.append(jax.ShapeDtypeStruct((w.shape[1], width), BF16))
    return in_specs, out_specs, out_shapes


def _side_cast(side_in, side_out):
    for wi, wo in zip(side_in, side_out):
        wo[...] = wi[0].astype(BF16)


def _norm_mm_kernel(*refs, emit_h, n_side, keep_w):
    x_ref, vec_ref, w_ref = refs[:3]
    side_in = refs[3:3 + n_side]
    o_ref = refs[3 + n_side]
    n_out = 2 if emit_h else 1
    side_out = refs[3 + n_side + n_out:3 + 2 * n_side + n_out]
    scratch = refs[3 + 2 * n_side + n_out:]
    h_ref = refs[3 + n_side + 1] if emit_h else scratch[0]
    _side_cast(side_in, side_out)
    if keep_w:
        wb_ref = scratch[-1]

        @pl.when(pl.program_id(0) == 0)
        def _():
            wb_ref[pl.program_id(1)] = w_ref[0].astype(BF16)

        w = wb_ref[pl.program_id(1)]
    else:
        w = w_ref[0].astype(BF16)

    @pl.when(pl.program_id(1) == 0)
    def _():
        for rs in _row_subblocks(x_ref.shape[0]):
            h = (_rms(x_ref[rs, :]) * vec_ref[0:1] + vec_ref[1:2]).astype(BF16)
            h_ref[rs, :] = h
            o_ref[rs, :] = jnp.dot(h, w, preferred_element_type=F32).astype(o_ref.dtype)

    @pl.when(pl.program_id(1) > 0)
    def _():
        o_ref[...] = jnp.dot(h_ref[...], w, preferred_element_type=F32).astype(o_ref.dtype)


def _norm_mm(x, vec, w, n_cols, tm, tn, emit_h=False, side=(), keep_w=False):
    m = x.shape[0]
    nj = n_cols // tn
    out_specs = [pl.BlockSpec((tm, tn), lambda i, j: (i, j))]
    out_shape = [jax.ShapeDtypeStruct((m, n_cols), BF16)]
    scratch = []
    if emit_h:
        out_specs.append(pl.BlockSpec((tm, D_MODEL), lambda i, j: (i, 0)))
        out_shape.append(jax.ShapeDtypeStruct((m, D_MODEL), BF16))
    else:
        scratch.append(pltpu.VMEM((tm, D_MODEL), BF16))
    if keep_w:
        scratch.append(pltpu.VMEM((nj, D_MODEL, tn), BF16))
        w_spec = pl.BlockSpec((1, D_MODEL, tn), lambda i, j: (0, 0, jnp.where(i == 0, j, nj - 1)))
    else:
        w_spec = pl.BlockSpec((1, D_MODEL, tn), lambda i, j: (0, 0, j))
    side_in, side_out, side_shapes = _side_cast_specs(side, (m // tm) * nj, lambda i, j: i * nj + j)
    return pl.pallas_call(
        functools.partial(_norm_mm_kernel, emit_h=emit_h, n_side=len(side), keep_w=keep_w),
        grid=(m // tm, nj),
        in_specs=[pl.BlockSpec((tm, D_MODEL), lambda i, j: (i, 0)),
                  pl.BlockSpec((8, D_MODEL), lambda i, j: (0, 0)), w_spec] + side_in,
        out_specs=out_specs + side_out, out_shape=out_shape + side_shapes,
        scratch_shapes=scratch,
        compiler_params=_params(("arbitrary", "arbitrary"), 56),
        name="norm_mm",
    )(x, vec, w, *[s[0] for s in side])


def _seg_cumsum(g, row_in_chunk):
    p = g
    s = 1
    while s < CHUNK:
        p = p + jnp.where(row_in_chunk >= s, pltpu.roll(p, s, axis=0), 0.0)
        s *= 2
    return p


def _ctx_state_kernel(f_ref, i_ref, lb_ref, sfin_ref, p_buf, g_buf, k_buf, *, nchunk):
    d = pl.program_id(0)
    rows = f_ref.shape[0]
    lb = lb_ref[0, 0]
    f = lb + (1.0 - lb) * jax.nn.sigmoid(f_ref[...].astype(F32))
    g = jnp.log(f) * LOG2E
    ric = lax.broadcasted_iota(jnp.int32, (rows, LANES), 0) & (CHUNK - 1)
    p_buf[...] = _seg_cumsum(g, ric)
    g_buf[...] = g
    k_buf[...] = 1.0 - f

    s = jnp.zeros((HG_DK, HG_DK), F32)
    for j in range(nchunk):
        c = jnp.where(d == 0, j, nchunk - 1 - j)
        base = pl.multiple_of(c * CHUNK, CHUNK)
        sl = pl.ds(base, CHUNK)
        p = p_buf[sl, :]
        tot = p_buf[pl.ds(base + CHUNK - 1, 1), :]
        ex = jnp.where(d == 0, tot - p, p - g_buf[sl, :])
        kt = (k_buf[sl, :] * jnp.exp2(ex)).astype(BF16)
        u = lax.dot_general(i_ref[sl, :], kt, (((0,), (0,)), ((), ())), preferred_element_type=F32)
        s = s * jnp.exp2(tot) + u
    sfin_ref[0, 0] = s


def _ctx_states(z_ctx, lb4):
    rows = z_ctx.shape[0]
    return pl.pallas_call(
        functools.partial(_ctx_state_kernel, nchunk=rows // CHUNK),
        grid=(2, HG_HEADS),
        in_specs=[pl.BlockSpec((rows, HG_DK), lambda d, h: (0, d * HG_HEADS + h)),
                  pl.BlockSpec((rows, HG_DK), lambda d, h: (0, 2 * HG_HEADS + h)),
                  pl.BlockSpec((1, 1, 1, HG_DK), lambda d, h: (d, h, 0, 0))],
        out_specs=pl.BlockSpec((1, 1, HG_DK, HG_DK), lambda d, h: (d, h, 0, 0)),
        out_shape=jax.ShapeDtypeStruct((2, HG_HEADS, HG_DK, HG_DK), F32),
        scratch_shapes=[pltpu.VMEM((rows, HG_DK), F32) for _ in range(3)],
        compiler_params=_params(("parallel", "parallel"), 32),
        name="ctx_states",
    )(z_ctx, z_ctx, lb4)


def _gla_local_kernel(*refs, n_side):
    h_ref, wgu_ref, wv_ref, ff_ref, fb_ref, i_ref, q_ref, lb_ref = refs[:8]
    side_in = refs[8:8 + n_side]
    zb_ref, oi_ref, qe_ref, u_ref, d_ref = refs[8 + n_side:13 + n_side]
    side_out = refs[13 + n_side:13 + 2 * n_side]
    bufs = refs[13 + 2 * n_side:]
    _side_cast(side_in, side_out)
    rows = h_ref.shape[0]
    nchunk = rows // CHUNK
    proj_rows = min(rows, PROJ_ROWS)
    row_parts = rows // proj_rows

    def project(piece):
        lo = (piece // row_parts) * MXU_COLS
        rs = slice((piece % row_parts) * proj_rows, (piece % row_parts + 1) * proj_rows)
        w_ref, off = (wgu_ref, lo) if lo < 2 * HG_WIDTH else (wv_ref, lo - 2 * HG_WIDTH)
        zb_ref[rs, lo:lo + MXU_COLS] = jnp.dot(h_ref[rs, :], w_ref[:, off:off + MXU_COLS],
                                               preferred_element_type=F32).astype(BF16)

    n_pieces = row_parts * 3 * HG_WIDTH // MXU_COLS
    pieces_before = [(k + 1) * n_pieces // (HG_HEADS * nchunk) - k * n_pieces // (HG_HEADS * nchunk)
                     for k in range(HG_HEADS * nchunk)]

    ric = lax.broadcasted_iota(jnp.int32, (rows, LANES), 0) & (CHUNK - 1)
    blk = lax.broadcasted_iota(jnp.int32, (CHUNK, LANES), 0) // SUB
    r2 = lax.broadcasted_iota(jnp.int32, (CHUNK, CHUNK), 0)
    c2 = lax.broadcasted_iota(jnp.int32, (CHUNK, CHUNK), 1)
    nt = (((1,), (1,)), ((), ()))
    tn_dims = (((0,), (0,)), ((), ()))
    nsub = CHUNK // SUB
    emitted = [0]

    def before_chunk(k):
        for _ in range(pieces_before[k]):
            project(emitted[0])
            emitted[0] += 1

    for hd in range(HG_HEADS):
        _gla_local_head(hd, nchunk, ric, blk, r2, c2, nt, tn_dims, nsub, ff_ref, fb_ref, i_ref, q_ref, lb_ref,
                        oi_ref, qe_ref, u_ref, d_ref, bufs[6 * (hd % 2):6 * (hd % 2) + 6], before_chunk)
    assert emitted[0] == n_pieces


def _gla_local_head(hd, nchunk, ric, blk, r2, c2, nt, tn_dims, nsub, ff_ref, fb_ref, i_ref, q_ref, lb_ref,
                    oi_ref, qe_ref, u_ref, d_ref, bufs, before_chunk):
    pf_buf, pb_buf, gb_buf, kf_buf, kb_buf, q_buf = bufs
    cs = slice(hd * HG_DK, (hd + 1) * HG_DK)
    lbf = lb_ref[0, hd]
    lbb = lb_ref[1, hd]
    ff = lbf + (1.0 - lbf) * jax.nn.sigmoid(ff_ref[:, cs].astype(F32))
    pf_buf[...] = _seg_cumsum(jnp.log(ff) * LOG2E, ric)
    kf_buf[...] = 1.0 - ff
    fb = lbb + (1.0 - lbb) * jax.nn.sigmoid(fb_ref[:, cs].astype(F32))
    gb = jnp.log(fb) * LOG2E
    pb_buf[...] = _seg_cumsum(gb, ric)
    gb_buf[...] = gb
    kb_buf[...] = 1.0 - fb
    q_buf[...] = _silu(q_ref[:, cs].astype(F32)) * (HG_DK ** -0.5)

    for c in range(nchunk):
        before_chunk(hd * nchunk + c)
        base = c * CHUNK
        sl = pl.ds(base, CHUNK)
        pf = pf_buf[sl, :]
        pb = pb_buf[sl, :]
        gbc = gb_buf[sl, :]
        kf = kf_buf[sl, :]
        kb = kb_buf[sl, :]
        q = q_buf[sl, :]
        totb = pb_buf[pl.ds(base + CHUNK - 1, 1), :]
        suf = totb - pb + gbc
        mids, nids = [], []
        for i in range(nsub):
            mids.append(pf_buf[pl.ds(base + SUB * i + SUB // 2 - 1, 1), :])
            rmid = pl.ds(base + SUB * i + SUB // 2, 1)
            nids.append(totb - pb_buf[rmid, :] + gb_buf[rmid, :])
        mid_rows = jnp.concatenate([jnp.broadcast_to(m, (SUB, LANES)) for m in mids], axis=0)
        nid_rows = jnp.concatenate([jnp.broadcast_to(m, (SUB, LANES)) for m in nids], axis=0)
        qf_all = q * jnp.exp2(jnp.minimum(pf - mid_rows, EXP_CLAMP))
        qb_all = q * jnp.exp2(jnp.minimum(suf - nid_rows, EXP_CLAMP))
        qf_seg, kf_seg, qb_seg, kb_seg = [], [], [], []
        for i in range(nsub):
            qf_seg.append(jnp.where(blk == i, qf_all, 0.0))
            qb_seg.append(jnp.where(blk == i, qb_all, 0.0))
            hi = SUB * (i + 1)
            top = kf[:hi] * jnp.exp2(jnp.minimum(mids[i] - pf[:hi], EXP_CLAMP))
            kf_seg.append(top if hi == CHUNK else
                          jnp.concatenate([top, jnp.zeros((CHUNK - hi, LANES), F32)], axis=0))
            lo = SUB * i
            bot = kb[lo:] * jnp.exp2(jnp.minimum(nids[i] - suf[lo:], EXP_CLAMP))
            kb_seg.append(bot if lo == 0 else
                          jnp.concatenate([jnp.zeros((lo, LANES), F32), bot], axis=0))
        qf_big = jnp.concatenate(qf_seg, axis=1).astype(BF16)
        kf_big = jnp.concatenate(kf_seg, axis=1).astype(BF16)
        qb_big = jnp.concatenate(qb_seg, axis=1).astype(BF16)
        kb_big = jnp.concatenate(kb_seg, axis=1).astype(BF16)
        sc_f = lax.dot_general(qf_big, kf_big, nt, preferred_element_type=F32)
        sc_b = lax.dot_general(qb_big, kb_big, nt, preferred_element_type=F32)
        a = (jnp.where(c2 <= r2, sc_f, 0.0) + jnp.where(c2 >= r2, sc_b, 0.0)).astype(BF16)
        v = i_ref[sl, cs]
        oi_ref[sl, cs] = jnp.dot(a, v, preferred_element_type=F32).astype(BF16)
        qe_ref[sl, 2 * hd * HG_DK:2 * (hd + 1) * HG_DK] = jnp.concatenate(
            [q * jnp.exp2(pf), q * jnp.exp2(suf)], axis=1).astype(BF16)
        totf = pf_buf[pl.ds(base + CHUNK - 1, 1), :]
        kt = jnp.concatenate([kf * jnp.exp2(totf - pf), kb * jnp.exp2(pb - gbc)], axis=1).astype(BF16)
        u = lax.dot_general(v, kt, tn_dims, preferred_element_type=F32).astype(BF16)
        u_ref[0, hd, c] = u[:, :HG_DK]
        u_ref[1, hd, c] = u[:, HG_DK:]
        d_ref[0, hd, c * SUBLANES:(c + 1) * SUBLANES, :] = jnp.broadcast_to(jnp.exp2(totf), (SUBLANES, HG_DK))
        d_ref[1, hd, c * SUBLANES:(c + 1) * SUBLANES, :] = jnp.broadcast_to(jnp.exp2(totb), (SUBLANES, HG_DK))


def _gla_local(h, w_gu, w_v, z_a, lb4, rows_per_block, side=()):
    m = h.shape[0]
    nb = m // rows_per_block
    nchunk = rows_per_block // CHUNK
    n_chunks = m // CHUNK
    resident = lambda a: pl.BlockSpec(a.shape, lambda b: (0,) * a.ndim, pipeline_mode=pl.Buffered(1))
    col = lambda k: pl.BlockSpec((rows_per_block, HG_WIDTH), lambda b, k=k: (b, k))
    side_in, side_out, side_shapes = _side_cast_specs(side, nb, lambda b: b)
    return pl.pallas_call(
        functools.partial(_gla_local_kernel, n_side=len(side)),
        grid=(nb,),
        in_specs=[pl.BlockSpec((rows_per_block, D_MODEL), lambda b: (b, 0)),
                  resident(w_gu), resident(w_v), col(0), col(1), col(2), col(3), resident(lb4)] + side_in,
        out_specs=[pl.BlockSpec((rows_per_block, 3 * HG_WIDTH), lambda b: (b, 0)),
                   pl.BlockSpec((rows_per_block, HG_WIDTH), lambda b: (b, 0)),
                   pl.BlockSpec((rows_per_block, 2 * HG_WIDTH), lambda b: (b, 0)),
                   pl.BlockSpec((2, HG_HEADS, nchunk, HG_DK, HG_DK), lambda b: (0, 0, b, 0, 0)),
                   pl.BlockSpec((2, HG_HEADS, nchunk * SUBLANES, HG_DK), lambda b: (0, 0, b, 0))] + side_out,
        out_shape=[jax.ShapeDtypeStruct((m, 3 * HG_WIDTH), BF16),
                   jax.ShapeDtypeStruct((m, HG_WIDTH), BF16),
                   jax.ShapeDtypeStruct((m, 2 * HG_WIDTH), BF16),
                   jax.ShapeDtypeStruct((2, HG_HEADS, n_chunks, HG_DK, HG_DK), BF16),
                   jax.ShapeDtypeStruct((2, HG_HEADS, n_chunks * SUBLANES, HG_DK), F32)] + side_shapes,
        scratch_shapes=[pltpu.VMEM((rows_per_block, HG_DK), F32) for _ in range(12)],
        compiler_params=_params(("parallel",), 56),
        name="gla_local",
    )(h, w_gu, w_v, z_a, z_a, z_a, z_a, lb4, *[s[0] for s in side])


def _gla_finish_kernel(*refs, nchunk, n_chunks, n_side):
    oi_ref, qe_ref, gt_ref, gain_ref, u_ref, d_ref, s0_ref = refs[:7]
    side_in = refs[7:7 + n_side]
    o_ref = refs[7 + n_side]
    side_out = refs[8 + n_side:8 + 2 * n_side]
    s_all = refs[-1]
    _side_cast(side_in, side_out)

    @pl.when(pl.program_id(1) == 0)
    def _():
        def step(c, carry):
            sf, sb = carry
            cb = n_chunks - 1 - c
            s_all[0, c] = sf.astype(BF16)
            s_all[1, cb] = sb.astype(BF16)
            df = d_ref[0, 0, pl.ds(pl.multiple_of(c * SUBLANES, SUBLANES), 1), :]
            db = d_ref[1, 0, pl.ds(pl.multiple_of(cb * SUBLANES, SUBLANES), 1), :]
            return (sf * df + u_ref[0, 0, c].astype(F32), sb * db + u_ref[1, 0, cb].astype(F32))

        lax.fori_loop(0, n_chunks, step, (s0_ref[0, 0], s0_ref[1, 0]), unroll=4)

    nt = (((1,), (1,)), ((), ()))
    gain = gain_ref[0]
    first = pl.program_id(1) * nchunk
    for c in range(nchunk):
        sl = slice(c * CHUNK, (c + 1) * CHUNK)
        scat = jnp.concatenate([s_all[0, first + c], s_all[1, first + c]], axis=1)
        o = oi_ref[sl, :].astype(F32) + lax.dot_general(qe_ref[sl, :], scat, nt, preferred_element_type=F32)
        o = _rms(o) * gain
        o_ref[sl, :] = (o * _silu(gt_ref[sl, :].astype(F32))).astype(BF16)


def _gla_finish(oi, qe, z_b, hg_gain, s_inc, s_dec, s0, rows_per_block, side=()):
    m = oi.shape[0]
    nb = m // rows_per_block
    nchunk = rows_per_block // CHUNK
    n_chunks = m // CHUNK
    side_in, side_out, side_shapes = _side_cast_specs(side, HG_HEADS * nb, lambda h, b: h * nb + b)
    return pl.pallas_call(
        functools.partial(_gla_finish_kernel, nchunk=nchunk, n_chunks=n_chunks, n_side=len(side)),
        grid=(HG_HEADS, nb),
        in_specs=[pl.BlockSpec((rows_per_block, HG_DK), lambda h, b: (b, h)),
                  pl.BlockSpec((rows_per_block, 2 * HG_DK), lambda h, b: (b, h)),
                  pl.BlockSpec((rows_per_block, HG_DK), lambda h, b: (b, h)),
                  pl.BlockSpec((1, 1, HG_DK), lambda h, b: (h, 0, 0)),
                  pl.BlockSpec((2, 1, n_chunks, HG_DK, HG_DK), lambda h, b: (0, h, 0, 0, 0)),
                  pl.BlockSpec((2, 1, n_chunks * SUBLANES, HG_DK), lambda h, b: (0, h, 0, 0)),
                  pl.BlockSpec((2, 1, HG_DK, HG_DK), lambda h, b: (0, h, 0, 0))] + side_in,
        out_specs=[pl.BlockSpec((rows_per_block, HG_DK), lambda h, b: (b, h))] + side_out,
        out_shape=[jax.ShapeDtypeStruct((m, HG_WIDTH), BF16)] + side_shapes,
        scratch_shapes=[pltpu.VMEM((2, n_chunks, HG_DK, HG_DK), BF16)],
        compiler_params=_params(("parallel", "arbitrary"), 48),
        name="gla_finish",
    )(oi, qe, z_b, hg_gain.reshape(HG_HEADS, 1, HG_DK), s_inc, s_dec, s0, *[s[0] for s in side])


def _spatial_kernel(u_ref, v_ref, w_ref, bias_ref, lng_ref, lnb_ref, o_ref, y_buf):
    v = _gelu_tanh(v_ref[...].astype(F32))
    xc = v - jnp.mean(v, axis=-1, keepdims=True)
    y = xc * lax.rsqrt(jnp.mean(xc * xc, axis=-1, keepdims=True) + EPS) * lng_ref[...] + lnb_ref[...]
    y_buf[...] = y.astype(BF16)
    for n in range(u_ref.shape[0] // SG_CHUNK):
        rs = slice(n * SG_CHUNK, (n + 1) * SG_CHUNK)
        for g in range(SG_GROUPS):
            cs = slice(g * SG_CH, (g + 1) * SG_CH)
            mixed = jnp.dot(w_ref[g], y_buf[rs, cs], preferred_element_type=F32) + bias_ref[:, cs]
            o_ref[rs, cs] = (_gelu_tanh(u_ref[rs, cs].astype(F32)) * mixed).astype(BF16)


def _spatial(z_b, sg_w, bias_full, ln_g, ln_b, rows_per_block):
    m = z_b.shape[0]
    return pl.pallas_call(
        _spatial_kernel,
        grid=(m // rows_per_block,),
        in_specs=[pl.BlockSpec((rows_per_block, SG_WIDTH), lambda b: (b, 1)),
                  pl.BlockSpec((rows_per_block, SG_WIDTH), lambda b: (b, 2)),
                  pl.BlockSpec((SG_GROUPS, SG_CHUNK, SG_CHUNK), lambda b: (0, 0, 0)),
                  pl.BlockSpec((SG_CHUNK, SG_WIDTH), lambda b: (0, 0)),
                  pl.BlockSpec((1, SG_WIDTH), lambda b: (0, 0)),
                  pl.BlockSpec((1, SG_WIDTH), lambda b: (0, 0))],
        out_specs=pl.BlockSpec((rows_per_block, SG_WIDTH), lambda b: (b, 0)),
        out_shape=jax.ShapeDtypeStruct((m, SG_WIDTH), BF16),
        scratch_shapes=[pltpu.VMEM((rows_per_block, SG_WIDTH), BF16)],
        compiler_params=_params(("parallel",), 32),
        name="spatial_gate",
    )(z_b, z_b, sg_w, bias_full, ln_g, ln_b)


def _residual_epilogue(mix, x_ref, vec_ref, xo_ref, ho_ref, rs):
    xn = x_ref[rs, :] + _rms(mix) * vec_ref[0:1]
    xo_ref[rs, :] = xn
    if ho_ref is not None:
        hn = (_rms(xn) * vec_ref[1:2] + vec_ref[2:3]).astype(BF16)
        outs = ho_ref if isinstance(ho_ref, (tuple, list)) else (ho_ref,)
        width = hn.shape[1] // len(outs)
        for k, out in enumerate(outs):
            out[rs, :] = hn[:, k * width:(k + 1) * width]


def _mm_res_kernel(*refs, n_a, emit_h):
    a_refs, w_refs = refs[:n_a], refs[n_a:2 * n_a]
    x_ref, vec_ref, xo_ref = refs[2 * n_a], refs[2 * n_a + 1], refs[2 * n_a + 2]
    ho_ref = refs[2 * n_a + 3] if emit_h else None
    for rs in _row_subblocks(x_ref.shape[0], 128):
        mix = jnp.dot(a_refs[0][rs, :], w_refs[0][...], preferred_element_type=F32)
        for a_ref, w_ref in zip(a_refs[1:], w_refs[1:]):
            mix = mix + jnp.dot(a_ref[rs, :], w_ref[...], preferred_element_type=F32)
        _residual_epilogue(mix, x_ref, vec_ref, xo_ref, ho_ref, rs)


def _mm_res(a_list, w, x, vec, tm, emit_h=True):
    m = x.shape[0]
    n_a = len(a_list)
    w_list = [w] * n_a
    offs = np.cumsum([0] + [a.shape[1] for a in a_list])
    in_specs = ([pl.BlockSpec((tm, a.shape[1]), lambda i: (i, 0)) for a in a_list]
                + [pl.BlockSpec((a.shape[1], D_MODEL), lambda i, k=int(o) // a.shape[1]: (k, 0))
                   for a, o in zip(a_list, offs)]
                + [pl.BlockSpec((tm, D_MODEL), lambda i: (i, 0)),
                   pl.BlockSpec((8, D_MODEL), lambda i: (0, 0))])
    out_specs = [pl.BlockSpec((tm, D_MODEL), lambda i: (i, 0))]
    out_shape = [jax.ShapeDtypeStruct((m, D_MODEL), F32)]
    if emit_h:
        out_specs.append(pl.BlockSpec((tm, D_MODEL), lambda i: (i, 0)))
        out_shape.append(jax.ShapeDtypeStruct((m, D_MODEL), BF16))
    return pl.pallas_call(
        functools.partial(_mm_res_kernel, n_a=n_a, emit_h=emit_h),
        grid=(m // tm,),
        in_specs=in_specs, out_specs=out_specs, out_shape=out_shape,
        compiler_params=_params(("parallel",), 56),
        name="mm_residual",
    )(*a_list, *w_list, x, vec)


def _mlp_kernel(*refs, n_h, n_side):
    h_ref, w1_ref, w2_ref, x_ref, vec_ref = refs[:5]
    side_in = refs[5:5 + n_side]
    xo_ref = refs[5 + n_side]
    ho_ref = tuple(refs[6 + n_side:6 + n_side + n_h]) or None
    n_out = 1 + n_h
    side_out = refs[5 + n_side + n_out:5 + 2 * n_side + n_out]
    acc_ref = refs[-1]
    _side_cast(side_in, side_out)
    j = pl.program_id(1)

    last = pl.num_programs(1) - 1

    def partial_product(rs):
        a = jnp.dot(h_ref[rs, :], w1_ref[...], preferred_element_type=F32)
        a = jnp.square(jnp.maximum(a, 0.0)).astype(BF16)
        return jnp.dot(a, w2_ref[...], preferred_element_type=F32)

    @pl.when(j == 0)
    def _():
        for rs in _row_subblocks(h_ref.shape[0]):
            acc_ref[rs, :] = partial_product(rs)

    @pl.when(jnp.logical_and(j > 0, j < last))
    def _():
        for rs in _row_subblocks(h_ref.shape[0]):
            acc_ref[rs, :] += partial_product(rs)

    @pl.when(j == last)
    def _():
        for rs in _row_subblocks(h_ref.shape[0]):
            _residual_epilogue(acc_ref[rs, :] + partial_product(rs), x_ref, vec_ref, xo_ref, ho_ref, rs)


def _mlp(h, w1, w2, x, vec, tm, tf, n_h, side=()):
    m = x.shape[0]
    nj = D_FF // tf
    out_specs = [pl.BlockSpec((tm, D_MODEL), lambda i, j: (i, 0))]
    out_shape = [jax.ShapeDtypeStruct((m, D_MODEL), F32)]
    for _ in range(n_h):
        out_specs.append(pl.BlockSpec((tm, D_MODEL // n_h), lambda i, j: (i, 0)))
        out_shape.append(jax.ShapeDtypeStruct((m, D_MODEL // n_h), BF16))
    side_in, side_out, side_shapes = _side_cast_specs(side, (m // tm) * nj, lambda i, j: i * nj + j)
    return pl.pallas_call(
        functools.partial(_mlp_kernel, n_h=n_h, n_side=len(side)),
        grid=(m // tm, nj),
        in_specs=[pl.BlockSpec((tm, D_MODEL), lambda i, j: (i, 0)),
                  pl.BlockSpec((D_MODEL, tf), lambda i, j: (0, j)),
                  pl.BlockSpec((tf, D_MODEL), lambda i, j: (j, 0)),
                  pl.BlockSpec((tm, D_MODEL), lambda i, j: (i, 0)),
                  pl.BlockSpec((8, D_MODEL), lambda i, j: (0, 0))] + side_in,
        out_specs=out_specs + side_out, out_shape=out_shape + side_shapes,
        scratch_shapes=[pltpu.VMEM((tm, D_MODEL), F32)],
        compiler_params=_params(("parallel", "arbitrary"), 56),
        name="mlp",
    )(h, w1, w2, x, vec, *[s[0] for s in side])


def _dft_constants():
    def cs(n):
        k = np.arange(n)
        ang = 2.0 * np.pi * ((k[:, None] * k[None, :]) % n) / n
        return np.cos(ang) / np.sqrt(n), np.sin(ang) / np.sqrt(n)

    cc, sc = cs(FT_CH)
    c1, s1 = cs(FFT_L1)
    m1 = np.concatenate([c1, -s1], axis=0)
    c2, s2 = cs(FFT_L2)
    m2 = np.block([[c2, s2], [-s2, c2]])
    u1 = np.arange(FFT_L1)[:, None]
    t2 = np.arange(FFT_L2)[None, :]
    ang = 2.0 * np.pi * ((u1 * t2) % SEQ) / SEQ
    tw_c = np.cos(ang).reshape(SEQ, 1)
    tw_s = np.sin(ang).reshape(SEQ, 1)
    f = lambda a: jnp.asarray(a, F32)
    return f(cc), f(sc), f(m1), f(m2), f(tw_c), f(tw_s)


def _swap_row_factors(a, n_outer, n_inner):
    return a.reshape(n_outer, n_inner, a.shape[1]).transpose(1, 0, 2).reshape(a.shape)


def _pos_dft1_kernel(x_ref, m_ref, zr_ref, zi_ref):
    for k in range(x_ref.shape[0] // FFT_L1):
        sl = slice(k * FFT_L1, (k + 1) * FFT_L1)
        z = jnp.dot(m_ref[...], x_ref[sl, :], preferred_element_type=F32)
        zr_ref[sl, :] = z[:FFT_L1].astype(BF16)
        zi_ref[sl, :] = z[FFT_L1:].astype(BF16)


def _pos_dft1(xt, m1, slabs):
    rows = slabs * FFT_L1
    blk = pl.BlockSpec((rows, xt.shape[1]), lambda j: (j, 0))
    return pl.pallas_call(
        _pos_dft1_kernel,
        grid=(SEQ // rows,),
        in_specs=[blk, pl.BlockSpec((2 * FFT_L1, FFT_L1), lambda j: (0, 0))],
        out_specs=[blk, blk],
        out_shape=[jax.ShapeDtypeStruct(xt.shape, BF16)] * 2,
        compiler_params=_params(("parallel",), 32),
        name="pos_dft1",
    )(xt, m1)


def _pos_dft2_kernel(zr_ref, zi_ref, c_ref, s_ref, m_ref, wc_ref, ws_ref, y_ref, yr_buf, yi_buf):
    for k in range(zr_ref.shape[0] // FFT_L2):
        sl = slice(k * FFT_L2, (k + 1) * FFT_L2)
        zr = zr_ref[sl, :].astype(F32)
        zi = zi_ref[sl, :].astype(F32)
        c = c_ref[sl, :]
        s = s_ref[sl, :]
        t = jnp.concatenate([(zr * c + zi * s).astype(BF16), (zi * c - zr * s).astype(BF16)], axis=0)
        y = jnp.dot(m_ref[...], t, preferred_element_type=F32)
        yr_buf[sl, :] = y[:FFT_L2].astype(BF16)
        yi_buf[sl, :] = y[FFT_L2:].astype(BF16)
    for g in range(zr_ref.shape[1] // FT_CH):
        cs = slice(g * FT_CH, (g + 1) * FT_CH)
        y_ref[:, cs] = (jnp.dot(yr_buf[:, cs], wc_ref[...], preferred_element_type=F32)
                        + jnp.dot(yi_buf[:, cs], ws_ref[...], preferred_element_type=F32)).astype(BF16)


def _pos_dft2(zr, zi, tw_c, tw_s, m2, wc, ws, slabs):
    rows = slabs * FFT_L2
    width = zr.shape[1]
    blk = pl.BlockSpec((rows, width), lambda j: (j, 0))
    tw = pl.BlockSpec((rows, 1), lambda j: (j, 0))
    const = lambda a: pl.BlockSpec(a.shape, lambda j: (0, 0))
    return pl.pallas_call(
        _pos_dft2_kernel,
        grid=(SEQ // rows,),
        in_specs=[blk, blk, tw, tw, const(m2), const(wc), const(ws)],
        out_specs=blk,
        out_shape=jax.ShapeDtypeStruct(zr.shape, BF16),
        scratch_shapes=[pltpu.VMEM((rows, width), BF16), pltpu.VMEM((rows, width), BF16)],
        compiler_params=_params(("parallel",), 40),
        name="pos_dft2",
    )(zr, zi, tw_c, tw_s, m2, wc, ws)


def _fourier_mix(h_parts):
    cc, sc, m1, m2, tw_c, tw_s = _dft_constants()
    m1, m2, cc, sc = m1.astype(BF16), m2.astype(BF16), cc.astype(BF16), sc.astype(BF16)
    hts = [_swap_row_factors(h, FFT_L1, FFT_L2) for h in h_parts]
    zs = [_pos_dft1(ht, m1, DFT_ROWS // FFT_L1) for ht in hts]
    zs = [[_swap_row_factors(z, FFT_L2, FFT_L1) for z in zri] for zri in zs]
    yps = [_pos_dft2(zr, zi, tw_c, tw_s, m2, cc, sc, DFT_ROWS // FFT_L2) for zr, zi in zs]
    return [_swap_row_factors(yp, FFT_L1, FFT_L2) for yp in yps]


def _pack_rows(*rows):
    rows = [r.reshape(1, D_MODEL).astype(F32) for r in rows]
    rows += [jnp.zeros((1, D_MODEL), F32)] * (8 - len(rows))
    return jnp.concatenate(rows, axis=0)


def kernel(x, c, ctx, c_ctx, w_ada, b_ada, norm_gain, w_in, w_out, lb_raw, hg_norm_gain,
           sg_w, sg_b, sg_ln_gain, sg_ln_bias, w_fourier, w_mlp_in, w_mlp_out):
    assert x.shape == (1, SEQ, D_MODEL) and ctx.shape == (1, CTX_LEN, D_MODEL)
    xs = x.reshape(SEQ, D_MODEL)
    cs = ctx.reshape(CTX_LEN, D_MODEL)

    mods = _ada_mods(jnp.concatenate([c.reshape(1, D_MODEL), c_ctx.reshape(1, D_MODEL)], axis=0), w_ada, b_ada)
    mod = lambda l, r, j: mods[l, r, j * D_MODEL:(j + 1) * D_MODEL]

    lb = jnp.cumsum(jax.nn.softmax(lb_raw.astype(F32), axis=1), axis=1)[:, 0]
    lb4 = lb.reshape(2, HG_HEADS, 1, HG_DK)

    pre = lambda gain, l, r, j: _pack_rows(gain * (1.0 + mod(l, r, j + 1)), mod(l, r, j))
    post = lambda gain, l, j, *nxt: _pack_rows(mod(l, 0, j) * gain, *nxt)

    z_a, h0, w_gu, w_v = _norm_mm(xs, pre(norm_gain[0, 0], 0, 0, 0), w_in, 4 * HG_WIDTH, MM_ROWS, MM_COLS,
                                  emit_h=True, keep_w=True,
                                  side=[(w_in, 0, 2 * HG_WIDTH, 2), (w_in, 0, HG_WIDTH, 6)])
    (z_ctx,) = _norm_mm(cs, pre(norm_gain[0, 0], 0, 1, 0), w_in, 3 * HG_WIDTH, CTX_LEN, MM_COLS)
    s_ctx = _ctx_states(z_ctx, lb4)
    z_b, o_intra, q_dec, s_inc, s_dec, w_out0, w1_0, w2_0 = _gla_local(
        h0, w_gu, w_v, z_a, lb4, GLA_LOCAL_ROWS, side=[(w_out, 0), (w_mlp_in, 0), (w_mlp_out, 0)])
    o, w_fou = _gla_finish(o_intra, q_dec, z_b, hg_norm_gain[0], s_inc, s_dec, s_ctx, GLA_FINISH_ROWS,
                           side=[(w_fourier, 0)])
    bias_full = jnp.repeat(sg_b[0].T.astype(F32), SG_CH, axis=1)
    s = _spatial(z_b, sg_w[0].astype(BF16), bias_full, sg_ln_gain[0].reshape(1, SG_WIDTH),
                 sg_ln_bias[0].reshape(1, SG_WIDTH), MM_ROWS)
    vec = post(norm_gain[0, 1], 0, 2, *pre(norm_gain[0, 2], 0, 0, 3)[:2])
    x1, h = _mm_res([o, s], w_out0, xs, vec, MM_ROWS)
    vec = post(norm_gain[0, 3], 0, 5, *pre(norm_gain[1, 0], 1, 0, 0)[:2])
    x2, h_a, h_b, w1_1, w2_1 = _mlp(h, w1_0, w2_0, x1, vec, MM_ROWS, MLP_FF_COLS, 2,
                                    side=[(w_mlp_in, 1), (w_mlp_out, 1)])

    y_parts = _fourier_mix([h_a, h_b])
    vec = post(norm_gain[1, 1], 1, 2, *pre(norm_gain[1, 2], 1, 0, 3)[:2])
    x3, h = _mm_res(y_parts, w_fou, x2, vec, MM_ROWS)
    (x4,) = _mlp(h, w1_1, w2_1, x3, post(norm_gain[1, 3], 1, 5), MM_ROWS, MLP_FF_COLS, 0)
    return x4.reshape(1, SEQ, D_MODEL)
```

```python
import functools

import numpy as np
import jax
import jax.numpy as jnp
from jax import lax
from jax.experimental import pallas as pl
from jax.experimental.pallas import tpu as pltpu

D_MODEL = 2048
SEQ = 8192
DEPTH = 2
CTX_LEN = 256
CHUNK = 64
SUB = 16
MM_ROWS = 512
MM_COLS = 1024
MLP_FF_COLS = 1024
SUBROWS = 256
GLA_LOCAL_ROWS = 256
GLA_FINISH_ROWS = 4096
DFT_ROWS = 1024
HG_HEADS = 8
HG_DK = 128
HG_WIDTH = HG_HEADS * HG_DK
SG_GROUPS = 8
SG_CH = 128
SG_WIDTH = SG_GROUPS * SG_CH
SG_CHUNK = 128
FT_GROUPS = 4
FT_CH = D_MODEL // FT_GROUPS
FFT_L1 = 64
FFT_L2 = 128
IN_WIDTH = 5 * HG_WIDTH + 2 * SG_WIDTH
D_FF = 4 * D_MODEL
N_MOD = 6
EPS = 1e-6
EXP_CLAMP = 115.0
LOG2E = 1.0 / float(np.log(2.0))
LANES = 128
SUBLANES = 8
MXU_COLS = 256
PROJ_ROWS = 256

F32 = jnp.float32
BF16 = jnp.bfloat16
MIB = 1024 * 1024


def _params(semantics, vmem_mib):
    return pltpu.CompilerParams(dimension_semantics=semantics, vmem_limit_bytes=vmem_mib * MIB)


def _rms(x):
    return x * lax.rsqrt(jnp.mean(x * x, axis=-1, keepdims=True) + EPS)


def _silu(x):
    return x * jax.nn.sigmoid(x)


def _row_subblocks(rows, step=SUBROWS):
    step = min(rows, step)
    return [slice(s, s + step) for s in range(0, rows, step)]


def _gelu_tanh(x):
    cdf = 0.5 * (1.0 + jnp.tanh(float(np.sqrt(2.0 / np.pi)) * (x + 0.044715 * (x * x * x))))
    return x * cdf


def _ada_kernel(c_ref, w_ref, b_ref, o_ref):
    tn = o_ref.shape[-1]
    for r in range(2):
        s = _silu(c_ref[r])
        cols = [jnp.sum(w_ref[0, :, j * LANES:(j + 1) * LANES] * s, axis=0, keepdims=True)
                for j in range(tn // LANES)]
        o_ref[0, r:r + 1, :] = jnp.concatenate(cols, axis=1) + b_ref[0]


def _ada_mods(c2, w_ada, b_ada):
    tn = 1024
    n = N_MOD * D_MODEL
    cb = jnp.broadcast_to(c2[:, :, None], (2, D_MODEL, LANES))
    return pl.pallas_call(
        _ada_kernel,
        grid=(DEPTH, n // tn),
        in_specs=[pl.BlockSpec((2, D_MODEL, LANES), lambda l, j: (0, 0, 0)),
                  pl.BlockSpec((1, D_MODEL, tn), lambda l, j: (l, 0, j)),
                  pl.BlockSpec((1, 1, tn), lambda l, j: (l, 0, j))],
        out_specs=pl.BlockSpec((1, 2, tn), lambda l, j: (l, 0, j)),
        out_shape=jax.ShapeDtypeStruct((DEPTH, 2, n), F32),
        compiler_params=_params(("parallel", "parallel"), 32),
        name="ada_mods",
    )(cb, w_ada, b_ada.reshape(DEPTH, 1, n))


def _side_cast_specs(weights, n_steps, step_of):
    in_specs, out_specs, out_shapes = [], [], []
    for w, layer, *cols in weights:
        width, cidx = cols if cols else (w.shape[2], 0)
        rows = w.shape[1] // n_steps
        in_specs.append(pl.BlockSpec((1, rows, width),
                                     lambda *g, layer=layer, cidx=cidx: (layer, step_of(*g), cidx)))
        out_specs.append(pl.BlockSpec((rows, width), lambda *g: (step_of(*g), 0)))
        out_shapes.append(jax.ShapeDtypeStruct((w.shape[1], width), BF16))
    return in_specs, out_specs, out_shapes


def _side_cast(side_in, side_out):
    for wi, wo in zip(side_in, side_out):
        wo[...] = wi[0].astype(BF16)


def _norm_mm_kernel(*refs, emit_h, n_side, keep_w):
    x_ref, vec_ref, w_ref = refs[:3]
    side_in = refs[3:3 + n_side]
    o_ref = refs[3 + n_side]
    n_out = 2 if emit_h else 1
    side_out = refs[3 + n_side + n_out:3 + 2 * n_side + n_out]
    scratch = refs[3 + 2 * n_side + n_out:]
    h_ref = refs[3 + n_side + 1] if emit_h else scratch[0]
    _side_cast(side_in, side_out)
    if keep_w:
        wb_ref = scratch[-1]

        @pl.when(pl.program_id(0) == 0)
        def _():
            wb_ref[pl.program_id(1)] = w_ref[0].astype(BF16)

        w = wb_ref[pl.program_id(1)]
    else:
        w = w_ref[0].astype(BF16)

    @pl.when(pl.program_id(1) == 0)
    def _():
        for rs in _row_subblocks(x_ref.shape[0]):
            h = (_rms(x_ref[rs, :]) * vec_ref[0:1] + vec_ref[1:2]).astype(BF16)
            h_ref[rs, :] = h
            o_ref[rs, :] = jnp.dot(h, w, preferred_element_type=F32).astype(o_ref.dtype)

    @pl.when(pl.program_id(1) > 0)
    def _():
        o_ref[...] = jnp.dot(h_ref[...], w, preferred_element_type=F32).astype(o_ref.dtype)


def _norm_mm(x, vec, w, n_cols, tm, tn, emit_h=False, side=(), keep_w=False):
    m = x.shape[0]
    nj = n_cols // tn
    out_specs = [pl.BlockSpec((tm, tn), lambda i, j: (i, j))]
    out_shape = [jax.ShapeDtypeStruct((m, n_cols), BF16)]
    scratch = []
    if emit_h:
        out_specs.append(pl.BlockSpec((tm, D_MODEL), lambda i, j: (i, 0)))
        out_shape.append(jax.ShapeDtypeStruct((m, D_MODEL), BF16))
    else:
        scratch.append(pltpu.VMEM((tm, D_MODEL), BF16))
    if keep_w:
        scratch.append(pltpu.VMEM((nj, D_MODEL, tn), BF16))
        w_spec = pl.BlockSpec((1, D_MODEL, tn), lambda i, j: (0, 0, jnp.where(i == 0, j, nj - 1)))
    else:
        w_spec = pl.BlockSpec((1, D_MODEL, tn), lambda i, j: (0, 0, j))
    side_in, side_out, side_shapes = _side_cast_specs(side, (m // tm) * nj, lambda i, j: i * nj + j)
    return pl.pallas_call(
        functools.partial(_norm_mm_kernel, emit_h=emit_h, n_side=len(side), keep_w=keep_w),
        grid=(m // tm, nj),
        in_specs=[pl.BlockSpec((tm, D_MODEL), lambda i, j: (i, 0)),
                  pl.BlockSpec((8, D_MODEL), lambda i, j: (0, 0)), w_spec] + side_in,
        out_specs=out_specs + side_out, out_shape=out_shape + side_shapes,
        scratch_shapes=scratch,
        compiler_params=_params(("arbitrary", "arbitrary"), 56),
        name="norm_mm",
    )(x, vec, w, *[s[0] for s in side])


def _seg_cumsum(g, row_in_chunk):
    p = g
    s = 1
    while s < CHUNK:
        p = p + jnp.where(row_in_chunk >= s, pltpu.roll(p, s, axis=0), 0.0)
        s *= 2
    return p


def _ctx_state_kernel(f_ref, i_ref, lb_ref, sfin_ref, p_buf, g_buf, k_buf, *, nchunk):
    d = pl.program_id(0)
    rows = f_ref.shape[0]
    lb = lb_ref[0, 0]
    f = lb + (1.0 - lb) * jax.nn.sigmoid(f_ref[...].astype(F32))
    g = jnp.log(f) * LOG2E
    ric = lax.broadcasted_iota(jnp.int32, (rows, LANES), 0) & (CHUNK - 1)
    p_buf[...] = _seg_cumsum(g, ric)
    g_buf[...] = g
    k_buf[...] = 1.0 - f

    s = jnp.zeros((HG_DK, HG_DK), F32)
    for j in range(nchunk):
        c = jnp.where(d == 0, j, nchunk - 1 - j)
        base = pl.multiple_of(c * CHUNK, CHUNK)
        sl = pl.ds(base, CHUNK)
        p = p_buf[sl, :]
        tot = p_buf[pl.ds(base + CHUNK - 1, 1), :]
        ex = jnp.where(d == 0, tot - p, p - g_buf[sl, :])
        kt = (k_buf[sl, :] * jnp.exp2(ex)).astype(BF16)
        u = lax.dot_general(i_ref[sl, :], kt, (((0,), (0,)), ((), ())), preferred_element_type=F32)
        s = s * jnp.exp2(tot) + u
    sfin_ref[0, 0] = s


def _ctx_states(z_ctx, lb4):
    rows = z_ctx.shape[0]
    return pl.pallas_call(
        functools.partial(_ctx_state_kernel, nchunk=rows // CHUNK),
        grid=(2, HG_HEADS),
        in_specs=[pl.BlockSpec((rows, HG_DK), lambda d, h: (0, d * HG_HEADS + h)),
                  pl.BlockSpec((rows, HG_DK), lambda d, h: (0, 2 * HG_HEADS + h)),
                  pl.BlockSpec((1, 1, 1, HG_DK), lambda d, h: (d, h, 0, 0))],
        out_specs=pl.BlockSpec((1, 1, HG_DK, HG_DK), lambda d, h: (d, h, 0, 0)),
        out_shape=jax.ShapeDtypeStruct((2, HG_HEADS, HG_DK, HG_DK), F32),
        scratch_shapes=[pltpu.VMEM((rows, HG_DK), F32) for _ in range(3)],
        compiler_params=_params(("parallel", "parallel"), 32),
        name="ctx_states",
    )(z_ctx, z_ctx, lb4)


def _gla_local_kernel(*refs, n_side):
    h_ref, wgu_ref, wv_ref, ff_ref, fb_ref, i_ref, q_ref, lb_ref = refs[:8]
    side_in = refs[8:8 + n_side]
    zb_ref, oi_ref, qe_ref, u_ref, d_ref = refs[8 + n_side:13 + n_side]
    side_out = refs[13 + n_side:13 + 2 * n_side]
    bufs = refs[13 + 2 * n_side:]
    _side_cast(side_in, side_out)
    rows = h_ref.shape[0]
    nchunk = rows // CHUNK
    proj_rows = min(rows, PROJ_ROWS)
    row_parts = rows // proj_rows

    def project(piece):
        lo = (piece // row_parts) * MXU_COLS
        rs = slice((piece % row_parts) * proj_rows, (piece % row_parts + 1) * proj_rows)
        w_ref, off = (wgu_ref, lo) if lo < 2 * HG_WIDTH else (wv_ref, lo - 2 * HG_WIDTH)
        zb_ref[rs, lo:lo + MXU_COLS] = jnp.dot(h_ref[rs, :], w_ref[:, off:off + MXU_COLS],
                                               preferred_element_type=F32).astype(BF16)

    n_pieces = row_parts * 3 * HG_WIDTH // MXU_COLS
    pieces_before = [(k + 1) * n_pieces // (HG_HEADS * nchunk) - k * n_pieces // (HG_HEADS * nchunk)
                     for k in range(HG_HEADS * nchunk)]

    ric = lax.broadcasted_iota(jnp.int32, (rows, LANES), 0) & (CHUNK - 1)
    blk = lax.broadcasted_iota(jnp.int32, (CHUNK, LANES), 0) // SUB
    r2 = lax.broadcasted_iota(jnp.int32, (CHUNK, CHUNK), 0)
    c2 = lax.broadcasted_iota(jnp.int32, (CHUNK, CHUNK), 1)
    nt = (((1,), (1,)), ((), ()))
    tn_dims = (((0,), (0,)), ((), ()))
    nsub = CHUNK // SUB
    emitted = [0]

    def before_chunk(k):
        for _ in range(pieces_before[k]):
            project(emitted[0])
            emitted[0] += 1

    for hd in range(HG_HEADS):
        _gla_local_head(hd, nchunk, ric, blk, r2, c2, nt, tn_dims, nsub, ff_ref, fb_ref, i_ref, q_ref, lb_ref,
                        oi_ref, qe_ref, u_ref, d_ref, bufs[6 * (hd % 2):6 * (hd % 2) + 6], before_chunk)
    assert emitted[0] == n_pieces


def _gla_local_head(hd, nchunk, ric, blk, r2, c2, nt, tn_dims, nsub, ff_ref, fb_ref, i_ref, q_ref, lb_ref,
                    oi_ref, qe_ref, u_ref, d_ref, bufs, before_chunk):
    pf_buf, pb_buf, gb_buf, kf_buf, kb_buf, q_buf = bufs
    cs = slice(hd * HG_DK, (hd + 1) * HG_DK)
    lbf = lb_ref[0, hd]
    lbb = lb_ref[1, hd]
    ff = lbf + (1.0 - lbf) * jax.nn.sigmoid(ff_ref[:, cs].astype(F32))
    pf_buf[...] = _seg_cumsum(jnp.log(ff) * LOG2E, ric)
    kf_buf[...] = 1.0 - ff
    fb = lbb + (1.0 - lbb) * jax.nn.sigmoid(fb_ref[:, cs].astype(F32))
    gb = jnp.log(fb) * LOG2E
    pb_buf[...] = _seg_cumsum(gb, ric)
    gb_buf[...] = gb
    kb_buf[...] = 1.0 - fb
    q_buf[...] = _silu(q_ref[:, cs].astype(F32)) * (HG_DK ** -0.5)

    for c in range(nchunk):
        before_chunk(hd * nchunk + c)
        base = c * CHUNK
        sl = pl.ds(base, CHUNK)
        pf = pf_buf[sl, :]
        pb = pb_buf[sl, :]
        gbc = gb_buf[sl, :]
        kf = kf_buf[sl, :]
        kb = kb_buf[sl, :]
        q = q_buf[sl, :]
        totb = pb_buf[pl.ds(base + CHUNK - 1, 1), :]
        suf = totb - pb + gbc
        mids, nids = [], []
        for i in range(nsub):
            mids.append(pf_buf[pl.ds(base + SUB * i + SUB // 2 - 1, 1), :])
            rmid = pl.ds(base + SUB * i + SUB // 2, 1)
            nids.append(totb - pb_buf[rmid, :] + gb_buf[rmid, :])
        mid_rows = jnp.concatenate([jnp.broadcast_to(m, (SUB, LANES)) for m in mids], axis=0)
        nid_rows = jnp.concatenate([jnp.broadcast_to(m, (SUB, LANES)) for m in nids], axis=0)
        qf_all = q * jnp.exp2(jnp.minimum(pf - mid_rows, EXP_CLAMP))
        qb_all = q * jnp.exp2(jnp.minimum(suf - nid_rows, EXP_CLAMP))
        qf_seg, kf_seg, qb_seg, kb_seg = [], [], [], []
        for i in range(nsub):
            qf_seg.append(jnp.where(blk == i, qf_all, 0.0))
            qb_seg.append(jnp.where(blk == i, qb_all, 0.0))
            hi = SUB * (i + 1)
            top = kf[:hi] * jnp.exp2(jnp.minimum(mids[i] - pf[:hi], EXP_CLAMP))
            kf_seg.append(top if hi == CHUNK else
                          jnp.concatenate([top, jnp.zeros((CHUNK - hi, LANES), F32)], axis=0))
            lo = SUB * i
            bot = kb[lo:] * jnp.exp2(jnp.minimum(nids[i] - suf[lo:], EXP_CLAMP))
            kb_seg.append(bot if lo == 0 else
                          jnp.concatenate([jnp.zeros((lo, LANES), F32), bot], axis=0))
        qf_big = jnp.concatenate(qf_seg, axis=1).astype(BF16)
        kf_big = jnp.concatenate(kf_seg, axis=1).astype(BF16)
        qb_big = jnp.concatenate(qb_seg, axis=1).astype(BF16)
        kb_big = jnp.concatenate(kb_seg, axis=1).astype(BF16)
        sc_f = lax.dot_general(qf_big, kf_big, nt, preferred_element_type=F32)
        sc_b = lax.dot_general(qb_big, kb_big, nt, preferred_element_type=F32)
        a = (jnp.where(c2 <= r2, sc_f, 0.0) + jnp.where(c2 >= r2, sc_b, 0.0)).astype(BF16)
        v = i_ref[sl, cs]
        oi_ref[sl, cs] = jnp.dot(a, v, preferred_element_type=F32).astype(BF16)
        qe_ref[sl, 2 * hd * HG_DK:2 * (hd + 1) * HG_DK] = jnp.concatenate(
            [q * jnp.exp2(pf), q * jnp.exp2(suf)], axis=1).astype(BF16)
        totf = pf_buf[pl.ds(base + CHUNK - 1, 1), :]
        kt = jnp.concatenate([kf * jnp.exp2(totf - pf), kb * jnp.exp2(pb - gbc)], axis=1).astype(BF16)
        u = lax.dot_general(v, kt, tn_dims, preferred_element_type=F32).astype(BF16)
        u_ref[0, hd, c] = u[:, :HG_DK]
        u_ref[1, hd, c] = u[:, HG_DK:]
        d_ref[0, hd, c * SUBLANES:(c + 1) * SUBLANES, :] = jnp.broadcast_to(jnp.exp2(totf), (SUBLANES, HG_DK))
        d_ref[1, hd, c * SUBLANES:(c + 1) * SUBLANES, :] = jnp.broadcast_to(jnp.exp2(totb), (SUBLANES, HG_DK))


def _gla_local(h, w_gu, w_v, z_a, lb4, rows_per_block, side=()):
    m = h.shape[0]
    nb = m // rows_per_block
    nchunk = rows_per_block // CHUNK
    n_chunks = m // CHUNK
    resident = lambda a: pl.BlockSpec(a.shape, lambda b: (0,) * a.ndim, pipeline_mode=pl.Buffered(1))
    col = lambda k: pl.BlockSpec((rows_per_block, HG_WIDTH), lambda b, k=k: (b, k))
    side_in, side_out, side_shapes = _side_cast_specs(side, nb, lambda b: b)
    return pl.pallas_call(
        functools.partial(_gla_local_kernel, n_side=len(side)),
        grid=(nb,),
        in_specs=[pl.BlockSpec((rows_per_block, D_MODEL), lambda b: (b, 0)),
                  resident(w_gu), resident(w_v), col(0), col(1), col(2), col(3), resident(lb4)] + side_in,
        out_specs=[pl.BlockSpec((rows_per_block, 3 * HG_WIDTH), lambda b: (b, 0)),
                   pl.BlockSpec((rows_per_block, HG_WIDTH), lambda b: (b, 0)),
                   pl.BlockSpec((rows_per_block, 2 * HG_WIDTH), lambda b: (b, 0)),
                   pl.BlockSpec((2, HG_HEADS, nchunk, HG_DK, HG_DK), lambda b: (0, 0, b, 0, 0)),
                   pl.BlockSpec((2, HG_HEADS, nchunk * SUBLANES, HG_DK), lambda b: (0, 0, b, 0))] + side_out,
        out_shape=[jax.ShapeDtypeStruct((m, 3 * HG_WIDTH), BF16),
                   jax.ShapeDtypeStruct((m, HG_WIDTH), BF16),
                   jax.ShapeDtypeStruct((m, 2 * HG_WIDTH), BF16),
                   jax.ShapeDtypeStruct((2, HG_HEADS, n_chunks, HG_DK, HG_DK), BF16),
                   jax.ShapeDtypeStruct((2, HG_HEADS, n_chunks * SUBLANES, HG_DK), F32)] + side_shapes,
        scratch_shapes=[pltpu.VMEM((rows_per_block, HG_DK), F32) for _ in range(12)],
        compiler_params=_params(("parallel",), 56),
        name="gla_local",
    )(h, w_gu, w_v, z_a, z_a, z_a, z_a, lb4, *[s[0] for s in side])


def _gla_finish_kernel(*refs, nchunk, n_chunks, n_side):
    oi_ref, qe_ref, gt_ref, gain_ref, u_ref, d_ref, s0_ref = refs[:7]
    side_in = refs[7:7 + n_side]
    o_ref = refs[7 + n_side]
    side_out = refs[8 + n_side:8 + 2 * n_side]
    s_all = refs[-1]
    _side_cast(side_in, side_out)

    @pl.when(pl.program_id(1) == 0)
    def _():
        def step(c, carry):
            sf, sb = carry
            cb = n_chunks - 1 - c
            s_all[0, c] = sf.astype(BF16)
            s_all[1, cb] = sb.astype(BF16)
            df = d_ref[0, 0, pl.ds(pl.multiple_of(c * SUBLANES, SUBLANES), 1), :]
            db = d_ref[1, 0, pl.ds(pl.multiple_of(cb * SUBLANES, SUBLANES), 1), :]
            return (sf * df + u_ref[0, 0, c].astype(F32), sb * db + u_ref[1, 0, cb].astype(F32))

        lax.fori_loop(0, n_chunks, step, (s0_ref[0, 0], s0_ref[1, 0]), unroll=4)

    nt = (((1,), (1,)), ((), ()))
    gain = gain_ref[0]
    first = pl.program_id(1) * nchunk
    for c in range(nchunk):
        sl = slice(c * CHUNK, (c + 1) * CHUNK)
        scat = jnp.concatenate([s_all[0, first + c], s_all[1, first + c]], axis=1)
        o = oi_ref[sl, :].astype(F32) + lax.dot_general(qe_ref[sl, :], scat, nt, preferred_element_type=F32)
        o = _rms(o) * gain
        o_ref[sl, :] = (o * _silu(gt_ref[sl, :].astype(F32))).astype(BF16)


def _gla_finish(oi, qe, z_b, hg_gain, s_inc, s_dec, s0, rows_per_block, side=()):
    m = oi.shape[0]
    nb = m // rows_per_block
    nchunk = rows_per_block // CHUNK
    n_chunks = m // CHUNK
    side_in, side_out, side_shapes = _side_cast_specs(side, HG_HEADS * nb, lambda h, b: h * nb + b)
    return pl.pallas_call(
        functools.partial(_gla_finish_kernel, nchunk=nchunk, n_chunks=n_chunks, n_side=len(side)),
        grid=(HG_HEADS, nb),
        in_specs=[pl.BlockSpec((rows_per_block, HG_DK), lambda h, b: (b, h)),
                  pl.BlockSpec((rows_per_block, 2 * HG_DK), lambda h, b: (b, h)),
                  pl.BlockSpec((rows_per_block, HG_DK), lambda h, b: (b, h)),
                  pl.BlockSpec((1, 1, HG_DK), lambda h, b: (h, 0, 0)),
                  pl.BlockSpec((2, 1, n_chunks, HG_DK, HG_DK), lambda h, b: (0, h, 0, 0, 0)),
                  pl.BlockSpec((2, 1, n_chunks * SUBLANES, HG_DK), lambda h, b: (0, h, 0, 0)),
                  pl.BlockSpec((2, 1, HG_DK, HG_DK), lambda h, b: (0, h, 0, 0))] + side_in,
        out_specs=[pl.BlockSpec((rows_per_block, HG_DK), lambda h, b: (b, h))] + side_out,
        out_shape=[jax.ShapeDtypeStruct((m, HG_WIDTH), BF16)] + side_shapes,
        scratch_shapes=[pltpu.VMEM((2, n_chunks, HG_DK, HG_DK), BF16)],
        compiler_params=_params(("parallel", "arbitrary"), 48),
        name="gla_finish",
    )(oi, qe, z_b, hg_gain.reshape(HG_HEADS, 1, HG_DK), s_inc, s_dec, s0, *[s[0] for s in side])


def _residual_epilogue(mix, x_ref, vec_ref, xo_ref, ho_ref, rs):
    xn = x_ref[rs, :] + _rms(mix) * vec_ref[0:1]
    xo_ref[rs, :] = xn
    if ho_ref is not None:
        hn = (_rms(xn) * vec_ref[1:2] + vec_ref[2:3]).astype(BF16)
        outs = ho_ref if isinstance(ho_ref, (tuple, list)) else (ho_ref,)
        width = hn.shape[1] // len(outs)
        for k, out in enumerate(outs):
            out[rs, :] = hn[:, k * width:(k + 1) * width]


def _spatial_chunk(u_raw, v_raw, w_ref, bias_ref, lng_ref, lnb_ref):
    v = _gelu_tanh(v_raw.astype(F32))
    xc = v - jnp.mean(v, axis=-1, keepdims=True)
    y = (xc * lax.rsqrt(jnp.mean(xc * xc, axis=-1, keepdims=True) + EPS) * lng_ref[...] + lnb_ref[...]).astype(BF16)
    u = _gelu_tanh(u_raw.astype(F32))
    parts = []
    for g in range(SG_GROUPS):
        cs = slice(g * SG_CH, (g + 1) * SG_CH)
        mixed = jnp.dot(w_ref[g], y[:, cs], preferred_element_type=F32) + bias_ref[:, cs]
        parts.append((u[:, cs] * mixed).astype(BF16))
    return jnp.concatenate(parts, axis=1)


def _mm_res_kernel(*refs, n_a, n_sp, emit_h):
    a_refs = refs[:n_a]
    sp_refs = refs[n_a:n_a + 6 * n_sp]
    n_w = n_a + n_sp
    w_refs = refs[n_a + 6 * n_sp:n_a + 6 * n_sp + n_w]
    x_ref, vec_ref, xo_ref = refs[n_a + 6 * n_sp + n_w:n_a + 6 * n_sp + n_w + 3]
    ho_ref = refs[n_a + 6 * n_sp + n_w + 3] if emit_h else None
    for rs in _row_subblocks(x_ref.shape[0], SG_CHUNK):
        mix = jnp.dot(a_refs[0][rs, :], w_refs[0][...], preferred_element_type=F32)
        for a_ref, w_ref in zip(a_refs[1:], w_refs[1:]):
            mix = mix + jnp.dot(a_ref[rs, :], w_ref[...], preferred_element_type=F32)
        if n_sp:
            u_ref, v_ref = sp_refs[:2]
            s = _spatial_chunk(u_ref[rs, :], v_ref[rs, :], *sp_refs[2:])
            mix = mix + jnp.dot(s, w_refs[-1][...], preferred_element_type=F32)
        _residual_epilogue(mix, x_ref, vec_ref, xo_ref, ho_ref, rs)


def _mm_res(a_list, w, x, vec, tm, emit_h=True, spatial=None):
    m = x.shape[0]
    n_a = len(a_list)
    widths = [a.shape[1] for a in a_list] + ([SG_WIDTH] if spatial else [])
    offs = np.cumsum([0] + widths)
    sp_specs, sp_args = [], []
    if spatial:
        z_b = spatial[0]
        sp_specs = [pl.BlockSpec((tm, SG_WIDTH), lambda i: (i, 1)), pl.BlockSpec((tm, SG_WIDTH), lambda i: (i, 2)),
                    pl.BlockSpec((SG_GROUPS, SG_CHUNK, SG_CHUNK), lambda i: (0, 0, 0)),
                    pl.BlockSpec((SG_CHUNK, SG_WIDTH), lambda i: (0, 0)),
                    pl.BlockSpec((1, SG_WIDTH), lambda i: (0, 0)), pl.BlockSpec((1, SG_WIDTH), lambda i: (0, 0))]
        sp_args = [z_b, z_b, *spatial[1:]]
    in_specs = ([pl.BlockSpec((tm, a.shape[1]), lambda i: (i, 0)) for a in a_list] + sp_specs
                + [pl.BlockSpec((wd, D_MODEL), lambda i, k=int(o) // wd: (k, 0)) for wd, o in zip(widths, offs)]
                + [pl.BlockSpec((tm, D_MODEL), lambda i: (i, 0)),
                   pl.BlockSpec((8, D_MODEL), lambda i: (0, 0))])
    out_specs = [pl.BlockSpec((tm, D_MODEL), lambda i: (i, 0))]
    out_shape = [jax.ShapeDtypeStruct((m, D_MODEL), F32)]
    if emit_h:
        out_specs.append(pl.BlockSpec((tm, D_MODEL), lambda i: (i, 0)))
        out_shape.append(jax.ShapeDtypeStruct((m, D_MODEL), BF16))
    return pl.pallas_call(
        functools.partial(_mm_res_kernel, n_a=n_a, n_sp=1 if spatial else 0, emit_h=emit_h),
        grid=(m // tm,),
        in_specs=in_specs, out_specs=out_specs, out_shape=out_shape,
        compiler_params=_params(("parallel",), 56),
        name="mm_residual",
    )(*a_list, *sp_args, *[w] * len(widths), x, vec)


def _mlp_kernel(*refs, n_h, n_side):
    h_ref, w1_ref, w2_ref, x_ref, vec_ref = refs[:5]
    side_in = refs[5:5 + n_side]
    xo_ref = refs[5 + n_side]
    ho_ref = tuple(refs[6 + n_side:6 + n_side + n_h]) or None
    n_out = 1 + n_h
    side_out = refs[5 + n_side + n_out:5 + 2 * n_side + n_out]
    acc_ref = refs[-1]
    _side_cast(side_in, side_out)
    j = pl.program_id(1)

    last = pl.num_programs(1) - 1

    def partial_product(rs):
        a = jnp.dot(h_ref[rs, :], w1_ref[...], preferred_element_type=F32)
        a = jnp.square(jnp.maximum(a, 0.0)).astype(BF16)
        return jnp.dot(a, w2_ref[...], preferred_element_type=F32)

    @pl.when(j == 0)
    def _():
        for rs in _row_subblocks(h_ref.shape[0]):
            acc_ref[rs, :] = partial_product(rs)

    @pl.when(jnp.logical_and(j > 0, j < last))
    def _():
        for rs in _row_subblocks(h_ref.shape[0]):
            acc_ref[rs, :] += partial_product(rs)

    @pl.when(j == last)
    def _():
        for rs in _row_subblocks(h_ref.shape[0]):
            _residual_epilogue(acc_ref[rs, :] + partial_product(rs), x_ref, vec_ref, xo_ref, ho_ref, rs)


def _mlp(h, w1, w2, x, vec, tm, tf, n_h, side=()):
    m = x.shape[0]
    nj = D_FF // tf
    out_specs = [pl.BlockSpec((tm, D_MODEL), lambda i, j: (i, 0))]
    out_shape = [jax.ShapeDtypeStruct((m, D_MODEL), F32)]
    for _ in range(n_h):
        out_specs.append(pl.BlockSpec((tm, D_MODEL // n_h), lambda i, j: (i, 0)))
        out_shape.append(jax.ShapeDtypeStruct((m, D_MODEL // n_h), BF16))
    side_in, side_out, side_shapes = _side_cast_specs(side, (m // tm) * nj, lambda i, j: i * nj + j)
    return pl.pallas_call(
        functools.partial(_mlp_kernel, n_h=n_h, n_side=len(side)),
        grid=(m // tm, nj),
        in_specs=[pl.BlockSpec((tm, D_MODEL), lambda i, j: (i, 0)),
                  pl.BlockSpec((D_MODEL, tf), lambda i, j: (0, j)),
                  pl.BlockSpec((tf, D_MODEL), lambda i, j: (j, 0)),
                  pl.BlockSpec((tm, D_MODEL), lambda i, j: (i, 0)),
                  pl.BlockSpec((8, D_MODEL), lambda i, j: (0, 0))] + side_in,
        out_specs=out_specs + side_out, out_shape=out_shape + side_shapes,
        scratch_shapes=[pltpu.VMEM((tm, D_MODEL), F32)],
        compiler_params=_params(("parallel", "arbitrary"), 56),
        name="mlp",
    )(h, w1, w2, x, vec, *[s[0] for s in side])


def _dft_constants():
    def cs(n):
        k = np.arange(n)
        ang = 2.0 * np.pi * ((k[:, None] * k[None, :]) % n) / n
        return np.cos(ang) / np.sqrt(n), np.sin(ang) / np.sqrt(n)

    cc, sc = cs(FT_CH)
    c1, s1 = cs(FFT_L1)
    m1 = np.concatenate([c1, -s1], axis=0)
    c2, s2 = cs(FFT_L2)
    m2 = np.block([[c2, s2], [-s2, c2]])
    u1 = np.arange(FFT_L1)[:, None]
    t2 = np.arange(FFT_L2)[None, :]
    ang = 2.0 * np.pi * ((u1 * t2) % SEQ) / SEQ
    tw_c = np.cos(ang).reshape(SEQ, 1)
    tw_s = np.sin(ang).reshape(SEQ, 1)
    f = lambda a: jnp.asarray(a, F32)
    return f(cc), f(sc), f(m1), f(m2), f(tw_c), f(tw_s)


def _swap_row_factors(a, n_outer, n_inner):
    return a.reshape(n_outer, n_inner, a.shape[1]).transpose(1, 0, 2).reshape(a.shape)


def _pos_dft1_kernel(x_ref, m_ref, zr_ref, zi_ref):
    for k in range(x_ref.shape[0] // FFT_L1):
        sl = slice(k * FFT_L1, (k + 1) * FFT_L1)
        z = jnp.dot(m_ref[...], x_ref[sl, :], preferred_element_type=F32)
        zr_ref[sl, :] = z[:FFT_L1].astype(BF16)
        zi_ref[sl, :] = z[FFT_L1:].astype(BF16)


def _pos_dft1(xt, m1, slabs):
    rows = slabs * FFT_L1
    blk = pl.BlockSpec((rows, xt.shape[1]), lambda j: (j, 0))
    return pl.pallas_call(
        _pos_dft1_kernel,
        grid=(SEQ // rows,),
        in_specs=[blk, pl.BlockSpec((2 * FFT_L1, FFT_L1), lambda j: (0, 0))],
        out_specs=[blk, blk],
        out_shape=[jax.ShapeDtypeStruct(xt.shape, BF16)] * 2,
        compiler_params=_params(("parallel",), 32),
        name="pos_dft1",
    )(xt, m1)


def _pos_dft2_kernel(zr_ref, zi_ref, c_ref, s_ref, m_ref, wc_ref, ws_ref, y_ref, yr_buf, yi_buf):
    for k in range(zr_ref.shape[0] // FFT_L2):
        sl = slice(k * FFT_L2, (k + 1) * FFT_L2)
        zr = zr_ref[sl, :].astype(F32)
        zi = zi_ref[sl, :].astype(F32)
        c = c_ref[sl, :]
        s = s_ref[sl, :]
        t = jnp.concatenate([(zr * c + zi * s).astype(BF16), (zi * c - zr * s).astype(BF16)], axis=0)
        y = jnp.dot(m_ref[...], t, preferred_element_type=F32)
        yr_buf[sl, :] = y[:FFT_L2].astype(BF16)
        yi_buf[sl, :] = y[FFT_L2:].astype(BF16)
    for g in range(zr_ref.shape[1] // FT_CH):
        cs = slice(g * FT_CH, (g + 1) * FT_CH)
        y_ref[:, cs] = (jnp.dot(yr_buf[:, cs], wc_ref[...], preferred_element_type=F32)
                        + jnp.dot(yi_buf[:, cs], ws_ref[...], preferred_element_type=F32)).astype(BF16)


def _pos_dft2(zr, zi, tw_c, tw_s, m2, wc, ws, slabs):
    rows = slabs * FFT_L2
    width = zr.shape[1]
    blk = pl.BlockSpec((rows, width), lambda j: (j, 0))
    tw = pl.BlockSpec((rows, 1), lambda j: (j, 0))
    const = lambda a: pl.BlockSpec(a.shape, lambda j: (0, 0))
    return pl.pallas_call(
        _pos_dft2_kernel,
        grid=(SEQ // rows,),
        in_specs=[blk, blk, tw, tw, const(m2), const(wc), const(ws)],
        out_specs=blk,
        out_shape=jax.ShapeDtypeStruct(zr.shape, BF16),
        scratch_shapes=[pltpu.VMEM((rows, width), BF16), pltpu.VMEM((rows, width), BF16)],
        compiler_params=_params(("parallel",), 40),
        name="pos_dft2",
    )(zr, zi, tw_c, tw_s, m2, wc, ws)


def _fourier_mix(h_parts):
    cc, sc, m1, m2, tw_c, tw_s = _dft_constants()
    m1, m2, cc, sc = m1.astype(BF16), m2.astype(BF16), cc.astype(BF16), sc.astype(BF16)
    hts = [_swap_row_factors(h, FFT_L1, FFT_L2) for h in h_parts]
    zs = [_pos_dft1(ht, m1, DFT_ROWS // FFT_L1) for ht in hts]
    zs = [[_swap_row_factors(z, FFT_L2, FFT_L1) for z in zri] for zri in zs]
    yps = [_pos_dft2(zr, zi, tw_c, tw_s, m2, cc, sc, DFT_ROWS // FFT_L2) for zr, zi in zs]
    return [_swap_row_factors(yp, FFT_L1, FFT_L2) for yp in yps]


def _pack_rows(*rows):
    rows = [r.reshape(1, D_MODEL).astype(F32) for r in rows]
    rows += [jnp.zeros((1, D_MODEL), F32)] * (8 - len(rows))
    return jnp.concatenate(rows, axis=0)


def kernel(x, c, ctx, c_ctx, w_ada, b_ada, norm_gain, w_in, w_out, lb_raw, hg_norm_gain,
           sg_w, sg_b, sg_ln_gain, sg_ln_bias, w_fourier, w_mlp_in, w_mlp_out):
    assert x.shape == (1, SEQ, D_MODEL) and ctx.shape == (1, CTX_LEN, D_MODEL)
    xs = x.reshape(SEQ, D_MODEL)
    cs = ctx.reshape(CTX_LEN, D_MODEL)

    mods = _ada_mods(jnp.concatenate([c.reshape(1, D_MODEL), c_ctx.reshape(1, D_MODEL)], axis=0), w_ada, b_ada)
    mod = lambda l, r, j: mods[l, r, j * D_MODEL:(j + 1) * D_MODEL]

    lb = jnp.cumsum(jax.nn.softmax(lb_raw.astype(F32), axis=1), axis=1)[:, 0]
    lb4 = lb.reshape(2, HG_HEADS, 1, HG_DK)

    pre = lambda gain, l, r, j: _pack_rows(gain * (1.0 + mod(l, r, j + 1)), mod(l, r, j))
    post = lambda gain, l, j, *nxt: _pack_rows(mod(l, 0, j) * gain, *nxt)

    z_a, h0, w_gu, w_v = _norm_mm(xs, pre(norm_gain[0, 0], 0, 0, 0), w_in, 4 * HG_WIDTH, MM_ROWS, MM_COLS,
                                  emit_h=True, keep_w=True,
                                  side=[(w_in, 0, 2 * HG_WIDTH, 2), (w_in, 0, HG_WIDTH, 6)])
    (z_ctx,) = _norm_mm(cs, pre(norm_gain[0, 0], 0, 1, 0), w_in, 3 * HG_WIDTH, CTX_LEN, MM_COLS)
    s_ctx = _ctx_states(z_ctx, lb4)
    z_b, o_intra, q_dec, s_inc, s_dec, w_out0, w1_0, w2_0 = _gla_local(
        h0, w_gu, w_v, z_a, lb4, GLA_LOCAL_ROWS, side=[(w_out, 0), (w_mlp_in, 0), (w_mlp_out, 0)])
    o, w_fou = _gla_finish(o_intra, q_dec, z_b, hg_norm_gain[0], s_inc, s_dec, s_ctx, GLA_FINISH_ROWS,
                           side=[(w_fourier, 0)])
    bias_full = jnp.repeat(sg_b[0].T.astype(F32), SG_CH, axis=1)
    vec = post(norm_gain[0, 1], 0, 2, *pre(norm_gain[0, 2], 0, 0, 3)[:2])
    x1, h = _mm_res([o], w_out0, xs, vec, MM_ROWS,
                    spatial=(z_b, sg_w[0].astype(BF16), bias_full, sg_ln_gain[0].reshape(1, SG_WIDTH),
                             sg_ln_bias[0].reshape(1, SG_WIDTH)))
    vec = post(norm_gain[0, 3], 0, 5, *pre(norm_gain[1, 0], 1, 0, 0)[:2])
    x2, h_a, h_b, w1_1, w2_1 = _mlp(h, w1_0, w2_0, x1, vec, MM_ROWS, MLP_FF_COLS, 2,
                                    side=[(w_mlp_in, 1), (w_mlp_out, 1)])

    y_parts = _fourier_mix([h_a, h_b])
    vec = post(norm_gain[1, 1], 1, 2, *pre(norm_gain[1, 2], 1, 0, 3)[:2])
    x3, h = _mm_res(y_parts, w_fou, x2, vec, MM_ROWS)
    (x4,) = _mlp(h, w1_1, w2_1, x3, post(norm_gain[1, 3], 1, 5), MM_ROWS, MLP_FF_COLS, 0)
    return x4.reshape(1, SEQ, D_MODEL)
```

```python
import functools

import numpy as np
import jax
import jax.numpy as jnp
from jax import lax
from jax.experimental import pallas as pl
from jax.experimental.pallas import tpu as pltpu

D_MODEL = 2048
SEQ = 8192
DEPTH = 2
CTX_LEN = 256
CHUNK = 64
SUB = 16
ADA_COLS = 2048
MM_ROWS = 512
MM_COLS = 1024
MLP_FF_COLS = 1024
SUBROWS = 256
GLA_LOCAL_ROWS = 256
GLA_FINISH_ROWS = 4096
DFT_ROWS = 1024
HG_HEADS = 8
HG_DK = 128
HG_WIDTH = HG_HEADS * HG_DK
SG_GROUPS = 8
SG_CH = 128
SG_WIDTH = SG_GROUPS * SG_CH
SG_CHUNK = 128
FT_GROUPS = 4
FT_CH = D_MODEL // FT_GROUPS
FFT_L1 = 64
FFT_L2 = 128
IN_WIDTH = 5 * HG_WIDTH + 2 * SG_WIDTH
D_FF = 4 * D_MODEL
N_MOD = 6
EPS = 1e-6
EXP_CLAMP = 115.0
LOG2E = 1.0 / float(np.log(2.0))
LANES = 128
SUBLANES = 8
MXU_COLS = 256
PROJ_ROWS = 256

F32 = jnp.float32
BF16 = jnp.bfloat16
MIB = 1024 * 1024


def _params(semantics, vmem_mib):
    return pltpu.CompilerParams(dimension_semantics=semantics, vmem_limit_bytes=vmem_mib * MIB)


def _rms(x):
    return x * lax.rsqrt(jnp.mean(x * x, axis=-1, keepdims=True) + EPS)


def _silu(x):
    return x * jax.nn.sigmoid(x)


def _row_subblocks(rows, step=SUBROWS):
    step = min(rows, step)
    return [slice(s, s + step) for s in range(0, rows, step)]


def _gelu_tanh(x):
    cdf = 0.5 * (1.0 + jnp.tanh(float(np.sqrt(2.0 / np.pi)) * (x + 0.044715 * (x * x * x))))
    return x * cdf


def _ada_kernel(c_ref, w_ref, b_ref, o_ref, s_buf):
    @pl.when((pl.program_id(0) == 0) & (pl.program_id(1) == 0))
    def _():
        s_buf[...] = _silu(c_ref[...])

    tn = o_ref.shape[-1]
    for r in range(2):
        s = s_buf[r]
        cols = [jnp.sum(w_ref[0, :, j * LANES:(j + 1) * LANES] * s, axis=0, keepdims=True)
                for j in range(tn // LANES)]
        o_ref[0, r:r + 1, :] = jnp.concatenate(cols, axis=1) + b_ref[0]


def _ada_mods(c2, w_ada, b_ada):
    tn = ADA_COLS
    n = N_MOD * D_MODEL
    cb = jnp.broadcast_to(c2[:, :, None], (2, D_MODEL, LANES))
    return pl.pallas_call(
        _ada_kernel,
        grid=(DEPTH, n // tn),
        in_specs=[pl.BlockSpec((2, D_MODEL, LANES), lambda l, j: (0, 0, 0)),
                  pl.BlockSpec((1, D_MODEL, tn), lambda l, j: (l, 0, j)),
                  pl.BlockSpec((1, 1, tn), lambda l, j: (l, 0, j))],
        out_specs=pl.BlockSpec((1, 2, tn), lambda l, j: (l, 0, j)),
        out_shape=jax.ShapeDtypeStruct((DEPTH, 2, n), F32),
        scratch_shapes=[pltpu.VMEM((2, D_MODEL, LANES), F32)],
        compiler_params=_params(("arbitrary", "arbitrary"), 40),
        name="ada_mods",
    )(cb, w_ada, b_ada.reshape(DEPTH, 1, n))


def _side_cast_specs(weights, n_steps, step_of):
    in_specs, out_specs, out_shapes = [], [], []
    for w, layer, *cols in weights:
        width, cidx = cols if cols else (w.shape[2], 0)
        assert w.shape[1] % n_steps == 0, "every grid step must get an equal row slab"
        rows = w.shape[1] // n_steps
        in_specs.append(pl.BlockSpec((1, rows, width),
                                     lambda *g, layer=layer, cidx=cidx: (layer, step_of(*g), cidx)))
        out_specs.append(pl.BlockSpec((rows, width), lambda *g: (step_of(*g), 0)))
        out_shapes.append(jax.ShapeDtypeStruct((w.shape[1], width), BF16))
    return in_specs, out_specs, out_shapes


def _side_cast(side_in, side_out):
    for wi, wo in zip(side_in, side_out):
        wo[...] = wi[0].astype(BF16)


def _norm_mm_kernel(*refs, emit_h, n_side, keep_w):
    x_ref, vec_ref, w_ref = refs[:3]
    side_in = refs[3:3 + n_side]
    o_ref = refs[3 + n_side]
    n_out = 2 if emit_h else 1
    side_out = refs[3 + n_side + n_out:3 + 2 * n_side + n_out]
    scratch = refs[3 + 2 * n_side + n_out:]
    h_ref = refs[3 + n_side + 1] if emit_h else scratch[0]
    _side_cast(side_in, side_out)
    if keep_w:
        wb_ref = scratch[-1]

        @pl.when(pl.program_id(0) == 0)
        def _():
            wb_ref[pl.program_id(1)] = w_ref[0].astype(BF16)

        w = wb_ref[pl.program_id(1)]
    else:
        w = w_ref[0].astype(BF16)

    @pl.when(pl.program_id(1) == 0)
    def _():
        for rs in _row_subblocks(x_ref.shape[0]):
            h = (_rms(x_ref[rs, :]) * vec_ref[0:1] + vec_ref[1:2]).astype(BF16)
            h_ref[rs, :] = h
            o_ref[rs, :] = jnp.dot(h, w, preferred_element_type=F32).astype(o_ref.dtype)

    @pl.when(pl.program_id(1) > 0)
    def _():
        o_ref[...] = jnp.dot(h_ref[...], w, preferred_element_type=F32).astype(o_ref.dtype)


def _norm_mm(x, vec, w, n_cols, tm, tn, emit_h=False, side=(), keep_w=False):
    m = x.shape[0]
    nj = n_cols // tn
    out_specs = [pl.BlockSpec((tm, tn), lambda i, j: (i, j))]
    out_shape = [jax.ShapeDtypeStruct((m, n_cols), BF16)]
    scratch = []
    if emit_h:
        out_specs.append(pl.BlockSpec((tm, D_MODEL), lambda i, j: (i, 0)))
        out_shape.append(jax.ShapeDtypeStruct((m, D_MODEL), BF16))
    else:
        scratch.append(pltpu.VMEM((tm, D_MODEL), BF16))
    if keep_w:
        scratch.append(pltpu.VMEM((nj, D_MODEL, tn), BF16))
        w_spec = pl.BlockSpec((1, D_MODEL, tn), lambda i, j: (0, 0, jnp.where(i == 0, j, nj - 1)))
    else:
        w_spec = pl.BlockSpec((1, D_MODEL, tn), lambda i, j: (0, 0, j))
    side_in, side_out, side_shapes = _side_cast_specs(side, (m // tm) * nj, lambda i, j: i * nj + j)
    return pl.pallas_call(
        functools.partial(_norm_mm_kernel, emit_h=emit_h, n_side=len(side), keep_w=keep_w),
        grid=(m // tm, nj),
        in_specs=[pl.BlockSpec((tm, D_MODEL), lambda i, j: (i, 0)),
                  pl.BlockSpec((8, D_MODEL), lambda i, j: (0, 0)), w_spec] + side_in,
        out_specs=out_specs + side_out, out_shape=out_shape + side_shapes,
        scratch_shapes=scratch,
        compiler_params=_params(("arbitrary", "arbitrary"), 56),
        name="norm_mm",
    )(x, vec, w, *[s[0] for s in side])


def _seg_cumsum(g, row_in_chunk):
    p = g
    s = 1
    while s < CHUNK:
        p = p + jnp.where(row_in_chunk >= s, pltpu.roll(p, s, axis=0), 0.0)
        s *= 2
    return p


def _ctx_state_kernel(f_ref, i_ref, lb_ref, sfin_ref, p_buf, g_buf, k_buf, *, nchunk):
    d = pl.program_id(0)
    rows = f_ref.shape[0]
    lb = lb_ref[0, 0]
    f = lb + (1.0 - lb) * jax.nn.sigmoid(f_ref[...].astype(F32))
    g = jnp.log(f) * LOG2E
    ric = lax.broadcasted_iota(jnp.int32, (rows, LANES), 0) & (CHUNK - 1)
    p_buf[...] = _seg_cumsum(g, ric)
    g_buf[...] = g
    k_buf[...] = 1.0 - f

    s = jnp.zeros((HG_DK, HG_DK), F32)
    for j in range(nchunk):
        c = jnp.where(d == 0, j, nchunk - 1 - j)
        base = pl.multiple_of(c * CHUNK, CHUNK)
        sl = pl.ds(base, CHUNK)
        p = p_buf[sl, :]
        tot = p_buf[pl.ds(base + CHUNK - 1, 1), :]
        ex = jnp.where(d == 0, tot - p, p - g_buf[sl, :])
        kt = (k_buf[sl, :] * jnp.exp2(ex)).astype(BF16)
        u = lax.dot_general(i_ref[sl, :], kt, (((0,), (0,)), ((), ())), preferred_element_type=F32)
        s = s * jnp.exp2(tot) + u
    sfin_ref[0, 0] = s


def _ctx_states(z_ctx, lb4):
    rows = z_ctx.shape[0]
    return pl.pallas_call(
        functools.partial(_ctx_state_kernel, nchunk=rows // CHUNK),
        grid=(2, HG_HEADS),
        in_specs=[pl.BlockSpec((rows, HG_DK), lambda d, h: (0, d * HG_HEADS + h)),
                  pl.BlockSpec((rows, HG_DK), lambda d, h: (0, 2 * HG_HEADS + h)),
                  pl.BlockSpec((1, 1, 1, HG_DK), lambda d, h: (d, h, 0, 0))],
        out_specs=pl.BlockSpec((1, 1, HG_DK, HG_DK), lambda d, h: (d, h, 0, 0)),
        out_shape=jax.ShapeDtypeStruct((2, HG_HEADS, HG_DK, HG_DK), F32),
        scratch_shapes=[pltpu.VMEM((rows, HG_DK), F32) for _ in range(3)],
        compiler_params=_params(("parallel", "parallel"), 32),
        name="ctx_states",
    )(z_ctx, z_ctx, lb4)


def _gla_local_kernel(*refs, n_side):
    h_ref, wgu_ref, wv_ref, ff_ref, fb_ref, i_ref, q_ref, lb_ref = refs[:8]
    side_in = refs[8:8 + n_side]
    zb_ref, oi_ref, qe_ref, u_ref, d_ref = refs[8 + n_side:13 + n_side]
    side_out = refs[13 + n_side:13 + 2 * n_side]
    bufs = refs[13 + 2 * n_side:]
    _side_cast(side_in, side_out)
    rows = h_ref.shape[0]
    nchunk = rows // CHUNK
    proj_rows = min(rows, PROJ_ROWS)
    row_parts = rows // proj_rows

    def project(piece):
        lo = (piece // row_parts) * MXU_COLS
        rs = slice((piece % row_parts) * proj_rows, (piece % row_parts + 1) * proj_rows)
        w_ref, off = (wgu_ref, lo) if lo < 2 * HG_WIDTH else (wv_ref, lo - 2 * HG_WIDTH)
        zb_ref[rs, lo:lo + MXU_COLS] = jnp.dot(h_ref[rs, :], w_ref[:, off:off + MXU_COLS],
                                               preferred_element_type=F32).astype(BF16)

    n_pieces = row_parts * 3 * HG_WIDTH // MXU_COLS
    pieces_before = [(k + 1) * n_pieces // (HG_HEADS * nchunk) - k * n_pieces // (HG_HEADS * nchunk)
                     for k in range(HG_HEADS * nchunk)]

    ric = lax.broadcasted_iota(jnp.int32, (rows, LANES), 0) & (CHUNK - 1)
    blk = lax.broadcasted_iota(jnp.int32, (CHUNK, LANES), 0) // SUB
    r2 = lax.broadcasted_iota(jnp.int32, (CHUNK, CHUNK), 0)
    c2 = lax.broadcasted_iota(jnp.int32, (CHUNK, CHUNK), 1)
    nt = (((1,), (1,)), ((), ()))
    tn_dims = (((0,), (0,)), ((), ()))
    nsub = CHUNK // SUB
    emitted = [0]

    def before_chunk(k):
        for _ in range(pieces_before[k]):
            project(emitted[0])
            emitted[0] += 1

    for hd in range(HG_HEADS):
        _gla_local_head(hd, nchunk, ric, blk, r2, c2, nt, tn_dims, nsub, ff_ref, fb_ref, i_ref, q_ref, lb_ref,
                        oi_ref, qe_ref, u_ref, d_ref, bufs[6 * (hd % 2):6 * (hd % 2) + 6], before_chunk)
    assert emitted[0] == n_pieces


def _gla_local_head(hd, nchunk, ric, blk, r2, c2, nt, tn_dims, nsub, ff_ref, fb_ref, i_ref, q_ref, lb_ref,
                    oi_ref, qe_ref, u_ref, d_ref, bufs, before_chunk):
    pf_buf, pb_buf, gb_buf, kf_buf, kb_buf, q_buf = bufs
    cs = slice(hd * HG_DK, (hd + 1) * HG_DK)
    lbf = lb_ref[0, hd]
    lbb = lb_ref[1, hd]
    ff = lbf + (1.0 - lbf) * jax.nn.sigmoid(ff_ref[:, cs].astype(F32))
    pf_buf[...] = _seg_cumsum(jnp.log(ff) * LOG2E, ric)
    kf_buf[...] = 1.0 - ff
    fb = lbb + (1.0 - lbb) * jax.nn.sigmoid(fb_ref[:, cs].astype(F32))
    gb = jnp.log(fb) * LOG2E
    pb_buf[...] = _seg_cumsum(gb, ric)
    gb_buf[...] = gb
    kb_buf[...] = 1.0 - fb
    q_buf[...] = _silu(q_ref[:, cs].astype(F32)) * (HG_DK ** -0.5)

    for c in range(nchunk):
        before_chunk(hd * nchunk + c)
        base = c * CHUNK
        sl = pl.ds(base, CHUNK)
        pf = pf_buf[sl, :]
        pb = pb_buf[sl, :]
        gbc = gb_buf[sl, :]
        kf = kf_buf[sl, :]
        kb = kb_buf[sl, :]
        q = q_buf[sl, :]
        totb = pb_buf[pl.ds(base + CHUNK - 1, 1), :]
        suf = totb - pb + gbc
        mids, nids = [], []
        for i in range(nsub):
            mids.append(pf_buf[pl.ds(base + SUB * i + SUB // 2 - 1, 1), :])
            rmid = pl.ds(base + SUB * i + SUB // 2, 1)
            nids.append(totb - pb_buf[rmid, :] + gb_buf[rmid, :])
        mid_rows = jnp.concatenate([jnp.broadcast_to(m, (SUB, LANES)) for m in mids], axis=0)
        nid_rows = jnp.concatenate([jnp.broadcast_to(m, (SUB, LANES)) for m in nids], axis=0)
        qf_all = q * jnp.exp2(jnp.minimum(pf - mid_rows, EXP_CLAMP))
        qb_all = q * jnp.exp2(jnp.minimum(suf - nid_rows, EXP_CLAMP))
        qf_seg, kf_seg, qb_seg, kb_seg = [], [], [], []
        for i in range(nsub):
            qf_seg.append(jnp.where(blk == i, qf_all, 0.0))
            qb_seg.append(jnp.where(blk == i, qb_all, 0.0))
            hi = SUB * (i + 1)
            top = kf[:hi] * jnp.exp2(jnp.minimum(mids[i] - pf[:hi], EXP_CLAMP))
            kf_seg.append(top if hi == CHUNK else
                          jnp.concatenate([top, jnp.zeros((CHUNK - hi, LANES), F32)], axis=0))
            lo = SUB * i
            bot = kb[lo:] * jnp.exp2(jnp.minimum(nids[i] - suf[lo:], EXP_CLAMP))
            kb_seg.append(bot if lo == 0 else
                          jnp.concatenate([jnp.zeros((lo, LANES), F32), bot], axis=0))
        qf_big = jnp.concatenate(qf_seg, axis=1).astype(BF16)
        kf_big = jnp.concatenate(kf_seg, axis=1).astype(BF16)
        qb_big = jnp.concatenate(qb_seg, axis=1).astype(BF16)
        kb_big = jnp.concatenate(kb_seg, axis=1).astype(BF16)
        sc_f = lax.dot_general(qf_big, kf_big, nt, preferred_element_type=F32)
        sc_b = lax.dot_general(qb_big, kb_big, nt, preferred_element_type=F32)
        a = (jnp.where(c2 <= r2, sc_f, 0.0) + jnp.where(c2 >= r2, sc_b, 0.0)).astype(BF16)
        v = i_ref[sl, cs]
        oi_ref[sl, cs] = jnp.dot(a, v, preferred_element_type=F32).astype(BF16)
        qe_ref[sl, 2 * hd * HG_DK:2 * (hd + 1) * HG_DK] = jnp.concatenate(
            [q * jnp.exp2(pf), q * jnp.exp2(suf)], axis=1).astype(BF16)
        totf = pf_buf[pl.ds(base + CHUNK - 1, 1), :]
        kt = jnp.concatenate([kf * jnp.exp2(totf - pf), kb * jnp.exp2(pb - gbc)], axis=1).astype(BF16)
        u = lax.dot_general(v, kt, tn_dims, preferred_element_type=F32).astype(BF16)
        u_ref[0, hd, c] = u[:, :HG_DK]
        u_ref[1, hd, c] = u[:, HG_DK:]
        d_ref[0, hd, c * SUBLANES:(c + 1) * SUBLANES, :] = jnp.broadcast_to(jnp.exp2(totf), (SUBLANES, HG_DK))
        d_ref[1, hd, c * SUBLANES:(c + 1) * SUBLANES, :] = jnp.broadcast_to(jnp.exp2(totb), (SUBLANES, HG_DK))


def _gla_local(h, w_gu, w_v, z_a, lb4, rows_per_block, side=()):
    m = h.shape[0]
    nb = m // rows_per_block
    nchunk = rows_per_block // CHUNK
    n_chunks = m // CHUNK
    resident = lambda a: pl.BlockSpec(a.shape, lambda b: (0,) * a.ndim, pipeline_mode=pl.Buffered(1))
    col = lambda k: pl.BlockSpec((rows_per_block, HG_WIDTH), lambda b, k=k: (b, k))
    side_in, side_out, side_shapes = _side_cast_specs(side, nb, lambda b: b)
    return pl.pallas_call(
        functools.partial(_gla_local_kernel, n_side=len(side)),
        grid=(nb,),
        in_specs=[pl.BlockSpec((rows_per_block, D_MODEL), lambda b: (b, 0)),
                  resident(w_gu), resident(w_v), col(0), col(1), col(2), col(3), resident(lb4)] + side_in,
        out_specs=[pl.BlockSpec((rows_per_block, 3 * HG_WIDTH), lambda b: (b, 0)),
                   pl.BlockSpec((rows_per_block, HG_WIDTH), lambda b: (b, 0)),
                   pl.BlockSpec((rows_per_block, 2 * HG_WIDTH), lambda b: (b, 0)),
                   pl.BlockSpec((2, HG_HEADS, nchunk, HG_DK, HG_DK), lambda b: (0, 0, b, 0, 0)),
                   pl.BlockSpec((2, HG_HEADS, nchunk * SUBLANES, HG_DK), lambda b: (0, 0, b, 0))] + side_out,
        out_shape=[jax.ShapeDtypeStruct((m, 3 * HG_WIDTH), BF16),
                   jax.ShapeDtypeStruct((m, HG_WIDTH), BF16),
                   jax.ShapeDtypeStruct((m, 2 * HG_WIDTH), BF16),
                   jax.ShapeDtypeStruct((2, HG_HEADS, n_chunks, HG_DK, HG_DK), BF16),
                   jax.ShapeDtypeStruct((2, HG_HEADS, n_chunks * SUBLANES, HG_DK), F32)] + side_shapes,
        scratch_shapes=[pltpu.VMEM((rows_per_block, HG_DK), F32) for _ in range(12)],
        compiler_params=_params(("parallel",), 56),
        name="gla_local",
    )(h, w_gu, w_v, z_a, z_a, z_a, z_a, lb4, *[s[0] for s in side])


def _gla_finish_kernel(*refs, nchunk, n_chunks, n_side):
    oi_ref, qe_ref, gt_ref, gain_ref, u_ref, d_ref, s0_ref = refs[:7]
    side_in = refs[7:7 + n_side]
    o_ref = refs[7 + n_side]
    side_out = refs[8 + n_side:8 + 2 * n_side]
    s_all = refs[-1]
    _side_cast(side_in, side_out)

    @pl.when(pl.program_id(1) == 0)
    def _():
        def step(c, carry):
            sf, sb = carry
            cb = n_chunks - 1 - c
            s_all[0, c] = sf.astype(BF16)
            s_all[1, cb] = sb.astype(BF16)
            df = d_ref[0, 0, pl.ds(pl.multiple_of(c * SUBLANES, SUBLANES), 1), :]
            db = d_ref[1, 0, pl.ds(pl.multiple_of(cb * SUBLANES, SUBLANES), 1), :]
            return (sf * df + u_ref[0, 0, c].astype(F32), sb * db + u_ref[1, 0, cb].astype(F32))

        lax.fori_loop(0, n_chunks, step, (s0_ref[0, 0], s0_ref[1, 0]), unroll=4)

    nt = (((1,), (1,)), ((), ()))
    gain = gain_ref[0]
    first = pl.program_id(1) * nchunk
    for c in range(nchunk):
        sl = slice(c * CHUNK, (c + 1) * CHUNK)
        scat = jnp.concatenate([s_all[0, first + c], s_all[1, first + c]], axis=1)
        o = oi_ref[sl, :].astype(F32) + lax.dot_general(qe_ref[sl, :], scat, nt, preferred_element_type=F32)
        o = _rms(o) * gain
        o_ref[sl, :] = (o * _silu(gt_ref[sl, :].astype(F32))).astype(BF16)


def _gla_finish(oi, qe, z_b, hg_gain, s_inc, s_dec, s0, rows_per_block, side=()):
    m = oi.shape[0]
    nb = m // rows_per_block
    nchunk = rows_per_block // CHUNK
    n_chunks = m // CHUNK
    side_in, side_out, side_shapes = _side_cast_specs(side, HG_HEADS * nb, lambda h, b: h * nb + b)
    return pl.pallas_call(
        functools.partial(_gla_finish_kernel, nchunk=nchunk, n_chunks=n_chunks, n_side=len(side)),
        grid=(HG_HEADS, nb),
        in_specs=[pl.BlockSpec((rows_per_block, HG_DK), lambda h, b: (b, h)),
                  pl.BlockSpec((rows_per_block, 2 * HG_DK), lambda h, b: (b, h)),
                  pl.BlockSpec((rows_per_block, HG_DK), lambda h, b: (b, h)),
                  pl.BlockSpec((1, 1, HG_DK), lambda h, b: (h, 0, 0)),
                  pl.BlockSpec((2, 1, n_chunks, HG_DK, HG_DK), lambda h, b: (0, h, 0, 0, 0)),
                  pl.BlockSpec((2, 1, n_chunks * SUBLANES, HG_DK), lambda h, b: (0, h, 0, 0)),
                  pl.BlockSpec((2, 1, HG_DK, HG_DK), lambda h, b: (0, h, 0, 0))] + side_in,
        out_specs=[pl.BlockSpec((rows_per_block, HG_DK), lambda h, b: (b, h))] + side_out,
        out_shape=[jax.ShapeDtypeStruct((m, HG_WIDTH), BF16)] + side_shapes,
        scratch_shapes=[pltpu.VMEM((2, n_chunks, HG_DK, HG_DK), BF16)],
        compiler_params=_params(("parallel", "arbitrary"), 48),
        name="gla_finish",
    )(oi, qe, z_b, hg_gain.reshape(HG_HEADS, 1, HG_DK), s_inc, s_dec, s0, *[s[0] for s in side])


def _residual_epilogue(mix, x_ref, vec_ref, xo_ref, ho_ref, rs):
    xn = x_ref[rs, :] + _rms(mix) * vec_ref[0:1]
    xo_ref[rs, :] = xn
    if ho_ref is not None:
        hn = (_rms(xn) * vec_ref[1:2] + vec_ref[2:3]).astype(BF16)
        outs = ho_ref if isinstance(ho_ref, (tuple, list)) else (ho_ref,)
        width = hn.shape[1] // len(outs)
        for k, out in enumerate(outs):
            out[rs, :] = hn[:, k * width:(k + 1) * width]


def _spatial_chunk(u_raw, v_raw, w_ref, bias_ref, lng_ref, lnb_ref):
    v = _gelu_tanh(v_raw.astype(F32))
    xc = v - jnp.mean(v, axis=-1, keepdims=True)
    y = (xc * lax.rsqrt(jnp.mean(xc * xc, axis=-1, keepdims=True) + EPS) * lng_ref[...] + lnb_ref[...]).astype(BF16)
    u = _gelu_tanh(u_raw.astype(F32))
    parts = []
    for g in range(SG_GROUPS):
        cs = slice(g * SG_CH, (g + 1) * SG_CH)
        mixed = jnp.dot(w_ref[g], y[:, cs], preferred_element_type=F32) + bias_ref[:, cs]
        parts.append((u[:, cs] * mixed).astype(BF16))
    return jnp.concatenate(parts, axis=1)


def _mm_res_kernel(*refs, n_a, n_sp, emit_h):
    a_refs = refs[:n_a]
    sp_refs = refs[n_a:n_a + 6 * n_sp]
    n_w = n_a + n_sp
    w_refs = refs[n_a + 6 * n_sp:n_a + 6 * n_sp + n_w]
    x_ref, vec_ref, xo_ref = refs[n_a + 6 * n_sp + n_w:n_a + 6 * n_sp + n_w + 3]
    ho_ref = refs[n_a + 6 * n_sp + n_w + 3] if emit_h else None
    for rs in _row_subblocks(x_ref.shape[0], SG_CHUNK):
        mix = jnp.dot(a_refs[0][rs, :], w_refs[0][...], preferred_element_type=F32)
        for a_ref, w_ref in zip(a_refs[1:], w_refs[1:]):
            mix = mix + jnp.dot(a_ref[rs, :], w_ref[...], preferred_element_type=F32)
        if n_sp:
            u_ref, v_ref = sp_refs[:2]
            s = _spatial_chunk(u_ref[rs, :], v_ref[rs, :], *sp_refs[2:])
            mix = mix + jnp.dot(s, w_refs[-1][...], preferred_element_type=F32)
        _residual_epilogue(mix, x_ref, vec_ref, xo_ref, ho_ref, rs)


def _mm_res(a_list, w, x, vec, tm, emit_h=True, spatial=None):
    m = x.shape[0]
    n_a = len(a_list)
    widths = [a.shape[1] for a in a_list] + ([SG_WIDTH] if spatial else [])
    offs = np.cumsum([0] + widths)
    sp_specs, sp_args = [], []
    if spatial:
        z_b = spatial[0]
        sp_specs = [pl.BlockSpec((tm, SG_WIDTH), lambda i: (i, 1)), pl.BlockSpec((tm, SG_WIDTH), lambda i: (i, 2)),
                    pl.BlockSpec((SG_GROUPS, SG_CHUNK, SG_CHUNK), lambda i: (0, 0, 0)),
                    pl.BlockSpec((SG_CHUNK, SG_WIDTH), lambda i: (0, 0)),
                    pl.BlockSpec((1, SG_WIDTH), lambda i: (0, 0)), pl.BlockSpec((1, SG_WIDTH), lambda i: (0, 0))]
        sp_args = [z_b, z_b, *spatial[1:]]
    in_specs = ([pl.BlockSpec((tm, a.shape[1]), lambda i: (i, 0)) for a in a_list] + sp_specs
                + [pl.BlockSpec((wd, D_MODEL), lambda i, k=int(o) // wd: (k, 0)) for wd, o in zip(widths, offs)]
                + [pl.BlockSpec((tm, D_MODEL), lambda i: (i, 0)),
                   pl.BlockSpec((8, D_MODEL), lambda i: (0, 0))])
    out_specs = [pl.BlockSpec((tm, D_MODEL), lambda i: (i, 0))]
    out_shape = [jax.ShapeDtypeStruct((m, D_MODEL), F32)]
    if emit_h:
        out_specs.append(pl.BlockSpec((tm, D_MODEL), lambda i: (i, 0)))
        out_shape.append(jax.ShapeDtypeStruct((m, D_MODEL), BF16))
    return pl.pallas_call(
        functools.partial(_mm_res_kernel, n_a=n_a, n_sp=1 if spatial else 0, emit_h=emit_h),
        grid=(m // tm,),
        in_specs=in_specs, out_specs=out_specs, out_shape=out_shape,
        compiler_params=_params(("parallel",), 56),
        name="mm_residual",
    )(*a_list, *sp_args, *[w] * len(widths), x, vec)


def _mlp_kernel(*refs, n_h, n_side):
    h_ref, w1_ref, w2_ref, x_ref, vec_ref = refs[:5]
    side_in = refs[5:5 + n_side]
    xo_ref = refs[5 + n_side]
    ho_ref = tuple(refs[6 + n_side:6 + n_side + n_h]) or None
    n_out = 1 + n_h
    side_out = refs[5 + n_side + n_out:5 + 2 * n_side + n_out]
    acc_ref = refs[-1]
    _side_cast(side_in, side_out)
    j = pl.program_id(1)

    last = pl.num_programs(1) - 1

    def partial_product(rs):
        a = jnp.dot(h_ref[rs, :], w1_ref[...], preferred_element_type=F32)
        a = jnp.square(jnp.maximum(a, 0.0)).astype(BF16)
        return jnp.dot(a, w2_ref[...], preferred_element_type=F32)

    @pl.when(j == 0)
    def _():
        for rs in _row_subblocks(h_ref.shape[0]):
            acc_ref[rs, :] = partial_product(rs)

    @pl.when(jnp.logical_and(j > 0, j < last))
    def _():
        for rs in _row_subblocks(h_ref.shape[0]):
            acc_ref[rs, :] += partial_product(rs)

    @pl.when(j == last)
    def _():
        for rs in _row_subblocks(h_ref.shape[0]):
            _residual_epilogue(acc_ref[rs, :] + partial_product(rs), x_ref, vec_ref, xo_ref, ho_ref, rs)


def _mlp(h, w1, w2, x, vec, tm, tf, n_h, side=()):
    m = x.shape[0]
    nj = D_FF // tf
    out_specs = [pl.BlockSpec((tm, D_MODEL), lambda i, j: (i, 0))]
    out_shape = [jax.ShapeDtypeStruct((m, D_MODEL), F32)]
    for _ in range(n_h):
        out_specs.append(pl.BlockSpec((tm, D_MODEL // n_h), lambda i, j: (i, 0)))
        out_shape.append(jax.ShapeDtypeStruct((m, D_MODEL // n_h), BF16))
    side_in, side_out, side_shapes = _side_cast_specs(side, (m // tm) * nj, lambda i, j: i * nj + j)
    return pl.pallas_call(
        functools.partial(_mlp_kernel, n_h=n_h, n_side=len(side)),
        grid=(m // tm, nj),
        in_specs=[pl.BlockSpec((tm, D_MODEL), lambda i, j: (i, 0)),
                  pl.BlockSpec((D_MODEL, tf), lambda i, j: (0, j)),
                  pl.BlockSpec((tf, D_MODEL), lambda i, j: (j, 0)),
                  pl.BlockSpec((tm, D_MODEL), lambda i, j: (i, 0)),
                  pl.BlockSpec((8, D_MODEL), lambda i, j: (0, 0))] + side_in,
        out_specs=out_specs + side_out, out_shape=out_shape + side_shapes,
        scratch_shapes=[pltpu.VMEM((tm, D_MODEL), F32)],
        compiler_params=_params(("parallel", "arbitrary"), 56),
        name="mlp",
    )(h, w1, w2, x, vec, *[s[0] for s in side])


def _dft_constants():
    def cs(n):
        k = np.arange(n)
        ang = 2.0 * np.pi * ((k[:, None] * k[None, :]) % n) / n
        return np.cos(ang) / np.sqrt(n), np.sin(ang) / np.sqrt(n)

    cc, sc = cs(FT_CH)
    c1, s1 = cs(FFT_L1)
    m1 = np.concatenate([c1, -s1], axis=0)
    c2, s2 = cs(FFT_L2)
    m2 = np.block([[c2, s2], [-s2, c2]])
    u1 = np.arange(FFT_L1)[:, None]
    t2 = np.arange(FFT_L2)[None, :]
    ang = 2.0 * np.pi * ((u1 * t2) % SEQ) / SEQ
    tw_c = np.cos(ang).reshape(SEQ, 1)
    tw_s = np.sin(ang).reshape(SEQ, 1)
    f = lambda a: jnp.asarray(a, F32)
    return f(cc), f(sc), f(m1), f(m2), f(tw_c), f(tw_s)


def _swap_row_factors(a, n_outer, n_inner):
    return a.reshape(n_outer, n_inner, a.shape[1]).transpose(1, 0, 2).reshape(a.shape)


def _pos_dft1_kernel(x_ref, m_ref, zr_ref, zi_ref):
    for k in range(x_ref.shape[0] // FFT_L1):
        sl = slice(k * FFT_L1, (k + 1) * FFT_L1)
        z = jnp.dot(m_ref[...], x_ref[sl, :], preferred_element_type=F32)
        zr_ref[sl, :] = z[:FFT_L1].astype(BF16)
        zi_ref[sl, :] = z[FFT_L1:].astype(BF16)


def _pos_dft1(xt, m1, slabs):
    rows = slabs * FFT_L1
    blk = pl.BlockSpec((rows, xt.shape[1]), lambda j: (j, 0))
    return pl.pallas_call(
        _pos_dft1_kernel,
        grid=(SEQ // rows,),
        in_specs=[blk, pl.BlockSpec((2 * FFT_L1, FFT_L1), lambda j: (0, 0))],
        out_specs=[blk, blk],
        out_shape=[jax.ShapeDtypeStruct(xt.shape, BF16)] * 2,
        compiler_params=_params(("parallel",), 32),
        name="pos_dft1",
    )(xt, m1)


def _pos_dft2_kernel(zr_ref, zi_ref, c_ref, s_ref, m_ref, wc_ref, ws_ref, y_ref, yr_buf, yi_buf):
    for k in range(zr_ref.shape[0] // FFT_L2):
        sl = slice(k * FFT_L2, (k + 1) * FFT_L2)
        zr = zr_ref[sl, :].astype(F32)
        zi = zi_ref[sl, :].astype(F32)
        c = c_ref[sl, :]
        s = s_ref[sl, :]
        t = jnp.concatenate([(zr * c + zi * s).astype(BF16), (zi * c - zr * s).astype(BF16)], axis=0)
        y = jnp.dot(m_ref[...], t, preferred_element_type=F32)
        yr_buf[sl, :] = y[:FFT_L2].astype(BF16)
        yi_buf[sl, :] = y[FFT_L2:].astype(BF16)
    for g in range(zr_ref.shape[1] // FT_CH):
        cs = slice(g * FT_CH, (g + 1) * FT_CH)
        y_ref[:, cs] = (jnp.dot(yr_buf[:, cs], wc_ref[...], preferred_element_type=F32)
                        + jnp.dot(yi_buf[:, cs], ws_ref[...], preferred_element_type=F32)).astype(BF16)


def _pos_dft2(zr, zi, tw_c, tw_s, m2, wc, ws, slabs):
    rows = slabs * FFT_L2
    width = zr.shape[1]
    blk = pl.BlockSpec((rows, width), lambda j: (j, 0))
    tw = pl.BlockSpec((rows, 1), lambda j: (j, 0))
    const = lambda a: pl.BlockSpec(a.shape, lambda j: (0, 0))
    return pl.pallas_call(
        _pos_dft2_kernel,
        grid=(SEQ // rows,),
        in_specs=[blk, blk, tw, tw, const(m2), const(wc), const(ws)],
        out_specs=blk,
        out_shape=jax.ShapeDtypeStruct(zr.shape, BF16),
        scratch_shapes=[pltpu.VMEM((rows, width), BF16), pltpu.VMEM((rows, width), BF16)],
        compiler_params=_params(("parallel",), 40),
        name="pos_dft2",
    )(zr, zi, tw_c, tw_s, m2, wc, ws)


def _fourier_mix(h_parts):
    cc, sc, m1, m2, tw_c, tw_s = _dft_constants()
    m1, m2, cc, sc = m1.astype(BF16), m2.astype(BF16), cc.astype(BF16), sc.astype(BF16)
    hts = [_swap_row_factors(h, FFT_L1, FFT_L2) for h in h_parts]
    zs = [_pos_dft1(ht, m1, DFT_ROWS // FFT_L1) for ht in hts]
    zs = [[_swap_row_factors(z, FFT_L2, FFT_L1) for z in zri] for zri in zs]
    yps = [_pos_dft2(zr, zi, tw_c, tw_s, m2, cc, sc, DFT_ROWS // FFT_L2) for zr, zi in zs]
    return [_swap_row_factors(yp, FFT_L1, FFT_L2) for yp in yps]


def _pack_rows(*rows):
    rows = [r.reshape(1, D_MODEL).astype(F32) for r in rows]
    rows += [jnp.zeros((1, D_MODEL), F32)] * (8 - len(rows))
    return jnp.concatenate(rows, axis=0)


def kernel(x, c, ctx, c_ctx, w_ada, b_ada, norm_gain, w_in, w_out, lb_raw, hg_norm_gain,
           sg_w, sg_b, sg_ln_gain, sg_ln_bias, w_fourier, w_mlp_in, w_mlp_out):
    assert x.shape == (1, SEQ, D_MODEL) and ctx.shape == (1, CTX_LEN, D_MODEL)
    xs = x.reshape(SEQ, D_MODEL)
    cs = ctx.reshape(CTX_LEN, D_MODEL)

    mods = _ada_mods(jnp.concatenate([c.reshape(1, D_MODEL), c_ctx.reshape(1, D_MODEL)], axis=0), w_ada, b_ada)
    mod = lambda l, r, j: mods[l, r, j * D_MODEL:(j + 1) * D_MODEL]

    lb = jnp.cumsum(jax.nn.softmax(lb_raw.astype(F32), axis=1), axis=1)[:, 0]
    lb4 = lb.reshape(2, HG_HEADS, 1, HG_DK)

    pre = lambda gain, l, r, j: _pack_rows(gain * (1.0 + mod(l, r, j + 1)), mod(l, r, j))
    post = lambda gain, l, j, *nxt: _pack_rows(mod(l, 0, j) * gain, *nxt)

    z_a, h0, w_gu, w_v = _norm_mm(xs, pre(norm_gain[0, 0], 0, 0, 0), w_in, 4 * HG_WIDTH, MM_ROWS, MM_COLS,
                                  emit_h=True, keep_w=True,
                                  side=[(w_in, 0, 2 * HG_WIDTH, 2), (w_in, 0, HG_WIDTH, 6)])
    (z_ctx,) = _norm_mm(cs, pre(norm_gain[0, 0], 0, 1, 0), w_in, 3 * HG_WIDTH, CTX_LEN, MM_COLS)
    s_ctx = _ctx_states(z_ctx, lb4)
    z_b, o_intra, q_dec, s_inc, s_dec, w_out0, w1_0, w2_0 = _gla_local(
        h0, w_gu, w_v, z_a, lb4, GLA_LOCAL_ROWS, side=[(w_out, 0), (w_mlp_in, 0), (w_mlp_out, 0)])
    o, w_fou = _gla_finish(o_intra, q_dec, z_b, hg_norm_gain[0], s_inc, s_dec, s_ctx, GLA_FINISH_ROWS,
                           side=[(w_fourier, 0)])
    bias_full = jnp.repeat(sg_b[0].T.astype(F32), SG_CH, axis=1)
    vec = post(norm_gain[0, 1], 0, 2, *pre(norm_gain[0, 2], 0, 0, 3)[:2])
    x1, h = _mm_res([o], w_out0, xs, vec, MM_ROWS,
                    spatial=(z_b, sg_w[0].astype(BF16), bias_full, sg_ln_gain[0].reshape(1, SG_WIDTH),
                             sg_ln_bias[0].reshape(1, SG_WIDTH)))
    vec = post(norm_gain[0, 3], 0, 5, *pre(norm_gain[1, 0], 1, 0, 0)[:2])
    x2, h_a, h_b, w1_1, w2_1 = _mlp(h, w1_0, w2_0, x1, vec, MM_ROWS, MLP_FF_COLS, 2,
                                    side=[(w_mlp_in, 1), (w_mlp_out, 1)])

    y_parts = _fourier_mix([h_a, h_b])
    vec = post(norm_gain[1, 1], 1, 2, *pre(norm_gain[1, 2], 1, 0, 3)[:2])
    x3, h = _mm_res(y_parts, w_fou, x2, vec, MM_ROWS)
    (x4,) = _mlp(h, w1_1, w2_1, x3, post(norm_gain[1, 3], 1, 5), MM_ROWS, MLP_FF_COLS, 0)
    return x4.reshape(1, SEQ, D_MODEL)
```

```python
import functools

import numpy as np
import jax
import jax.numpy as jnp
from jax import lax
from jax.experimental import pallas as pl
from jax.experimental.pallas import tpu as pltpu

D_MODEL = 2048
SEQ = 8192
DEPTH = 2
CTX_LEN = 256
CHUNK = 64
SUB = 16
ADA_COLS = 2048
MM_ROWS = 512
MM_COLS = 1024
MLP_FF_COLS = 1024
WEIGHT_SLOTS = 3
SUBROWS = 256
GLA_LOCAL_ROWS = 256
GLA_FINISH_ROWS = 4096
DFT_ROWS = 1024
HG_HEADS = 8
HG_DK = 128
HG_WIDTH = HG_HEADS * HG_DK
SG_GROUPS = 8
SG_CH = 128
SG_WIDTH = SG_GROUPS * SG_CH
SG_CHUNK = 128
FT_GROUPS = 4
FT_CH = D_MODEL // FT_GROUPS
FFT_L1 = 64
FFT_L2 = 128
IN_WIDTH = 5 * HG_WIDTH + 2 * SG_WIDTH
D_FF = 4 * D_MODEL
N_MOD = 6
EPS = 1e-6
EXP_CLAMP = 115.0
LOG2E = 1.0 / float(np.log(2.0))
LANES = 128
SUBLANES = 8
MXU_COLS = 256
PROJ_ROWS = 256

F32 = jnp.float32
BF16 = jnp.bfloat16
MIB = 1024 * 1024


def _params(semantics, vmem_mib):
    return pltpu.CompilerParams(dimension_semantics=semantics, vmem_limit_bytes=vmem_mib * MIB)


def _rms(x):
    return x * lax.rsqrt(jnp.mean(x * x, axis=-1, keepdims=True) + EPS)


def _silu(x):
    return x * jax.nn.sigmoid(x)


def _row_subblocks(rows, step=SUBROWS):
    step = min(rows, step)
    return [slice(s, s + step) for s in range(0, rows, step)]


def _gelu_tanh(x):
    cdf = 0.5 * (1.0 + jnp.tanh(float(np.sqrt(2.0 / np.pi)) * (x + 0.044715 * (x * x * x))))
    return x * cdf


def _ada_kernel(c_ref, w_ref, b_ref, o_ref, s_buf):
    @pl.when((pl.program_id(0) == 0) & (pl.program_id(1) == 0))
    def _():
        s_buf[...] = _silu(c_ref[...])

    tn = o_ref.shape[-1]
    for r in range(2):
        s = s_buf[r]
        cols = [jnp.sum(w_ref[0, :, j * LANES:(j + 1) * LANES] * s, axis=0, keepdims=True)
                for j in range(tn // LANES)]
        o_ref[0, r:r + 1, :] = jnp.concatenate(cols, axis=1) + b_ref[0]


def _ada_mods(c2, w_ada, b_ada):
    tn = ADA_COLS
    n = N_MOD * D_MODEL
    cb = jnp.broadcast_to(c2[:, :, None], (2, D_MODEL, LANES))
    return pl.pallas_call(
        _ada_kernel,
        grid=(DEPTH, n // tn),
        in_specs=[pl.BlockSpec((2, D_MODEL, LANES), lambda l, j: (0, 0, 0)),
                  pl.BlockSpec((1, D_MODEL, tn), lambda l, j: (l, 0, j)),
                  pl.BlockSpec((1, 1, tn), lambda l, j: (l, 0, j))],
        out_specs=pl.BlockSpec((1, 2, tn), lambda l, j: (l, 0, j)),
        out_shape=jax.ShapeDtypeStruct((DEPTH, 2, n), F32),
        scratch_shapes=[pltpu.VMEM((2, D_MODEL, LANES), F32)],
        compiler_params=_params(("arbitrary", "arbitrary"), 40),
        name="ada_mods",
    )(cb, w_ada, b_ada.reshape(DEPTH, 1, n))


def _side_cast_specs(weights, n_steps, step_of):
    in_specs, out_specs, out_shapes = [], [], []
    for w, layer, *cols in weights:
        width, cidx = cols if cols else (w.shape[2], 0)
        assert w.shape[1] % n_steps == 0, "every grid step must get an equal row slab"
        rows = w.shape[1] // n_steps
        in_specs.append(pl.BlockSpec((1, rows, width),
                                     lambda *g, layer=layer, cidx=cidx: (layer, step_of(*g), cidx)))
        out_specs.append(pl.BlockSpec((rows, width), lambda *g: (step_of(*g), 0)))
        out_shapes.append(jax.ShapeDtypeStruct((w.shape[1], width), BF16))
    return in_specs, out_specs, out_shapes


def _side_cast(side_in, side_out):
    for wi, wo in zip(side_in, side_out):
        wo[...] = wi[0].astype(BF16)


def _norm_mm_kernel(*refs, emit_h, n_side, keep_w):
    x_ref, vec_ref, w_ref = refs[:3]
    side_in = refs[3:3 + n_side]
    o_ref = refs[3 + n_side]
    n_out = 2 if emit_h else 1
    side_out = refs[3 + n_side + n_out:3 + 2 * n_side + n_out]
    scratch = refs[3 + 2 * n_side + n_out:]
    h_ref = refs[3 + n_side + 1] if emit_h else scratch[0]
    _side_cast(side_in, side_out)
    if keep_w:
        wb_ref = scratch[-1]

        @pl.when(pl.program_id(0) == 0)
        def _():
            wb_ref[pl.program_id(1)] = w_ref[0].astype(BF16)

        w = wb_ref[pl.program_id(1)]
    else:
        w = w_ref[0].astype(BF16)

    @pl.when(pl.program_id(1) == 0)
    def _():
        for rs in _row_subblocks(x_ref.shape[0]):
            h = (_rms(x_ref[rs, :]) * vec_ref[0:1] + vec_ref[1:2]).astype(BF16)
            h_ref[rs, :] = h
            o_ref[rs, :] = jnp.dot(h, w, preferred_element_type=F32).astype(o_ref.dtype)

    @pl.when(pl.program_id(1) > 0)
    def _():
        o_ref[...] = jnp.dot(h_ref[...], w, preferred_element_type=F32).astype(o_ref.dtype)


def _norm_mm(x, vec, w, n_cols, tm, tn, emit_h=False, side=(), keep_w=False):
    m = x.shape[0]
    nj = n_cols // tn
    out_specs = [pl.BlockSpec((tm, tn), lambda i, j: (i, j))]
    out_shape = [jax.ShapeDtypeStruct((m, n_cols), BF16)]
    scratch = []
    if emit_h:
        out_specs.append(pl.BlockSpec((tm, D_MODEL), lambda i, j: (i, 0)))
        out_shape.append(jax.ShapeDtypeStruct((m, D_MODEL), BF16))
    else:
        scratch.append(pltpu.VMEM((tm, D_MODEL), BF16))
    if keep_w:
        scratch.append(pltpu.VMEM((nj, D_MODEL, tn), BF16))
        w_spec = pl.BlockSpec((1, D_MODEL, tn), lambda i, j: (0, 0, jnp.where(i == 0, j, nj - 1)))
    else:
        w_spec = pl.BlockSpec((1, D_MODEL, tn), lambda i, j: (0, 0, j))
    side_in, side_out, side_shapes = _side_cast_specs(side, (m // tm) * nj, lambda i, j: i * nj + j)
    return pl.pallas_call(
        functools.partial(_norm_mm_kernel, emit_h=emit_h, n_side=len(side), keep_w=keep_w),
        grid=(m // tm, nj),
        in_specs=[pl.BlockSpec((tm, D_MODEL), lambda i, j: (i, 0)),
                  pl.BlockSpec((8, D_MODEL), lambda i, j: (0, 0)), w_spec] + side_in,
        out_specs=out_specs + side_out, out_shape=out_shape + side_shapes,
        scratch_shapes=scratch,
        compiler_params=_params(("arbitrary", "arbitrary"), 56),
        name="norm_mm",
    )(x, vec, w, *[s[0] for s in side])


def _seg_cumsum(g, row_in_chunk):
    p = g
    s = 1
    while s < CHUNK:
        p = p + jnp.where(row_in_chunk >= s, pltpu.roll(p, s, axis=0), 0.0)
        s *= 2
    return p


def _ctx_state_kernel(f_ref, i_ref, lb_ref, sfin_ref, p_buf, g_buf, k_buf, *, nchunk):
    d = pl.program_id(0)
    rows = f_ref.shape[0]
    lb = lb_ref[0, 0]
    f = lb + (1.0 - lb) * jax.nn.sigmoid(f_ref[...].astype(F32))
    g = jnp.log(f) * LOG2E
    ric = lax.broadcasted_iota(jnp.int32, (rows, LANES), 0) & (CHUNK - 1)
    p_buf[...] = _seg_cumsum(g, ric)
    g_buf[...] = g
    k_buf[...] = 1.0 - f

    s = jnp.zeros((HG_DK, HG_DK), F32)
    for j in range(nchunk):
        c = jnp.where(d == 0, j, nchunk - 1 - j)
        base = pl.multiple_of(c * CHUNK, CHUNK)
        sl = pl.ds(base, CHUNK)
        p = p_buf[sl, :]
        tot = p_buf[pl.ds(base + CHUNK - 1, 1), :]
        ex = jnp.where(d == 0, tot - p, p - g_buf[sl, :])
        kt = (k_buf[sl, :] * jnp.exp2(ex)).astype(BF16)
        u = lax.dot_general(i_ref[sl, :], kt, (((0,), (0,)), ((), ())), preferred_element_type=F32)
        s = s * jnp.exp2(tot) + u
    sfin_ref[0, 0] = s


def _ctx_states(z_ctx, lb4):
    rows = z_ctx.shape[0]
    return pl.pallas_call(
        functools.partial(_ctx_state_kernel, nchunk=rows // CHUNK),
        grid=(2, HG_HEADS),
        in_specs=[pl.BlockSpec((rows, HG_DK), lambda d, h: (0, d * HG_HEADS + h)),
                  pl.BlockSpec((rows, HG_DK), lambda d, h: (0, 2 * HG_HEADS + h)),
                  pl.BlockSpec((1, 1, 1, HG_DK), lambda d, h: (d, h, 0, 0))],
        out_specs=pl.BlockSpec((1, 1, HG_DK, HG_DK), lambda d, h: (d, h, 0, 0)),
        out_shape=jax.ShapeDtypeStruct((2, HG_HEADS, HG_DK, HG_DK), F32),
        scratch_shapes=[pltpu.VMEM((rows, HG_DK), F32) for _ in range(3)],
        compiler_params=_params(("parallel", "parallel"), 32),
        name="ctx_states",
    )(z_ctx, z_ctx, lb4)


def _gla_local_kernel(*refs, n_side):
    h_ref, wgu_ref, wv_ref, ff_ref, fb_ref, i_ref, q_ref, lb_ref = refs[:8]
    side_in = refs[8:8 + n_side]
    zb_ref, oi_ref, qe_ref, u_ref, d_ref = refs[8 + n_side:13 + n_side]
    side_out = refs[13 + n_side:13 + 2 * n_side]
    bufs = refs[13 + 2 * n_side:]
    _side_cast(side_in, side_out)
    rows = h_ref.shape[0]
    nchunk = rows // CHUNK
    proj_rows = min(rows, PROJ_ROWS)
    row_parts = rows // proj_rows

    def project(piece):
        lo = (piece // row_parts) * MXU_COLS
        rs = slice((piece % row_parts) * proj_rows, (piece % row_parts + 1) * proj_rows)
        w_ref, off = (wgu_ref, lo) if lo < 2 * HG_WIDTH else (wv_ref, lo - 2 * HG_WIDTH)
        zb_ref[rs, lo:lo + MXU_COLS] = jnp.dot(h_ref[rs, :], w_ref[:, off:off + MXU_COLS],
                                               preferred_element_type=F32).astype(BF16)

    n_pieces = row_parts * 3 * HG_WIDTH // MXU_COLS
    pieces_before = [(k + 1) * n_pieces // (HG_HEADS * nchunk) - k * n_pieces // (HG_HEADS * nchunk)
                     for k in range(HG_HEADS * nchunk)]

    ric = lax.broadcasted_iota(jnp.int32, (rows, LANES), 0) & (CHUNK - 1)
    blk = lax.broadcasted_iota(jnp.int32, (CHUNK, LANES), 0) // SUB
    r2 = lax.broadcasted_iota(jnp.int32, (CHUNK, CHUNK), 0)
    c2 = lax.broadcasted_iota(jnp.int32, (CHUNK, CHUNK), 1)
    nt = (((1,), (1,)), ((), ()))
    tn_dims = (((0,), (0,)), ((), ()))
    nsub = CHUNK // SUB
    emitted = [0]

    def before_chunk(k):
        for _ in range(pieces_before[k]):
            project(emitted[0])
            emitted[0] += 1

    for hd in range(HG_HEADS):
        _gla_local_head(hd, nchunk, ric, blk, r2, c2, nt, tn_dims, nsub, ff_ref, fb_ref, i_ref, q_ref, lb_ref,
                        oi_ref, qe_ref, u_ref, d_ref, bufs[6 * (hd % 2):6 * (hd % 2) + 6], before_chunk)
    assert emitted[0] == n_pieces


def _gla_local_head(hd, nchunk, ric, blk, r2, c2, nt, tn_dims, nsub, ff_ref, fb_ref, i_ref, q_ref, lb_ref,
                    oi_ref, qe_ref, u_ref, d_ref, bufs, before_chunk):
    pf_buf, pb_buf, gb_buf, kf_buf, kb_buf, q_buf = bufs
    cs = slice(hd * HG_DK, (hd + 1) * HG_DK)
    lbf = lb_ref[0, hd]
    lbb = lb_ref[1, hd]
    ff = lbf + (1.0 - lbf) * jax.nn.sigmoid(ff_ref[:, cs].astype(F32))
    pf_buf[...] = _seg_cumsum(jnp.log(ff) * LOG2E, ric)
    kf_buf[...] = 1.0 - ff
    fb = lbb + (1.0 - lbb) * jax.nn.sigmoid(fb_ref[:, cs].astype(F32))
    gb = jnp.log(fb) * LOG2E
    pb_buf[...] = _seg_cumsum(gb, ric)
    gb_buf[...] = gb
    kb_buf[...] = 1.0 - fb
    q_buf[...] = _silu(q_ref[:, cs].astype(F32)) * (HG_DK ** -0.5)

    for c in range(nchunk):
        before_chunk(hd * nchunk + c)
        base = c * CHUNK
        sl = pl.ds(base, CHUNK)
        pf = pf_buf[sl, :]
        pb = pb_buf[sl, :]
        gbc = gb_buf[sl, :]
        kf = kf_buf[sl, :]
        kb = kb_buf[sl, :]
        q = q_buf[sl, :]
        totb = pb_buf[pl.ds(base + CHUNK - 1, 1), :]
        suf = totb - pb + gbc
        mids, nids = [], []
        for i in range(nsub):
            mids.append(pf_buf[pl.ds(base + SUB * i + SUB // 2 - 1, 1), :])
            rmid = pl.ds(base + SUB * i + SUB // 2, 1)
            nids.append(totb - pb_buf[rmid, :] + gb_buf[rmid, :])
        mid_rows = jnp.concatenate([jnp.broadcast_to(m, (SUB, LANES)) for m in mids], axis=0)
        nid_rows = jnp.concatenate([jnp.broadcast_to(m, (SUB, LANES)) for m in nids], axis=0)
        qf_all = q * jnp.exp2(jnp.minimum(pf - mid_rows, EXP_CLAMP))
        qb_all = q * jnp.exp2(jnp.minimum(suf - nid_rows, EXP_CLAMP))
        qf_seg, kf_seg, qb_seg, kb_seg = [], [], [], []
        for i in range(nsub):
            qf_seg.append(jnp.where(blk == i, qf_all, 0.0))
            qb_seg.append(jnp.where(blk == i, qb_all, 0.0))
            hi = SUB * (i + 1)
            top = kf[:hi] * jnp.exp2(jnp.minimum(mids[i] - pf[:hi], EXP_CLAMP))
            kf_seg.append(top if hi == CHUNK else
                          jnp.concatenate([top, jnp.zeros((CHUNK - hi, LANES), F32)], axis=0))
            lo = SUB * i
            bot = kb[lo:] * jnp.exp2(jnp.minimum(nids[i] - suf[lo:], EXP_CLAMP))
            kb_seg.append(bot if lo == 0 else
                          jnp.concatenate([jnp.zeros((lo, LANES), F32), bot], axis=0))
        qf_big = jnp.concatenate(qf_seg, axis=1).astype(BF16)
        kf_big = jnp.concatenate(kf_seg, axis=1).astype(BF16)
        qb_big = jnp.concatenate(qb_seg, axis=1).astype(BF16)
        kb_big = jnp.concatenate(kb_seg, axis=1).astype(BF16)
        sc_f = lax.dot_general(qf_big, kf_big, nt, preferred_element_type=F32)
        sc_b = lax.dot_general(qb_big, kb_big, nt, preferred_element_type=F32)
        a = (jnp.where(c2 <= r2, sc_f, 0.0) + jnp.where(c2 >= r2, sc_b, 0.0)).astype(BF16)
        v = i_ref[sl, cs]
        oi_ref[sl, cs] = jnp.dot(a, v, preferred_element_type=F32).astype(BF16)
        qe_ref[sl, 2 * hd * HG_DK:2 * (hd + 1) * HG_DK] = jnp.concatenate(
            [q * jnp.exp2(pf), q * jnp.exp2(suf)], axis=1).astype(BF16)
        totf = pf_buf[pl.ds(base + CHUNK - 1, 1), :]
        kt = jnp.concatenate([kf * jnp.exp2(totf - pf), kb * jnp.exp2(pb - gbc)], axis=1).astype(BF16)
        u = lax.dot_general(v, kt, tn_dims, preferred_element_type=F32).astype(BF16)
        u_ref[0, hd, c] = u[:, :HG_DK]
        u_ref[1, hd, c] = u[:, HG_DK:]
        d_ref[0, hd, c * SUBLANES:(c + 1) * SUBLANES, :] = jnp.broadcast_to(jnp.exp2(totf), (SUBLANES, HG_DK))
        d_ref[1, hd, c * SUBLANES:(c + 1) * SUBLANES, :] = jnp.broadcast_to(jnp.exp2(totb), (SUBLANES, HG_DK))


def _gla_local(h, w_gu, w_v, z_a, lb4, rows_per_block, side=()):
    m = h.shape[0]
    nb = m // rows_per_block
    nchunk = rows_per_block // CHUNK
    n_chunks = m // CHUNK
    resident = lambda a: pl.BlockSpec(a.shape, lambda b: (0,) * a.ndim, pipeline_mode=pl.Buffered(1))
    col = lambda k: pl.BlockSpec((rows_per_block, HG_WIDTH), lambda b, k=k: (b, k))
    side_in, side_out, side_shapes = _side_cast_specs(side, nb, lambda b: b)
    return pl.pallas_call(
        functools.partial(_gla_local_kernel, n_side=len(side)),
        grid=(nb,),
        in_specs=[pl.BlockSpec((rows_per_block, D_MODEL), lambda b: (b, 0)),
                  resident(w_gu), resident(w_v), col(0), col(1), col(2), col(3), resident(lb4)] + side_in,
        out_specs=[pl.BlockSpec((rows_per_block, 3 * HG_WIDTH), lambda b: (b, 0)),
                   pl.BlockSpec((rows_per_block, HG_WIDTH), lambda b: (b, 0)),
                   pl.BlockSpec((rows_per_block, 2 * HG_WIDTH), lambda b: (b, 0)),
                   pl.BlockSpec((2, HG_HEADS, nchunk, HG_DK, HG_DK), lambda b: (0, 0, b, 0, 0)),
                   pl.BlockSpec((2, HG_HEADS, nchunk * SUBLANES, HG_DK), lambda b: (0, 0, b, 0))] + side_out,
        out_shape=[jax.ShapeDtypeStruct((m, 3 * HG_WIDTH), BF16),
                   jax.ShapeDtypeStruct((m, HG_WIDTH), BF16),
                   jax.ShapeDtypeStruct((m, 2 * HG_WIDTH), BF16),
                   jax.ShapeDtypeStruct((2, HG_HEADS, n_chunks, HG_DK, HG_DK), BF16),
                   jax.ShapeDtypeStruct((2, HG_HEADS, n_chunks * SUBLANES, HG_DK), F32)] + side_shapes,
        scratch_shapes=[pltpu.VMEM((rows_per_block, HG_DK), F32) for _ in range(12)],
        compiler_params=_params(("parallel",), 56),
        name="gla_local",
    )(h, w_gu, w_v, z_a, z_a, z_a, z_a, lb4, *[s[0] for s in side])


def _gla_finish_kernel(*refs, nchunk, n_chunks, n_side):
    oi_ref, qe_ref, gt_ref, gain_ref, u_ref, d_ref, s0_ref = refs[:7]
    side_in = refs[7:7 + n_side]
    o_ref = refs[7 + n_side]
    side_out = refs[8 + n_side:8 + 2 * n_side]
    s_all = refs[-1]
    _side_cast(side_in, side_out)

    @pl.when(pl.program_id(1) == 0)
    def _():
        def step(c, carry):
            sf, sb = carry
            cb = n_chunks - 1 - c
            s_all[0, c] = sf.astype(BF16)
            s_all[1, cb] = sb.astype(BF16)
            df = d_ref[0, 0, pl.ds(pl.multiple_of(c * SUBLANES, SUBLANES), 1), :]
            db = d_ref[1, 0, pl.ds(pl.multiple_of(cb * SUBLANES, SUBLANES), 1), :]
            return (sf * df + u_ref[0, 0, c].astype(F32), sb * db + u_ref[1, 0, cb].astype(F32))

        lax.fori_loop(0, n_chunks, step, (s0_ref[0, 0], s0_ref[1, 0]), unroll=4)

    nt = (((1,), (1,)), ((), ()))
    gain = gain_ref[0]
    first = pl.program_id(1) * nchunk
    for c in range(nchunk):
        sl = slice(c * CHUNK, (c + 1) * CHUNK)
        scat = jnp.concatenate([s_all[0, first + c], s_all[1, first + c]], axis=1)
        o = oi_ref[sl, :].astype(F32) + lax.dot_general(qe_ref[sl, :], scat, nt, preferred_element_type=F32)
        o = _rms(o) * gain
        o_ref[sl, :] = (o * _silu(gt_ref[sl, :].astype(F32))).astype(BF16)


def _gla_finish(oi, qe, z_b, hg_gain, s_inc, s_dec, s0, rows_per_block, side=()):
    m = oi.shape[0]
    nb = m // rows_per_block
    nchunk = rows_per_block // CHUNK
    n_chunks = m // CHUNK
    side_in, side_out, side_shapes = _side_cast_specs(side, HG_HEADS * nb, lambda h, b: h * nb + b)
    return pl.pallas_call(
        functools.partial(_gla_finish_kernel, nchunk=nchunk, n_chunks=n_chunks, n_side=len(side)),
        grid=(HG_HEADS, nb),
        in_specs=[pl.BlockSpec((rows_per_block, HG_DK), lambda h, b: (b, h)),
                  pl.BlockSpec((rows_per_block, 2 * HG_DK), lambda h, b: (b, h)),
                  pl.BlockSpec((rows_per_block, HG_DK), lambda h, b: (b, h)),
                  pl.BlockSpec((1, 1, HG_DK), lambda h, b: (h, 0, 0)),
                  pl.BlockSpec((2, 1, n_chunks, HG_DK, HG_DK), lambda h, b: (0, h, 0, 0, 0)),
                  pl.BlockSpec((2, 1, n_chunks * SUBLANES, HG_DK), lambda h, b: (0, h, 0, 0)),
                  pl.BlockSpec((2, 1, HG_DK, HG_DK), lambda h, b: (0, h, 0, 0))] + side_in,
        out_specs=[pl.BlockSpec((rows_per_block, HG_DK), lambda h, b: (b, h))] + side_out,
        out_shape=[jax.ShapeDtypeStruct((m, HG_WIDTH), BF16)] + side_shapes,
        scratch_shapes=[pltpu.VMEM((2, n_chunks, HG_DK, HG_DK), BF16)],
        compiler_params=_params(("parallel", "arbitrary"), 48),
        name="gla_finish",
    )(oi, qe, z_b, hg_gain.reshape(HG_HEADS, 1, HG_DK), s_inc, s_dec, s0, *[s[0] for s in side])


def _residual_epilogue(mix, x_ref, vec_ref, xo_ref, ho_ref, rs):
    xn = x_ref[rs, :] + _rms(mix) * vec_ref[0:1]
    xo_ref[rs, :] = xn
    if ho_ref is not None:
        hn = (_rms(xn) * vec_ref[1:2] + vec_ref[2:3]).astype(BF16)
        outs = ho_ref if isinstance(ho_ref, (tuple, list)) else (ho_ref,)
        width = hn.shape[1] // len(outs)
        for k, out in enumerate(outs):
            out[rs, :] = hn[:, k * width:(k + 1) * width]


def _spatial_chunk(u_raw, v_raw, w_ref, bias_ref, lng_ref, lnb_ref):
    v = _gelu_tanh(v_raw.astype(F32))
    xc = v - jnp.mean(v, axis=-1, keepdims=True)
    y = (xc * lax.rsqrt(jnp.mean(xc * xc, axis=-1, keepdims=True) + EPS) * lng_ref[...] + lnb_ref[...]).astype(BF16)
    u = _gelu_tanh(u_raw.astype(F32))
    parts = []
    for g in range(SG_GROUPS):
        cs = slice(g * SG_CH, (g + 1) * SG_CH)
        mixed = jnp.dot(w_ref[g], y[:, cs], preferred_element_type=F32) + bias_ref[:, cs]
        parts.append((u[:, cs] * mixed).astype(BF16))
    return jnp.concatenate(parts, axis=1)


def _mm_res_kernel(*refs, n_a, n_sp, emit_h):
    a_refs = refs[:n_a]
    sp_refs = refs[n_a:n_a + 6 * n_sp]
    n_w = n_a + n_sp
    w_refs = refs[n_a + 6 * n_sp:n_a + 6 * n_sp + n_w]
    x_ref, vec_ref, xo_ref = refs[n_a + 6 * n_sp + n_w:n_a + 6 * n_sp + n_w + 3]
    ho_ref = refs[n_a + 6 * n_sp + n_w + 3] if emit_h else None
    for rs in _row_subblocks(x_ref.shape[0], SG_CHUNK):
        mix = jnp.dot(a_refs[0][rs, :], w_refs[0][...], preferred_element_type=F32)
        for a_ref, w_ref in zip(a_refs[1:], w_refs[1:]):
            mix = mix + jnp.dot(a_ref[rs, :], w_ref[...], preferred_element_type=F32)
        if n_sp:
            u_ref, v_ref = sp_refs[:2]
            s = _spatial_chunk(u_ref[rs, :], v_ref[rs, :], *sp_refs[2:])
            mix = mix + jnp.dot(s, w_refs[-1][...], preferred_element_type=F32)
        _residual_epilogue(mix, x_ref, vec_ref, xo_ref, ho_ref, rs)


def _mm_res(a_list, w, x, vec, tm, emit_h=True, spatial=None):
    m = x.shape[0]
    n_a = len(a_list)
    widths = [a.shape[1] for a in a_list] + ([SG_WIDTH] if spatial else [])
    offs = np.cumsum([0] + widths)
    sp_specs, sp_args = [], []
    if spatial:
        z_b = spatial[0]
        sp_specs = [pl.BlockSpec((tm, SG_WIDTH), lambda i: (i, 1)), pl.BlockSpec((tm, SG_WIDTH), lambda i: (i, 2)),
                    pl.BlockSpec((SG_GROUPS, SG_CHUNK, SG_CHUNK), lambda i: (0, 0, 0)),
                    pl.BlockSpec((SG_CHUNK, SG_WIDTH), lambda i: (0, 0)),
                    pl.BlockSpec((1, SG_WIDTH), lambda i: (0, 0)), pl.BlockSpec((1, SG_WIDTH), lambda i: (0, 0))]
        sp_args = [z_b, z_b, *spatial[1:]]
    in_specs = ([pl.BlockSpec((tm, a.shape[1]), lambda i: (i, 0)) for a in a_list] + sp_specs
                + [pl.BlockSpec((wd, D_MODEL), lambda i, k=int(o) // wd: (k, 0)) for wd, o in zip(widths, offs)]
                + [pl.BlockSpec((tm, D_MODEL), lambda i: (i, 0)),
                   pl.BlockSpec((8, D_MODEL), lambda i: (0, 0))])
    out_specs = [pl.BlockSpec((tm, D_MODEL), lambda i: (i, 0))]
    out_shape = [jax.ShapeDtypeStruct((m, D_MODEL), F32)]
    if emit_h:
        out_specs.append(pl.BlockSpec((tm, D_MODEL), lambda i: (i, 0)))
        out_shape.append(jax.ShapeDtypeStruct((m, D_MODEL), BF16))
    return pl.pallas_call(
        functools.partial(_mm_res_kernel, n_a=n_a, n_sp=1 if spatial else 0, emit_h=emit_h),
        grid=(m // tm,),
        in_specs=in_specs, out_specs=out_specs, out_shape=out_shape,
        compiler_params=_params(("parallel",), 56),
        name="mm_residual",
    )(*a_list, *sp_args, *[w] * len(widths), x, vec)


def _mlp_kernel(*refs, n_h, n_side, nj, n_steps):
    h_ref, w1_ref, w2_ref, x_ref, vec_ref = refs[:5]
    side_in = refs[5:5 + n_side]
    xo_ref = refs[5 + n_side]
    ho_ref = tuple(refs[6 + n_side:6 + n_side + n_h]) or None
    n_out = 1 + n_h
    side_out = refs[5 + n_side + n_out:5 + 2 * n_side + n_out]
    w1_buf, w2_buf, sems = refs[-3:]
    acc_ref = xo_ref
    _side_cast(side_in, side_out)
    j = pl.program_id(1)
    last = nj - 1
    tf = w1_buf.shape[2]

    step = pl.program_id(0) * nj + j

    def tile_copies(s):
        slot = s % WEIGHT_SLOTS
        col = (s % nj) * tf
        if not isinstance(col, int):
            col = pl.multiple_of(col, tf)
        return (pltpu.make_async_copy(w1_ref.at[:, pl.ds(col, tf)], w1_buf.at[slot], sems.at[0, slot]),
                pltpu.make_async_copy(w2_ref.at[pl.ds(col, tf), :], w2_buf.at[slot], sems.at[1, slot]))

    @pl.when(step == 0)
    def _():
        for s in range(WEIGHT_SLOTS - 1):
            for cp in tile_copies(s):
                cp.start()

    @pl.when(step + (WEIGHT_SLOTS - 1) < n_steps)
    def _():
        for cp in tile_copies(step + (WEIGHT_SLOTS - 1)):
            cp.start()

    for cp in tile_copies(step):
        cp.wait()
    slot = step % WEIGHT_SLOTS

    def partial_product(rs):
        a = jnp.dot(h_ref[rs, :], w1_buf[slot], preferred_element_type=F32)
        a = jnp.square(jnp.maximum(a, 0.0)).astype(BF16)
        return jnp.dot(a, w2_buf[slot], preferred_element_type=F32)

    @pl.when(j == 0)
    def _():
        for rs in _row_subblocks(h_ref.shape[0]):
            acc_ref[rs, :] = partial_product(rs)

    @pl.when(jnp.logical_and(j > 0, j < last))
    def _():
        for rs in _row_subblocks(h_ref.shape[0]):
            acc_ref[rs, :] += partial_product(rs)

    @pl.when(j == last)
    def _():
        for rs in _row_subblocks(h_ref.shape[0]):
            _residual_epilogue(acc_ref[rs, :] + partial_product(rs), x_ref, vec_ref, xo_ref, ho_ref, rs)


def _mlp(h, w1, w2, x, vec, tm, tf, n_h, side=()):
    m = x.shape[0]
    nj = D_FF // tf
    out_specs = [pl.BlockSpec((tm, D_MODEL), lambda i, j: (i, 0))]
    out_shape = [jax.ShapeDtypeStruct((m, D_MODEL), F32)]
    for _ in range(n_h):
        out_specs.append(pl.BlockSpec((tm, D_MODEL // n_h), lambda i, j: (i, 0)))
        out_shape.append(jax.ShapeDtypeStruct((m, D_MODEL // n_h), BF16))
    side_in, side_out, side_shapes = _side_cast_specs(side, (m // tm) * nj, lambda i, j: i * nj + j)
    return pl.pallas_call(
        functools.partial(_mlp_kernel, n_h=n_h, n_side=len(side), nj=nj, n_steps=(m // tm) * nj),
        grid=(m // tm, nj),
        in_specs=[pl.BlockSpec((tm, D_MODEL), lambda i, j: (i, 0)),
                  pl.BlockSpec(memory_space=pl.ANY),
                  pl.BlockSpec(memory_space=pl.ANY),
                  pl.BlockSpec((tm, D_MODEL), lambda i, j: (i, 0)),
                  pl.BlockSpec((8, D_MODEL), lambda i, j: (0, 0))] + side_in,
        out_specs=out_specs + side_out, out_shape=out_shape + side_shapes,
        scratch_shapes=[pltpu.VMEM((WEIGHT_SLOTS, D_MODEL, tf), BF16),
                        pltpu.VMEM((WEIGHT_SLOTS, tf, D_MODEL), BF16),
                        pltpu.SemaphoreType.DMA((2, WEIGHT_SLOTS))],
        compiler_params=_params(("arbitrary", "arbitrary"), 60),
        name="mlp",
    )(h, w1, w2, x, vec, *[s[0] for s in side])


def _dft_constants():
    def cs(n):
        k = np.arange(n)
        ang = 2.0 * np.pi * ((k[:, None] * k[None, :]) % n) / n
        return np.cos(ang) / np.sqrt(n), np.sin(ang) / np.sqrt(n)

    cc, sc = cs(FT_CH)
    c1, s1 = cs(FFT_L1)
    m1 = np.concatenate([c1, -s1], axis=0)
    c2, s2 = cs(FFT_L2)
    m2 = np.block([[c2, s2], [-s2, c2]])
    u1 = np.arange(FFT_L1)[:, None]
    t2 = np.arange(FFT_L2)[None, :]
    ang = 2.0 * np.pi * ((u1 * t2) % SEQ) / SEQ
    tw_c = np.cos(ang).reshape(SEQ, 1)
    tw_s = np.sin(ang).reshape(SEQ, 1)
    f = lambda a: jnp.asarray(a, F32)
    return f(cc), f(sc), f(m1), f(m2), f(tw_c), f(tw_s)


def _swap_row_factors(a, n_outer, n_inner):
    return a.reshape(n_outer, n_inner, a.shape[1]).transpose(1, 0, 2).reshape(a.shape)


def _pos_dft1_kernel(x_ref, m_ref, zr_ref, zi_ref):
    for k in range(x_ref.shape[0] // FFT_L1):
        sl = slice(k * FFT_L1, (k + 1) * FFT_L1)
        z = jnp.dot(m_ref[...], x_ref[sl, :], preferred_element_type=F32)
        zr_ref[sl, :] = z[:FFT_L1].astype(BF16)
        zi_ref[sl, :] = z[FFT_L1:].astype(BF16)


def _pos_dft1(xt, m1, slabs):
    rows = slabs * FFT_L1
    blk = pl.BlockSpec((rows, xt.shape[1]), lambda j: (j, 0))
    return pl.pallas_call(
        _pos_dft1_kernel,
        grid=(SEQ // rows,),
        in_specs=[blk, pl.BlockSpec((2 * FFT_L1, FFT_L1), lambda j: (0, 0))],
        out_specs=[blk, blk],
        out_shape=[jax.ShapeDtypeStruct(xt.shape, BF16)] * 2,
        compiler_params=_params(("parallel",), 32),
        name="pos_dft1",
    )(xt, m1)


def _pos_dft2_kernel(zr_ref, zi_ref, c_ref, s_ref, m_ref, wc_ref, ws_ref, y_ref, yr_buf, yi_buf):
    for k in range(zr_ref.shape[0] // FFT_L2):
        sl = slice(k * FFT_L2, (k + 1) * FFT_L2)
        zr = zr_ref[sl, :].astype(F32)
        zi = zi_ref[sl, :].astype(F32)
        c = c_ref[sl, :]
        s = s_ref[sl, :]
        t = jnp.concatenate([(zr * c + zi * s).astype(BF16), (zi * c - zr * s).astype(BF16)], axis=0)
        y = jnp.dot(m_ref[...], t, preferred_element_type=F32)
        yr_buf[sl, :] = y[:FFT_L2].astype(BF16)
        yi_buf[sl, :] = y[FFT_L2:].astype(BF16)
    for g in range(zr_ref.shape[1] // FT_CH):
        cs = slice(g * FT_CH, (g + 1) * FT_CH)
        y_ref[:, cs] = (jnp.dot(yr_buf[:, cs], wc_ref[...], preferred_element_type=F32)
                        + jnp.dot(yi_buf[:, cs], ws_ref[...], preferred_element_type=F32)).astype(BF16)


def _pos_dft2(zr, zi, tw_c, tw_s, m2, wc, ws, slabs):
    rows = slabs * FFT_L2
    width = zr.shape[1]
    blk = pl.BlockSpec((rows, width), lambda j: (j, 0))
    tw = pl.BlockSpec((rows, 1), lambda j: (j, 0))
    const = lambda a: pl.BlockSpec(a.shape, lambda j: (0, 0))
    return pl.pallas_call(
        _pos_dft2_kernel,
        grid=(SEQ // rows,),
        in_specs=[blk, blk, tw, tw, const(m2), const(wc), const(ws)],
        out_specs=blk,
        out_shape=jax.ShapeDtypeStruct(zr.shape, BF16),
        scratch_shapes=[pltpu.VMEM((rows, width), BF16), pltpu.VMEM((rows, width), BF16)],
        compiler_params=_params(("parallel",), 40),
        name="pos_dft2",
    )(zr, zi, tw_c, tw_s, m2, wc, ws)


def _fourier_mix(h_parts):
    cc, sc, m1, m2, tw_c, tw_s = _dft_constants()
    m1, m2, cc, sc = m1.astype(BF16), m2.astype(BF16), cc.astype(BF16), sc.astype(BF16)
    hts = [_swap_row_factors(h, FFT_L1, FFT_L2) for h in h_parts]
    zs = [_pos_dft1(ht, m1, DFT_ROWS // FFT_L1) for ht in hts]
    zs = [[_swap_row_factors(z, FFT_L2, FFT_L1) for z in zri] for zri in zs]
    yps = [_pos_dft2(zr, zi, tw_c, tw_s, m2, cc, sc, DFT_ROWS // FFT_L2) for zr, zi in zs]
    return [_swap_row_factors(yp, FFT_L1, FFT_L2) for yp in yps]


def _pack_rows(*rows):
    rows = [r.reshape(1, D_MODEL).astype(F32) for r in rows]
    rows += [jnp.zeros((1, D_MODEL), F32)] * (8 - len(rows))
    return jnp.concatenate(rows, axis=0)


def kernel(x, c, ctx, c_ctx, w_ada, b_ada, norm_gain, w_in, w_out, lb_raw, hg_norm_gain,
           sg_w, sg_b, sg_ln_gain, sg_ln_bias, w_fourier, w_mlp_in, w_mlp_out):
    assert x.shape == (1, SEQ, D_MODEL) and ctx.shape == (1, CTX_LEN, D_MODEL)
    xs = x.reshape(SEQ, D_MODEL)
    cs = ctx.reshape(CTX_LEN, D_MODEL)

    mods = _ada_mods(jnp.concatenate([c.reshape(1, D_MODEL), c_ctx.reshape(1, D_MODEL)], axis=0), w_ada, b_ada)
    mod = lambda l, r, j: mods[l, r, j * D_MODEL:(j + 1) * D_MODEL]

    lb = jnp.cumsum(jax.nn.softmax(lb_raw.astype(F32), axis=1), axis=1)[:, 0]
    lb4 = lb.reshape(2, HG_HEADS, 1, HG_DK)

    pre = lambda gain, l, r, j: _pack_rows(gain * (1.0 + mod(l, r, j + 1)), mod(l, r, j))
    post = lambda gain, l, j, *nxt: _pack_rows(mod(l, 0, j) * gain, *nxt)

    z_a, h0, w_gu, w_v = _norm_mm(xs, pre(norm_gain[0, 0], 0, 0, 0), w_in, 4 * HG_WIDTH, MM_ROWS, MM_COLS,
                                  emit_h=True, keep_w=True,
                                  side=[(w_in, 0, 2 * HG_WIDTH, 2), (w_in, 0, HG_WIDTH, 6)])
    (z_ctx,) = _norm_mm(cs, pre(norm_gain[0, 0], 0, 1, 0), w_in, 3 * HG_WIDTH, CTX_LEN, MM_COLS)
    s_ctx = _ctx_states(z_ctx, lb4)
    z_b, o_intra, q_dec, s_inc, s_dec, w_out0, w1_0, w2_0 = _gla_local(
        h0, w_gu, w_v, z_a, lb4, GLA_LOCAL_ROWS, side=[(w_out, 0), (w_mlp_in, 0), (w_mlp_out, 0)])
    o, w_fou = _gla_finish(o_intra, q_dec, z_b, hg_norm_gain[0], s_inc, s_dec, s_ctx, GLA_FINISH_ROWS,
                           side=[(w_fourier, 0)])
    bias_full = jnp.repeat(sg_b[0].T.astype(F32), SG_CH, axis=1)
    vec = post(norm_gain[0, 1], 0, 2, *pre(norm_gain[0, 2], 0, 0, 3)[:2])
    x1, h = _mm_res([o], w_out0, xs, vec, MM_ROWS,
                    spatial=(z_b, sg_w[0].astype(BF16), bias_full, sg_ln_gain[0].reshape(1, SG_WIDTH),
                             sg_ln_bias[0].reshape(1, SG_WIDTH)))
    vec = post(norm_gain[0, 3], 0, 5, *pre(norm_gain[1, 0], 1, 0, 0)[:2])
    x2, h_a, h_b, w1_1, w2_1 = _mlp(h, w1_0, w2_0, x1, vec, MM_ROWS, MLP_FF_COLS, 2,
                                    side=[(w_mlp_in, 1), (w_mlp_out, 1)])

    y_parts = _fourier_mix([h_a, h_b])
    vec = post(norm_gain[1, 1], 1, 2, *pre(norm_gain[1, 2], 1, 0, 3)[:2])
    x3, h = _mm_res(y_parts, w_fou, x2, vec, MM_ROWS)
    (x4,) = _mlp(h, w1_1, w2_1, x3, post(norm_gain[1, 3], 1, 5), MM_ROWS, MLP_FF_COLS, 0)
    return x4.reshape(1, SEQ, D_MODEL)
```

```python
import functools

import numpy as np
import jax
import jax.numpy as jnp
from jax import lax
from jax.experimental import pallas as pl
from jax.experimental.pallas import tpu as pltpu

D_MODEL = 2048
SEQ = 8192
DEPTH = 2
CTX_LEN = 256
CHUNK = 64
SUB = 16
ADA_COLS = 2048
MM_ROWS = 512
MM_COLS = 1024
MLP_FF_COLS = 1024
WEIGHT_SLOTS = 3
SUBROWS = 256
GLA_LOCAL_ROWS = 256
GLA_FINISH_ROWS = 4096
DFT_ROWS = 1024
HG_HEADS = 8
HG_DK = 128
HG_WIDTH = HG_HEADS * HG_DK
SG_GROUPS = 8
SG_CH = 128
SG_WIDTH = SG_GROUPS * SG_CH
SG_CHUNK = 128
FT_GROUPS = 4
FT_CH = D_MODEL // FT_GROUPS
FFT_L1 = 64
FFT_L2 = 128
IN_WIDTH = 5 * HG_WIDTH + 2 * SG_WIDTH
D_FF = 4 * D_MODEL
N_MOD = 6
EPS = 1e-6
EXP_CLAMP = 115.0
LOG2E = 1.0 / float(np.log(2.0))
LANES = 128
SUBLANES = 8
MXU_COLS = 256
PROJ_ROWS = 256

F32 = jnp.float32
BF16 = jnp.bfloat16
MIB = 1024 * 1024


def _params(semantics, vmem_mib):
    return pltpu.CompilerParams(dimension_semantics=semantics, vmem_limit_bytes=vmem_mib * MIB)


def _rms(x):
    return x * lax.rsqrt(jnp.mean(x * x, axis=-1, keepdims=True) + EPS)


def _silu(x):
    return x * jax.nn.sigmoid(x)


def _row_subblocks(rows, step=SUBROWS):
    step = min(rows, step)
    return [slice(s, s + step) for s in range(0, rows, step)]


def _gelu_tanh(x):
    cdf = 0.5 * (1.0 + jnp.tanh(float(np.sqrt(2.0 / np.pi)) * (x + 0.044715 * (x * x * x))))
    return x * cdf


def _ada_kernel(c_ref, w_ref, b_ref, o_ref, s_buf):
    @pl.when((pl.program_id(0) == 0) & (pl.program_id(1) == 0))
    def _():
        s_buf[...] = _silu(c_ref[...])

    tn = o_ref.shape[-1]
    for r in range(2):
        s = s_buf[r]
        cols = [jnp.sum(w_ref[0, :, j * LANES:(j + 1) * LANES] * s, axis=0, keepdims=True)
                for j in range(tn // LANES)]
        o_ref[0, r:r + 1, :] = jnp.concatenate(cols, axis=1) + b_ref[0]


def _ada_mods(c2, w_ada, b_ada):
    tn = ADA_COLS
    n = N_MOD * D_MODEL
    cb = jnp.broadcast_to(c2[:, :, None], (2, D_MODEL, LANES))
    return pl.pallas_call(
        _ada_kernel,
        grid=(DEPTH, n // tn),
        in_specs=[pl.BlockSpec((2, D_MODEL, LANES), lambda l, j: (0, 0, 0)),
                  pl.BlockSpec((1, D_MODEL, tn), lambda l, j: (l, 0, j)),
                  pl.BlockSpec((1, 1, tn), lambda l, j: (l, 0, j))],
        out_specs=pl.BlockSpec((1, 2, tn), lambda l, j: (l, 0, j)),
        out_shape=jax.ShapeDtypeStruct((DEPTH, 2, n), F32),
        scratch_shapes=[pltpu.VMEM((2, D_MODEL, LANES), F32)],
        compiler_params=_params(("arbitrary", "arbitrary"), 40),
        name="ada_mods",
    )(cb, w_ada, b_ada.reshape(DEPTH, 1, n))


def _side_cast_specs(weights, n_steps, step_of):
    in_specs, out_specs, out_shapes = [], [], []
    for w, layer, *cols in weights:
        width, cidx = cols if cols else (w.shape[2], 0)
        assert w.shape[1] % n_steps == 0, "every grid step must get an equal row slab"
        rows = w.shape[1] // n_steps
        in_specs.append(pl.BlockSpec((1, rows, width),
                                     lambda *g, layer=layer, cidx=cidx: (layer, step_of(*g), cidx)))
        out_specs.append(pl.BlockSpec((rows, width), lambda *g: (step_of(*g), 0)))
        out_shapes.append(jax.ShapeDtypeStruct((w.shape[1], width), BF16))
    return in_specs, out_specs, out_shapes


def _side_cast(side_in, side_out):
    for wi, wo in zip(side_in, side_out):
        wo[...] = wi[0].astype(BF16)


def _norm_mm_kernel(*refs, emit_h, n_side, keep_w):
    x_ref, vec_ref, w_ref = refs[:3]
    side_in = refs[3:3 + n_side]
    o_ref = refs[3 + n_side]
    n_out = 2 if emit_h else 1
    side_out = refs[3 + n_side + n_out:3 + 2 * n_side + n_out]
    scratch = refs[3 + 2 * n_side + n_out:]
    h_ref = refs[3 + n_side + 1] if emit_h else scratch[0]
    _side_cast(side_in, side_out)
    if keep_w:
        wb_ref = scratch[-1]

        @pl.when(pl.program_id(0) == 0)
        def _():
            wb_ref[pl.program_id(1)] = w_ref[0].astype(BF16)

        w = wb_ref[pl.program_id(1)]
    else:
        w = w_ref[0].astype(BF16)

    @pl.when(pl.program_id(1) == 0)
    def _():
        for rs in _row_subblocks(x_ref.shape[0]):
            h = (_rms(x_ref[rs, :]) * vec_ref[0:1] + vec_ref[1:2]).astype(BF16)
            h_ref[rs, :] = h
            o_ref[rs, :] = jnp.dot(h, w, preferred_element_type=F32).astype(o_ref.dtype)

    @pl.when(pl.program_id(1) > 0)
    def _():
        o_ref[...] = jnp.dot(h_ref[...], w, preferred_element_type=F32).astype(o_ref.dtype)


def _norm_mm(x, vec, w, n_cols, tm, tn, emit_h=False, side=(), keep_w=False):
    m = x.shape[0]
    nj = n_cols // tn
    out_specs = [pl.BlockSpec((tm, tn), lambda i, j: (i, j))]
    out_shape = [jax.ShapeDtypeStruct((m, n_cols), BF16)]
    scratch = []
    if emit_h:
        out_specs.append(pl.BlockSpec((tm, D_MODEL), lambda i, j: (i, 0)))
        out_shape.append(jax.ShapeDtypeStruct((m, D_MODEL), BF16))
    else:
        scratch.append(pltpu.VMEM((tm, D_MODEL), BF16))
    if keep_w:
        scratch.append(pltpu.VMEM((nj, D_MODEL, tn), BF16))
        w_spec = pl.BlockSpec((1, D_MODEL, tn), lambda i, j: (0, 0, jnp.where(i == 0, j, nj - 1)))
    else:
        w_spec = pl.BlockSpec((1, D_MODEL, tn), lambda i, j: (0, 0, j))
    side_in, side_out, side_shapes = _side_cast_specs(side, (m // tm) * nj, lambda i, j: i * nj + j)
    return pl.pallas_call(
        functools.partial(_norm_mm_kernel, emit_h=emit_h, n_side=len(side), keep_w=keep_w),
        grid=(m // tm, nj),
        in_specs=[pl.BlockSpec((tm, D_MODEL), lambda i, j: (i, 0)),
                  pl.BlockSpec((8, D_MODEL), lambda i, j: (0, 0)), w_spec] + side_in,
        out_specs=out_specs + side_out, out_shape=out_shape + side_shapes,
        scratch_shapes=scratch,
        compiler_params=_params(("arbitrary", "arbitrary"), 56),
        name="norm_mm",
    )(x, vec, w, *[s[0] for s in side])


def _seg_cumsum(g, row_in_chunk):
    p = g
    s = 1
    while s < CHUNK:
        p = p + jnp.where(row_in_chunk >= s, pltpu.roll(p, s, axis=0), 0.0)
        s *= 2
    return p


def _ctx_state_kernel(f_ref, i_ref, lb_ref, sfin_ref, p_buf, g_buf, k_buf, *, nchunk):
    d = pl.program_id(0)
    rows = f_ref.shape[0]
    lb = lb_ref[0, 0]
    f = lb + (1.0 - lb) * jax.nn.sigmoid(f_ref[...].astype(F32))
    g = jnp.log(f) * LOG2E
    ric = lax.broadcasted_iota(jnp.int32, (rows, LANES), 0) & (CHUNK - 1)
    p_buf[...] = _seg_cumsum(g, ric)
    g_buf[...] = g
    k_buf[...] = 1.0 - f

    s = jnp.zeros((HG_DK, HG_DK), F32)
    for j in range(nchunk):
        c = jnp.where(d == 0, j, nchunk - 1 - j)
        base = pl.multiple_of(c * CHUNK, CHUNK)
        sl = pl.ds(base, CHUNK)
        p = p_buf[sl, :]
        tot = p_buf[pl.ds(base + CHUNK - 1, 1), :]
        ex = jnp.where(d == 0, tot - p, p - g_buf[sl, :])
        kt = (k_buf[sl, :] * jnp.exp2(ex)).astype(BF16)
        u = lax.dot_general(i_ref[sl, :], kt, (((0,), (0,)), ((), ())), preferred_element_type=F32)
        s = s * jnp.exp2(tot) + u
    sfin_ref[0, 0] = s


def _ctx_states(z_ctx, lb4):
    rows = z_ctx.shape[0]
    return pl.pallas_call(
        functools.partial(_ctx_state_kernel, nchunk=rows // CHUNK),
        grid=(2, HG_HEADS),
        in_specs=[pl.BlockSpec((rows, HG_DK), lambda d, h: (0, d * HG_HEADS + h)),
                  pl.BlockSpec((rows, HG_DK), lambda d, h: (0, 2 * HG_HEADS + h)),
                  pl.BlockSpec((1, 1, 1, HG_DK), lambda d, h: (d, h, 0, 0))],
        out_specs=pl.BlockSpec((1, 1, HG_DK, HG_DK), lambda d, h: (d, h, 0, 0)),
        out_shape=jax.ShapeDtypeStruct((2, HG_HEADS, HG_DK, HG_DK), F32),
        scratch_shapes=[pltpu.VMEM((rows, HG_DK), F32) for _ in range(3)],
        compiler_params=_params(("parallel", "parallel"), 32),
        name="ctx_states",
    )(z_ctx, z_ctx, lb4)


def _gla_local_kernel(*refs, n_side):
    h_ref, wgu_ref, wv_ref, ff_ref, fb_ref, i_ref, q_ref, lb_ref = refs[:8]
    side_in = refs[8:8 + n_side]
    zb_ref, oi_ref, qe_ref, u_ref, d_ref = refs[8 + n_side:13 + n_side]
    side_out = refs[13 + n_side:13 + 2 * n_side]
    bufs = refs[13 + 2 * n_side:]
    _side_cast(side_in, side_out)
    rows = h_ref.shape[0]
    nchunk = rows // CHUNK
    proj_rows = min(rows, PROJ_ROWS)
    row_parts = rows // proj_rows

    def project(piece):
        lo = (piece // row_parts) * MXU_COLS
        rs = slice((piece % row_parts) * proj_rows, (piece % row_parts + 1) * proj_rows)
        w_ref, off = (wgu_ref, lo) if lo < 2 * HG_WIDTH else (wv_ref, lo - 2 * HG_WIDTH)
        zb_ref[rs, lo:lo + MXU_COLS] = jnp.dot(h_ref[rs, :], w_ref[:, off:off + MXU_COLS],
                                               preferred_element_type=F32).astype(BF16)

    n_pieces = row_parts * 3 * HG_WIDTH // MXU_COLS
    pieces_before = [(k + 1) * n_pieces // (HG_HEADS * nchunk) - k * n_pieces // (HG_HEADS * nchunk)
                     for k in range(HG_HEADS * nchunk)]

    ric = lax.broadcasted_iota(jnp.int32, (rows, LANES), 0) & (CHUNK - 1)
    blk = lax.broadcasted_iota(jnp.int32, (CHUNK, LANES), 0) // SUB
    r2 = lax.broadcasted_iota(jnp.int32, (CHUNK, CHUNK), 0)
    c2 = lax.broadcasted_iota(jnp.int32, (CHUNK, CHUNK), 1)
    nt = (((1,), (1,)), ((), ()))
    tn_dims = (((0,), (0,)), ((), ()))
    nsub = CHUNK // SUB
    emitted = [0]

    def before_chunk(k):
        for _ in range(pieces_before[k]):
            project(emitted[0])
            emitted[0] += 1

    for hd in range(HG_HEADS):
        _gla_local_head(hd, nchunk, ric, blk, r2, c2, nt, tn_dims, nsub, ff_ref, fb_ref, i_ref, q_ref, lb_ref,
                        oi_ref, qe_ref, u_ref, d_ref, bufs[6 * (hd % 2):6 * (hd % 2) + 6], before_chunk)
    assert emitted[0] == n_pieces


def _gla_local_head(hd, nchunk, ric, blk, r2, c2, nt, tn_dims, nsub, ff_ref, fb_ref, i_ref, q_ref, lb_ref,
                    oi_ref, qe_ref, u_ref, d_ref, bufs, before_chunk):
    pf_buf, pb_buf, gb_buf, kf_buf, kb_buf, q_buf = bufs
    cs = slice(hd * HG_DK, (hd + 1) * HG_DK)
    lbf = lb_ref[0, hd]
    lbb = lb_ref[1, hd]
    ff = lbf + (1.0 - lbf) * jax.nn.sigmoid(ff_ref[:, cs].astype(F32))
    pf_buf[...] = _seg_cumsum(jnp.log(ff) * LOG2E, ric)
    kf_buf[...] = 1.0 - ff
    fb = lbb + (1.0 - lbb) * jax.nn.sigmoid(fb_ref[:, cs].astype(F32))
    gb = jnp.log(fb) * LOG2E
    pb_buf[...] = _seg_cumsum(gb, ric)
    gb_buf[...] = gb
    kb_buf[...] = 1.0 - fb
    q_buf[...] = _silu(q_ref[:, cs].astype(F32)) * (HG_DK ** -0.5)

    for c in range(nchunk):
        before_chunk(hd * nchunk + c)
        base = c * CHUNK
        sl = pl.ds(base, CHUNK)
        pf = pf_buf[sl, :]
        pb = pb_buf[sl, :]
        gbc = gb_buf[sl, :]
        kf = kf_buf[sl, :]
        kb = kb_buf[sl, :]
        q = q_buf[sl, :]
        totb = pb_buf[pl.ds(base + CHUNK - 1, 1), :]
        suf = totb - pb + gbc
        mids, nids = [], []
        for i in range(nsub):
            mids.append(pf_buf[pl.ds(base + SUB * i + SUB // 2 - 1, 1), :])
            rmid = pl.ds(base + SUB * i + SUB // 2, 1)
            nids.append(totb - pb_buf[rmid, :] + gb_buf[rmid, :])
        mid_rows = jnp.concatenate([jnp.broadcast_to(m, (SUB, LANES)) for m in mids], axis=0)
        nid_rows = jnp.concatenate([jnp.broadcast_to(m, (SUB, LANES)) for m in nids], axis=0)
        qf_all = q * jnp.exp2(jnp.minimum(pf - mid_rows, EXP_CLAMP))
        qb_all = q * jnp.exp2(jnp.minimum(suf - nid_rows, EXP_CLAMP))
        qf_seg, kf_seg, qb_seg, kb_seg = [], [], [], []
        for i in range(nsub):
            qf_seg.append(jnp.where(blk == i, qf_all, 0.0))
            qb_seg.append(jnp.where(blk == i, qb_all, 0.0))
            hi = SUB * (i + 1)
            top = kf[:hi] * jnp.exp2(jnp.minimum(mids[i] - pf[:hi], EXP_CLAMP))
            kf_seg.append(top if hi == CHUNK else
                          jnp.concatenate([top, jnp.zeros((CHUNK - hi, LANES), F32)], axis=0))
            lo = SUB * i
            bot = kb[lo:] * jnp.exp2(jnp.minimum(nids[i] - suf[lo:], EXP_CLAMP))
            kb_seg.append(bot if lo == 0 else
                          jnp.concatenate([jnp.zeros((lo, LANES), F32), bot], axis=0))
        qf_big = jnp.concatenate(qf_seg, axis=1).astype(BF16)
        kf_big = jnp.concatenate(kf_seg, axis=1).astype(BF16)
        qb_big = jnp.concatenate(qb_seg, axis=1).astype(BF16)
        kb_big = jnp.concatenate(kb_seg, axis=1).astype(BF16)
        sc_f = lax.dot_general(qf_big, kf_big, nt, preferred_element_type=F32)
        sc_b = lax.dot_general(qb_big, kb_big, nt, preferred_element_type=F32)
        a = (jnp.where(c2 <= r2, sc_f, 0.0) + jnp.where(c2 >= r2, sc_b, 0.0)).astype(BF16)
        v = i_ref[sl, cs]
        oi_ref[sl, cs] = jnp.dot(a, v, preferred_element_type=F32).astype(BF16)
        qe_ref[sl, 2 * hd * HG_DK:2 * (hd + 1) * HG_DK] = jnp.concatenate(
            [q * jnp.exp2(pf), q * jnp.exp2(suf)], axis=1).astype(BF16)
        totf = pf_buf[pl.ds(base + CHUNK - 1, 1), :]
        kt = jnp.concatenate([kf * jnp.exp2(totf - pf), kb * jnp.exp2(pb - gbc)], axis=1).astype(BF16)
        u = lax.dot_general(v, kt, tn_dims, preferred_element_type=F32).astype(BF16)
        u_ref[0, hd, c] = u[:, :HG_DK]
        u_ref[1, hd, c] = u[:, HG_DK:]
        d_ref[0, hd, c * SUBLANES:(c + 1) * SUBLANES, :] = jnp.broadcast_to(jnp.exp2(totf), (SUBLANES, HG_DK))
        d_ref[1, hd, c * SUBLANES:(c + 1) * SUBLANES, :] = jnp.broadcast_to(jnp.exp2(totb), (SUBLANES, HG_DK))


def _gla_local(h, w_gu, w_v, z_a, lb4, rows_per_block, side=()):
    m = h.shape[0]
    nb = m // rows_per_block
    nchunk = rows_per_block // CHUNK
    n_chunks = m // CHUNK
    resident = lambda a: pl.BlockSpec(a.shape, lambda b: (0,) * a.ndim, pipeline_mode=pl.Buffered(1))
    col = lambda k: pl.BlockSpec((rows_per_block, HG_WIDTH), lambda b, k=k: (b, k))
    side_in, side_out, side_shapes = _side_cast_specs(side, nb, lambda b: b)
    return pl.pallas_call(
        functools.partial(_gla_local_kernel, n_side=len(side)),
        grid=(nb,),
        in_specs=[pl.BlockSpec((rows_per_block, D_MODEL), lambda b: (b, 0)),
                  resident(w_gu), resident(w_v), col(0), col(1), col(2), col(3), resident(lb4)] + side_in,
        out_specs=[pl.BlockSpec((rows_per_block, 3 * HG_WIDTH), lambda b: (b, 0)),
                   pl.BlockSpec((rows_per_block, HG_WIDTH), lambda b: (b, 0)),
                   pl.BlockSpec((rows_per_block, 2 * HG_WIDTH), lambda b: (b, 0)),
                   pl.BlockSpec((2, HG_HEADS, nchunk, HG_DK, HG_DK), lambda b: (0, 0, b, 0, 0)),
                   pl.BlockSpec((2, HG_HEADS, nchunk * SUBLANES, HG_DK), lambda b: (0, 0, b, 0))] + side_out,
        out_shape=[jax.ShapeDtypeStruct((m, 3 * HG_WIDTH), BF16),
                   jax.ShapeDtypeStruct((m, HG_WIDTH), BF16),
                   jax.ShapeDtypeStruct((m, 2 * HG_WIDTH), BF16),
                   jax.ShapeDtypeStruct((2, HG_HEADS, n_chunks, HG_DK, HG_DK), BF16),
                   jax.ShapeDtypeStruct((2, HG_HEADS, n_chunks * SUBLANES, HG_DK), F32)] + side_shapes,
        scratch_shapes=[pltpu.VMEM((rows_per_block, HG_DK), F32) for _ in range(12)],
        compiler_params=_params(("parallel",), 56),
        name="gla_local",
    )(h, w_gu, w_v, z_a, z_a, z_a, z_a, lb4, *[s[0] for s in side])


def _gla_finish_kernel(*refs, nchunk, n_chunks, n_side):
    oi_ref, qe_ref, gt_ref, gain_ref, u_hbm, d_ref, s0_ref = refs[:7]
    side_in = refs[7:7 + n_side]
    o_ref = refs[7 + n_side]
    side_out = refs[8 + n_side:8 + 2 * n_side]
    s_all, u_buf, u_sems = refs[-3:]
    _side_cast(side_in, side_out)
    head = pl.program_id(0)

    def u_copy(hd):
        slot = hd % 2
        return pltpu.make_async_copy(u_hbm.at[:, hd], u_buf.at[slot], u_sems.at[slot])

    @pl.when((pl.program_id(1) == 0) & (head == 0))
    def _():
        u_copy(0).start()

    @pl.when((pl.program_id(1) == 0) & (head + 1 < HG_HEADS))
    def _():
        u_copy(head + 1).start()

    @pl.when(pl.program_id(1) == 0)
    def _():
        u_copy(head).wait()
        slot = head % 2

        def step(c, carry):
            sf, sb = carry
            cb = n_chunks - 1 - c
            s_all[0, c] = sf.astype(BF16)
            s_all[1, cb] = sb.astype(BF16)
            df = d_ref[0, 0, pl.ds(pl.multiple_of(c * SUBLANES, SUBLANES), 1), :]
            db = d_ref[1, 0, pl.ds(pl.multiple_of(cb * SUBLANES, SUBLANES), 1), :]
            return (sf * df + u_buf[slot, 0, c].astype(F32), sb * db + u_buf[slot, 1, cb].astype(F32))

        lax.fori_loop(0, n_chunks, step, (s0_ref[0, 0], s0_ref[1, 0]), unroll=4)

    nt = (((1,), (1,)), ((), ()))
    gain = gain_ref[0]
    first = pl.program_id(1) * nchunk
    for c in range(nchunk):
        sl = slice(c * CHUNK, (c + 1) * CHUNK)
        scat = jnp.concatenate([s_all[0, first + c], s_all[1, first + c]], axis=1)
        o = oi_ref[sl, :].astype(F32) + lax.dot_general(qe_ref[sl, :], scat, nt, preferred_element_type=F32)
        o = _rms(o) * gain
        o_ref[sl, :] = (o * _silu(gt_ref[sl, :].astype(F32))).astype(BF16)


def _gla_finish(oi, qe, z_b, hg_gain, s_inc, s_dec, s0, rows_per_block, side=()):
    m = oi.shape[0]
    nb = m // rows_per_block
    nchunk = rows_per_block // CHUNK
    n_chunks = m // CHUNK
    side_in, side_out, side_shapes = _side_cast_specs(side, HG_HEADS * nb, lambda h, b: h * nb + b)
    return pl.pallas_call(
        functools.partial(_gla_finish_kernel, nchunk=nchunk, n_chunks=n_chunks, n_side=len(side)),
        grid=(HG_HEADS, nb),
        in_specs=[pl.BlockSpec((rows_per_block, HG_DK), lambda h, b: (b, h)),
                  pl.BlockSpec((rows_per_block, 2 * HG_DK), lambda h, b: (b, h)),
                  pl.BlockSpec((rows_per_block, HG_DK), lambda h, b: (b, h)),
                  pl.BlockSpec((1, 1, HG_DK), lambda h, b: (h, 0, 0)),
                  pl.BlockSpec(memory_space=pl.ANY),
                  pl.BlockSpec((2, 1, n_chunks * SUBLANES, HG_DK), lambda h, b: (0, h, 0, 0)),
                  pl.BlockSpec((2, 1, HG_DK, HG_DK), lambda h, b: (0, h, 0, 0))] + side_in,
        out_specs=[pl.BlockSpec((rows_per_block, HG_DK), lambda h, b: (b, h))] + side_out,
        out_shape=[jax.ShapeDtypeStruct((m, HG_WIDTH), BF16)] + side_shapes,
        scratch_shapes=[pltpu.VMEM((2, n_chunks, HG_DK, HG_DK), BF16),
                        pltpu.VMEM((2, 2, n_chunks, HG_DK, HG_DK), BF16),
                        pltpu.SemaphoreType.DMA((2,))],
        compiler_params=_params(("arbitrary", "arbitrary"), 48),
        name="gla_finish",
    )(oi, qe, z_b, hg_gain.reshape(HG_HEADS, 1, HG_DK), s_inc, s_dec, s0, *[s[0] for s in side])


def _residual_epilogue(mix, x_ref, vec_ref, xo_ref, ho_ref, rs):
    xn = x_ref[rs, :] + _rms(mix) * vec_ref[0:1]
    xo_ref[rs, :] = xn
    if ho_ref is not None:
        hn = (_rms(xn) * vec_ref[1:2] + vec_ref[2:3]).astype(BF16)
        outs = ho_ref if isinstance(ho_ref, (tuple, list)) else (ho_ref,)
        width = hn.shape[1] // len(outs)
        for k, out in enumerate(outs):
            out[rs, :] = hn[:, k * width:(k + 1) * width]


def _spatial_chunk(u_raw, v_raw, w_ref, bias_ref, lng_ref, lnb_ref):
    v = _gelu_tanh(v_raw.astype(F32))
    xc = v - jnp.mean(v, axis=-1, keepdims=True)
    y = (xc * lax.rsqrt(jnp.mean(xc * xc, axis=-1, keepdims=True) + EPS) * lng_ref[...] + lnb_ref[...]).astype(BF16)
    u = _gelu_tanh(u_raw.astype(F32))
    parts = []
    for g in range(SG_GROUPS):
        cs = slice(g * SG_CH, (g + 1) * SG_CH)
        mixed = jnp.dot(w_ref[g], y[:, cs], preferred_element_type=F32) + bias_ref[:, cs]
        parts.append((u[:, cs] * mixed).astype(BF16))
    return jnp.concatenate(parts, axis=1)


def _mm_res_kernel(*refs, n_a, n_sp, emit_h):
    a_refs = refs[:n_a]
    sp_refs = refs[n_a:n_a + 6 * n_sp]
    n_w = n_a + n_sp
    w_refs = refs[n_a + 6 * n_sp:n_a + 6 * n_sp + n_w]
    x_ref, vec_ref, xo_ref = refs[n_a + 6 * n_sp + n_w:n_a + 6 * n_sp + n_w + 3]
    ho_ref = refs[n_a + 6 * n_sp + n_w + 3] if emit_h else None
    for rs in _row_subblocks(x_ref.shape[0], SG_CHUNK):
        mix = jnp.dot(a_refs[0][rs, :], w_refs[0][...], preferred_element_type=F32)
        for a_ref, w_ref in zip(a_refs[1:], w_refs[1:]):
            mix = mix + jnp.dot(a_ref[rs, :], w_ref[...], preferred_element_type=F32)
        if n_sp:
            u_ref, v_ref = sp_refs[:2]
            s = _spatial_chunk(u_ref[rs, :], v_ref[rs, :], *sp_refs[2:])
            mix = mix + jnp.dot(s, w_refs[-1][...], preferred_element_type=F32)
        _residual_epilogue(mix, x_ref, vec_ref, xo_ref, ho_ref, rs)


def _mm_res(a_list, w, x, vec, tm, emit_h=True, spatial=None):
    m = x.shape[0]
    n_a = len(a_list)
    widths = [a.shape[1] for a in a_list] + ([SG_WIDTH] if spatial else [])
    offs = np.cumsum([0] + widths)
    sp_specs, sp_args = [], []
    if spatial:
        z_b = spatial[0]
        sp_specs = [pl.BlockSpec((tm, SG_WIDTH), lambda i: (i, 1)), pl.BlockSpec((tm, SG_WIDTH), lambda i: (i, 2)),
                    pl.BlockSpec((SG_GROUPS, SG_CHUNK, SG_CHUNK), lambda i: (0, 0, 0)),
                    pl.BlockSpec((SG_CHUNK, SG_WIDTH), lambda i: (0, 0)),
                    pl.BlockSpec((1, SG_WIDTH), lambda i: (0, 0)), pl.BlockSpec((1, SG_WIDTH), lambda i: (0, 0))]
        sp_args = [z_b, z_b, *spatial[1:]]
    in_specs = ([pl.BlockSpec((tm, a.shape[1]), lambda i: (i, 0)) for a in a_list] + sp_specs
                + [pl.BlockSpec((wd, D_MODEL), lambda i, k=int(o) // wd: (k, 0)) for wd, o in zip(widths, offs)]
                + [pl.BlockSpec((tm, D_MODEL), lambda i: (i, 0)),
                   pl.BlockSpec((8, D_MODEL), lambda i: (0, 0))])
    out_specs = [pl.BlockSpec((tm, D_MODEL), lambda i: (i, 0))]
    out_shape = [jax.ShapeDtypeStruct((m, D_MODEL), F32)]
    if emit_h:
        out_specs.append(pl.BlockSpec((tm, D_MODEL), lambda i: (i, 0)))
        out_shape.append(jax.ShapeDtypeStruct((m, D_MODEL), BF16))
    return pl.pallas_call(
        functools.partial(_mm_res_kernel, n_a=n_a, n_sp=1 if spatial else 0, emit_h=emit_h),
        grid=(m // tm,),
        in_specs=in_specs, out_specs=out_specs, out_shape=out_shape,
        compiler_params=_params(("parallel",), 56),
        name="mm_residual",
    )(*a_list, *sp_args, *[w] * len(widths), x, vec)


def _mlp_kernel(*refs, n_h, n_side, nj, n_steps):
    h_ref, w1_ref, w2_ref, x_ref, vec_ref = refs[:5]
    side_in = refs[5:5 + n_side]
    xo_ref = refs[5 + n_side]
    ho_ref = tuple(refs[6 + n_side:6 + n_side + n_h]) or None
    n_out = 1 + n_h
    side_out = refs[5 + n_side + n_out:5 + 2 * n_side + n_out]
    w1_buf, w2_buf, sems = refs[-3:]
    acc_ref = xo_ref
    _side_cast(side_in, side_out)
    j = pl.program_id(1)
    last = nj - 1
    tf = w1_buf.shape[2]

    step = pl.program_id(0) * nj + j

    def tile_copies(s):
        slot = s % WEIGHT_SLOTS
        col = (s % nj) * tf
        if not isinstance(col, int):
            col = pl.multiple_of(col, tf)
        return (pltpu.make_async_copy(w1_ref.at[:, pl.ds(col, tf)], w1_buf.at[slot], sems.at[0, slot]),
                pltpu.make_async_copy(w2_ref.at[pl.ds(col, tf), :], w2_buf.at[slot], sems.at[1, slot]))

    @pl.when(step == 0)
    def _():
        for s in range(WEIGHT_SLOTS - 1):
            for cp in tile_copies(s):
                cp.start()

    @pl.when(step + (WEIGHT_SLOTS - 1) < n_steps)
    def _():
        for cp in tile_copies(step + (WEIGHT_SLOTS - 1)):
            cp.start()

    for cp in tile_copies(step):
        cp.wait()
    slot = step % WEIGHT_SLOTS

    def partial_product(rs):
        a = jnp.dot(h_ref[rs, :], w1_buf[slot], preferred_element_type=F32)
        a = jnp.square(jnp.maximum(a, 0.0)).astype(BF16)
        return jnp.dot(a, w2_buf[slot], preferred_element_type=F32)

    @pl.when(j == 0)
    def _():
        for rs in _row_subblocks(h_ref.shape[0]):
            acc_ref[rs, :] = partial_product(rs)

    @pl.when(jnp.logical_and(j > 0, j < last))
    def _():
        for rs in _row_subblocks(h_ref.shape[0]):
            acc_ref[rs, :] += partial_product(rs)

    @pl.when(j == last)
    def _():
        for rs in _row_subblocks(h_ref.shape[0]):
            _residual_epilogue(acc_ref[rs, :] + partial_product(rs), x_ref, vec_ref, xo_ref, ho_ref, rs)


def _mlp(h, w1, w2, x, vec, tm, tf, n_h, side=()):
    m = x.shape[0]
    nj = D_FF // tf
    out_specs = [pl.BlockSpec((tm, D_MODEL), lambda i, j: (i, 0))]
    out_shape = [jax.ShapeDtypeStruct((m, D_MODEL), F32)]
    for _ in range(n_h):
        out_specs.append(pl.BlockSpec((tm, D_MODEL // n_h), lambda i, j: (i, 0)))
        out_shape.append(jax.ShapeDtypeStruct((m, D_MODEL // n_h), BF16))
    side_in, side_out, side_shapes = _side_cast_specs(side, (m // tm) * nj, lambda i, j: i * nj + j)
    return pl.pallas_call(
        functools.partial(_mlp_kernel, n_h=n_h, n_side=len(side), nj=nj, n_steps=(m // tm) * nj),
        grid=(m // tm, nj),
        in_specs=[pl.BlockSpec((tm, D_MODEL), lambda i, j: (i, 0)),
                  pl.BlockSpec(memory_space=pl.ANY),
                  pl.BlockSpec(memory_space=pl.ANY),
                  pl.BlockSpec((tm, D_MODEL), lambda i, j: (i, 0)),
                  pl.BlockSpec((8, D_MODEL), lambda i, j: (0, 0))] + side_in,
        out_specs=out_specs + side_out, out_shape=out_shape + side_shapes,
        scratch_shapes=[pltpu.VMEM((WEIGHT_SLOTS, D_MODEL, tf), BF16),
                        pltpu.VMEM((WEIGHT_SLOTS, tf, D_MODEL), BF16),
                        pltpu.SemaphoreType.DMA((2, WEIGHT_SLOTS))],
        compiler_params=_params(("arbitrary", "arbitrary"), 60),
        name="mlp",
    )(h, w1, w2, x, vec, *[s[0] for s in side])


def _dft_constants():
    def cs(n):
        k = np.arange(n)
        ang = 2.0 * np.pi * ((k[:, None] * k[None, :]) % n) / n
        return np.cos(ang) / np.sqrt(n), np.sin(ang) / np.sqrt(n)

    cc, sc = cs(FT_CH)
    c1, s1 = cs(FFT_L1)
    m1 = np.concatenate([c1, -s1], axis=0)
    c2, s2 = cs(FFT_L2)
    m2 = np.block([[c2, s2], [-s2, c2]])
    u1 = np.arange(FFT_L1)[:, None]
    t2 = np.arange(FFT_L2)[None, :]
    ang = 2.0 * np.pi * ((u1 * t2) % SEQ) / SEQ
    tw_c = np.cos(ang).reshape(SEQ, 1)
    tw_s = np.sin(ang).reshape(SEQ, 1)
    f = lambda a: jnp.asarray(a, F32)
    return f(cc), f(sc), f(m1), f(m2), f(tw_c), f(tw_s)


def _swap_row_factors(a, n_outer, n_inner):
    return a.reshape(n_outer, n_inner, a.shape[1]).transpose(1, 0, 2).reshape(a.shape)


def _pos_dft1_kernel(x_ref, m_ref, zr_ref, zi_ref):
    for k in range(x_ref.shape[0] // FFT_L1):
        sl = slice(k * FFT_L1, (k + 1) * FFT_L1)
        z = jnp.dot(m_ref[...], x_ref[sl, :], preferred_element_type=F32)
        zr_ref[sl, :] = z[:FFT_L1].astype(BF16)
        zi_ref[sl, :] = z[FFT_L1:].astype(BF16)


def _pos_dft1(xt, m1, slabs):
    rows = slabs * FFT_L1
    blk = pl.BlockSpec((rows, xt.shape[1]), lambda j: (j, 0))
    return pl.pallas_call(
        _pos_dft1_kernel,
        grid=(SEQ // rows,),
        in_specs=[blk, pl.BlockSpec((2 * FFT_L1, FFT_L1), lambda j: (0, 0))],
        out_specs=[blk, blk],
        out_shape=[jax.ShapeDtypeStruct(xt.shape, BF16)] * 2,
        compiler_params=_params(("parallel",), 32),
        name="pos_dft1",
    )(xt, m1)


def _pos_dft2_kernel(zr_ref, zi_ref, c_ref, s_ref, m_ref, wc_ref, ws_ref, y_ref, yr_buf, yi_buf):
    for k in range(zr_ref.shape[0] // FFT_L2):
        sl = slice(k * FFT_L2, (k + 1) * FFT_L2)
        zr = zr_ref[sl, :].astype(F32)
        zi = zi_ref[sl, :].astype(F32)
        c = c_ref[sl, :]
        s = s_ref[sl, :]
        t = jnp.concatenate([(zr * c + zi * s).astype(BF16), (zi * c - zr * s).astype(BF16)], axis=0)
        y = jnp.dot(m_ref[...], t, preferred_element_type=F32)
        yr_buf[sl, :] = y[:FFT_L2].astype(BF16)
        yi_buf[sl, :] = y[FFT_L2:].astype(BF16)
    for g in range(zr_ref.shape[1] // FT_CH):
        cs = slice(g * FT_CH, (g + 1) * FT_CH)
        y_ref[:, cs] = (jnp.dot(yr_buf[:, cs], wc_ref[...], preferred_element_type=F32)
                        + jnp.dot(yi_buf[:, cs], ws_ref[...], preferred_element_type=F32)).astype(BF16)


def _pos_dft2(zr, zi, tw_c, tw_s, m2, wc, ws, slabs):
    rows = slabs * FFT_L2
    width = zr.shape[1]
    blk = pl.BlockSpec((rows, width), lambda j: (j, 0))
    tw = pl.BlockSpec((rows, 1), lambda j: (j, 0))
    const = lambda a: pl.BlockSpec(a.shape, lambda j: (0, 0))
    return pl.pallas_call(
        _pos_dft2_kernel,
        grid=(SEQ // rows,),
        in_specs=[blk, blk, tw, tw, const(m2), const(wc), const(ws)],
        out_specs=blk,
        out_shape=jax.ShapeDtypeStruct(zr.shape, BF16),
        scratch_shapes=[pltpu.VMEM((rows, width), BF16), pltpu.VMEM((rows, width), BF16)],
        compiler_params=_params(("parallel",), 40),
        name="pos_dft2",
    )(zr, zi, tw_c, tw_s, m2, wc, ws)


def _fourier_mix(h_parts):
    cc, sc, m1, m2, tw_c, tw_s = _dft_constants()
    m1, m2, cc, sc = m1.astype(BF16), m2.astype(BF16), cc.astype(BF16), sc.astype(BF16)
    hts = [_swap_row_factors(h, FFT_L1, FFT_L2) for h in h_parts]
    zs = [_pos_dft1(ht, m1, DFT_ROWS // FFT_L1) for ht in hts]
    zs = [[_swap_row_factors(z, FFT_L2, FFT_L1) for z in zri] for zri in zs]
    yps = [_pos_dft2(zr, zi, tw_c, tw_s, m2, cc, sc, DFT_ROWS // FFT_L2) for zr, zi in zs]
    return [_swap_row_factors(yp, FFT_L1, FFT_L2) for yp in yps]


def _pack_rows(*rows):
    rows = [r.reshape(1, D_MODEL).astype(F32) for r in rows]
    rows += [jnp.zeros((1, D_MODEL), F32)] * (8 - len(rows))
    return jnp.concatenate(rows, axis=0)


def kernel(x, c, ctx, c_ctx, w_ada, b_ada, norm_gain, w_in, w_out, lb_raw, hg_norm_gain,
           sg_w, sg_b, sg_ln_gain, sg_ln_bias, w_fourier, w_mlp_in, w_mlp_out):
    assert x.shape == (1, SEQ, D_MODEL) and ctx.shape == (1, CTX_LEN, D_MODEL)
    xs = x.reshape(SEQ, D_MODEL)
    cs = ctx.reshape(CTX_LEN, D_MODEL)

    mods = _ada_mods(jnp.concatenate([c.reshape(1, D_MODEL), c_ctx.reshape(1, D_MODEL)], axis=0), w_ada, b_ada)
    mod = lambda l, r, j: mods[l, r, j * D_MODEL:(j + 1) * D_MODEL]

    lb = jnp.cumsum(jax.nn.softmax(lb_raw.astype(F32), axis=1), axis=1)[:, 0]
    lb4 = lb.reshape(2, HG_HEADS, 1, HG_DK)

    pre = lambda gain, l, r, j: _pack_rows(gain * (1.0 + mod(l, r, j + 1)), mod(l, r, j))
    post = lambda gain, l, j, *nxt: _pack_rows(mod(l, 0, j) * gain, *nxt)

    z_a, h0, w_gu, w_v = _norm_mm(xs, pre(norm_gain[0, 0], 0, 0, 0), w_in, 4 * HG_WIDTH, MM_ROWS, MM_COLS,
                                  emit_h=True, keep_w=True,
                                  side=[(w_in, 0, 2 * HG_WIDTH, 2), (w_in, 0, HG_WIDTH, 6)])
    (z_ctx,) = _norm_mm(cs, pre(norm_gain[0, 0], 0, 1, 0), w_in, 3 * HG_WIDTH, CTX_LEN, MM_COLS)
    s_ctx = _ctx_states(z_ctx, lb4)
    z_b, o_intra, q_dec, s_inc, s_dec, w_out0, w1_0, w2_0 = _gla_local(
        h0, w_gu, w_v, z_a, lb4, GLA_LOCAL_ROWS, side=[(w_out, 0), (w_mlp_in, 0), (w_mlp_out, 0)])
    o, w_fou = _gla_finish(o_intra, q_dec, z_b, hg_norm_gain[0], s_inc, s_dec, s_ctx, GLA_FINISH_ROWS,
                           side=[(w_fourier, 0)])
    bias_full = jnp.repeat(sg_b[0].T.astype(F32), SG_CH, axis=1)
    vec = post(norm_gain[0, 1], 0, 2, *pre(norm_gain[0, 2], 0, 0, 3)[:2])
    x1, h = _mm_res([o], w_out0, xs, vec, MM_ROWS,
                    spatial=(z_b, sg_w[0].astype(BF16), bias_full, sg_ln_gain[0].reshape(1, SG_WIDTH),
                             sg_ln_bias[0].reshape(1, SG_WIDTH)))
    vec = post(norm_gain[0, 3], 0, 5, *pre(norm_gain[1, 0], 1, 0, 0)[:2])
    x2, h_a, h_b, w1_1, w2_1 = _mlp(h, w1_0, w2_0, x1, vec, MM_ROWS, MLP_FF_COLS, 2,
                                    side=[(w_mlp_in, 1), (w_mlp_out, 1)])

    y_parts = _fourier_mix([h_a, h_b])
    vec = post(norm_gain[1, 1], 1, 2, *pre(norm_gain[1, 2], 1, 0, 3)[:2])
    x3, h = _mm_res(y_parts, w_fou, x2, vec, MM_ROWS)
    (x4,) = _mlp(h, w1_1, w2_1, x3, post(norm_gain[1, 3], 1, 5), MM_ROWS, MLP_FF_COLS, 0)
    return x4.reshape(1, SEQ, D_MODEL)
```

```python
import functools

import numpy as np
import jax
import jax.numpy as jnp
from jax import lax
from jax.experimental import pallas as pl
from jax.experimental.pallas import tpu as pltpu

D_MODEL = 2048
SEQ = 8192
DEPTH = 2
CTX_LEN = 256
CHUNK = 64
SUB = 16
ADA_COLS = 2048
MM_ROWS = 512
MM_COLS = 1024
MLP_FF_COLS = 1024
WEIGHT_SLOTS = 3
SUBROWS = 256
GLA_LOCAL_ROWS = 256
GLA_FINISH_ROWS = 4096
DFT_ROWS = 1024
HG_HEADS = 8
HG_DK = 128
HG_WIDTH = HG_HEADS * HG_DK
SG_GROUPS = 8
SG_CH = 128
SG_WIDTH = SG_GROUPS * SG_CH
SG_CHUNK = 128
FT_GROUPS = 4
FT_CH = D_MODEL // FT_GROUPS
FFT_L1 = 64
FFT_L2 = 128
IN_WIDTH = 5 * HG_WIDTH + 2 * SG_WIDTH
D_FF = 4 * D_MODEL
N_MOD = 6
EPS = 1e-6
EXP_CLAMP = 115.0
LOG2E = 1.0 / float(np.log(2.0))
LANES = 128
SUBLANES = 8
MXU_COLS = 256
PROJ_ROWS = 256

F32 = jnp.float32
BF16 = jnp.bfloat16
MIB = 1024 * 1024


def _params(semantics, vmem_mib):
    return pltpu.CompilerParams(dimension_semantics=semantics, vmem_limit_bytes=vmem_mib * MIB)


def _rms(x):
    return x * lax.rsqrt(jnp.mean(x * x, axis=-1, keepdims=True) + EPS)


def _silu(x):
    return x * jax.nn.sigmoid(x)


def _row_subblocks(rows, step=SUBROWS):
    step = min(rows, step)
    return [slice(s, s + step) for s in range(0, rows, step)]


def _gelu_tanh(x):
    cdf = 0.5 * (1.0 + jnp.tanh(float(np.sqrt(2.0 / np.pi)) * (x + 0.044715 * (x * x * x))))
    return x * cdf


def _ada_kernel(c_ref, w_ref, b_ref, o_ref, s_buf):
    @pl.when((pl.program_id(0) == 0) & (pl.program_id(1) == 0))
    def _():
        s_buf[...] = _silu(c_ref[...])

    tn = o_ref.shape[-1]
    for r in range(2):
        s = s_buf[r]
        cols = [jnp.sum(w_ref[0, :, j * LANES:(j + 1) * LANES] * s, axis=0, keepdims=True)
                for j in range(tn // LANES)]
        o_ref[0, r:r + 1, :] = jnp.concatenate(cols, axis=1) + b_ref[0]


def _ada_mods(c2, w_ada, b_ada):
    tn = ADA_COLS
    n = N_MOD * D_MODEL
    cb = jnp.broadcast_to(c2[:, :, None], (2, D_MODEL, LANES))
    return pl.pallas_call(
        _ada_kernel,
        grid=(DEPTH, n // tn),
        in_specs=[pl.BlockSpec((2, D_MODEL, LANES), lambda l, j: (0, 0, 0)),
                  pl.BlockSpec((1, D_MODEL, tn), lambda l, j: (l, 0, j)),
                  pl.BlockSpec((1, 1, tn), lambda l, j: (l, 0, j))],
        out_specs=pl.BlockSpec((1, 2, tn), lambda l, j: (l, 0, j)),
        out_shape=jax.ShapeDtypeStruct((DEPTH, 2, n), F32),
        scratch_shapes=[pltpu.VMEM((2, D_MODEL, LANES), F32)],
        compiler_params=_params(("arbitrary", "arbitrary"), 40),
        name="ada_mods",
    )(cb, w_ada, b_ada.reshape(DEPTH, 1, n))


def _side_cast_specs(weights, n_steps, step_of):
    in_specs, out_specs, out_shapes = [], [], []
    for w, layer, *cols in weights:
        width, cidx = cols if cols else (w.shape[2], 0)
        assert w.shape[1] % n_steps == 0, "every grid step must get an equal row slab"
        rows = w.shape[1] // n_steps
        in_specs.append(pl.BlockSpec((1, rows, width),
                                     lambda *g, layer=layer, cidx=cidx: (layer, step_of(*g), cidx)))
        out_specs.append(pl.BlockSpec((rows, width), lambda *g: (step_of(*g), 0)))
        out_shapes.append(jax.ShapeDtypeStruct((w.shape[1], width), BF16))
    return in_specs, out_specs, out_shapes


def _side_cast(side_in, side_out):
    for wi, wo in zip(side_in, side_out):
        wo[...] = wi[0].astype(BF16)


def _norm_mm_kernel(*refs, emit_h, n_side, keep_w):
    x_ref, vec_ref, w_ref = refs[:3]
    side_in = refs[3:3 + n_side]
    o_ref = refs[3 + n_side]
    n_out = 2 if emit_h else 1
    side_out = refs[3 + n_side + n_out:3 + 2 * n_side + n_out]
    scratch = refs[3 + 2 * n_side + n_out:]
    h_ref = refs[3 + n_side + 1] if emit_h else scratch[0]
    _side_cast(side_in, side_out)
    if keep_w:
        wb_ref = scratch[-1]

        @pl.when(pl.program_id(0) == 0)
        def _():
            wb_ref[pl.program_id(1)] = w_ref[0].astype(BF16)

        w = wb_ref[pl.program_id(1)]
    else:
        w = w_ref[0].astype(BF16)

    @pl.when(pl.program_id(1) == 0)
    def _():
        for rs in _row_subblocks(x_ref.shape[0]):
            h = (_rms(x_ref[rs, :]) * vec_ref[0:1] + vec_ref[1:2]).astype(BF16)
            h_ref[rs, :] = h
            o_ref[rs, :] = jnp.dot(h, w, preferred_element_type=F32).astype(o_ref.dtype)

    @pl.when(pl.program_id(1) > 0)
    def _():
        o_ref[...] = jnp.dot(h_ref[...], w, preferred_element_type=F32).astype(o_ref.dtype)


def _norm_mm(x, vec, w, n_cols, tm, tn, emit_h=False, side=(), keep_w=False):
    m = x.shape[0]
    nj = n_cols // tn
    out_specs = [pl.BlockSpec((tm, tn), lambda i, j: (i, j))]
    out_shape = [jax.ShapeDtypeStruct((m, n_cols), BF16)]
    scratch = []
    if emit_h:
        out_specs.append(pl.BlockSpec((tm, D_MODEL), lambda i, j: (i, 0)))
        out_shape.append(jax.ShapeDtypeStruct((m, D_MODEL), BF16))
    else:
        scratch.append(pltpu.VMEM((tm, D_MODEL), BF16))
    if keep_w:
        scratch.append(pltpu.VMEM((nj, D_MODEL, tn), BF16))
        w_spec = pl.BlockSpec((1, D_MODEL, tn), lambda i, j: (0, 0, jnp.where(i == 0, j, nj - 1)))
    else:
        w_spec = pl.BlockSpec((1, D_MODEL, tn), lambda i, j: (0, 0, j))
    side_in, side_out, side_shapes = _side_cast_specs(side, (m // tm) * nj, lambda i, j: i * nj + j)
    return pl.pallas_call(
        functools.partial(_norm_mm_kernel, emit_h=emit_h, n_side=len(side), keep_w=keep_w),
        grid=(m // tm, nj),
        in_specs=[pl.BlockSpec((tm, D_MODEL), lambda i, j: (i, 0)),
                  pl.BlockSpec((8, D_MODEL), lambda i, j: (0, 0)), w_spec] + side_in,
        out_specs=out_specs + side_out, out_shape=out_shape + side_shapes,
        scratch_shapes=scratch,
        compiler_params=_params(("arbitrary", "arbitrary"), 56),
        name="norm_mm",
    )(x, vec, w, *[s[0] for s in side])


def _seg_cumsum(g, row_in_chunk):
    p = g
    s = 1
    while s < CHUNK:
        p = p + jnp.where(row_in_chunk >= s, pltpu.roll(p, s, axis=0), 0.0)
        s *= 2
    return p


def _ctx_state_kernel(f_ref, i_ref, lb_ref, sfin_ref, p_buf, g_buf, k_buf, *, nchunk):
    d = pl.program_id(0)
    rows = f_ref.shape[0]
    lb = lb_ref[0, 0]
    f = lb + (1.0 - lb) * jax.nn.sigmoid(f_ref[...].astype(F32))
    g = jnp.log(f) * LOG2E
    ric = lax.broadcasted_iota(jnp.int32, (rows, LANES), 0) & (CHUNK - 1)
    p_buf[...] = _seg_cumsum(g, ric)
    g_buf[...] = g
    k_buf[...] = 1.0 - f

    s = jnp.zeros((HG_DK, HG_DK), F32)
    for j in range(nchunk):
        c = jnp.where(d == 0, j, nchunk - 1 - j)
        base = pl.multiple_of(c * CHUNK, CHUNK)
        sl = pl.ds(base, CHUNK)
        p = p_buf[sl, :]
        tot = p_buf[pl.ds(base + CHUNK - 1, 1), :]
        ex = jnp.where(d == 0, tot - p, p - g_buf[sl, :])
        kt = (k_buf[sl, :] * jnp.exp2(ex)).astype(BF16)
        u = lax.dot_general(i_ref[sl, :], kt, (((0,), (0,)), ((), ())), preferred_element_type=F32)
        s = s * jnp.exp2(tot) + u
    sfin_ref[0, 0] = s


def _ctx_states(z_ctx, lb4):
    rows = z_ctx.shape[0]
    return pl.pallas_call(
        functools.partial(_ctx_state_kernel, nchunk=rows // CHUNK),
        grid=(2, HG_HEADS),
        in_specs=[pl.BlockSpec((rows, HG_DK), lambda d, h: (0, d * HG_HEADS + h)),
                  pl.BlockSpec((rows, HG_DK), lambda d, h: (0, 2 * HG_HEADS + h)),
                  pl.BlockSpec((1, 1, 1, HG_DK), lambda d, h: (d, h, 0, 0))],
        out_specs=pl.BlockSpec((1, 1, HG_DK, HG_DK), lambda d, h: (d, h, 0, 0)),
        out_shape=jax.ShapeDtypeStruct((2, HG_HEADS, HG_DK, HG_DK), F32),
        scratch_shapes=[pltpu.VMEM((rows, HG_DK), F32) for _ in range(3)],
        compiler_params=_params(("parallel", "parallel"), 32),
        name="ctx_states",
    )(z_ctx, z_ctx, lb4)


def _gla_local_kernel(*refs, n_side):
    h_ref, wgu_ref, wv_ref, ff_ref, fb_ref, i_ref, q_ref, lb_ref = refs[:8]
    side_in = refs[8:8 + n_side]
    zb_ref, oi_ref, qe_ref, u_ref, d_ref = refs[8 + n_side:13 + n_side]
    side_out = refs[13 + n_side:13 + 2 * n_side]
    bufs = refs[13 + 2 * n_side:]
    _side_cast(side_in, side_out)
    rows = h_ref.shape[0]
    nchunk = rows // CHUNK
    proj_rows = min(rows, PROJ_ROWS)
    row_parts = rows // proj_rows

    def project(piece):
        lo = (piece // row_parts) * MXU_COLS
        rs = slice((piece % row_parts) * proj_rows, (piece % row_parts + 1) * proj_rows)
        w_ref, off = (wgu_ref, lo) if lo < 2 * HG_WIDTH else (wv_ref, lo - 2 * HG_WIDTH)
        zb_ref[rs, lo:lo + MXU_COLS] = jnp.dot(h_ref[rs, :], w_ref[:, off:off + MXU_COLS],
                                               preferred_element_type=F32).astype(BF16)

    n_pieces = row_parts * 3 * HG_WIDTH // MXU_COLS
    pieces_before = [(k + 1) * n_pieces // (HG_HEADS * nchunk) - k * n_pieces // (HG_HEADS * nchunk)
                     for k in range(HG_HEADS * nchunk)]

    ric = lax.broadcasted_iota(jnp.int32, (rows, LANES), 0) & (CHUNK - 1)
    blk = lax.broadcasted_iota(jnp.int32, (CHUNK, LANES), 0) // SUB
    r2 = lax.broadcasted_iota(jnp.int32, (CHUNK, CHUNK), 0)
    c2 = lax.broadcasted_iota(jnp.int32, (CHUNK, CHUNK), 1)
    nt = (((1,), (1,)), ((), ()))
    tn_dims = (((0,), (0,)), ((), ()))
    nsub = CHUNK // SUB
    emitted = [0]

    def before_chunk(k):
        for _ in range(pieces_before[k]):
            project(emitted[0])
            emitted[0] += 1

    for hd in range(HG_HEADS):
        _gla_local_head(hd, nchunk, ric, blk, r2, c2, nt, tn_dims, nsub, ff_ref, fb_ref, i_ref, q_ref, lb_ref,
                        oi_ref, qe_ref, u_ref, d_ref, bufs[6 * (hd % 2):6 * (hd % 2) + 6], before_chunk)
    assert emitted[0] == n_pieces


def _gla_local_head(hd, nchunk, ric, blk, r2, c2, nt, tn_dims, nsub, ff_ref, fb_ref, i_ref, q_ref, lb_ref,
                    oi_ref, qe_ref, u_ref, d_ref, bufs, before_chunk):
    pf_buf, pb_buf, gb_buf, kf_buf, kb_buf, q_buf = bufs
    cs = slice(hd * HG_DK, (hd + 1) * HG_DK)
    lbf = lb_ref[0, hd]
    lbb = lb_ref[1, hd]
    ff = lbf + (1.0 - lbf) * jax.nn.sigmoid(ff_ref[:, cs].astype(F32))
    pf_buf[...] = _seg_cumsum(jnp.log(ff) * LOG2E, ric)
    kf_buf[...] = 1.0 - ff
    fb = lbb + (1.0 - lbb) * jax.nn.sigmoid(fb_ref[:, cs].astype(F32))
    gb = jnp.log(fb) * LOG2E
    pb_buf[...] = _seg_cumsum(gb, ric)
    gb_buf[...] = gb
    kb_buf[...] = 1.0 - fb
    q_buf[...] = _silu(q_ref[:, cs].astype(F32)) * (HG_DK ** -0.5)

    for c in range(nchunk):
        before_chunk(hd * nchunk + c)
        base = c * CHUNK
        sl = pl.ds(base, CHUNK)
        pf = pf_buf[sl, :]
        pb = pb_buf[sl, :]
        gbc = gb_buf[sl, :]
        kf = kf_buf[sl, :]
        kb = kb_buf[sl, :]
        q = q_buf[sl, :]
        totb = pb_buf[pl.ds(base + CHUNK - 1, 1), :]
        suf = totb - pb + gbc
        mids, nids = [], []
        for i in range(nsub):
            mids.append(pf_buf[pl.ds(base + SUB * i + SUB // 2 - 1, 1), :])
            rmid = pl.ds(base + SUB * i + SUB // 2, 1)
            nids.append(totb - pb_buf[rmid, :] + gb_buf[rmid, :])
        mid_rows = jnp.concatenate([jnp.broadcast_to(m, (SUB, LANES)) for m in mids], axis=0)
        nid_rows = jnp.concatenate([jnp.broadcast_to(m, (SUB, LANES)) for m in nids], axis=0)
        qf_all = q * jnp.exp2(jnp.minimum(pf - mid_rows, EXP_CLAMP))
        qb_all = q * jnp.exp2(jnp.minimum(suf - nid_rows, EXP_CLAMP))
        qf_seg, kf_seg, qb_seg, kb_seg = [], [], [], []
        for i in range(nsub):
            qf_seg.append(jnp.where(blk == i, qf_all, 0.0))
            qb_seg.append(jnp.where(blk == i, qb_all, 0.0))
            hi = SUB * (i + 1)
            top = kf[:hi] * jnp.exp2(jnp.minimum(mids[i] - pf[:hi], EXP_CLAMP))
            kf_seg.append(top if hi == CHUNK else
                          jnp.concatenate([top, jnp.zeros((CHUNK - hi, LANES), F32)], axis=0))
            lo = SUB * i
            bot = kb[lo:] * jnp.exp2(jnp.minimum(nids[i] - suf[lo:], EXP_CLAMP))
            kb_seg.append(bot if lo == 0 else
                          jnp.concatenate([jnp.zeros((lo, LANES), F32), bot], axis=0))
        qf_big = jnp.concatenate(qf_seg, axis=1).astype(BF16)
        kf_big = jnp.concatenate(kf_seg, axis=1).astype(BF16)
        qb_big = jnp.concatenate(qb_seg, axis=1).astype(BF16)
        kb_big = jnp.concatenate(kb_seg, axis=1).astype(BF16)
        sc_f = lax.dot_general(qf_big, kf_big, nt, preferred_element_type=F32)
        sc_b = lax.dot_general(qb_big, kb_big, nt, preferred_element_type=F32)
        a = (jnp.where(c2 <= r2, sc_f, 0.0) + jnp.where(c2 >= r2, sc_b, 0.0)).astype(BF16)
        v = i_ref[sl, cs]
        oi_ref[sl, cs] = jnp.dot(a, v, preferred_element_type=F32).astype(BF16)
        qe_ref[sl, 2 * hd * HG_DK:2 * (hd + 1) * HG_DK] = jnp.concatenate(
            [q * jnp.exp2(pf), q * jnp.exp2(suf)], axis=1).astype(BF16)
        totf = pf_buf[pl.ds(base + CHUNK - 1, 1), :]
        kt = jnp.concatenate([kf * jnp.exp2(totf - pf), kb * jnp.exp2(pb - gbc)], axis=1).astype(BF16)
        u = lax.dot_general(v, kt, tn_dims, preferred_element_type=F32).astype(BF16)
        u_ref[0, hd, c] = u[:, :HG_DK]
        u_ref[1, hd, c] = u[:, HG_DK:]
        d_ref[0, hd, c * SUBLANES:(c + 1) * SUBLANES, :] = jnp.broadcast_to(jnp.exp2(totf), (SUBLANES, HG_DK))
        d_ref[1, hd, c * SUBLANES:(c + 1) * SUBLANES, :] = jnp.broadcast_to(jnp.exp2(totb), (SUBLANES, HG_DK))


def _gla_local(h, w_gu, w_v, z_a, lb4, rows_per_block, side=()):
    m = h.shape[0]
    nb = m // rows_per_block
    nchunk = rows_per_block // CHUNK
    n_chunks = m // CHUNK
    resident = lambda a: pl.BlockSpec(a.shape, lambda b: (0,) * a.ndim, pipeline_mode=pl.Buffered(1))
    col = lambda k: pl.BlockSpec((rows_per_block, HG_WIDTH), lambda b, k=k: (b, k))
    side_in, side_out, side_shapes = _side_cast_specs(side, nb, lambda b: b)
    return pl.pallas_call(
        functools.partial(_gla_local_kernel, n_side=len(side)),
        grid=(nb,),
        in_specs=[pl.BlockSpec((rows_per_block, D_MODEL), lambda b: (b, 0)),
                  resident(w_gu), resident(w_v), col(0), col(1), col(2), col(3), resident(lb4)] + side_in,
        out_specs=[pl.BlockSpec((rows_per_block, 3 * HG_WIDTH), lambda b: (b, 0)),
                   pl.BlockSpec((rows_per_block, HG_WIDTH), lambda b: (b, 0)),
                   pl.BlockSpec((rows_per_block, 2 * HG_WIDTH), lambda b: (b, 0)),
                   pl.BlockSpec((2, HG_HEADS, nchunk, HG_DK, HG_DK), lambda b: (0, 0, b, 0, 0)),
                   pl.BlockSpec((2, HG_HEADS, nchunk * SUBLANES, HG_DK), lambda b: (0, 0, b, 0))] + side_out,
        out_shape=[jax.ShapeDtypeStruct((m, 3 * HG_WIDTH), BF16),
                   jax.ShapeDtypeStruct((m, HG_WIDTH), BF16),
                   jax.ShapeDtypeStruct((m, 2 * HG_WIDTH), BF16),
                   jax.ShapeDtypeStruct((2, HG_HEADS, n_chunks, HG_DK, HG_DK), BF16),
                   jax.ShapeDtypeStruct((2, HG_HEADS, n_chunks * SUBLANES, HG_DK), F32)] + side_shapes,
        scratch_shapes=[pltpu.VMEM((rows_per_block, HG_DK), F32) for _ in range(12)],
        compiler_params=_params(("parallel",), 56),
        name="gla_local",
    )(h, w_gu, w_v, z_a, z_a, z_a, z_a, lb4, *[s[0] for s in side])


def _gla_finish_kernel(*refs, nchunk, n_chunks, n_side):
    oi_ref, qe_ref, gt_ref, gain_ref, u_hbm, d_ref, s0_ref = refs[:7]
    side_in = refs[7:7 + n_side]
    o_ref = refs[7 + n_side]
    side_out = refs[8 + n_side:8 + 2 * n_side]
    s_all, u_buf, u_sems = refs[-3:]
    _side_cast(side_in, side_out)
    head = pl.program_id(0)

    def u_copy(hd):
        slot = hd % 2
        return pltpu.make_async_copy(u_hbm.at[:, hd], u_buf.at[slot], u_sems.at[slot])

    @pl.when((pl.program_id(1) == 0) & (head == 0))
    def _():
        u_copy(0).start()

    @pl.when((pl.program_id(1) == 0) & (head + 1 < HG_HEADS))
    def _():
        u_copy(head + 1).start()

    @pl.when(pl.program_id(1) == 0)
    def _():
        u_copy(head).wait()
        slot = head % 2

        def step(c, carry):
            sf, sb = carry
            cb = n_chunks - 1 - c
            s_all[0, c] = sf.astype(BF16)
            s_all[1, cb] = sb.astype(BF16)
            df = d_ref[0, 0, pl.ds(pl.multiple_of(c * SUBLANES, SUBLANES), 1), :]
            db = d_ref[1, 0, pl.ds(pl.multiple_of(cb * SUBLANES, SUBLANES), 1), :]
            return (sf * df + u_buf[slot, 0, c].astype(F32), sb * db + u_buf[slot, 1, cb].astype(F32))

        lax.fori_loop(0, n_chunks, step, (s0_ref[0, 0], s0_ref[1, 0]), unroll=4)

    nt = (((1,), (1,)), ((), ()))
    gain = gain_ref[0]
    first = pl.program_id(1) * nchunk
    for c in range(nchunk):
        sl = slice(c * CHUNK, (c + 1) * CHUNK)
        scat = jnp.concatenate([s_all[0, first + c], s_all[1, first + c]], axis=1)
        o = oi_ref[sl, :].astype(F32) + lax.dot_general(qe_ref[sl, :], scat, nt, preferred_element_type=F32)
        o = _rms(o) * gain
        o_ref[sl, :] = (o * _silu(gt_ref[sl, :].astype(F32))).astype(BF16)


def _gla_finish(oi, qe, z_b, hg_gain, s_inc, s_dec, s0, rows_per_block, side=()):
    m = oi.shape[0]
    nb = m // rows_per_block
    nchunk = rows_per_block // CHUNK
    n_chunks = m // CHUNK
    side_in, side_out, side_shapes = _side_cast_specs(side, HG_HEADS * nb, lambda h, b: h * nb + b)
    return pl.pallas_call(
        functools.partial(_gla_finish_kernel, nchunk=nchunk, n_chunks=n_chunks, n_side=len(side)),
        grid=(HG_HEADS, nb),
        in_specs=[pl.BlockSpec((rows_per_block, HG_DK), lambda h, b: (b, h)),
                  pl.BlockSpec((rows_per_block, 2 * HG_DK), lambda h, b: (b, h)),
                  pl.BlockSpec((rows_per_block, HG_DK), lambda h, b: (b, h)),
                  pl.BlockSpec((1, 1, HG_DK), lambda h, b: (h, 0, 0)),
                  pl.BlockSpec(memory_space=pl.ANY),
                  pl.BlockSpec((2, 1, n_chunks * SUBLANES, HG_DK), lambda h, b: (0, h, 0, 0)),
                  pl.BlockSpec((2, 1, HG_DK, HG_DK), lambda h, b: (0, h, 0, 0))] + side_in,
        out_specs=[pl.BlockSpec((rows_per_block, HG_DK), lambda h, b: (b, h))] + side_out,
        out_shape=[jax.ShapeDtypeStruct((m, HG_WIDTH), BF16)] + side_shapes,
        scratch_shapes=[pltpu.VMEM((2, n_chunks, HG_DK, HG_DK), BF16),
                        pltpu.VMEM((2, 2, n_chunks, HG_DK, HG_DK), BF16),
                        pltpu.SemaphoreType.DMA((2,))],
        compiler_params=_params(("arbitrary", "arbitrary"), 48),
        name="gla_finish",
    )(oi, qe, z_b, hg_gain.reshape(HG_HEADS, 1, HG_DK), s_inc, s_dec, s0, *[s[0] for s in side])


def _residual_epilogue(mix, x_ref, vec_ref, xo_ref, ho_ref, rs):
    xn = x_ref[rs, :] + _rms(mix) * vec_ref[0:1]
    xo_ref[rs, :] = xn
    if ho_ref is not None:
        hn = (_rms(xn) * vec_ref[1:2] + vec_ref[2:3]).astype(BF16)
        outs = ho_ref if isinstance(ho_ref, (tuple, list)) else (ho_ref,)
        width = hn.shape[1] // len(outs)
        for k, out in enumerate(outs):
            out[rs, :] = hn[:, k * width:(k + 1) * width]


def _spatial_chunk(u_raw, v_raw, w_ref, bias_ref, lng_ref, lnb_ref):
    v = _gelu_tanh(v_raw.astype(F32))
    xc = v - jnp.mean(v, axis=-1, keepdims=True)
    y = (xc * lax.rsqrt(jnp.mean(xc * xc, axis=-1, keepdims=True) + EPS) * lng_ref[...] + lnb_ref[...]).astype(BF16)
    u = _gelu_tanh(u_raw.astype(F32))
    parts = []
    for g in range(SG_GROUPS):
        cs = slice(g * SG_CH, (g + 1) * SG_CH)
        mixed = jnp.dot(w_ref[g], y[:, cs], preferred_element_type=F32) + bias_ref[:, cs]
        parts.append((u[:, cs] * mixed).astype(BF16))
    return jnp.concatenate(parts, axis=1)


def _mm_res_kernel(*refs, n_a, n_sp, emit_h):
    a_refs = refs[:n_a]
    sp_refs = refs[n_a:n_a + 6 * n_sp]
    n_w = n_a + n_sp
    w_refs = refs[n_a + 6 * n_sp:n_a + 6 * n_sp + n_w]
    x_ref, vec_ref, xo_ref = refs[n_a + 6 * n_sp + n_w:n_a + 6 * n_sp + n_w + 3]
    ho_ref = refs[n_a + 6 * n_sp + n_w + 3] if emit_h else None
    for rs in _row_subblocks(x_ref.shape[0], SG_CHUNK):
        mix = jnp.dot(a_refs[0][rs, :], w_refs[0][...], preferred_element_type=F32)
        for a_ref, w_ref in zip(a_refs[1:], w_refs[1:]):
            mix = mix + jnp.dot(a_ref[rs, :], w_ref[...], preferred_element_type=F32)
        if n_sp:
            u_ref, v_ref = sp_refs[:2]
            s = _spatial_chunk(u_ref[rs, :], v_ref[rs, :], *sp_refs[2:])
            mix = mix + jnp.dot(s, w_refs[-1][...], preferred_element_type=F32)
        _residual_epilogue(mix, x_ref, vec_ref, xo_ref, ho_ref, rs)


def _mm_res(a_list, w, x, vec, tm, emit_h=True, spatial=None):
    m = x.shape[0]
    n_a = len(a_list)
    widths = [a.shape[1] for a in a_list] + ([SG_WIDTH] if spatial else [])
    offs = np.cumsum([0] + widths)
    sp_specs, sp_args = [], []
    if spatial:
        z_b = spatial[0]
        sp_specs = [pl.BlockSpec((tm, SG_WIDTH), lambda i: (i, 1)), pl.BlockSpec((tm, SG_WIDTH), lambda i: (i, 2)),
                    pl.BlockSpec((SG_GROUPS, SG_CHUNK, SG_CHUNK), lambda i: (0, 0, 0)),
                    pl.BlockSpec((SG_CHUNK, SG_WIDTH), lambda i: (0, 0)),
                    pl.BlockSpec((1, SG_WIDTH), lambda i: (0, 0)), pl.BlockSpec((1, SG_WIDTH), lambda i: (0, 0))]
        sp_args = [z_b, z_b, *spatial[1:]]
    in_specs = ([pl.BlockSpec((tm, a.shape[1]), lambda i: (i, 0)) for a in a_list] + sp_specs
                + [pl.BlockSpec((wd, D_MODEL), lambda i, k=int(o) // wd: (k, 0)) for wd, o in zip(widths, offs)]
                + [pl.BlockSpec((tm, D_MODEL), lambda i: (i, 0)),
                   pl.BlockSpec((8, D_MODEL), lambda i: (0, 0))])
    out_specs = [pl.BlockSpec((tm, D_MODEL), lambda i: (i, 0))]
    out_shape = [jax.ShapeDtypeStruct((m, D_MODEL), F32)]
    if emit_h:
        out_specs.append(pl.BlockSpec((tm, D_MODEL), lambda i: (i, 0)))
        out_shape.append(jax.ShapeDtypeStruct((m, D_MODEL), BF16))
    return pl.pallas_call(
        functools.partial(_mm_res_kernel, n_a=n_a, n_sp=1 if spatial else 0, emit_h=emit_h),
        grid=(m // tm,),
        in_specs=in_specs, out_specs=out_specs, out_shape=out_shape,
        compiler_params=_params(("parallel",), 56),
        name="mm_residual",
    )(*a_list, *sp_args, *[w] * len(widths), x, vec)


def _mlp_kernel(*refs, n_h, n_side, nj, n_steps, loop_ff):
    h_ref, w1_ref, w2_ref, x_ref, vec_ref = refs[:5]
    side_in = refs[5:5 + n_side]
    xo_ref = refs[5 + n_side]
    ho_ref = tuple(refs[6 + n_side:6 + n_side + n_h]) or None
    n_out = 1 + n_h
    side_out = refs[5 + n_side + n_out:5 + 2 * n_side + n_out]
    w1_buf, w2_buf, sems = refs[-3:]
    acc_ref = xo_ref
    _side_cast(side_in, side_out)
    last = nj - 1
    tf = w1_buf.shape[2]
    first_step = pl.program_id(0) * nj + (0 if loop_ff else pl.program_id(1))

    def tile_copies(s):
        slot = s % WEIGHT_SLOTS
        col = (s % nj) * tf
        if not isinstance(col, int):
            col = pl.multiple_of(col, tf)
        return (pltpu.make_async_copy(w1_ref.at[:, pl.ds(col, tf)], w1_buf.at[slot], sems.at[0, slot]),
                pltpu.make_async_copy(w2_ref.at[pl.ds(col, tf), :], w2_buf.at[slot], sems.at[1, slot]))

    @pl.when(first_step == 0)
    def _():
        for s in range(WEIGHT_SLOTS - 1):
            for cp in tile_copies(s):
                cp.start()

    def ff_step(jj, mode):
        step = pl.program_id(0) * nj + jj

        @pl.when(step + (WEIGHT_SLOTS - 1) < n_steps)
        def _():
            for cp in tile_copies(step + (WEIGHT_SLOTS - 1)):
                cp.start()

        for cp in tile_copies(step):
            cp.wait()
        slot = step % WEIGHT_SLOTS
        for rs in _row_subblocks(h_ref.shape[0]):
            a = jnp.dot(h_ref[rs, :], w1_buf[slot], preferred_element_type=F32)
            a = jnp.square(jnp.maximum(a, 0.0)).astype(BF16)
            p = jnp.dot(a, w2_buf[slot], preferred_element_type=F32)
            if mode == "first":
                acc_ref[rs, :] = p
            elif mode == "last":
                _residual_epilogue(acc_ref[rs, :] + p, x_ref, vec_ref, xo_ref, ho_ref, rs)
            else:
                acc_ref[rs, :] += p

    if loop_ff:
        ff_step(0, "first")

        def middle(jj, carry):
            ff_step(jj, "middle")
            return carry

        lax.fori_loop(1, last, middle, 0)
        ff_step(last, "last")
    else:
        j = pl.program_id(1)
        pl.when(j == 0)(lambda: ff_step(j, "first"))
        pl.when(jnp.logical_and(j > 0, j < last))(lambda: ff_step(j, "middle"))
        pl.when(j == last)(lambda: ff_step(j, "last"))


def _mlp(h, w1, w2, x, vec, tm, tf, n_h, side=()):
    m = x.shape[0]
    nj = D_FF // tf
    loop_ff = not side
    row = (lambda i: (i, 0)) if loop_ff else (lambda i, j: (i, 0))
    const = (lambda i: (0, 0)) if loop_ff else (lambda i, j: (0, 0))
    out_specs = [pl.BlockSpec((tm, D_MODEL), row)]
    out_shape = [jax.ShapeDtypeStruct((m, D_MODEL), F32)]
    for _ in range(n_h):
        out_specs.append(pl.BlockSpec((tm, D_MODEL // n_h), row))
        out_shape.append(jax.ShapeDtypeStruct((m, D_MODEL // n_h), BF16))
    side_in, side_out, side_shapes = _side_cast_specs(side, (m // tm) * nj, lambda i, j: i * nj + j)
    return pl.pallas_call(
        functools.partial(_mlp_kernel, n_h=n_h, n_side=len(side), nj=nj, n_steps=(m // tm) * nj, loop_ff=loop_ff),
        grid=(m // tm,) if loop_ff else (m // tm, nj),
        in_specs=[pl.BlockSpec((tm, D_MODEL), row),
                  pl.BlockSpec(memory_space=pl.ANY),
                  pl.BlockSpec(memory_space=pl.ANY),
                  pl.BlockSpec((tm, D_MODEL), row),
                  pl.BlockSpec((8, D_MODEL), const)] + side_in,
        out_specs=out_specs + side_out, out_shape=out_shape + side_shapes,
        scratch_shapes=[pltpu.VMEM((WEIGHT_SLOTS, D_MODEL, tf), BF16),
                        pltpu.VMEM((WEIGHT_SLOTS, tf, D_MODEL), BF16),
                        pltpu.SemaphoreType.DMA((2, WEIGHT_SLOTS))],
        compiler_params=_params(("arbitrary",) if loop_ff else ("arbitrary", "arbitrary"), 60),
        name="mlp",
    )(h, w1, w2, x, vec, *[s[0] for s in side])


def _dft_constants():
    def cs(n):
        k = np.arange(n)
        ang = 2.0 * np.pi * ((k[:, None] * k[None, :]) % n) / n
        return np.cos(ang) / np.sqrt(n), np.sin(ang) / np.sqrt(n)

    cc, sc = cs(FT_CH)
    c1, s1 = cs(FFT_L1)
    m1 = np.concatenate([c1, -s1], axis=0)
    c2, s2 = cs(FFT_L2)
    m2 = np.block([[c2, s2], [-s2, c2]])
    u1 = np.arange(FFT_L1)[:, None]
    t2 = np.arange(FFT_L2)[None, :]
    ang = 2.0 * np.pi * ((u1 * t2) % SEQ) / SEQ
    tw_c = np.cos(ang).reshape(SEQ, 1)
    tw_s = np.sin(ang).reshape(SEQ, 1)
    f = lambda a: jnp.asarray(a, F32)
    return f(cc), f(sc), f(m1), f(m2), f(tw_c), f(tw_s)


def _swap_row_factors(a, n_outer, n_inner):
    return a.reshape(n_outer, n_inner, a.shape[1]).transpose(1, 0, 2).reshape(a.shape)


def _pos_dft1_kernel(x_ref, m_ref, zr_ref, zi_ref):
    for k in range(x_ref.shape[0] // FFT_L1):
        sl = slice(k * FFT_L1, (k + 1) * FFT_L1)
        z = jnp.dot(m_ref[...], x_ref[sl, :], preferred_element_type=F32)
        zr_ref[sl, :] = z[:FFT_L1].astype(BF16)
        zi_ref[sl, :] = z[FFT_L1:].astype(BF16)


def _pos_dft1(xt, m1, slabs):
    rows = slabs * FFT_L1
    blk = pl.BlockSpec((rows, xt.shape[1]), lambda j: (j, 0))
    return pl.pallas_call(
        _pos_dft1_kernel,
        grid=(SEQ // rows,),
        in_specs=[blk, pl.BlockSpec((2 * FFT_L1, FFT_L1), lambda j: (0, 0))],
        out_specs=[blk, blk],
        out_shape=[jax.ShapeDtypeStruct(xt.shape, BF16)] * 2,
        compiler_params=_params(("parallel",), 32),
        name="pos_dft1",
    )(xt, m1)


def _pos_dft2_kernel(zr_ref, zi_ref, c_ref, s_ref, m_ref, wc_ref, ws_ref, y_ref, yr_buf, yi_buf):
    for k in range(zr_ref.shape[0] // FFT_L2):
        sl = slice(k * FFT_L2, (k + 1) * FFT_L2)
        zr = zr_ref[sl, :].astype(F32)
        zi = zi_ref[sl, :].astype(F32)
        c = c_ref[sl, :]
        s = s_ref[sl, :]
        t = jnp.concatenate([(zr * c + zi * s).astype(BF16), (zi * c - zr * s).astype(BF16)], axis=0)
        y = jnp.dot(m_ref[...], t, preferred_element_type=F32)
        yr_buf[sl, :] = y[:FFT_L2].astype(BF16)
        yi_buf[sl, :] = y[FFT_L2:].astype(BF16)
    for g in range(zr_ref.shape[1] // FT_CH):
        cs = slice(g * FT_CH, (g + 1) * FT_CH)
        y_ref[:, cs] = (jnp.dot(yr_buf[:, cs], wc_ref[...], preferred_element_type=F32)
                        + jnp.dot(yi_buf[:, cs], ws_ref[...], preferred_element_type=F32)).astype(BF16)


def _pos_dft2(zr, zi, tw_c, tw_s, m2, wc, ws, slabs):
    rows = slabs * FFT_L2
    width = zr.shape[1]
    blk = pl.BlockSpec((rows, width), lambda j: (j, 0))
    tw = pl.BlockSpec((rows, 1), lambda j: (j, 0))
    const = lambda a: pl.BlockSpec(a.shape, lambda j: (0, 0))
    return pl.pallas_call(
        _pos_dft2_kernel,
        grid=(SEQ // rows,),
        in_specs=[blk, blk, tw, tw, const(m2), const(wc), const(ws)],
        out_specs=blk,
        out_shape=jax.ShapeDtypeStruct(zr.shape, BF16),
        scratch_shapes=[pltpu.VMEM((rows, width), BF16), pltpu.VMEM((rows, width), BF16)],
        compiler_params=_params(("parallel",), 40),
        name="pos_dft2",
    )(zr, zi, tw_c, tw_s, m2, wc, ws)


def _fourier_mix(h_parts):
    cc, sc, m1, m2, tw_c, tw_s = _dft_constants()
    m1, m2, cc, sc = m1.astype(BF16), m2.astype(BF16), cc.astype(BF16), sc.astype(BF16)
    hts = [_swap_row_factors(h, FFT_L1, FFT_L2) for h in h_parts]
    zs = [_pos_dft1(ht, m1, DFT_ROWS // FFT_L1) for ht in hts]
    zs = [[_swap_row_factors(z, FFT_L2, FFT_L1) for z in zri] for zri in zs]
    yps = [_pos_dft2(zr, zi, tw_c, tw_s, m2, cc, sc, DFT_ROWS // FFT_L2) for zr, zi in zs]
    return [_swap_row_factors(yp, FFT_L1, FFT_L2) for yp in yps]


def _pack_rows(*rows):
    rows = [r.reshape(1, D_MODEL).astype(F32) for r in rows]
    rows += [jnp.zeros((1, D_MODEL), F32)] * (8 - len(rows))
    return jnp.concatenate(rows, axis=0)


def kernel(x, c, ctx, c_ctx, w_ada, b_ada, norm_gain, w_in, w_out, lb_raw, hg_norm_gain,
           sg_w, sg_b, sg_ln_gain, sg_ln_bias, w_fourier, w_mlp_in, w_mlp_out):
    assert x.shape == (1, SEQ, D_MODEL) and ctx.shape == (1, CTX_LEN, D_MODEL)
    xs = x.reshape(SEQ, D_MODEL)
    cs = ctx.reshape(CTX_LEN, D_MODEL)

    mods = _ada_mods(jnp.concatenate([c.reshape(1, D_MODEL), c_ctx.reshape(1, D_MODEL)], axis=0), w_ada, b_ada)
    mod = lambda l, r, j: mods[l, r, j * D_MODEL:(j + 1) * D_MODEL]

    lb = jnp.cumsum(jax.nn.softmax(lb_raw.astype(F32), axis=1), axis=1)[:, 0]
    lb4 = lb.reshape(2, HG_HEADS, 1, HG_DK)

    pre = lambda gain, l, r, j: _pack_rows(gain * (1.0 + mod(l, r, j + 1)), mod(l, r, j))
    post = lambda gain, l, j, *nxt: _pack_rows(mod(l, 0, j) * gain, *nxt)

    z_a, h0, w_gu, w_v = _norm_mm(xs, pre(norm_gain[0, 0], 0, 0, 0), w_in, 4 * HG_WIDTH, MM_ROWS, MM_COLS,
                                  emit_h=True, keep_w=True,
                                  side=[(w_in, 0, 2 * HG_WIDTH, 2), (w_in, 0, HG_WIDTH, 6)])
    (z_ctx,) = _norm_mm(cs, pre(norm_gain[0, 0], 0, 1, 0), w_in, 3 * HG_WIDTH, CTX_LEN, MM_COLS)
    s_ctx = _ctx_states(z_ctx, lb4)
    z_b, o_intra, q_dec, s_inc, s_dec, w_out0, w1_0, w2_0 = _gla_local(
        h0, w_gu, w_v, z_a, lb4, GLA_LOCAL_ROWS, side=[(w_out, 0), (w_mlp_in, 0), (w_mlp_out, 0)])
    o, w_fou = _gla_finish(o_intra, q_dec, z_b, hg_norm_gain[0], s_inc, s_dec, s_ctx, GLA_FINISH_ROWS,
                           side=[(w_fourier, 0)])
    bias_full = jnp.repeat(sg_b[0].T.astype(F32), SG_CH, axis=1)
    vec = post(norm_gain[0, 1], 0, 2, *pre(norm_gain[0, 2], 0, 0, 3)[:2])
    x1, h = _mm_res([o], w_out0, xs, vec, MM_ROWS,
                    spatial=(z_b, sg_w[0].astype(BF16), bias_full, sg_ln_gain[0].reshape(1, SG_WIDTH),
                             sg_ln_bias[0].reshape(1, SG_WIDTH)))
    vec = post(norm_gain[0, 3], 0, 5, *pre(norm_gain[1, 0], 1, 0, 0)[:2])
    x2, h_a, h_b, w1_1, w2_1 = _mlp(h, w1_0, w2_0, x1, vec, MM_ROWS, MLP_FF_COLS, 2,
                                    side=[(w_mlp_in, 1), (w_mlp_out, 1)])

    y_parts = _fourier_mix([h_a, h_b])
    vec = post(norm_gain[1, 1], 1, 2, *pre(norm_gain[1, 2], 1, 0, 3)[:2])
    x3, h = _mm_res(y_parts, w_fou, x2, vec, MM_ROWS)
    (x4,) = _mlp(h, w1_1, w2_1, x3, post(norm_gain[1, 3], 1, 5), MM_ROWS, MLP_FF_COLS, 0)
    return x4.reshape(1, SEQ, D_MODEL)
```

```python
import functools

import numpy as np
import jax
import jax.numpy as jnp
from jax import lax
from jax.experimental import pallas as pl
from jax.experimental.pallas import tpu as pltpu

D_MODEL = 2048
SEQ = 8192
DEPTH = 2
CTX_LEN = 256
CHUNK = 64
SUB = 16
ADA_COLS = 2048
MM_ROWS = 512
MM_COLS = 1024
MLP_FF_COLS = 1024
WEIGHT_SLOTS = 3
SUBROWS = 256
GLA_LOCAL_ROWS = 256
GLA_FINISH_ROWS = 4096
DFT_ROWS = 1024
HG_HEADS = 8
HG_DK = 128
HG_WIDTH = HG_HEADS * HG_DK
SG_GROUPS = 8
SG_CH = 128
SG_WIDTH = SG_GROUPS * SG_CH
SG_CHUNK = 128
FT_GROUPS = 4
FT_CH = D_MODEL // FT_GROUPS
FFT_L1 = 64
FFT_L2 = 128
IN_WIDTH = 5 * HG_WIDTH + 2 * SG_WIDTH
D_FF = 4 * D_MODEL
N_MOD = 6
EPS = 1e-6
EXP_CLAMP = 115.0
LOG2E = 1.0 / float(np.log(2.0))
LANES = 128
SUBLANES = 8
MXU_COLS = 256
PROJ_ROWS = 256

F32 = jnp.float32
BF16 = jnp.bfloat16
MIB = 1024 * 1024


def _params(semantics, vmem_mib):
    return pltpu.CompilerParams(dimension_semantics=semantics, vmem_limit_bytes=vmem_mib * MIB)


def _rms(x):
    return x * lax.rsqrt(jnp.mean(x * x, axis=-1, keepdims=True) + EPS)


def _silu(x):
    return x * jax.nn.sigmoid(x)


def _row_subblocks(rows, step=SUBROWS):
    step = min(rows, step)
    return [slice(s, s + step) for s in range(0, rows, step)]


def _gelu_tanh(x):
    cdf = 0.5 * (1.0 + jnp.tanh(float(np.sqrt(2.0 / np.pi)) * (x + 0.044715 * (x * x * x))))
    return x * cdf


def _ada_kernel(c_ref, w_ref, b_ref, o_ref, s_buf):
    @pl.when((pl.program_id(0) == 0) & (pl.program_id(1) == 0))
    def _():
        s_buf[...] = _silu(c_ref[...])

    tn = o_ref.shape[-1]
    for r in range(2):
        s = s_buf[r]
        cols = [jnp.sum(w_ref[0, :, j * LANES:(j + 1) * LANES] * s, axis=0, keepdims=True)
                for j in range(tn // LANES)]
        o_ref[0, r:r + 1, :] = jnp.concatenate(cols, axis=1) + b_ref[0]


def _ada_mods(c2, w_ada, b_ada):
    tn = ADA_COLS
    n = N_MOD * D_MODEL
    cb = jnp.broadcast_to(c2[:, :, None], (2, D_MODEL, LANES))
    return pl.pallas_call(
        _ada_kernel,
        grid=(DEPTH, n // tn),
        in_specs=[pl.BlockSpec((2, D_MODEL, LANES), lambda l, j: (0, 0, 0)),
                  pl.BlockSpec((1, D_MODEL, tn), lambda l, j: (l, 0, j)),
                  pl.BlockSpec((1, 1, tn), lambda l, j: (l, 0, j))],
        out_specs=pl.BlockSpec((1, 2, tn), lambda l, j: (l, 0, j)),
        out_shape=jax.ShapeDtypeStruct((DEPTH, 2, n), F32),
        scratch_shapes=[pltpu.VMEM((2, D_MODEL, LANES), F32)],
        compiler_params=_params(("arbitrary", "arbitrary"), 40),
        name="ada_mods",
    )(cb, w_ada, b_ada.reshape(DEPTH, 1, n))


def _side_cast_specs(weights, n_steps, step_of):
    in_specs, out_specs, out_shapes = [], [], []
    for w, layer, *cols in weights:
        width, cidx = cols if cols else (w.shape[2], 0)
        assert w.shape[1] % n_steps == 0, "every grid step must get an equal row slab"
        rows = w.shape[1] // n_steps
        in_specs.append(pl.BlockSpec((1, rows, width),
                                     lambda *g, layer=layer, cidx=cidx: (layer, step_of(*g), cidx)))
        out_specs.append(pl.BlockSpec((rows, width), lambda *g: (step_of(*g), 0)))
        out_shapes.append(jax.ShapeDtypeStruct((w.shape[1], width), BF16))
    return in_specs, out_specs, out_shapes


def _side_cast(side_in, side_out):
    for wi, wo in zip(side_in, side_out):
        wo[...] = wi[0].astype(BF16)


def _norm_mm_kernel(x_ref, vec_ref, w_ref, o_ref, h_ref):
    w = w_ref[0].astype(BF16)

    @pl.when(pl.program_id(1) == 0)
    def _():
        h_ref[...] = (_rms(x_ref[...]) * vec_ref[0:1] + vec_ref[1:2]).astype(BF16)

    o_ref[...] = jnp.dot(h_ref[...], w, preferred_element_type=F32).astype(o_ref.dtype)


def _norm_mm(x, vec, w, n_cols, tm, tn):
    m = x.shape[0]
    return pl.pallas_call(
        _norm_mm_kernel,
        grid=(m // tm, n_cols // tn),
        in_specs=[pl.BlockSpec((tm, D_MODEL), lambda i, j: (i, 0)),
                  pl.BlockSpec((8, D_MODEL), lambda i, j: (0, 0)),
                  pl.BlockSpec((1, D_MODEL, tn), lambda i, j: (0, 0, j))],
        out_specs=pl.BlockSpec((tm, tn), lambda i, j: (i, j)),
        out_shape=jax.ShapeDtypeStruct((m, n_cols), BF16),
        scratch_shapes=[pltpu.VMEM((tm, D_MODEL), BF16)],
        compiler_params=_params(("parallel", "arbitrary"), 40),
        name="norm_mm",
    )(x, vec, w)


def _norm_mm_rows_kernel(*refs, n_side, nj):
    x_ref, vec_ref, w_hbm = refs[:3]
    side_in = refs[3:3 + n_side]
    o_ref, h_ref = refs[3 + n_side:5 + n_side]
    side_out = refs[5 + n_side:5 + 2 * n_side]
    wb_ref, stage, sems = refs[5 + 2 * n_side:]
    _side_cast(side_in, side_out)
    tn = wb_ref.shape[2]

    def tile_copy(t):
        return pltpu.make_async_copy(w_hbm.at[0, :, pl.ds(t * tn, tn)], stage.at[t % 2], sems.at[t % 2])

    def row_block(build_weights):
        if build_weights:
            tile_copy(0).start()
        for t in range(nj):
            if build_weights:
                if t + 1 < nj:
                    tile_copy(t + 1).start()
                tile_copy(t).wait()
                wb_ref[t] = stage[t % 2].astype(BF16)
            w = wb_ref[t]
            cols = slice(t * tn, (t + 1) * tn)
            if t == 0:
                for rs in _row_subblocks(x_ref.shape[0]):
                    h = (_rms(x_ref[rs, :]) * vec_ref[0:1] + vec_ref[1:2]).astype(BF16)
                    h_ref[rs, :] = h
                    o_ref[rs, cols] = jnp.dot(h, w, preferred_element_type=F32).astype(o_ref.dtype)
            else:
                o_ref[:, cols] = jnp.dot(h_ref[...], w, preferred_element_type=F32).astype(o_ref.dtype)

    pl.when(pl.program_id(0) == 0)(lambda: row_block(True))
    pl.when(pl.program_id(0) > 0)(lambda: row_block(False))


def _norm_mm_rows(x, vec, w, n_cols, tm, tn, side=()):
    m = x.shape[0]
    nj = n_cols // tn
    side_in, side_out, side_shapes = _side_cast_specs(side, m // tm, lambda i: i)
    return pl.pallas_call(
        functools.partial(_norm_mm_rows_kernel, n_side=len(side), nj=nj),
        grid=(m // tm,),
        in_specs=[pl.BlockSpec((tm, D_MODEL), lambda i: (i, 0)),
                  pl.BlockSpec((8, D_MODEL), lambda i: (0, 0)),
                  pl.BlockSpec(memory_space=pl.ANY)] + side_in,
        out_specs=[pl.BlockSpec((tm, n_cols), lambda i: (i, 0)),
                   pl.BlockSpec((tm, D_MODEL), lambda i: (i, 0))] + side_out,
        out_shape=[jax.ShapeDtypeStruct((m, n_cols), BF16),
                   jax.ShapeDtypeStruct((m, D_MODEL), BF16)] + side_shapes,
        scratch_shapes=[pltpu.VMEM((nj, D_MODEL, tn), BF16),
                        pltpu.VMEM((2, D_MODEL, tn), F32),
                        pltpu.SemaphoreType.DMA((2,))],
        compiler_params=_params(("arbitrary",), 62),
        name="norm_mm_rows",
    )(x, vec, w, *[s[0] for s in side])


def _seg_cumsum(g, row_in_chunk):
    p = g
    s = 1
    while s < CHUNK:
        p = p + jnp.where(row_in_chunk >= s, pltpu.roll(p, s, axis=0), 0.0)
        s *= 2
    return p


def _ctx_state_kernel(f_ref, i_ref, lb_ref, sfin_ref, p_buf, g_buf, k_buf, *, nchunk):
    d = pl.program_id(0)
    rows = f_ref.shape[0]
    lb = lb_ref[0, 0]
    f = lb + (1.0 - lb) * jax.nn.sigmoid(f_ref[...].astype(F32))
    g = jnp.log(f) * LOG2E
    ric = lax.broadcasted_iota(jnp.int32, (rows, LANES), 0) & (CHUNK - 1)
    p_buf[...] = _seg_cumsum(g, ric)
    g_buf[...] = g
    k_buf[...] = 1.0 - f

    s = jnp.zeros((HG_DK, HG_DK), F32)
    for j in range(nchunk):
        c = jnp.where(d == 0, j, nchunk - 1 - j)
        base = pl.multiple_of(c * CHUNK, CHUNK)
        sl = pl.ds(base, CHUNK)
        p = p_buf[sl, :]
        tot = p_buf[pl.ds(base + CHUNK - 1, 1), :]
        ex = jnp.where(d == 0, tot - p, p - g_buf[sl, :])
        kt = (k_buf[sl, :] * jnp.exp2(ex)).astype(BF16)
        u = lax.dot_general(i_ref[sl, :], kt, (((0,), (0,)), ((), ())), preferred_element_type=F32)
        s = s * jnp.exp2(tot) + u
    sfin_ref[0, 0] = s


def _ctx_states(z_ctx, lb4):
    rows = z_ctx.shape[0]
    return pl.pallas_call(
        functools.partial(_ctx_state_kernel, nchunk=rows // CHUNK),
        grid=(2, HG_HEADS),
        in_specs=[pl.BlockSpec((rows, HG_DK), lambda d, h: (0, d * HG_HEADS + h)),
                  pl.BlockSpec((rows, HG_DK), lambda d, h: (0, 2 * HG_HEADS + h)),
                  pl.BlockSpec((1, 1, 1, HG_DK), lambda d, h: (d, h, 0, 0))],
        out_specs=pl.BlockSpec((1, 1, HG_DK, HG_DK), lambda d, h: (d, h, 0, 0)),
        out_shape=jax.ShapeDtypeStruct((2, HG_HEADS, HG_DK, HG_DK), F32),
        scratch_shapes=[pltpu.VMEM((rows, HG_DK), F32) for _ in range(3)],
        compiler_params=_params(("parallel", "parallel"), 32),
        name="ctx_states",
    )(z_ctx, z_ctx, lb4)


def _gla_local_kernel(*refs, n_side):
    h_ref, wgu_ref, wv_ref, ff_ref, fb_ref, i_ref, q_ref, lb_ref = refs[:8]
    side_in = refs[8:8 + n_side]
    zb_ref, oi_ref, qe_ref, u_ref, d_ref = refs[8 + n_side:13 + n_side]
    side_out = refs[13 + n_side:13 + 2 * n_side]
    bufs = refs[13 + 2 * n_side:]
    _side_cast(side_in, side_out)
    rows = h_ref.shape[0]
    nchunk = rows // CHUNK
    proj_rows = min(rows, PROJ_ROWS)
    row_parts = rows // proj_rows

    def project(piece):
        lo = (piece // row_parts) * MXU_COLS
        rs = slice((piece % row_parts) * proj_rows, (piece % row_parts + 1) * proj_rows)
        w_ref, off = (wgu_ref, lo) if lo < 2 * HG_WIDTH else (wv_ref, lo - 2 * HG_WIDTH)
        zb_ref[rs, lo:lo + MXU_COLS] = jnp.dot(h_ref[rs, :], w_ref[:, off:off + MXU_COLS],
                                               preferred_element_type=F32).astype(BF16)

    n_pieces = row_parts * 3 * HG_WIDTH // MXU_COLS
    pieces_before = [(k + 1) * n_pieces // (HG_HEADS * nchunk) - k * n_pieces // (HG_HEADS * nchunk)
                     for k in range(HG_HEADS * nchunk)]

    ric = lax.broadcasted_iota(jnp.int32, (rows, LANES), 0) & (CHUNK - 1)
    blk = lax.broadcasted_iota(jnp.int32, (CHUNK, LANES), 0) // SUB
    r2 = lax.broadcasted_iota(jnp.int32, (CHUNK, CHUNK), 0)
    c2 = lax.broadcasted_iota(jnp.int32, (CHUNK, CHUNK), 1)
    nt = (((1,), (1,)), ((), ()))
    tn_dims = (((0,), (0,)), ((), ()))
    nsub = CHUNK // SUB
    emitted = [0]

    def before_chunk(k):
        for _ in range(pieces_before[k]):
            project(emitted[0])
            emitted[0] += 1

    for hd in range(HG_HEADS):
        _gla_local_head(hd, nchunk, ric, blk, r2, c2, nt, tn_dims, nsub, ff_ref, fb_ref, i_ref, q_ref, lb_ref,
                        oi_ref, qe_ref, u_ref, d_ref, bufs[6 * (hd % 2):6 * (hd % 2) + 6], before_chunk)
    assert emitted[0] == n_pieces


def _gla_local_head(hd, nchunk, ric, blk, r2, c2, nt, tn_dims, nsub, ff_ref, fb_ref, i_ref, q_ref, lb_ref,
                    oi_ref, qe_ref, u_ref, d_ref, bufs, before_chunk):
    pf_buf, pb_buf, gb_buf, kf_buf, kb_buf, q_buf = bufs
    cs = slice(hd * HG_DK, (hd + 1) * HG_DK)
    lbf = lb_ref[0, hd]
    lbb = lb_ref[1, hd]
    ff = lbf + (1.0 - lbf) * jax.nn.sigmoid(ff_ref[:, cs].astype(F32))
    pf_buf[...] = _seg_cumsum(jnp.log(ff) * LOG2E, ric)
    kf_buf[...] = 1.0 - ff
    fb = lbb + (1.0 - lbb) * jax.nn.sigmoid(fb_ref[:, cs].astype(F32))
    gb = jnp.log(fb) * LOG2E
    pb_buf[...] = _seg_cumsum(gb, ric)
    gb_buf[...] = gb
    kb_buf[...] = 1.0 - fb
    q_buf[...] = _silu(q_ref[:, cs].astype(F32)) * (HG_DK ** -0.5)

    for c in range(nchunk):
        before_chunk(hd * nchunk + c)
        base = c * CHUNK
        sl = pl.ds(base, CHUNK)
        pf = pf_buf[sl, :]
        pb = pb_buf[sl, :]
        gbc = gb_buf[sl, :]
        kf = kf_buf[sl, :]
        kb = kb_buf[sl, :]
        q = q_buf[sl, :]
        totb = pb_buf[pl.ds(base + CHUNK - 1, 1), :]
        suf = totb - pb + gbc
        mids, nids = [], []
        for i in range(nsub):
            mids.append(pf_buf[pl.ds(base + SUB * i + SUB // 2 - 1, 1), :])
            rmid = pl.ds(base + SUB * i + SUB // 2, 1)
            nids.append(totb - pb_buf[rmid, :] + gb_buf[rmid, :])
        mid_rows = jnp.concatenate([jnp.broadcast_to(m, (SUB, LANES)) for m in mids], axis=0)
        nid_rows = jnp.concatenate([jnp.broadcast_to(m, (SUB, LANES)) for m in nids], axis=0)
        qf_all = q * jnp.exp2(jnp.minimum(pf - mid_rows, EXP_CLAMP))
        qb_all = q * jnp.exp2(jnp.minimum(suf - nid_rows, EXP_CLAMP))
        qf_seg, kf_seg, qb_seg, kb_seg = [], [], [], []
        for i in range(nsub):
            qf_seg.append(jnp.where(blk == i, qf_all, 0.0))
            qb_seg.append(jnp.where(blk == i, qb_all, 0.0))
            hi = SUB * (i + 1)
            top = kf[:hi] * jnp.exp2(jnp.minimum(mids[i] - pf[:hi], EXP_CLAMP))
            kf_seg.append(top if hi == CHUNK else
                          jnp.concatenate([top, jnp.zeros((CHUNK - hi, LANES), F32)], axis=0))
            lo = SUB * i
            bot = kb[lo:] * jnp.exp2(jnp.minimum(nids[i] - suf[lo:], EXP_CLAMP))
            kb_seg.append(bot if lo == 0 else
                          jnp.concatenate([jnp.zeros((lo, LANES), F32), bot], axis=0))
        qf_big = jnp.concatenate(qf_seg, axis=1).astype(BF16)
        kf_big = jnp.concatenate(kf_seg, axis=1).astype(BF16)
        qb_big = jnp.concatenate(qb_seg, axis=1).astype(BF16)
        kb_big = jnp.concatenate(kb_seg, axis=1).astype(BF16)
        sc_f = lax.dot_general(qf_big, kf_big, nt, preferred_element_type=F32)
        sc_b = lax.dot_general(qb_big, kb_big, nt, preferred_element_type=F32)
        a = (jnp.where(c2 <= r2, sc_f, 0.0) + jnp.where(c2 >= r2, sc_b, 0.0)).astype(BF16)
        v = i_ref[sl, cs]
        oi_ref[sl, cs] = jnp.dot(a, v, preferred_element_type=F32).astype(BF16)
        qe_ref[sl, 2 * hd * HG_DK:2 * (hd + 1) * HG_DK] = jnp.concatenate(
            [q * jnp.exp2(pf), q * jnp.exp2(suf)], axis=1).astype(BF16)
        totf = pf_buf[pl.ds(base + CHUNK - 1, 1), :]
        kt = jnp.concatenate([kf * jnp.exp2(totf - pf), kb * jnp.exp2(pb - gbc)], axis=1).astype(BF16)
        u = lax.dot_general(v, kt, tn_dims, preferred_element_type=F32).astype(BF16)
        u_ref[0, hd, c] = u[:, :HG_DK]
        u_ref[1, hd, c] = u[:, HG_DK:]
        d_ref[0, hd, c * SUBLANES:(c + 1) * SUBLANES, :] = jnp.broadcast_to(jnp.exp2(totf), (SUBLANES, HG_DK))
        d_ref[1, hd, c * SUBLANES:(c + 1) * SUBLANES, :] = jnp.broadcast_to(jnp.exp2(totb), (SUBLANES, HG_DK))


def _gla_local(h, w_gu, w_v, z_a, lb4, rows_per_block, side=()):
    m = h.shape[0]
    nb = m // rows_per_block
    nchunk = rows_per_block // CHUNK
    n_chunks = m // CHUNK
    resident = lambda a: pl.BlockSpec(a.shape, lambda b: (0,) * a.ndim, pipeline_mode=pl.Buffered(1))
    col = lambda k: pl.BlockSpec((rows_per_block, HG_WIDTH), lambda b, k=k: (b, k))
    side_in, side_out, side_shapes = _side_cast_specs(side, nb, lambda b: b)
    return pl.pallas_call(
        functools.partial(_gla_local_kernel, n_side=len(side)),
        grid=(nb,),
        in_specs=[pl.BlockSpec((rows_per_block, D_MODEL), lambda b: (b, 0)),
                  resident(w_gu), resident(w_v), col(0), col(1), col(2), col(3), resident(lb4)] + side_in,
        out_specs=[pl.BlockSpec((rows_per_block, 3 * HG_WIDTH), lambda b: (b, 0)),
                   pl.BlockSpec((rows_per_block, HG_WIDTH), lambda b: (b, 0)),
                   pl.BlockSpec((rows_per_block, 2 * HG_WIDTH), lambda b: (b, 0)),
                   pl.BlockSpec((2, HG_HEADS, nchunk, HG_DK, HG_DK), lambda b: (0, 0, b, 0, 0)),
                   pl.BlockSpec((2, HG_HEADS, nchunk * SUBLANES, HG_DK), lambda b: (0, 0, b, 0))] + side_out,
        out_shape=[jax.ShapeDtypeStruct((m, 3 * HG_WIDTH), BF16),
                   jax.ShapeDtypeStruct((m, HG_WIDTH), BF16),
                   jax.ShapeDtypeStruct((m, 2 * HG_WIDTH), BF16),
                   jax.ShapeDtypeStruct((2, HG_HEADS, n_chunks, HG_DK, HG_DK), BF16),
                   jax.ShapeDtypeStruct((2, HG_HEADS, n_chunks * SUBLANES, HG_DK), F32)] + side_shapes,
        scratch_shapes=[pltpu.VMEM((rows_per_block, HG_DK), F32) for _ in range(12)],
        compiler_params=_params(("parallel",), 56),
        name="gla_local",
    )(h, w_gu, w_v, z_a, z_a, z_a, z_a, lb4, *[s[0] for s in side])


def _gla_finish_kernel(*refs, nchunk, n_chunks, n_side):
    oi_ref, qe_ref, gt_ref, gain_ref, u_hbm, d_ref, s0_ref = refs[:7]
    side_in = refs[7:7 + n_side]
    o_ref = refs[7 + n_side]
    side_out = refs[8 + n_side:8 + 2 * n_side]
    s_all, u_buf, u_sems = refs[-3:]
    _side_cast(side_in, side_out)
    head = pl.program_id(0)

    def u_copy(hd):
        slot = hd % 2
        return pltpu.make_async_copy(u_hbm.at[:, hd], u_buf.at[slot], u_sems.at[slot])

    @pl.when((pl.program_id(1) == 0) & (head == 0))
    def _():
        u_copy(0).start()

    @pl.when((pl.program_id(1) == 0) & (head + 1 < HG_HEADS))
    def _():
        u_copy(head + 1).start()

    @pl.when(pl.program_id(1) == 0)
    def _():
        u_copy(head).wait()
        slot = head % 2

        def step(c, carry):
            sf, sb = carry
            cb = n_chunks - 1 - c
            s_all[0, c] = sf.astype(BF16)
            s_all[1, cb] = sb.astype(BF16)
            df = d_ref[0, 0, pl.ds(pl.multiple_of(c * SUBLANES, SUBLANES), 1), :]
            db = d_ref[1, 0, pl.ds(pl.multiple_of(cb * SUBLANES, SUBLANES), 1), :]
            return (sf * df + u_buf[slot, 0, c].astype(F32), sb * db + u_buf[slot, 1, cb].astype(F32))

        lax.fori_loop(0, n_chunks, step, (s0_ref[0, 0], s0_ref[1, 0]), unroll=4)

    nt = (((1,), (1,)), ((), ()))
    gain = gain_ref[0]
    first = pl.program_id(1) * nchunk
    for c in range(nchunk):
        sl = slice(c * CHUNK, (c + 1) * CHUNK)
        scat = jnp.concatenate([s_all[0, first + c], s_all[1, first + c]], axis=1)
        o = oi_ref[sl, :].astype(F32) + lax.dot_general(qe_ref[sl, :], scat, nt, preferred_element_type=F32)
        o = _rms(o) * gain
        o_ref[sl, :] = (o * _silu(gt_ref[sl, :].astype(F32))).astype(BF16)


def _gla_finish(oi, qe, z_b, hg_gain, s_inc, s_dec, s0, rows_per_block, side=()):
    m = oi.shape[0]
    nb = m // rows_per_block
    nchunk = rows_per_block // CHUNK
    n_chunks = m // CHUNK
    side_in, side_out, side_shapes = _side_cast_specs(side, HG_HEADS * nb, lambda h, b: h * nb + b)
    return pl.pallas_call(
        functools.partial(_gla_finish_kernel, nchunk=nchunk, n_chunks=n_chunks, n_side=len(side)),
        grid=(HG_HEADS, nb),
        in_specs=[pl.BlockSpec((rows_per_block, HG_DK), lambda h, b: (b, h)),
                  pl.BlockSpec((rows_per_block, 2 * HG_DK), lambda h, b: (b, h)),
                  pl.BlockSpec((rows_per_block, HG_DK), lambda h, b: (b, h)),
                  pl.BlockSpec((1, 1, HG_DK), lambda h, b: (h, 0, 0)),
                  pl.BlockSpec(memory_space=pl.ANY),
                  pl.BlockSpec((2, 1, n_chunks * SUBLANES, HG_DK), lambda h, b: (0, h, 0, 0)),
                  pl.BlockSpec((2, 1, HG_DK, HG_DK), lambda h, b: (0, h, 0, 0))] + side_in,
        out_specs=[pl.BlockSpec((rows_per_block, HG_DK), lambda h, b: (b, h))] + side_out,
        out_shape=[jax.ShapeDtypeStruct((m, HG_WIDTH), BF16)] + side_shapes,
        scratch_shapes=[pltpu.VMEM((2, n_chunks, HG_DK, HG_DK), BF16),
                        pltpu.VMEM((2, 2, n_chunks, HG_DK, HG_DK), BF16),
                        pltpu.SemaphoreType.DMA((2,))],
        compiler_params=_params(("arbitrary", "arbitrary"), 48),
        name="gla_finish",
    )(oi, qe, z_b, hg_gain.reshape(HG_HEADS, 1, HG_DK), s_inc, s_dec, s0, *[s[0] for s in side])


def _residual_epilogue(mix, x_ref, vec_ref, xo_ref, ho_ref, rs):
    xn = x_ref[rs, :] + _rms(mix) * vec_ref[0:1]
    xo_ref[rs, :] = xn
    if ho_ref is not None:
        hn = (_rms(xn) * vec_ref[1:2] + vec_ref[2:3]).astype(BF16)
        outs = ho_ref if isinstance(ho_ref, (tuple, list)) else (ho_ref,)
        width = hn.shape[1] // len(outs)
        for k, out in enumerate(outs):
            out[rs, :] = hn[:, k * width:(k + 1) * width]


def _spatial_chunk(u_raw, v_raw, w_ref, bias_ref, lng_ref, lnb_ref):
    v = _gelu_tanh(v_raw.astype(F32))
    xc = v - jnp.mean(v, axis=-1, keepdims=True)
    y = (xc * lax.rsqrt(jnp.mean(xc * xc, axis=-1, keepdims=True) + EPS) * lng_ref[...] + lnb_ref[...]).astype(BF16)
    u = _gelu_tanh(u_raw.astype(F32))
    parts = []
    for g in range(SG_GROUPS):
        cs = slice(g * SG_CH, (g + 1) * SG_CH)
        mixed = jnp.dot(w_ref[g], y[:, cs], preferred_element_type=F32) + bias_ref[:, cs]
        parts.append((u[:, cs] * mixed).astype(BF16))
    return jnp.concatenate(parts, axis=1)


def _mm_res_kernel(*refs, n_a, n_sp, emit_h):
    a_refs = refs[:n_a]
    sp_refs = refs[n_a:n_a + 6 * n_sp]
    n_w = n_a + n_sp
    w_refs = refs[n_a + 6 * n_sp:n_a + 6 * n_sp + n_w]
    x_ref, vec_ref, xo_ref = refs[n_a + 6 * n_sp + n_w:n_a + 6 * n_sp + n_w + 3]
    ho_ref = refs[n_a + 6 * n_sp + n_w + 3] if emit_h else None
    for rs in _row_subblocks(x_ref.shape[0], SG_CHUNK):
        mix = jnp.dot(a_refs[0][rs, :], w_refs[0][...], preferred_element_type=F32)
        for a_ref, w_ref in zip(a_refs[1:], w_refs[1:]):
            mix = mix + jnp.dot(a_ref[rs, :], w_ref[...], preferred_element_type=F32)
        if n_sp:
            u_ref, v_ref = sp_refs[:2]
            s = _spatial_chunk(u_ref[rs, :], v_ref[rs, :], *sp_refs[2:])
            mix = mix + jnp.dot(s, w_refs[-1][...], preferred_element_type=F32)
        _residual_epilogue(mix, x_ref, vec_ref, xo_ref, ho_ref, rs)


def _mm_res(a_list, w, x, vec, tm, emit_h=True, spatial=None):
    m = x.shape[0]
    n_a = len(a_list)
    widths = [a.shape[1] for a in a_list] + ([SG_WIDTH] if spatial else [])
    offs = np.cumsum([0] + widths)
    sp_specs, sp_args = [], []
    if spatial:
        z_b = spatial[0]
        sp_specs = [pl.BlockSpec((tm, SG_WIDTH), lambda i: (i, 1)), pl.BlockSpec((tm, SG_WIDTH), lambda i: (i, 2)),
                    pl.BlockSpec((SG_GROUPS, SG_CHUNK, SG_CHUNK), lambda i: (0, 0, 0)),
                    pl.BlockSpec((SG_CHUNK, SG_WIDTH), lambda i: (0, 0)),
                    pl.BlockSpec((1, SG_WIDTH), lambda i: (0, 0)), pl.BlockSpec((1, SG_WIDTH), lambda i: (0, 0))]
        sp_args = [z_b, z_b, *spatial[1:]]
    in_specs = ([pl.BlockSpec((tm, a.shape[1]), lambda i: (i, 0)) for a in a_list] + sp_specs
                + [pl.BlockSpec((wd, D_MODEL), lambda i, k=int(o) // wd: (k, 0)) for wd, o in zip(widths, offs)]
                + [pl.BlockSpec((tm, D_MODEL), lambda i: (i, 0)),
                   pl.BlockSpec((8, D_MODEL), lambda i: (0, 0))])
    out_specs = [pl.BlockSpec((tm, D_MODEL), lambda i: (i, 0))]
    out_shape = [jax.ShapeDtypeStruct((m, D_MODEL), F32)]
    if emit_h:
        out_specs.append(pl.BlockSpec((tm, D_MODEL), lambda i: (i, 0)))
        out_shape.append(jax.ShapeDtypeStruct((m, D_MODEL), BF16))
    return pl.pallas_call(
        functools.partial(_mm_res_kernel, n_a=n_a, n_sp=1 if spatial else 0, emit_h=emit_h),
        grid=(m // tm,),
        in_specs=in_specs, out_specs=out_specs, out_shape=out_shape,
        compiler_params=_params(("parallel",), 56),
        name="mm_residual",
    )(*a_list, *sp_args, *[w] * len(widths), x, vec)


def _mlp_kernel(*refs, n_h, n_side, nj, n_steps, loop_ff):
    h_ref, w1_ref, w2_ref, x_ref, vec_ref = refs[:5]
    side_in = refs[5:5 + n_side]
    xo_ref = refs[5 + n_side]
    ho_ref = tuple(refs[6 + n_side:6 + n_side + n_h]) or None
    n_out = 1 + n_h
    side_out = refs[5 + n_side + n_out:5 + 2 * n_side + n_out]
    w1_buf, w2_buf, sems = refs[-3:]
    acc_ref = xo_ref
    _side_cast(side_in, side_out)
    last = nj - 1
    tf = w1_buf.shape[2]
    first_step = pl.program_id(0) * nj + (0 if loop_ff else pl.program_id(1))

    def tile_copies(s):
        slot = s % WEIGHT_SLOTS
        col = (s % nj) * tf
        if not isinstance(col, int):
            col = pl.multiple_of(col, tf)
        return (pltpu.make_async_copy(w1_ref.at[:, pl.ds(col, tf)], w1_buf.at[slot], sems.at[0, slot]),
                pltpu.make_async_copy(w2_ref.at[pl.ds(col, tf), :], w2_buf.at[slot], sems.at[1, slot]))

    @pl.when(first_step == 0)
    def _():
        for s in range(WEIGHT_SLOTS - 1):
            for cp in tile_copies(s):
                cp.start()

    def ff_step(jj, mode):
        step = pl.program_id(0) * nj + jj

        @pl.when(step + (WEIGHT_SLOTS - 1) < n_steps)
        def _():
            for cp in tile_copies(step + (WEIGHT_SLOTS - 1)):
                cp.start()

        for cp in tile_copies(step):
            cp.wait()
        slot = step % WEIGHT_SLOTS
        for rs in _row_subblocks(h_ref.shape[0]):
            a = jnp.dot(h_ref[rs, :], w1_buf[slot], preferred_element_type=F32)
            a = jnp.square(jnp.maximum(a, 0.0)).astype(BF16)
            p = jnp.dot(a, w2_buf[slot], preferred_element_type=F32)
            if mode == "first":
                acc_ref[rs, :] = p
            elif mode == "last":
                _residual_epilogue(acc_ref[rs, :] + p, x_ref, vec_ref, xo_ref, ho_ref, rs)
            else:
                acc_ref[rs, :] += p

    if loop_ff:
        ff_step(0, "first")

        def middle(jj, carry):
            ff_step(jj, "middle")
            return carry

        lax.fori_loop(1, last, middle, 0)
        ff_step(last, "last")
    else:
        j = pl.program_id(1)
        pl.when(j == 0)(lambda: ff_step(j, "first"))
        pl.when(jnp.logical_and(j > 0, j < last))(lambda: ff_step(j, "middle"))
        pl.when(j == last)(lambda: ff_step(j, "last"))


def _mlp(h, w1, w2, x, vec, tm, tf, n_h, side=()):
    m = x.shape[0]
    nj = D_FF // tf
    loop_ff = not side
    row = (lambda i: (i, 0)) if loop_ff else (lambda i, j: (i, 0))
    const = (lambda i: (0, 0)) if loop_ff else (lambda i, j: (0, 0))
    out_specs = [pl.BlockSpec((tm, D_MODEL), row)]
    out_shape = [jax.ShapeDtypeStruct((m, D_MODEL), F32)]
    for _ in range(n_h):
        out_specs.append(pl.BlockSpec((tm, D_MODEL // n_h), row))
        out_shape.append(jax.ShapeDtypeStruct((m, D_MODEL // n_h), BF16))
    side_in, side_out, side_shapes = _side_cast_specs(side, (m // tm) * nj, lambda i, j: i * nj + j)
    return pl.pallas_call(
        functools.partial(_mlp_kernel, n_h=n_h, n_side=len(side), nj=nj, n_steps=(m // tm) * nj, loop_ff=loop_ff),
        grid=(m // tm,) if loop_ff else (m // tm, nj),
        in_specs=[pl.BlockSpec((tm, D_MODEL), row),
                  pl.BlockSpec(memory_space=pl.ANY),
                  pl.BlockSpec(memory_space=pl.ANY),
                  pl.BlockSpec((tm, D_MODEL), row),
                  pl.BlockSpec((8, D_MODEL), const)] + side_in,
        out_specs=out_specs + side_out, out_shape=out_shape + side_shapes,
        scratch_shapes=[pltpu.VMEM((WEIGHT_SLOTS, D_MODEL, tf), BF16),
                        pltpu.VMEM((WEIGHT_SLOTS, tf, D_MODEL), BF16),
                        pltpu.SemaphoreType.DMA((2, WEIGHT_SLOTS))],
        compiler_params=_params(("arbitrary",) if loop_ff else ("arbitrary", "arbitrary"), 60),
        name="mlp",
    )(h, w1, w2, x, vec, *[s[0] for s in side])


def _dft_constants():
    def cs(n):
        k = np.arange(n)
        ang = 2.0 * np.pi * ((k[:, None] * k[None, :]) % n) / n
        return np.cos(ang) / np.sqrt(n), np.sin(ang) / np.sqrt(n)

    cc, sc = cs(FT_CH)
    c1, s1 = cs(FFT_L1)
    m1 = np.concatenate([c1, -s1], axis=0)
    c2, s2 = cs(FFT_L2)
    m2 = np.block([[c2, s2], [-s2, c2]])
    u1 = np.arange(FFT_L1)[:, None]
    t2 = np.arange(FFT_L2)[None, :]
    ang = 2.0 * np.pi * ((u1 * t2) % SEQ) / SEQ
    tw_c = np.cos(ang).reshape(SEQ, 1)
    tw_s = np.sin(ang).reshape(SEQ, 1)
    f = lambda a: jnp.asarray(a, F32)
    return f(cc), f(sc), f(m1), f(m2), f(tw_c), f(tw_s)


def _swap_row_factors(a, n_outer, n_inner):
    return a.reshape(n_outer, n_inner, a.shape[1]).transpose(1, 0, 2).reshape(a.shape)


def _pos_dft1_kernel(x_ref, m_ref, zr_ref, zi_ref):
    for k in range(x_ref.shape[0] // FFT_L1):
        sl = slice(k * FFT_L1, (k + 1) * FFT_L1)
        z = jnp.dot(m_ref[...], x_ref[sl, :], preferred_element_type=F32)
        zr_ref[sl, :] = z[:FFT_L1].astype(BF16)
        zi_ref[sl, :] = z[FFT_L1:].astype(BF16)


def _pos_dft1(xt, m1, slabs):
    rows = slabs * FFT_L1
    blk = pl.BlockSpec((rows, xt.shape[1]), lambda j: (j, 0))
    return pl.pallas_call(
        _pos_dft1_kernel,
        grid=(SEQ // rows,),
        in_specs=[blk, pl.BlockSpec((2 * FFT_L1, FFT_L1), lambda j: (0, 0))],
        out_specs=[blk, blk],
        out_shape=[jax.ShapeDtypeStruct(xt.shape, BF16)] * 2,
        compiler_params=_params(("parallel",), 32),
        name="pos_dft1",
    )(xt, m1)


def _pos_dft2_kernel(zr_ref, zi_ref, c_ref, s_ref, m_ref, wc_ref, ws_ref, y_ref, yr_buf, yi_buf):
    for k in range(zr_ref.shape[0] // FFT_L2):
        sl = slice(k * FFT_L2, (k + 1) * FFT_L2)
        zr = zr_ref[sl, :].astype(F32)
        zi = zi_ref[sl, :].astype(F32)
        c = c_ref[sl, :]
        s = s_ref[sl, :]
        t = jnp.concatenate([(zr * c + zi * s).astype(BF16), (zi * c - zr * s).astype(BF16)], axis=0)
        y = jnp.dot(m_ref[...], t, preferred_element_type=F32)
        yr_buf[sl, :] = y[:FFT_L2].astype(BF16)
        yi_buf[sl, :] = y[FFT_L2:].astype(BF16)
    for g in range(zr_ref.shape[1] // FT_CH):
        cs = slice(g * FT_CH, (g + 1) * FT_CH)
        y_ref[:, cs] = (jnp.dot(yr_buf[:, cs], wc_ref[...], preferred_element_type=F32)
                        + jnp.dot(yi_buf[:, cs], ws_ref[...], preferred_element_type=F32)).astype(BF16)


def _pos_dft2(zr, zi, tw_c, tw_s, m2, wc, ws, slabs):
    rows = slabs * FFT_L2
    width = zr.shape[1]
    blk = pl.BlockSpec((rows, width), lambda j: (j, 0))
    tw = pl.BlockSpec((rows, 1), lambda j: (j, 0))
    const = lambda a: pl.BlockSpec(a.shape, lambda j: (0, 0))
    return pl.pallas_call(
        _pos_dft2_kernel,
        grid=(SEQ // rows,),
        in_specs=[blk, blk, tw, tw, const(m2), const(wc), const(ws)],
        out_specs=blk,
        out_shape=jax.ShapeDtypeStruct(zr.shape, BF16),
        scratch_shapes=[pltpu.VMEM((rows, width), BF16), pltpu.VMEM((rows, width), BF16)],
        compiler_params=_params(("parallel",), 40),
        name="pos_dft2",
    )(zr, zi, tw_c, tw_s, m2, wc, ws)


def _fourier_mix(h_parts):
    cc, sc, m1, m2, tw_c, tw_s = _dft_constants()
    m1, m2, cc, sc = m1.astype(BF16), m2.astype(BF16), cc.astype(BF16), sc.astype(BF16)
    hts = [_swap_row_factors(h, FFT_L1, FFT_L2) for h in h_parts]
    zs = [_pos_dft1(ht, m1, DFT_ROWS // FFT_L1) for ht in hts]
    zs = [[_swap_row_factors(z, FFT_L2, FFT_L1) for z in zri] for zri in zs]
    yps = [_pos_dft2(zr, zi, tw_c, tw_s, m2, cc, sc, DFT_ROWS // FFT_L2) for zr, zi in zs]
    return [_swap_row_factors(yp, FFT_L1, FFT_L2) for yp in yps]


def _pack_rows(*rows):
    rows = [r.reshape(1, D_MODEL).astype(F32) for r in rows]
    rows += [jnp.zeros((1, D_MODEL), F32)] * (8 - len(rows))
    return jnp.concatenate(rows, axis=0)


def kernel(x, c, ctx, c_ctx, w_ada, b_ada, norm_gain, w_in, w_out, lb_raw, hg_norm_gain,
           sg_w, sg_b, sg_ln_gain, sg_ln_bias, w_fourier, w_mlp_in, w_mlp_out):
    assert x.shape == (1, SEQ, D_MODEL) and ctx.shape == (1, CTX_LEN, D_MODEL)
    xs = x.reshape(SEQ, D_MODEL)
    cs = ctx.reshape(CTX_LEN, D_MODEL)

    mods = _ada_mods(jnp.concatenate([c.reshape(1, D_MODEL), c_ctx.reshape(1, D_MODEL)], axis=0), w_ada, b_ada)
    mod = lambda l, r, j: mods[l, r, j * D_MODEL:(j + 1) * D_MODEL]

    lb = jnp.cumsum(jax.nn.softmax(lb_raw.astype(F32), axis=1), axis=1)[:, 0]
    lb4 = lb.reshape(2, HG_HEADS, 1, HG_DK)

    pre = lambda gain, l, r, j: _pack_rows(gain * (1.0 + mod(l, r, j + 1)), mod(l, r, j))
    post = lambda gain, l, j, *nxt: _pack_rows(mod(l, 0, j) * gain, *nxt)

    z_a, h0, w_gu, w_v = _norm_mm_rows(xs, pre(norm_gain[0, 0], 0, 0, 0), w_in, 4 * HG_WIDTH, MM_ROWS, MM_COLS,
                                       side=[(w_in, 0, 2 * HG_WIDTH, 2), (w_in, 0, HG_WIDTH, 6)])
    z_ctx = _norm_mm(cs, pre(norm_gain[0, 0], 0, 1, 0), w_in, 3 * HG_WIDTH, CTX_LEN, MM_COLS)
    s_ctx = _ctx_states(z_ctx, lb4)
    z_b, o_intra, q_dec, s_inc, s_dec, w_out0, w1_0, w2_0 = _gla_local(
        h0, w_gu, w_v, z_a, lb4, GLA_LOCAL_ROWS, side=[(w_out, 0), (w_mlp_in, 0), (w_mlp_out, 0)])
    o, w_fou = _gla_finish(o_intra, q_dec, z_b, hg_norm_gain[0], s_inc, s_dec, s_ctx, GLA_FINISH_ROWS,
                           side=[(w_fourier, 0)])
    bias_full = jnp.repeat(sg_b[0].T.astype(F32), SG_CH, axis=1)
    vec = post(norm_gain[0, 1], 0, 2, *pre(norm_gain[0, 2], 0, 0, 3)[:2])
    x1, h = _mm_res([o], w_out0, xs, vec, MM_ROWS,
                    spatial=(z_b, sg_w[0].astype(BF16), bias_full, sg_ln_gain[0].reshape(1, SG_WIDTH),
                             sg_ln_bias[0].reshape(1, SG_WIDTH)))
    vec = post(norm_gain[0, 3], 0, 5, *pre(norm_gain[1, 0], 1, 0, 0)[:2])
    x2, h_a, h_b, w1_1, w2_1 = _mlp(h, w1_0, w2_0, x1, vec, MM_ROWS, MLP_FF_COLS, 2,
                                    side=[(w_mlp_in, 1), (w_mlp_out, 1)])

    y_parts = _fourier_mix([h_a, h_b])
    vec = post(norm_gain[1, 1], 1, 2, *pre(norm_gain[1, 2], 1, 0, 3)[:2])
    x3, h = _mm_res(y_parts, w_fou, x2, vec, MM_ROWS)
    (x4,) = _mlp(h, w1_1, w2_1, x3, post(norm_gain[1, 3], 1, 5), MM_ROWS, MLP_FF_COLS, 0)
    return x4.reshape(1, SEQ, D_MODEL)
```

```python
import functools

import numpy as np
import jax
import jax.numpy as jnp
from jax import lax
from jax.experimental import pallas as pl
from jax.experimental.pallas import tpu as pltpu

D_MODEL = 2048
SEQ = 8192
DEPTH = 2
CTX_LEN = 256
CHUNK = 64
SUB = 16
ADA_COLS = 2048
MM_ROWS = 512
MM_COLS = 1024
MLP_FF_COLS = 1024
WEIGHT_SLOTS = 3
SUBROWS = 256
GLA_LOCAL_ROWS = 256
GLA_FINISH_ROWS = 8192
DFT_ROWS = 1024
HG_HEADS = 8
HG_DK = 128
HG_WIDTH = HG_HEADS * HG_DK
SG_GROUPS = 8
SG_CH = 128
SG_WIDTH = SG_GROUPS * SG_CH
SG_CHUNK = 128
FT_GROUPS = 4
FT_CH = D_MODEL // FT_GROUPS
FFT_L1 = 64
FFT_L2 = 128
IN_WIDTH = 5 * HG_WIDTH + 2 * SG_WIDTH
D_FF = 4 * D_MODEL
N_MOD = 6
EPS = 1e-6
EXP_CLAMP = 115.0
LOG2E = 1.0 / float(np.log(2.0))
LANES = 128
SUBLANES = 8
MXU_COLS = 256
PROJ_ROWS = 256

F32 = jnp.float32
BF16 = jnp.bfloat16
MIB = 1024 * 1024


def _params(semantics, vmem_mib):
    return pltpu.CompilerParams(dimension_semantics=semantics, vmem_limit_bytes=vmem_mib * MIB)


def _rms(x):
    return x * lax.rsqrt(jnp.mean(x * x, axis=-1, keepdims=True) + EPS)


def _silu(x):
    return x * jax.nn.sigmoid(x)


def _row_subblocks(rows, step=SUBROWS):
    step = min(rows, step)
    return [slice(s, s + step) for s in range(0, rows, step)]


def _gelu_tanh(x):
    cdf = 0.5 * (1.0 + jnp.tanh(float(np.sqrt(2.0 / np.pi)) * (x + 0.044715 * (x * x * x))))
    return x * cdf


def _ada_kernel(c_ref, w_ref, b_ref, o_ref, s_buf):
    @pl.when((pl.program_id(0) == 0) & (pl.program_id(1) == 0))
    def _():
        s_buf[...] = _silu(c_ref[...])

    tn = o_ref.shape[-1]
    for r in range(2):
        s = s_buf[r]
        cols = [jnp.sum(w_ref[0, :, j * LANES:(j + 1) * LANES] * s, axis=0, keepdims=True)
                for j in range(tn // LANES)]
        o_ref[0, r:r + 1, :] = jnp.concatenate(cols, axis=1) + b_ref[0]


def _ada_mods(c2, w_ada, b_ada):
    tn = ADA_COLS
    n = N_MOD * D_MODEL
    cb = jnp.broadcast_to(c2[:, :, None], (2, D_MODEL, LANES))
    return pl.pallas_call(
        _ada_kernel,
        grid=(DEPTH, n // tn),
        in_specs=[pl.BlockSpec((2, D_MODEL, LANES), lambda l, j: (0, 0, 0)),
                  pl.BlockSpec((1, D_MODEL, tn), lambda l, j: (l, 0, j)),
                  pl.BlockSpec((1, 1, tn), lambda l, j: (l, 0, j))],
        out_specs=pl.BlockSpec((1, 2, tn), lambda l, j: (l, 0, j)),
        out_shape=jax.ShapeDtypeStruct((DEPTH, 2, n), F32),
        scratch_shapes=[pltpu.VMEM((2, D_MODEL, LANES), F32)],
        compiler_params=_params(("arbitrary", "arbitrary"), 40),
        name="ada_mods",
    )(cb, w_ada, b_ada.reshape(DEPTH, 1, n))


def _side_cast_specs(weights, n_steps, step_of):
    in_specs, out_specs, out_shapes = [], [], []
    for w, layer, *cols in weights:
        width, cidx = cols if cols else (w.shape[2], 0)
        assert w.shape[1] % n_steps == 0, "every grid step must get an equal row slab"
        rows = w.shape[1] // n_steps
        in_specs.append(pl.BlockSpec((1, rows, width),
                                     lambda *g, layer=layer, cidx=cidx: (layer, step_of(*g), cidx)))
        out_specs.append(pl.BlockSpec((rows, width), lambda *g: (step_of(*g), 0)))
        out_shapes.append(jax.ShapeDtypeStruct((w.shape[1], width), BF16))
    return in_specs, out_specs, out_shapes


def _side_cast(side_in, side_out):
    for wi, wo in zip(side_in, side_out):
        wo[...] = wi[0].astype(BF16)


def _norm_mm_kernel(x_ref, vec_ref, w_ref, o_ref, h_ref):
    w = w_ref[0].astype(BF16)

    @pl.when(pl.program_id(1) == 0)
    def _():
        h_ref[...] = (_rms(x_ref[...]) * vec_ref[0:1] + vec_ref[1:2]).astype(BF16)

    o_ref[...] = jnp.dot(h_ref[...], w, preferred_element_type=F32).astype(o_ref.dtype)


def _norm_mm(x, vec, w, n_cols, tm, tn):
    m = x.shape[0]
    return pl.pallas_call(
        _norm_mm_kernel,
        grid=(m // tm, n_cols // tn),
        in_specs=[pl.BlockSpec((tm, D_MODEL), lambda i, j: (i, 0)),
                  pl.BlockSpec((8, D_MODEL), lambda i, j: (0, 0)),
                  pl.BlockSpec((1, D_MODEL, tn), lambda i, j: (0, 0, j))],
        out_specs=pl.BlockSpec((tm, tn), lambda i, j: (i, j)),
        out_shape=jax.ShapeDtypeStruct((m, n_cols), BF16),
        scratch_shapes=[pltpu.VMEM((tm, D_MODEL), BF16)],
        compiler_params=_params(("parallel", "arbitrary"), 40),
        name="norm_mm",
    )(x, vec, w)


def _norm_mm_rows_kernel(*refs, n_side, nj):
    x_ref, vec_ref, w_hbm = refs[:3]
    side_in = refs[3:3 + n_side]
    o_ref, h_ref = refs[3 + n_side:5 + n_side]
    side_out = refs[5 + n_side:5 + 2 * n_side]
    wb_ref, stage, sems = refs[5 + 2 * n_side:]
    _side_cast(side_in, side_out)
    tn = wb_ref.shape[2]

    def tile_copy(t):
        return pltpu.make_async_copy(w_hbm.at[0, :, pl.ds(t * tn, tn)], stage.at[t % 2], sems.at[t % 2])

    def row_block(build_weights):
        if build_weights:
            tile_copy(0).start()
        for t in range(nj):
            if build_weights:
                if t + 1 < nj:
                    tile_copy(t + 1).start()
                tile_copy(t).wait()
                wb_ref[t] = stage[t % 2].astype(BF16)
            w = wb_ref[t]
            cols = slice(t * tn, (t + 1) * tn)
            if t == 0:
                for rs in _row_subblocks(x_ref.shape[0]):
                    h = (_rms(x_ref[rs, :]) * vec_ref[0:1] + vec_ref[1:2]).astype(BF16)
                    h_ref[rs, :] = h
                    o_ref[rs, cols] = jnp.dot(h, w, preferred_element_type=F32).astype(o_ref.dtype)
            else:
                o_ref[:, cols] = jnp.dot(h_ref[...], w, preferred_element_type=F32).astype(o_ref.dtype)

    pl.when(pl.program_id(0) == 0)(lambda: row_block(True))
    pl.when(pl.program_id(0) > 0)(lambda: row_block(False))


def _norm_mm_rows(x, vec, w, n_cols, tm, tn, side=()):
    m = x.shape[0]
    nj = n_cols // tn
    side_in, side_out, side_shapes = _side_cast_specs(side, m // tm, lambda i: i)
    return pl.pallas_call(
        functools.partial(_norm_mm_rows_kernel, n_side=len(side), nj=nj),
        grid=(m // tm,),
        in_specs=[pl.BlockSpec((tm, D_MODEL), lambda i: (i, 0)),
                  pl.BlockSpec((8, D_MODEL), lambda i: (0, 0)),
                  pl.BlockSpec(memory_space=pl.ANY)] + side_in,
        out_specs=[pl.BlockSpec((tm, n_cols), lambda i: (i, 0)),
                   pl.BlockSpec((tm, D_MODEL), lambda i: (i, 0))] + side_out,
        out_shape=[jax.ShapeDtypeStruct((m, n_cols), BF16),
                   jax.ShapeDtypeStruct((m, D_MODEL), BF16)] + side_shapes,
        scratch_shapes=[pltpu.VMEM((nj, D_MODEL, tn), BF16),
                        pltpu.VMEM((2, D_MODEL, tn), F32),
                        pltpu.SemaphoreType.DMA((2,))],
        compiler_params=_params(("arbitrary",), 62),
        name="norm_mm_rows",
    )(x, vec, w, *[s[0] for s in side])


def _seg_cumsum(g, row_in_chunk):
    p = g
    s = 1
    while s < CHUNK:
        p = p + jnp.where(row_in_chunk >= s, pltpu.roll(p, s, axis=0), 0.0)
        s *= 2
    return p


def _ctx_state_kernel(f_ref, i_ref, lb_ref, sfin_ref, p_buf, g_buf, k_buf, *, nchunk):
    d = pl.program_id(0)
    rows = f_ref.shape[0]
    lb = lb_ref[0, 0]
    f = lb + (1.0 - lb) * jax.nn.sigmoid(f_ref[...].astype(F32))
    g = jnp.log(f) * LOG2E
    ric = lax.broadcasted_iota(jnp.int32, (rows, LANES), 0) & (CHUNK - 1)
    p_buf[...] = _seg_cumsum(g, ric)
    g_buf[...] = g
    k_buf[...] = 1.0 - f

    s = jnp.zeros((HG_DK, HG_DK), F32)
    for j in range(nchunk):
        c = jnp.where(d == 0, j, nchunk - 1 - j)
        base = pl.multiple_of(c * CHUNK, CHUNK)
        sl = pl.ds(base, CHUNK)
        p = p_buf[sl, :]
        tot = p_buf[pl.ds(base + CHUNK - 1, 1), :]
        ex = jnp.where(d == 0, tot - p, p - g_buf[sl, :])
        kt = (k_buf[sl, :] * jnp.exp2(ex)).astype(BF16)
        u = lax.dot_general(i_ref[sl, :], kt, (((0,), (0,)), ((), ())), preferred_element_type=F32)
        s = s * jnp.exp2(tot) + u
    sfin_ref[0, 0] = s


def _ctx_states(z_ctx, lb4):
    rows = z_ctx.shape[0]
    return pl.pallas_call(
        functools.partial(_ctx_state_kernel, nchunk=rows // CHUNK),
        grid=(2, HG_HEADS),
        in_specs=[pl.BlockSpec((rows, HG_DK), lambda d, h: (0, d * HG_HEADS + h)),
                  pl.BlockSpec((rows, HG_DK), lambda d, h: (0, 2 * HG_HEADS + h)),
                  pl.BlockSpec((1, 1, 1, HG_DK), lambda d, h: (d, h, 0, 0))],
        out_specs=pl.BlockSpec((1, 1, HG_DK, HG_DK), lambda d, h: (d, h, 0, 0)),
        out_shape=jax.ShapeDtypeStruct((2, HG_HEADS, HG_DK, HG_DK), F32),
        scratch_shapes=[pltpu.VMEM((rows, HG_DK), F32) for _ in range(3)],
        compiler_params=_params(("parallel", "parallel"), 32),
        name="ctx_states",
    )(z_ctx, z_ctx, lb4)


def _gla_local_kernel(*refs, n_side):
    h_ref, wgu_ref, wv_ref, ff_ref, fb_ref, i_ref, q_ref, lb_ref = refs[:8]
    side_in = refs[8:8 + n_side]
    zb_ref, oi_ref, qe_ref, u_ref, d_ref = refs[8 + n_side:13 + n_side]
    side_out = refs[13 + n_side:13 + 2 * n_side]
    bufs = refs[13 + 2 * n_side:]
    _side_cast(side_in, side_out)
    rows = h_ref.shape[0]
    nchunk = rows // CHUNK
    proj_rows = min(rows, PROJ_ROWS)
    row_parts = rows // proj_rows

    def project(piece):
        lo = (piece // row_parts) * MXU_COLS
        rs = slice((piece % row_parts) * proj_rows, (piece % row_parts + 1) * proj_rows)
        w_ref, off = (wgu_ref, lo) if lo < 2 * HG_WIDTH else (wv_ref, lo - 2 * HG_WIDTH)
        zb_ref[rs, lo:lo + MXU_COLS] = jnp.dot(h_ref[rs, :], w_ref[:, off:off + MXU_COLS],
                                               preferred_element_type=F32).astype(BF16)

    n_pieces = row_parts * 3 * HG_WIDTH // MXU_COLS
    pieces_before = [(k + 1) * n_pieces // (HG_HEADS * nchunk) - k * n_pieces // (HG_HEADS * nchunk)
                     for k in range(HG_HEADS * nchunk)]

    ric = lax.broadcasted_iota(jnp.int32, (rows, LANES), 0) & (CHUNK - 1)
    blk = lax.broadcasted_iota(jnp.int32, (CHUNK, LANES), 0) // SUB
    r2 = lax.broadcasted_iota(jnp.int32, (CHUNK, CHUNK), 0)
    c2 = lax.broadcasted_iota(jnp.int32, (CHUNK, CHUNK), 1)
    nt = (((1,), (1,)), ((), ()))
    tn_dims = (((0,), (0,)), ((), ()))
    nsub = CHUNK // SUB
    emitted = [0]

    def before_chunk(k):
        for _ in range(pieces_before[k]):
            project(emitted[0])
            emitted[0] += 1

    for hd in range(HG_HEADS):
        _gla_local_head(hd, nchunk, ric, blk, r2, c2, nt, tn_dims, nsub, ff_ref, fb_ref, i_ref, q_ref, lb_ref,
                        oi_ref, qe_ref, u_ref, d_ref, bufs[6 * (hd % 2):6 * (hd % 2) + 6], before_chunk)
    assert emitted[0] == n_pieces


def _gla_local_head(hd, nchunk, ric, blk, r2, c2, nt, tn_dims, nsub, ff_ref, fb_ref, i_ref, q_ref, lb_ref,
                    oi_ref, qe_ref, u_ref, d_ref, bufs, before_chunk):
    pf_buf, pb_buf, gb_buf, kf_buf, kb_buf, q_buf = bufs
    cs = slice(hd * HG_DK, (hd + 1) * HG_DK)
    lbf = lb_ref[0, hd]
    lbb = lb_ref[1, hd]
    ff = lbf + (1.0 - lbf) * jax.nn.sigmoid(ff_ref[:, cs].astype(F32))
    pf_buf[...] = _seg_cumsum(jnp.log(ff) * LOG2E, ric)
    kf_buf[...] = 1.0 - ff
    fb = lbb + (1.0 - lbb) * jax.nn.sigmoid(fb_ref[:, cs].astype(F32))
    gb = jnp.log(fb) * LOG2E
    pb_buf[...] = _seg_cumsum(gb, ric)
    gb_buf[...] = gb
    kb_buf[...] = 1.0 - fb
    q_buf[...] = _silu(q_ref[:, cs].astype(F32)) * (HG_DK ** -0.5)

    for c in range(nchunk):
        before_chunk(hd * nchunk + c)
        base = c * CHUNK
        sl = pl.ds(base, CHUNK)
        pf = pf_buf[sl, :]
        pb = pb_buf[sl, :]
        gbc = gb_buf[sl, :]
        kf = kf_buf[sl, :]
        kb = kb_buf[sl, :]
        q = q_buf[sl, :]
        totb = pb_buf[pl.ds(base + CHUNK - 1, 1), :]
        suf = totb - pb + gbc
        mids, nids = [], []
        for i in range(nsub):
            mids.append(pf_buf[pl.ds(base + SUB * i + SUB // 2 - 1, 1), :])
            rmid = pl.ds(base + SUB * i + SUB // 2, 1)
            nids.append(totb - pb_buf[rmid, :] + gb_buf[rmid, :])
        mid_rows = jnp.concatenate([jnp.broadcast_to(m, (SUB, LANES)) for m in mids], axis=0)
        nid_rows = jnp.concatenate([jnp.broadcast_to(m, (SUB, LANES)) for m in nids], axis=0)
        qf_all = q * jnp.exp2(jnp.minimum(pf - mid_rows, EXP_CLAMP))
        qb_all = q * jnp.exp2(jnp.minimum(suf - nid_rows, EXP_CLAMP))
        qf_seg, kf_seg, qb_seg, kb_seg = [], [], [], []
        for i in range(nsub):
            qf_seg.append(jnp.where(blk == i, qf_all, 0.0))
            qb_seg.append(jnp.where(blk == i, qb_all, 0.0))
            hi = SUB * (i + 1)
            top = kf[:hi] * jnp.exp2(jnp.minimum(mids[i] - pf[:hi], EXP_CLAMP))
            kf_seg.append(top if hi == CHUNK else
                          jnp.concatenate([top, jnp.zeros((CHUNK - hi, LANES), F32)], axis=0))
            lo = SUB * i
            bot = kb[lo:] * jnp.exp2(jnp.minimum(nids[i] - suf[lo:], EXP_CLAMP))
            kb_seg.append(bot if lo == 0 else
                          jnp.concatenate([jnp.zeros((lo, LANES), F32), bot], axis=0))
        qf_big = jnp.concatenate(qf_seg, axis=1).astype(BF16)
        kf_big = jnp.concatenate(kf_seg, axis=1).astype(BF16)
        qb_big = jnp.concatenate(qb_seg, axis=1).astype(BF16)
        kb_big = jnp.concatenate(kb_seg, axis=1).astype(BF16)
        sc_f = lax.dot_general(qf_big, kf_big, nt, preferred_element_type=F32)
        sc_b = lax.dot_general(qb_big, kb_big, nt, preferred_element_type=F32)
        a = (jnp.where(c2 <= r2, sc_f, 0.0) + jnp.where(c2 >= r2, sc_b, 0.0)).astype(BF16)
        v = i_ref[sl, cs]
        oi_ref[sl, cs] = jnp.dot(a, v, preferred_element_type=F32).astype(BF16)
        qe_ref[sl, 2 * hd * HG_DK:2 * (hd + 1) * HG_DK] = jnp.concatenate(
            [q * jnp.exp2(pf), q * jnp.exp2(suf)], axis=1).astype(BF16)
        totf = pf_buf[pl.ds(base + CHUNK - 1, 1), :]
        kt = jnp.concatenate([kf * jnp.exp2(totf - pf), kb * jnp.exp2(pb - gbc)], axis=1).astype(BF16)
        u = lax.dot_general(v, kt, tn_dims, preferred_element_type=F32).astype(BF16)
        u_ref[0, hd, c] = u[:, :HG_DK]
        u_ref[1, hd, c] = u[:, HG_DK:]
        d_ref[0, hd, c * SUBLANES:(c + 1) * SUBLANES, :] = jnp.broadcast_to(jnp.exp2(totf), (SUBLANES, HG_DK))
        d_ref[1, hd, c * SUBLANES:(c + 1) * SUBLANES, :] = jnp.broadcast_to(jnp.exp2(totb), (SUBLANES, HG_DK))


def _gla_local(h, w_gu, w_v, z_a, lb4, rows_per_block, side=()):
    m = h.shape[0]
    nb = m // rows_per_block
    nchunk = rows_per_block // CHUNK
    n_chunks = m // CHUNK
    resident = lambda a: pl.BlockSpec(a.shape, lambda b: (0,) * a.ndim, pipeline_mode=pl.Buffered(1))
    col = lambda k: pl.BlockSpec((rows_per_block, HG_WIDTH), lambda b, k=k: (b, k))
    side_in, side_out, side_shapes = _side_cast_specs(side, nb, lambda b: b)
    return pl.pallas_call(
        functools.partial(_gla_local_kernel, n_side=len(side)),
        grid=(nb,),
        in_specs=[pl.BlockSpec((rows_per_block, D_MODEL), lambda b: (b, 0)),
                  resident(w_gu), resident(w_v), col(0), col(1), col(2), col(3), resident(lb4)] + side_in,
        out_specs=[pl.BlockSpec((rows_per_block, 3 * HG_WIDTH), lambda b: (b, 0)),
                   pl.BlockSpec((rows_per_block, HG_WIDTH), lambda b: (b, 0)),
                   pl.BlockSpec((rows_per_block, 2 * HG_WIDTH), lambda b: (b, 0)),
                   pl.BlockSpec((2, HG_HEADS, nchunk, HG_DK, HG_DK), lambda b: (0, 0, b, 0, 0)),
                   pl.BlockSpec((2, HG_HEADS, nchunk * SUBLANES, HG_DK), lambda b: (0, 0, b, 0))] + side_out,
        out_shape=[jax.ShapeDtypeStruct((m, 3 * HG_WIDTH), BF16),
                   jax.ShapeDtypeStruct((m, HG_WIDTH), BF16),
                   jax.ShapeDtypeStruct((m, 2 * HG_WIDTH), BF16),
                   jax.ShapeDtypeStruct((2, HG_HEADS, n_chunks, HG_DK, HG_DK), BF16),
                   jax.ShapeDtypeStruct((2, HG_HEADS, n_chunks * SUBLANES, HG_DK), F32)] + side_shapes,
        scratch_shapes=[pltpu.VMEM((rows_per_block, HG_DK), F32) for _ in range(12)],
        compiler_params=_params(("parallel",), 56),
        name="gla_local",
    )(h, w_gu, w_v, z_a, z_a, z_a, z_a, lb4, *[s[0] for s in side])


def _gla_finish_kernel(*refs, nchunk, n_chunks, n_side):
    oi_ref, qe_ref, gt_ref, gain_ref, u_hbm, d_ref, s0_ref = refs[:7]
    side_in = refs[7:7 + n_side]
    o_ref = refs[7 + n_side]
    side_out = refs[8 + n_side:8 + 2 * n_side]
    s_all, u_buf, u_sems = refs[-3:]
    _side_cast(side_in, side_out)
    head = pl.program_id(0)

    def u_copy(hd):
        slot = hd % 2
        return pltpu.make_async_copy(u_hbm.at[:, hd], u_buf.at[slot], u_sems.at[slot])

    @pl.when((pl.program_id(1) == 0) & (head == 0))
    def _():
        u_copy(0).start()

    @pl.when((pl.program_id(1) == 0) & (head + 1 < HG_HEADS))
    def _():
        u_copy(head + 1).start()

    @pl.when(pl.program_id(1) == 0)
    def _():
        u_copy(head).wait()
        slot = head % 2

        def step(c, carry):
            sf, sb = carry
            cb = n_chunks - 1 - c
            s_all[0, c] = sf.astype(BF16)
            s_all[1, cb] = sb.astype(BF16)
            df = d_ref[0, 0, pl.ds(pl.multiple_of(c * SUBLANES, SUBLANES), 1), :]
            db = d_ref[1, 0, pl.ds(pl.multiple_of(cb * SUBLANES, SUBLANES), 1), :]
            return (sf * df + u_buf[slot, 0, c].astype(F32), sb * db + u_buf[slot, 1, cb].astype(F32))

        lax.fori_loop(0, n_chunks, step, (s0_ref[0, 0], s0_ref[1, 0]), unroll=4)

    nt = (((1,), (1,)), ((), ()))
    gain = gain_ref[0]
    first = pl.program_id(1) * nchunk
    for c in range(nchunk):
        sl = slice(c * CHUNK, (c + 1) * CHUNK)
        scat = jnp.concatenate([s_all[0, first + c], s_all[1, first + c]], axis=1)
        o = oi_ref[sl, :].astype(F32) + lax.dot_general(qe_ref[sl, :], scat, nt, preferred_element_type=F32)
        o = _rms(o) * gain
        o_ref[sl, :] = (o * _silu(gt_ref[sl, :].astype(F32))).astype(BF16)


def _gla_finish(oi, qe, z_b, hg_gain, s_inc, s_dec, s0, rows_per_block, side=()):
    m = oi.shape[0]
    nb = m // rows_per_block
    nchunk = rows_per_block // CHUNK
    n_chunks = m // CHUNK
    side_in, side_out, side_shapes = _side_cast_specs(side, HG_HEADS * nb, lambda h, b: h * nb + b)
    return pl.pallas_call(
        functools.partial(_gla_finish_kernel, nchunk=nchunk, n_chunks=n_chunks, n_side=len(side)),
        grid=(HG_HEADS, nb),
        in_specs=[pl.BlockSpec((rows_per_block, HG_DK), lambda h, b: (b, h)),
                  pl.BlockSpec((rows_per_block, 2 * HG_DK), lambda h, b: (b, h)),
                  pl.BlockSpec((rows_per_block, HG_DK), lambda h, b: (b, h)),
                  pl.BlockSpec((1, 1, HG_DK), lambda h, b: (h, 0, 0)),
                  pl.BlockSpec(memory_space=pl.ANY),
                  pl.BlockSpec((2, 1, n_chunks * SUBLANES, HG_DK), lambda h, b: (0, h, 0, 0)),
                  pl.BlockSpec((2, 1, HG_DK, HG_DK), lambda h, b: (0, h, 0, 0))] + side_in,
        out_specs=[pl.BlockSpec((rows_per_block, HG_DK), lambda h, b: (b, h))] + side_out,
        out_shape=[jax.ShapeDtypeStruct((m, HG_WIDTH), BF16)] + side_shapes,
        scratch_shapes=[pltpu.VMEM((2, n_chunks, HG_DK, HG_DK), BF16),
                        pltpu.VMEM((2, 2, n_chunks, HG_DK, HG_DK), BF16),
                        pltpu.SemaphoreType.DMA((2,))],
        compiler_params=_params(("arbitrary", "arbitrary"), 58),
        name="gla_finish",
    )(oi, qe, z_b, hg_gain.reshape(HG_HEADS, 1, HG_DK), s_inc, s_dec, s0, *[s[0] for s in side])


def _residual_epilogue(mix, x_ref, vec_ref, xo_ref, ho_ref, rs):
    xn = x_ref[rs, :] + _rms(mix) * vec_ref[0:1]
    xo_ref[rs, :] = xn
    if ho_ref is not None:
        hn = (_rms(xn) * vec_ref[1:2] + vec_ref[2:3]).astype(BF16)
        outs = ho_ref if isinstance(ho_ref, (tuple, list)) else (ho_ref,)
        width = hn.shape[1] // len(outs)
        for k, out in enumerate(outs):
            out[rs, :] = hn[:, k * width:(k + 1) * width]


def _spatial_chunk(u_raw, v_raw, w_ref, bias_ref, lng_ref, lnb_ref):
    v = _gelu_tanh(v_raw.astype(F32))
    xc = v - jnp.mean(v, axis=-1, keepdims=True)
    y = (xc * lax.rsqrt(jnp.mean(xc * xc, axis=-1, keepdims=True) + EPS) * lng_ref[...] + lnb_ref[...]).astype(BF16)
    u = _gelu_tanh(u_raw.astype(F32))
    parts = []
    for g in range(SG_GROUPS):
        cs = slice(g * SG_CH, (g + 1) * SG_CH)
        mixed = jnp.dot(w_ref[g], y[:, cs], preferred_element_type=F32) + bias_ref[:, cs]
        parts.append((u[:, cs] * mixed).astype(BF16))
    return jnp.concatenate(parts, axis=1)


def _mm_res_kernel(*refs, n_a, n_sp, emit_h):
    a_refs = refs[:n_a]
    sp_refs = refs[n_a:n_a + 6 * n_sp]
    n_w = n_a + n_sp
    w_refs = refs[n_a + 6 * n_sp:n_a + 6 * n_sp + n_w]
    x_ref, vec_ref, xo_ref = refs[n_a + 6 * n_sp + n_w:n_a + 6 * n_sp + n_w + 3]
    ho_ref = refs[n_a + 6 * n_sp + n_w + 3] if emit_h else None
    for rs in _row_subblocks(x_ref.shape[0], SG_CHUNK):
        mix = jnp.dot(a_refs[0][rs, :], w_refs[0][...], preferred_element_type=F32)
        for a_ref, w_ref in zip(a_refs[1:], w_refs[1:]):
            mix = mix + jnp.dot(a_ref[rs, :], w_ref[...], preferred_element_type=F32)
        if n_sp:
            u_ref, v_ref = sp_refs[:2]
            s = _spatial_chunk(u_ref[rs, :], v_ref[rs, :], *sp_refs[2:])
            mix = mix + jnp.dot(s, w_refs[-1][...], preferred_element_type=F32)
        _residual_epilogue(mix, x_ref, vec_ref, xo_ref, ho_ref, rs)


def _mm_res(a_list, w, x, vec, tm, emit_h=True, spatial=None):
    m = x.shape[0]
    n_a = len(a_list)
    widths = [a.shape[1] for a in a_list] + ([SG_WIDTH] if spatial else [])
    offs = np.cumsum([0] + widths)
    sp_specs, sp_args = [], []
    if spatial:
        z_b = spatial[0]
        sp_specs = [pl.BlockSpec((tm, SG_WIDTH), lambda i: (i, 1)), pl.BlockSpec((tm, SG_WIDTH), lambda i: (i, 2)),
                    pl.BlockSpec((SG_GROUPS, SG_CHUNK, SG_CHUNK), lambda i: (0, 0, 0)),
                    pl.BlockSpec((SG_CHUNK, SG_WIDTH), lambda i: (0, 0)),
                    pl.BlockSpec((1, SG_WIDTH), lambda i: (0, 0)), pl.BlockSpec((1, SG_WIDTH), lambda i: (0, 0))]
        sp_args = [z_b, z_b, *spatial[1:]]
    in_specs = ([pl.BlockSpec((tm, a.shape[1]), lambda i: (i, 0)) for a in a_list] + sp_specs
                + [pl.BlockSpec((wd, D_MODEL), lambda i, k=int(o) // wd: (k, 0)) for wd, o in zip(widths, offs)]
                + [pl.BlockSpec((tm, D_MODEL), lambda i: (i, 0)),
                   pl.BlockSpec((8, D_MODEL), lambda i: (0, 0))])
    out_specs = [pl.BlockSpec((tm, D_MODEL), lambda i: (i, 0))]
    out_shape = [jax.ShapeDtypeStruct((m, D_MODEL), F32)]
    if emit_h:
        out_specs.append(pl.BlockSpec((tm, D_MODEL), lambda i: (i, 0)))
        out_shape.append(jax.ShapeDtypeStruct((m, D_MODEL), BF16))
    return pl.pallas_call(
        functools.partial(_mm_res_kernel, n_a=n_a, n_sp=1 if spatial else 0, emit_h=emit_h),
        grid=(m // tm,),
        in_specs=in_specs, out_specs=out_specs, out_shape=out_shape,
        compiler_params=_params(("parallel",), 56),
        name="mm_residual",
    )(*a_list, *sp_args, *[w] * len(widths), x, vec)


def _mlp_kernel(*refs, n_h, n_side, nj, n_steps, loop_ff):
    h_ref, w1_ref, w2_ref, x_ref, vec_ref = refs[:5]
    side_in = refs[5:5 + n_side]
    xo_ref = refs[5 + n_side]
    ho_ref = tuple(refs[6 + n_side:6 + n_side + n_h]) or None
    n_out = 1 + n_h
    side_out = refs[5 + n_side + n_out:5 + 2 * n_side + n_out]
    w1_buf, w2_buf, sems = refs[-3:]
    acc_ref = xo_ref
    _side_cast(side_in, side_out)
    last = nj - 1
    tf = w1_buf.shape[2]
    first_step = pl.program_id(0) * nj + (0 if loop_ff else pl.program_id(1))

    def tile_copies(s):
        slot = s % WEIGHT_SLOTS
        col = (s % nj) * tf
        if not isinstance(col, int):
            col = pl.multiple_of(col, tf)
        return (pltpu.make_async_copy(w1_ref.at[:, pl.ds(col, tf)], w1_buf.at[slot], sems.at[0, slot]),
                pltpu.make_async_copy(w2_ref.at[pl.ds(col, tf), :], w2_buf.at[slot], sems.at[1, slot]))

    @pl.when(first_step == 0)
    def _():
        for s in range(WEIGHT_SLOTS - 1):
            for cp in tile_copies(s):
                cp.start()

    def ff_step(jj, mode):
        step = pl.program_id(0) * nj + jj

        @pl.when(step + (WEIGHT_SLOTS - 1) < n_steps)
        def _():
            for cp in tile_copies(step + (WEIGHT_SLOTS - 1)):
                cp.start()

        for cp in tile_copies(step):
            cp.wait()
        slot = step % WEIGHT_SLOTS
        for rs in _row_subblocks(h_ref.shape[0]):
            a = jnp.dot(h_ref[rs, :], w1_buf[slot], preferred_element_type=F32)
            a = jnp.square(jnp.maximum(a, 0.0)).astype(BF16)
            p = jnp.dot(a, w2_buf[slot], preferred_element_type=F32)
            if mode == "first":
                acc_ref[rs, :] = p
            elif mode == "last":
                _residual_epilogue(acc_ref[rs, :] + p, x_ref, vec_ref, xo_ref, ho_ref, rs)
            else:
                acc_ref[rs, :] += p

    if loop_ff:
        ff_step(0, "first")

        def middle(jj, carry):
            ff_step(jj, "middle")
            return carry

        lax.fori_loop(1, last, middle, 0)
        ff_step(last, "last")
    else:
        j = pl.program_id(1)
        pl.when(j == 0)(lambda: ff_step(j, "first"))
        pl.when(jnp.logical_and(j > 0, j < last))(lambda: ff_step(j, "middle"))
        pl.when(j == last)(lambda: ff_step(j, "last"))


def _mlp(h, w1, w2, x, vec, tm, tf, n_h, side=()):
    m = x.shape[0]
    nj = D_FF // tf
    loop_ff = not side
    row = (lambda i: (i, 0)) if loop_ff else (lambda i, j: (i, 0))
    const = (lambda i: (0, 0)) if loop_ff else (lambda i, j: (0, 0))
    out_specs = [pl.BlockSpec((tm, D_MODEL), row)]
    out_shape = [jax.ShapeDtypeStruct((m, D_MODEL), F32)]
    for _ in range(n_h):
        out_specs.append(pl.BlockSpec((tm, D_MODEL // n_h), row))
        out_shape.append(jax.ShapeDtypeStruct((m, D_MODEL // n_h), BF16))
    side_in, side_out, side_shapes = _side_cast_specs(side, (m // tm) * nj, lambda i, j: i * nj + j)
    return pl.pallas_call(
        functools.partial(_mlp_kernel, n_h=n_h, n_side=len(side), nj=nj, n_steps=(m // tm) * nj, loop_ff=loop_ff),
        grid=(m // tm,) if loop_ff else (m // tm, nj),
        in_specs=[pl.BlockSpec((tm, D_MODEL), row),
                  pl.BlockSpec(memory_space=pl.ANY),
                  pl.BlockSpec(memory_space=pl.ANY),
                  pl.BlockSpec((tm, D_MODEL), row),
                  pl.BlockSpec((8, D_MODEL), const)] + side_in,
        out_specs=out_specs + side_out, out_shape=out_shape + side_shapes,
        scratch_shapes=[pltpu.VMEM((WEIGHT_SLOTS, D_MODEL, tf), BF16),
                        pltpu.VMEM((WEIGHT_SLOTS, tf, D_MODEL), BF16),
                        pltpu.SemaphoreType.DMA((2, WEIGHT_SLOTS))],
        compiler_params=_params(("arbitrary",) if loop_ff else ("arbitrary", "arbitrary"), 60),
        name="mlp",
    )(h, w1, w2, x, vec, *[s[0] for s in side])


def _dft_constants():
    def cs(n):
        k = np.arange(n)
        ang = 2.0 * np.pi * ((k[:, None] * k[None, :]) % n) / n
        return np.cos(ang) / np.sqrt(n), np.sin(ang) / np.sqrt(n)

    cc, sc = cs(FT_CH)
    c1, s1 = cs(FFT_L1)
    m1 = np.concatenate([c1, -s1], axis=0)
    c2, s2 = cs(FFT_L2)
    m2 = np.block([[c2, s2], [-s2, c2]])
    u1 = np.arange(FFT_L1)[:, None]
    t2 = np.arange(FFT_L2)[None, :]
    ang = 2.0 * np.pi * ((u1 * t2) % SEQ) / SEQ
    tw_c = np.cos(ang).reshape(SEQ, 1)
    tw_s = np.sin(ang).reshape(SEQ, 1)
    f = lambda a: jnp.asarray(a, F32)
    return f(cc), f(sc), f(m1), f(m2), f(tw_c), f(tw_s)


def _swap_row_factors(a, n_outer, n_inner):
    return a.reshape(n_outer, n_inner, a.shape[1]).transpose(1, 0, 2).reshape(a.shape)


def _pos_dft1_kernel(x_ref, m_ref, zr_ref, zi_ref):
    for k in range(x_ref.shape[0] // FFT_L1):
        sl = slice(k * FFT_L1, (k + 1) * FFT_L1)
        z = jnp.dot(m_ref[...], x_ref[sl, :], preferred_element_type=F32)
        zr_ref[sl, :] = z[:FFT_L1].astype(BF16)
        zi_ref[sl, :] = z[FFT_L1:].astype(BF16)


def _pos_dft1(xt, m1, slabs):
    rows = slabs * FFT_L1
    blk = pl.BlockSpec((rows, xt.shape[1]), lambda j: (j, 0))
    return pl.pallas_call(
        _pos_dft1_kernel,
        grid=(SEQ // rows,),
        in_specs=[blk, pl.BlockSpec((2 * FFT_L1, FFT_L1), lambda j: (0, 0))],
        out_specs=[blk, blk],
        out_shape=[jax.ShapeDtypeStruct(xt.shape, BF16)] * 2,
        compiler_params=_params(("parallel",), 32),
        name="pos_dft1",
    )(xt, m1)


def _pos_dft2_kernel(zr_ref, zi_ref, c_ref, s_ref, m_ref, wc_ref, ws_ref, y_ref, yr_buf, yi_buf):
    for k in range(zr_ref.shape[0] // FFT_L2):
        sl = slice(k * FFT_L2, (k + 1) * FFT_L2)
        zr = zr_ref[sl, :].astype(F32)
        zi = zi_ref[sl, :].astype(F32)
        c = c_ref[sl, :]
        s = s_ref[sl, :]
        t = jnp.concatenate([(zr * c + zi * s).astype(BF16), (zi * c - zr * s).astype(BF16)], axis=0)
        y = jnp.dot(m_ref[...], t, preferred_element_type=F32)
        yr_buf[sl, :] = y[:FFT_L2].astype(BF16)
        yi_buf[sl, :] = y[FFT_L2:].astype(BF16)
    for g in range(zr_ref.shape[1] // FT_CH):
        cs = slice(g * FT_CH, (g + 1) * FT_CH)
        y_ref[:, cs] = (jnp.dot(yr_buf[:, cs], wc_ref[...], preferred_element_type=F32)
                        + jnp.dot(yi_buf[:, cs], ws_ref[...], preferred_element_type=F32)).astype(BF16)


def _pos_dft2(zr, zi, tw_c, tw_s, m2, wc, ws, slabs):
    rows = slabs * FFT_L2
    width = zr.shape[1]
    blk = pl.BlockSpec((rows, width), lambda j: (j, 0))
    tw = pl.BlockSpec((rows, 1), lambda j: (j, 0))
    const = lambda a: pl.BlockSpec(a.shape, lambda j: (0, 0))
    return pl.pallas_call(
        _pos_dft2_kernel,
        grid=(SEQ // rows,),
        in_specs=[blk, blk, tw, tw, const(m2), const(wc), const(ws)],
        out_specs=blk,
        out_shape=jax.ShapeDtypeStruct(zr.shape, BF16),
        scratch_shapes=[pltpu.VMEM((rows, width), BF16), pltpu.VMEM((rows, width), BF16)],
        compiler_params=_params(("parallel",), 40),
        name="pos_dft2",
    )(zr, zi, tw_c, tw_s, m2, wc, ws)


def _fourier_mix(h_parts):
    cc, sc, m1, m2, tw_c, tw_s = _dft_constants()
    m1, m2, cc, sc = m1.astype(BF16), m2.astype(BF16), cc.astype(BF16), sc.astype(BF16)
    hts = [_swap_row_factors(h, FFT_L1, FFT_L2) for h in h_parts]
    zs = [_pos_dft1(ht, m1, DFT_ROWS // FFT_L1) for ht in hts]
    zs = [[_swap_row_factors(z, FFT_L2, FFT_L1) for z in zri] for zri in zs]
    yps = [_pos_dft2(zr, zi, tw_c, tw_s, m2, cc, sc, DFT_ROWS // FFT_L2) for zr, zi in zs]
    return [_swap_row_factors(yp, FFT_L1, FFT_L2) for yp in yps]


def _pack_rows(*rows):
    rows = [r.reshape(1, D_MODEL).astype(F32) for r in rows]
    rows += [jnp.zeros((1, D_MODEL), F32)] * (8 - len(rows))
    return jnp.concatenate(rows, axis=0)


def kernel(x, c, ctx, c_ctx, w_ada, b_ada, norm_gain, w_in, w_out, lb_raw, hg_norm_gain,
           sg_w, sg_b, sg_ln_gain, sg_ln_bias, w_fourier, w_mlp_in, w_mlp_out):
    assert x.shape == (1, SEQ, D_MODEL) and ctx.shape == (1, CTX_LEN, D_MODEL)
    xs = x.reshape(SEQ, D_MODEL)
    cs = ctx.reshape(CTX_LEN, D_MODEL)

    mods = _ada_mods(jnp.concatenate([c.reshape(1, D_MODEL), c_ctx.reshape(1, D_MODEL)], axis=0), w_ada, b_ada)
    mod = lambda l, r, j: mods[l, r, j * D_MODEL:(j + 1) * D_MODEL]

    lb = jnp.cumsum(jax.nn.softmax(lb_raw.astype(F32), axis=1), axis=1)[:, 0]
    lb4 = lb.reshape(2, HG_HEADS, 1, HG_DK)

    pre = lambda gain, l, r, j: _pack_rows(gain * (1.0 + mod(l, r, j + 1)), mod(l, r, j))
    post = lambda gain, l, j, *nxt: _pack_rows(mod(l, 0, j) * gain, *nxt)

    z_a, h0, w_gu, w_v = _norm_mm_rows(xs, pre(norm_gain[0, 0], 0, 0, 0), w_in, 4 * HG_WIDTH, MM_ROWS, MM_COLS,
                                       side=[(w_in, 0, 2 * HG_WIDTH, 2), (w_in, 0, HG_WIDTH, 6)])
    z_ctx = _norm_mm(cs, pre(norm_gain[0, 0], 0, 1, 0), w_in, 3 * HG_WIDTH, CTX_LEN, MM_COLS)
    s_ctx = _ctx_states(z_ctx, lb4)
    z_b, o_intra, q_dec, s_inc, s_dec, w_out0, w1_0, w2_0 = _gla_local(
        h0, w_gu, w_v, z_a, lb4, GLA_LOCAL_ROWS, side=[(w_out, 0), (w_mlp_in, 0), (w_mlp_out, 0)])
    o, w_fou = _gla_finish(o_intra, q_dec, z_b, hg_norm_gain[0], s_inc, s_dec, s_ctx, GLA_FINISH_ROWS,
                           side=[(w_fourier, 0)])
    bias_full = jnp.repeat(sg_b[0].T.astype(F32), SG_CH, axis=1)
    vec = post(norm_gain[0, 1], 0, 2, *pre(norm_gain[0, 2], 0, 0, 3)[:2])
    x1, h = _mm_res([o], w_out0, xs, vec, MM_ROWS,
                    spatial=(z_b, sg_w[0].astype(BF16), bias_full, sg_ln_gain[0].reshape(1, SG_WIDTH),
                             sg_ln_bias[0].reshape(1, SG_WIDTH)))
    vec = post(norm_gain[0, 3], 0, 5, *pre(norm_gain[1, 0], 1, 0, 0)[:2])
    x2, h_a, h_b, w1_1, w2_1 = _mlp(h, w1_0, w2_0, x1, vec, MM_ROWS, MLP_FF_COLS, 2,
                                    side=[(w_mlp_in, 1), (w_mlp_out, 1)])

    y_parts = _fourier_mix([h_a, h_b])
    vec = post(norm_gain[1, 1], 1, 2, *pre(norm_gain[1, 2], 1, 0, 3)[:2])
    x3, h = _mm_res(y_parts, w_fou, x2, vec, MM_ROWS)
    (x4,) = _mlp(h, w1_1, w2_1, x3, post(norm_gain[1, 3], 1, 5), MM_ROWS, MLP_FF_COLS, 0)
    return x4.reshape(1, SEQ, D_MODEL)
```

```python
import functools

import numpy as np
import jax
import jax.numpy as jnp
from jax import lax
from jax.experimental import pallas as pl
from jax.experimental.pallas import tpu as pltpu

D_MODEL = 2048
SEQ = 8192
DEPTH = 2
CTX_LEN = 256
CHUNK = 64
SUB = 16
ADA_COLS = 2048
MM_ROWS = 512
MM_COLS = 1024
MLP_FF_COLS = 1024
WEIGHT_SLOTS = 3
SUBROWS = 256
GLA_LOCAL_ROWS = 256
GLA_FINISH_ROWS = 8192
DFT_ROWS = 1024
HG_HEADS = 8
HG_DK = 128
HG_WIDTH = HG_HEADS * HG_DK
SG_GROUPS = 8
SG_CH = 128
SG_WIDTH = SG_GROUPS * SG_CH
SG_CHUNK = 128
FT_GROUPS = 4
FT_CH = D_MODEL // FT_GROUPS
FFT_L1 = 64
FFT_L2 = 128
IN_WIDTH = 5 * HG_WIDTH + 2 * SG_WIDTH
D_FF = 4 * D_MODEL
N_MOD = 6
EPS = 1e-6
EXP_CLAMP = 115.0
LOG2E = 1.0 / float(np.log(2.0))
LANES = 128
SUBLANES = 8
MXU_COLS = 256
PROJ_ROWS = 256

F32 = jnp.float32
BF16 = jnp.bfloat16
MIB = 1024 * 1024


def _params(semantics, vmem_mib):
    return pltpu.CompilerParams(dimension_semantics=semantics, vmem_limit_bytes=vmem_mib * MIB)


def _rms(x):
    return x * lax.rsqrt(jnp.mean(x * x, axis=-1, keepdims=True) + EPS)


def _silu(x):
    return x * jax.nn.sigmoid(x)


def _row_subblocks(rows, step=SUBROWS):
    step = min(rows, step)
    return [slice(s, s + step) for s in range(0, rows, step)]


def _gelu_tanh(x):
    cdf = 0.5 * (1.0 + jnp.tanh(float(np.sqrt(2.0 / np.pi)) * (x + 0.044715 * (x * x * x))))
    return x * cdf


def _ada_kernel(c_ref, w_ref, b_ref, o_ref, s_buf):
    @pl.when((pl.program_id(0) == 0) & (pl.program_id(1) == 0))
    def _():
        s_buf[...] = _silu(c_ref[...])

    tn = o_ref.shape[-1]
    for r in range(2):
        s = s_buf[r]
        cols = [jnp.sum(w_ref[0, :, j * LANES:(j + 1) * LANES] * s, axis=0, keepdims=True)
                for j in range(tn // LANES)]
        o_ref[0, r:r + 1, :] = jnp.concatenate(cols, axis=1) + b_ref[0]


def _ada_mods(c2, w_ada, b_ada):
    tn = ADA_COLS
    n = N_MOD * D_MODEL
    cb = jnp.broadcast_to(c2[:, :, None], (2, D_MODEL, LANES))
    return pl.pallas_call(
        _ada_kernel,
        grid=(DEPTH, n // tn),
        in_specs=[pl.BlockSpec((2, D_MODEL, LANES), lambda l, j: (0, 0, 0)),
                  pl.BlockSpec((1, D_MODEL, tn), lambda l, j: (l, 0, j)),
                  pl.BlockSpec((1, 1, tn), lambda l, j: (l, 0, j))],
        out_specs=pl.BlockSpec((1, 2, tn), lambda l, j: (l, 0, j)),
        out_shape=jax.ShapeDtypeStruct((DEPTH, 2, n), F32),
        scratch_shapes=[pltpu.VMEM((2, D_MODEL, LANES), F32)],
        compiler_params=_params(("arbitrary", "arbitrary"), 40),
        name="ada_mods",
    )(cb, w_ada, b_ada.reshape(DEPTH, 1, n))


def _side_cast_specs(weights, n_steps, step_of):
    in_specs, out_specs, out_shapes = [], [], []
    for w, layer, *cols in weights:
        width, cidx = cols if cols else (w.shape[2], 0)
        assert w.shape[1] % n_steps == 0, "every grid step must get an equal row slab"
        rows = w.shape[1] // n_steps
        in_specs.append(pl.BlockSpec((1, rows, width),
                                     lambda *g, layer=layer, cidx=cidx: (layer, step_of(*g), cidx)))
        out_specs.append(pl.BlockSpec((rows, width), lambda *g: (step_of(*g), 0)))
        out_shapes.append(jax.ShapeDtypeStruct((w.shape[1], width), BF16))
    return in_specs, out_specs, out_shapes


def _side_cast(side_in, side_out):
    for wi, wo in zip(side_in, side_out):
        wo[...] = wi[0].astype(BF16)


def _norm_mm_kernel(x_ref, vec_ref, w_hbm, o_ref, stage, sems):
    nj, _, tn = stage.shape
    copies = [pltpu.make_async_copy(w_hbm.at[0, :, pl.ds(t * tn, tn)], stage.at[t], sems.at[t])
              for t in range(nj)]
    for c in copies:
        c.start()
    h = (_rms(x_ref[...]) * vec_ref[0:1] + vec_ref[1:2]).astype(BF16)
    for t in range(nj):
        copies[t].wait()
        o_ref[:, t * tn:(t + 1) * tn] = jnp.dot(h, stage[t].astype(BF16),
                                                preferred_element_type=F32).astype(o_ref.dtype)


def _norm_mm(x, vec, w, n_cols, tm, tn):
    m = x.shape[0]
    nj = n_cols // tn
    return pl.pallas_call(
        _norm_mm_kernel,
        grid=(m // tm,),
        in_specs=[pl.BlockSpec((tm, D_MODEL), lambda i: (i, 0)),
                  pl.BlockSpec((8, D_MODEL), lambda i: (0, 0)),
                  pl.BlockSpec(memory_space=pl.ANY)],
        out_specs=pl.BlockSpec((tm, n_cols), lambda i: (i, 0)),
        out_shape=jax.ShapeDtypeStruct((m, n_cols), BF16),
        scratch_shapes=[pltpu.VMEM((nj, D_MODEL, tn), F32), pltpu.SemaphoreType.DMA((nj,))],
        compiler_params=_params(("arbitrary",), 40),
        name="norm_mm",
    )(x, vec, w)


def _norm_mm_rows_kernel(*refs, n_side, nj):
    x_ref, vec_ref, w_hbm = refs[:3]
    side_in = refs[3:3 + n_side]
    o_ref, h_ref = refs[3 + n_side:5 + n_side]
    side_out = refs[5 + n_side:5 + 2 * n_side]
    wb_ref, stage, sems = refs[5 + 2 * n_side:]
    _side_cast(side_in, side_out)
    tn = wb_ref.shape[2]

    def tile_copy(t):
        return pltpu.make_async_copy(w_hbm.at[0, :, pl.ds(t * tn, tn)], stage.at[t % 2], sems.at[t % 2])

    def row_block(build_weights):
        if build_weights:
            tile_copy(0).start()
        for t in range(nj):
            if build_weights:
                if t + 1 < nj:
                    tile_copy(t + 1).start()
                tile_copy(t).wait()
                wb_ref[t] = stage[t % 2].astype(BF16)
            w = wb_ref[t]
            cols = slice(t * tn, (t + 1) * tn)
            if t == 0:
                for rs in _row_subblocks(x_ref.shape[0]):
                    h = (_rms(x_ref[rs, :]) * vec_ref[0:1] + vec_ref[1:2]).astype(BF16)
                    h_ref[rs, :] = h
                    o_ref[rs, cols] = jnp.dot(h, w, preferred_element_type=F32).astype(o_ref.dtype)
            else:
                o_ref[:, cols] = jnp.dot(h_ref[...], w, preferred_element_type=F32).astype(o_ref.dtype)

    pl.when(pl.program_id(0) == 0)(lambda: row_block(True))
    pl.when(pl.program_id(0) > 0)(lambda: row_block(False))


def _norm_mm_rows(x, vec, w, n_cols, tm, tn, side=()):
    m = x.shape[0]
    nj = n_cols // tn
    side_in, side_out, side_shapes = _side_cast_specs(side, m // tm, lambda i: i)
    return pl.pallas_call(
        functools.partial(_norm_mm_rows_kernel, n_side=len(side), nj=nj),
        grid=(m // tm,),
        in_specs=[pl.BlockSpec((tm, D_MODEL), lambda i: (i, 0)),
                  pl.BlockSpec((8, D_MODEL), lambda i: (0, 0)),
                  pl.BlockSpec(memory_space=pl.ANY)] + side_in,
        out_specs=[pl.BlockSpec((tm, n_cols), lambda i: (i, 0)),
                   pl.BlockSpec((tm, D_MODEL), lambda i: (i, 0))] + side_out,
        out_shape=[jax.ShapeDtypeStruct((m, n_cols), BF16),
                   jax.ShapeDtypeStruct((m, D_MODEL), BF16)] + side_shapes,
        scratch_shapes=[pltpu.VMEM((nj, D_MODEL, tn), BF16),
                        pltpu.VMEM((2, D_MODEL, tn), F32),
                        pltpu.SemaphoreType.DMA((2,))],
        compiler_params=_params(("arbitrary",), 62),
        name="norm_mm_rows",
    )(x, vec, w, *[s[0] for s in side])


def _seg_cumsum(g, row_in_chunk):
    p = g
    s = 1
    while s < CHUNK:
        p = p + jnp.where(row_in_chunk >= s, pltpu.roll(p, s, axis=0), 0.0)
        s *= 2
    return p


def _ctx_state_kernel(f_ref, i_ref, lb_ref, sfin_ref, p_buf, g_buf, k_buf, *, nchunk):
    d = pl.program_id(0)
    rows = f_ref.shape[0]
    lb = lb_ref[0, 0]
    f = lb + (1.0 - lb) * jax.nn.sigmoid(f_ref[...].astype(F32))
    g = jnp.log(f) * LOG2E
    ric = lax.broadcasted_iota(jnp.int32, (rows, LANES), 0) & (CHUNK - 1)
    p_buf[...] = _seg_cumsum(g, ric)
    g_buf[...] = g
    k_buf[...] = 1.0 - f

    s = jnp.zeros((HG_DK, HG_DK), F32)
    for j in range(nchunk):
        c = jnp.where(d == 0, j, nchunk - 1 - j)
        base = pl.multiple_of(c * CHUNK, CHUNK)
        sl = pl.ds(base, CHUNK)
        p = p_buf[sl, :]
        tot = p_buf[pl.ds(base + CHUNK - 1, 1), :]
        ex = jnp.where(d == 0, tot - p, p - g_buf[sl, :])
        kt = (k_buf[sl, :] * jnp.exp2(ex)).astype(BF16)
        u = lax.dot_general(i_ref[sl, :], kt, (((0,), (0,)), ((), ())), preferred_element_type=F32)
        s = s * jnp.exp2(tot) + u
    sfin_ref[0, 0] = s


def _ctx_states(z_ctx, lb4):
    rows = z_ctx.shape[0]
    return pl.pallas_call(
        functools.partial(_ctx_state_kernel, nchunk=rows // CHUNK),
        grid=(2, HG_HEADS),
        in_specs=[pl.BlockSpec((rows, HG_DK), lambda d, h: (0, d * HG_HEADS + h)),
                  pl.BlockSpec((rows, HG_DK), lambda d, h: (0, 2 * HG_HEADS + h)),
                  pl.BlockSpec((1, 1, 1, HG_DK), lambda d, h: (d, h, 0, 0))],
        out_specs=pl.BlockSpec((1, 1, HG_DK, HG_DK), lambda d, h: (d, h, 0, 0)),
        out_shape=jax.ShapeDtypeStruct((2, HG_HEADS, HG_DK, HG_DK), F32),
        scratch_shapes=[pltpu.VMEM((rows, HG_DK), F32) for _ in range(3)],
        compiler_params=_params(("parallel", "parallel"), 32),
        name="ctx_states",
    )(z_ctx, z_ctx, lb4)


def _gla_local_kernel(*refs, n_side):
    h_ref, wgu_ref, wv_ref, ff_ref, fb_ref, i_ref, q_ref, lb_ref = refs[:8]
    side_in = refs[8:8 + n_side]
    zb_ref, oi_ref, qe_ref, u_ref, d_ref = refs[8 + n_side:13 + n_side]
    side_out = refs[13 + n_side:13 + 2 * n_side]
    bufs = refs[13 + 2 * n_side:]
    _side_cast(side_in, side_out)
    rows = h_ref.shape[0]
    nchunk = rows // CHUNK
    proj_rows = min(rows, PROJ_ROWS)
    row_parts = rows // proj_rows

    def project(piece):
        lo = (piece // row_parts) * MXU_COLS
        rs = slice((piece % row_parts) * proj_rows, (piece % row_parts + 1) * proj_rows)
        w_ref, off = (wgu_ref, lo) if lo < 2 * HG_WIDTH else (wv_ref, lo - 2 * HG_WIDTH)
        zb_ref[rs, lo:lo + MXU_COLS] = jnp.dot(h_ref[rs, :], w_ref[:, off:off + MXU_COLS],
                                               preferred_element_type=F32).astype(BF16)

    n_pieces = row_parts * 3 * HG_WIDTH // MXU_COLS
    pieces_before = [(k + 1) * n_pieces // (HG_HEADS * nchunk) - k * n_pieces // (HG_HEADS * nchunk)
                     for k in range(HG_HEADS * nchunk)]

    ric = lax.broadcasted_iota(jnp.int32, (rows, LANES), 0) & (CHUNK - 1)
    blk = lax.broadcasted_iota(jnp.int32, (CHUNK, LANES), 0) // SUB
    r2 = lax.broadcasted_iota(jnp.int32, (CHUNK, CHUNK), 0)
    c2 = lax.broadcasted_iota(jnp.int32, (CHUNK, CHUNK), 1)
    nt = (((1,), (1,)), ((), ()))
    tn_dims = (((0,), (0,)), ((), ()))
    nsub = CHUNK // SUB
    emitted = [0]

    def before_chunk(k):
        for _ in range(pieces_before[k]):
            project(emitted[0])
            emitted[0] += 1

    for hd in range(HG_HEADS):
        _gla_local_head(hd, nchunk, ric, blk, r2, c2, nt, tn_dims, nsub, ff_ref, fb_ref, i_ref, q_ref, lb_ref,
                        oi_ref, qe_ref, u_ref, d_ref, bufs[6 * (hd % 2):6 * (hd % 2) + 6], before_chunk)
    assert emitted[0] == n_pieces


def _gla_local_head(hd, nchunk, ric, blk, r2, c2, nt, tn_dims, nsub, ff_ref, fb_ref, i_ref, q_ref, lb_ref,
                    oi_ref, qe_ref, u_ref, d_ref, bufs, before_chunk):
    pf_buf, pb_buf, gb_buf, kf_buf, kb_buf, q_buf = bufs
    cs = slice(hd * HG_DK, (hd + 1) * HG_DK)
    lbf = lb_ref[0, hd]
    lbb = lb_ref[1, hd]
    ff = lbf + (1.0 - lbf) * jax.nn.sigmoid(ff_ref[:, cs].astype(F32))
    pf_buf[...] = _seg_cumsum(jnp.log(ff) * LOG2E, ric)
    kf_buf[...] = 1.0 - ff
    fb = lbb + (1.0 - lbb) * jax.nn.sigmoid(fb_ref[:, cs].astype(F32))
    gb = jnp.log(fb) * LOG2E
    pb_buf[...] = _seg_cumsum(gb, ric)
    gb_buf[...] = gb
    kb_buf[...] = 1.0 - fb
    q_buf[...] = _silu(q_ref[:, cs].astype(F32)) * (HG_DK ** -0.5)

    for c in range(nchunk):
        before_chunk(hd * nchunk + c)
        base = c * CHUNK
        sl = pl.ds(base, CHUNK)
        pf = pf_buf[sl, :]
        pb = pb_buf[sl, :]
        gbc = gb_buf[sl, :]
        kf = kf_buf[sl, :]
        kb = kb_buf[sl, :]
        q = q_buf[sl, :]
        totb = pb_buf[pl.ds(base + CHUNK - 1, 1), :]
        suf = totb - pb + gbc
        mids, nids = [], []
        for i in range(nsub):
            mids.append(pf_buf[pl.ds(base + SUB * i + SUB // 2 - 1, 1), :])
            rmid = pl.ds(base + SUB * i + SUB // 2, 1)
            nids.append(totb - pb_buf[rmid, :] + gb_buf[rmid, :])
        mid_rows = jnp.concatenate([jnp.broadcast_to(m, (SUB, LANES)) for m in mids], axis=0)
        nid_rows = jnp.concatenate([jnp.broadcast_to(m, (SUB, LANES)) for m in nids], axis=0)
        qf_all = q * jnp.exp2(jnp.minimum(pf - mid_rows, EXP_CLAMP))
        qb_all = q * jnp.exp2(jnp.minimum(suf - nid_rows, EXP_CLAMP))
        qf_seg, kf_seg, qb_seg, kb_seg = [], [], [], []
        for i in range(nsub):
            qf_seg.append(jnp.where(blk == i, qf_all, 0.0))
            qb_seg.append(jnp.where(blk == i, qb_all, 0.0))
            hi = SUB * (i + 1)
            top = kf[:hi] * jnp.exp2(jnp.minimum(mids[i] - pf[:hi], EXP_CLAMP))
            kf_seg.append(top if hi == CHUNK else
                          jnp.concatenate([top, jnp.zeros((CHUNK - hi, LANES), F32)], axis=0))
            lo = SUB * i
            bot = kb[lo:] * jnp.exp2(jnp.minimum(nids[i] - suf[lo:], EXP_CLAMP))
            kb_seg.append(bot if lo == 0 else
                          jnp.concatenate([jnp.zeros((lo, LANES), F32), bot], axis=0))
        qf_big = jnp.concatenate(qf_seg, axis=1).astype(BF16)
        kf_big = jnp.concatenate(kf_seg, axis=1).astype(BF16)
        qb_big = jnp.concatenate(qb_seg, axis=1).astype(BF16)
        kb_big = jnp.concatenate(kb_seg, axis=1).astype(BF16)
        sc_f = lax.dot_general(qf_big, kf_big, nt, preferred_element_type=F32)
        sc_b = lax.dot_general(qb_big, kb_big, nt, preferred_element_type=F32)
        a = (jnp.where(c2 <= r2, sc_f, 0.0) + jnp.where(c2 >= r2, sc_b, 0.0)).astype(BF16)
        v = i_ref[sl, cs]
        oi_ref[sl, cs] = jnp.dot(a, v, preferred_element_type=F32).astype(BF16)
        qe_ref[sl, 2 * hd * HG_DK:2 * (hd + 1) * HG_DK] = jnp.concatenate(
            [q * jnp.exp2(pf), q * jnp.exp2(suf)], axis=1).astype(BF16)
        totf = pf_buf[pl.ds(base + CHUNK - 1, 1), :]
        kt = jnp.concatenate([kf * jnp.exp2(totf - pf), kb * jnp.exp2(pb - gbc)], axis=1).astype(BF16)
        u = lax.dot_general(v, kt, tn_dims, preferred_element_type=F32).astype(BF16)
        u_ref[0, hd, c] = u[:, :HG_DK]
        u_ref[1, hd, c] = u[:, HG_DK:]
        d_ref[0, hd, c * SUBLANES:(c + 1) * SUBLANES, :] = jnp.broadcast_to(jnp.exp2(totf), (SUBLANES, HG_DK))
        d_ref[1, hd, c * SUBLANES:(c + 1) * SUBLANES, :] = jnp.broadcast_to(jnp.exp2(totb), (SUBLANES, HG_DK))


def _gla_local(h, w_gu, w_v, z_a, lb4, rows_per_block, side=()):
    m = h.shape[0]
    nb = m // rows_per_block
    nchunk = rows_per_block // CHUNK
    n_chunks = m // CHUNK
    resident = lambda a: pl.BlockSpec(a.shape, lambda b: (0,) * a.ndim, pipeline_mode=pl.Buffered(1))
    col = lambda k: pl.BlockSpec((rows_per_block, HG_WIDTH), lambda b, k=k: (b, k))
    side_in, side_out, side_shapes = _side_cast_specs(side, nb, lambda b: b)
    return pl.pallas_call(
        functools.partial(_gla_local_kernel, n_side=len(side)),
        grid=(nb,),
        in_specs=[pl.BlockSpec((rows_per_block, D_MODEL), lambda b: (b, 0)),
                  resident(w_gu), resident(w_v), col(0), col(1), col(2), col(3), resident(lb4)] + side_in,
        out_specs=[pl.BlockSpec((rows_per_block, 3 * HG_WIDTH), lambda b: (b, 0)),
                   pl.BlockSpec((rows_per_block, HG_WIDTH), lambda b: (b, 0)),
                   pl.BlockSpec((rows_per_block, 2 * HG_WIDTH), lambda b: (b, 0)),
                   pl.BlockSpec((2, HG_HEADS, nchunk, HG_DK, HG_DK), lambda b: (0, 0, b, 0, 0)),
                   pl.BlockSpec((2, HG_HEADS, nchunk * SUBLANES, HG_DK), lambda b: (0, 0, b, 0))] + side_out,
        out_shape=[jax.ShapeDtypeStruct((m, 3 * HG_WIDTH), BF16),
                   jax.ShapeDtypeStruct((m, HG_WIDTH), BF16),
                   jax.ShapeDtypeStruct((m, 2 * HG_WIDTH), BF16),
                   jax.ShapeDtypeStruct((2, HG_HEADS, n_chunks, HG_DK, HG_DK), BF16),
                   jax.ShapeDtypeStruct((2, HG_HEADS, n_chunks * SUBLANES, HG_DK), F32)] + side_shapes,
        scratch_shapes=[pltpu.VMEM((rows_per_block, HG_DK), F32) for _ in range(12)],
        compiler_params=_params(("parallel",), 56),
        name="gla_local",
    )(h, w_gu, w_v, z_a, z_a, z_a, z_a, lb4, *[s[0] for s in side])


def _gla_finish_kernel(*refs, nchunk, n_chunks, n_side):
    oi_ref, qe_ref, gt_ref, gain_ref, u_hbm, d_ref, s0_ref = refs[:7]
    side_in = refs[7:7 + n_side]
    o_ref = refs[7 + n_side]
    side_out = refs[8 + n_side:8 + 2 * n_side]
    s_all, u_buf, u_sems = refs[-3:]
    _side_cast(side_in, side_out)
    head = pl.program_id(0)

    def u_copy(hd):
        slot = hd % 2
        return pltpu.make_async_copy(u_hbm.at[:, hd], u_buf.at[slot], u_sems.at[slot])

    @pl.when((pl.program_id(1) == 0) & (head == 0))
    def _():
        u_copy(0).start()

    @pl.when((pl.program_id(1) == 0) & (head + 1 < HG_HEADS))
    def _():
        u_copy(head + 1).start()

    @pl.when(pl.program_id(1) == 0)
    def _():
        u_copy(head).wait()
        slot = head % 2

        def step(c, carry):
            sf, sb = carry
            cb = n_chunks - 1 - c
            s_all[0, c] = sf.astype(BF16)
            s_all[1, cb] = sb.astype(BF16)
            df = d_ref[0, 0, pl.ds(pl.multiple_of(c * SUBLANES, SUBLANES), 1), :]
            db = d_ref[1, 0, pl.ds(pl.multiple_of(cb * SUBLANES, SUBLANES), 1), :]
            return (sf * df + u_buf[slot, 0, c].astype(F32), sb * db + u_buf[slot, 1, cb].astype(F32))

        lax.fori_loop(0, n_chunks, step, (s0_ref[0, 0], s0_ref[1, 0]), unroll=4)

    nt = (((1,), (1,)), ((), ()))
    gain = gain_ref[0]
    first = pl.program_id(1) * nchunk
    for c in range(nchunk):
        sl = slice(c * CHUNK, (c + 1) * CHUNK)
        scat = jnp.concatenate([s_all[0, first + c], s_all[1, first + c]], axis=1)
        o = oi_ref[sl, :].astype(F32) + lax.dot_general(qe_ref[sl, :], scat, nt, preferred_element_type=F32)
        o = _rms(o) * gain
        o_ref[sl, :] = (o * _silu(gt_ref[sl, :].astype(F32))).astype(BF16)


def _gla_finish(oi, qe, z_b, hg_gain, s_inc, s_dec, s0, rows_per_block, side=()):
    m = oi.shape[0]
    nb = m // rows_per_block
    nchunk = rows_per_block // CHUNK
    n_chunks = m // CHUNK
    side_in, side_out, side_shapes = _side_cast_specs(side, HG_HEADS * nb, lambda h, b: h * nb + b)
    return pl.pallas_call(
        functools.partial(_gla_finish_kernel, nchunk=nchunk, n_chunks=n_chunks, n_side=len(side)),
        grid=(HG_HEADS, nb),
        in_specs=[pl.BlockSpec((rows_per_block, HG_DK), lambda h, b: (b, h)),
                  pl.BlockSpec((rows_per_block, 2 * HG_DK), lambda h, b: (b, h)),
                  pl.BlockSpec((rows_per_block, HG_DK), lambda h, b: (b, h)),
                  pl.BlockSpec((1, 1, HG_DK), lambda h, b: (h, 0, 0)),
                  pl.BlockSpec(memory_space=pl.ANY),
                  pl.BlockSpec((2, 1, n_chunks * SUBLANES, HG_DK), lambda h, b: (0, h, 0, 0)),
                  pl.BlockSpec((2, 1, HG_DK, HG_DK), lambda h, b: (0, h, 0, 0))] + side_in,
        out_specs=[pl.BlockSpec((rows_per_block, HG_DK), lambda h, b: (b, h))] + side_out,
        out_shape=[jax.ShapeDtypeStruct((m, HG_WIDTH), BF16)] + side_shapes,
        scratch_shapes=[pltpu.VMEM((2, n_chunks, HG_DK, HG_DK), BF16),
                        pltpu.VMEM((2, 2, n_chunks, HG_DK, HG_DK), BF16),
                        pltpu.SemaphoreType.DMA((2,))],
        compiler_params=_params(("arbitrary", "arbitrary"), 58),
        name="gla_finish",
    )(oi, qe, z_b, hg_gain.reshape(HG_HEADS, 1, HG_DK), s_inc, s_dec, s0, *[s[0] for s in side])


def _residual_epilogue(mix, x_ref, vec_ref, xo_ref, ho_ref, rs):
    xn = x_ref[rs, :] + _rms(mix) * vec_ref[0:1]
    xo_ref[rs, :] = xn
    if ho_ref is not None:
        hn = (_rms(xn) * vec_ref[1:2] + vec_ref[2:3]).astype(BF16)
        outs = ho_ref if isinstance(ho_ref, (tuple, list)) else (ho_ref,)
        width = hn.shape[1] // len(outs)
        for k, out in enumerate(outs):
            out[rs, :] = hn[:, k * width:(k + 1) * width]


def _spatial_chunk(u_raw, v_raw, w_ref, bias_ref, lng_ref, lnb_ref):
    v = _gelu_tanh(v_raw.astype(F32))
    xc = v - jnp.mean(v, axis=-1, keepdims=True)
    y = (xc * lax.rsqrt(jnp.mean(xc * xc, axis=-1, keepdims=True) + EPS) * lng_ref[...] + lnb_ref[...]).astype(BF16)
    u = _gelu_tanh(u_raw.astype(F32))
    parts = []
    for g in range(SG_GROUPS):
        cs = slice(g * SG_CH, (g + 1) * SG_CH)
        mixed = jnp.dot(w_ref[g], y[:, cs], preferred_element_type=F32) + bias_ref[:, cs]
        parts.append((u[:, cs] * mixed).astype(BF16))
    return jnp.concatenate(parts, axis=1)


def _mm_res_kernel(*refs, n_a, n_sp, emit_h):
    a_refs = refs[:n_a]
    sp_refs = refs[n_a:n_a + 6 * n_sp]
    n_w = n_a + n_sp
    w_refs = refs[n_a + 6 * n_sp:n_a + 6 * n_sp + n_w]
    x_ref, vec_ref, xo_ref = refs[n_a + 6 * n_sp + n_w:n_a + 6 * n_sp + n_w + 3]
    ho_ref = refs[n_a + 6 * n_sp + n_w + 3] if emit_h else None
    for rs in _row_subblocks(x_ref.shape[0], SG_CHUNK):
        mix = jnp.dot(a_refs[0][rs, :], w_refs[0][...], preferred_element_type=F32)
        for a_ref, w_ref in zip(a_refs[1:], w_refs[1:]):
            mix = mix + jnp.dot(a_ref[rs, :], w_ref[...], preferred_element_type=F32)
        if n_sp:
            u_ref, v_ref = sp_refs[:2]
            s = _spatial_chunk(u_ref[rs, :], v_ref[rs, :], *sp_refs[2:])
            mix = mix + jnp.dot(s, w_refs[-1][...], preferred_element_type=F32)
        _residual_epilogue(mix, x_ref, vec_ref, xo_ref, ho_ref, rs)


def _mm_res(a_list, w, x, vec, tm, emit_h=True, spatial=None):
    m = x.shape[0]
    n_a = len(a_list)
    widths = [a.shape[1] for a in a_list] + ([SG_WIDTH] if spatial else [])
    offs = np.cumsum([0] + widths)
    sp_specs, sp_args = [], []
    if spatial:
        z_b = spatial[0]
        sp_specs = [pl.BlockSpec((tm, SG_WIDTH), lambda i: (i, 1)), pl.BlockSpec((tm, SG_WIDTH), lambda i: (i, 2)),
                    pl.BlockSpec((SG_GROUPS, SG_CHUNK, SG_CHUNK), lambda i: (0, 0, 0)),
                    pl.BlockSpec((SG_CHUNK, SG_WIDTH), lambda i: (0, 0)),
                    pl.BlockSpec((1, SG_WIDTH), lambda i: (0, 0)), pl.BlockSpec((1, SG_WIDTH), lambda i: (0, 0))]
        sp_args = [z_b, z_b, *spatial[1:]]
    in_specs = ([pl.BlockSpec((tm, a.shape[1]), lambda i: (i, 0)) for a in a_list] + sp_specs
                + [pl.BlockSpec((wd, D_MODEL), lambda i, k=int(o) // wd: (k, 0)) for wd, o in zip(widths, offs)]
                + [pl.BlockSpec((tm, D_MODEL), lambda i: (i, 0)),
                   pl.BlockSpec((8, D_MODEL), lambda i: (0, 0))])
    out_specs = [pl.BlockSpec((tm, D_MODEL), lambda i: (i, 0))]
    out_shape = [jax.ShapeDtypeStruct((m, D_MODEL), F32)]
    if emit_h:
        out_specs.append(pl.BlockSpec((tm, D_MODEL), lambda i: (i, 0)))
        out_shape.append(jax.ShapeDtypeStruct((m, D_MODEL), BF16))
    return pl.pallas_call(
        functools.partial(_mm_res_kernel, n_a=n_a, n_sp=1 if spatial else 0, emit_h=emit_h),
        grid=(m // tm,),
        in_specs=in_specs, out_specs=out_specs, out_shape=out_shape,
        compiler_params=_params(("parallel",), 56),
        name="mm_residual",
    )(*a_list, *sp_args, *[w] * len(widths), x, vec)


def _mlp_kernel(*refs, n_h, n_side, nj, n_steps, loop_ff):
    h_ref, w1_ref, w2_ref, x_ref, vec_ref = refs[:5]
    side_in = refs[5:5 + n_side]
    xo_ref = refs[5 + n_side]
    ho_ref = tuple(refs[6 + n_side:6 + n_side + n_h]) or None
    n_out = 1 + n_h
    side_out = refs[5 + n_side + n_out:5 + 2 * n_side + n_out]
    w1_buf, w2_buf, sems = refs[-3:]
    acc_ref = xo_ref
    _side_cast(side_in, side_out)
    last = nj - 1
    tf = w1_buf.shape[2]
    first_step = pl.program_id(0) * nj + (0 if loop_ff else pl.program_id(1))

    def tile_copies(s):
        slot = s % WEIGHT_SLOTS
        col = (s % nj) * tf
        if not isinstance(col, int):
            col = pl.multiple_of(col, tf)
        return (pltpu.make_async_copy(w1_ref.at[:, pl.ds(col, tf)], w1_buf.at[slot], sems.at[0, slot]),
                pltpu.make_async_copy(w2_ref.at[pl.ds(col, tf), :], w2_buf.at[slot], sems.at[1, slot]))

    @pl.when(first_step == 0)
    def _():
        for s in range(WEIGHT_SLOTS - 1):
            for cp in tile_copies(s):
                cp.start()

    def ff_step(jj, mode):
        step = pl.program_id(0) * nj + jj

        @pl.when(step + (WEIGHT_SLOTS - 1) < n_steps)
        def _():
            for cp in tile_copies(step + (WEIGHT_SLOTS - 1)):
                cp.start()

        for cp in tile_copies(step):
            cp.wait()
        slot = step % WEIGHT_SLOTS
        for rs in _row_subblocks(h_ref.shape[0]):
            a = jnp.dot(h_ref[rs, :], w1_buf[slot], preferred_element_type=F32)
            a = jnp.square(jnp.maximum(a, 0.0)).astype(BF16)
            p = jnp.dot(a, w2_buf[slot], preferred_element_type=F32)
            if mode == "first":
                acc_ref[rs, :] = p
            elif mode == "last":
                _residual_epilogue(acc_ref[rs, :] + p, x_ref, vec_ref, xo_ref, ho_ref, rs)
            else:
                acc_ref[rs, :] += p

    if loop_ff:
        ff_step(0, "first")

        def middle(jj, carry):
            ff_step(jj, "middle")
            return carry

        lax.fori_loop(1, last, middle, 0)
        ff_step(last, "last")
    else:
        j = pl.program_id(1)
        pl.when(j == 0)(lambda: ff_step(j, "first"))
        pl.when(jnp.logical_and(j > 0, j < last))(lambda: ff_step(j, "middle"))
        pl.when(j == last)(lambda: ff_step(j, "last"))


def _mlp(h, w1, w2, x, vec, tm, tf, n_h, side=()):
    m = x.shape[0]
    nj = D_FF // tf
    loop_ff = not side
    row = (lambda i: (i, 0)) if loop_ff else (lambda i, j: (i, 0))
    const = (lambda i: (0, 0)) if loop_ff else (lambda i, j: (0, 0))
    out_specs = [pl.BlockSpec((tm, D_MODEL), row)]
    out_shape = [jax.ShapeDtypeStruct((m, D_MODEL), F32)]
    for _ in range(n_h):
        out_specs.append(pl.BlockSpec((tm, D_MODEL // n_h), row))
        out_shape.append(jax.ShapeDtypeStruct((m, D_MODEL // n_h), BF16))
    side_in, side_out, side_shapes = _side_cast_specs(side, (m // tm) * nj, lambda i, j: i * nj + j)
    return pl.pallas_call(
        functools.partial(_mlp_kernel, n_h=n_h, n_side=len(side), nj=nj, n_steps=(m // tm) * nj, loop_ff=loop_ff),
        grid=(m // tm,) if loop_ff else (m // tm, nj),
        in_specs=[pl.BlockSpec((tm, D_MODEL), row),
                  pl.BlockSpec(memory_space=pl.ANY),
                  pl.BlockSpec(memory_space=pl.ANY),
                  pl.BlockSpec((tm, D_MODEL), row),
                  pl.BlockSpec((8, D_MODEL), const)] + side_in,
        out_specs=out_specs + side_out, out_shape=out_shape + side_shapes,
        scratch_shapes=[pltpu.VMEM((WEIGHT_SLOTS, D_MODEL, tf), BF16),
                        pltpu.VMEM((WEIGHT_SLOTS, tf, D_MODEL), BF16),
                        pltpu.SemaphoreType.DMA((2, WEIGHT_SLOTS))],
        compiler_params=_params(("arbitrary",) if loop_ff else ("arbitrary", "arbitrary"), 60),
        name="mlp",
    )(h, w1, w2, x, vec, *[s[0] for s in side])


def _dft_constants():
    def cs(n):
        k = np.arange(n)
        ang = 2.0 * np.pi * ((k[:, None] * k[None, :]) % n) / n
        return np.cos(ang) / np.sqrt(n), np.sin(ang) / np.sqrt(n)

    cc, sc = cs(FT_CH)
    c1, s1 = cs(FFT_L1)
    m1 = np.concatenate([c1, -s1], axis=0)
    c2, s2 = cs(FFT_L2)
    m2 = np.block([[c2, s2], [-s2, c2]])
    u1 = np.arange(FFT_L1)[:, None]
    t2 = np.arange(FFT_L2)[None, :]
    ang = 2.0 * np.pi * ((u1 * t2) % SEQ) / SEQ
    tw_c = np.cos(ang).reshape(SEQ, 1)
    tw_s = np.sin(ang).reshape(SEQ, 1)
    f = lambda a: jnp.asarray(a, F32)
    return f(cc), f(sc), f(m1), f(m2), f(tw_c), f(tw_s)


def _swap_row_factors(a, n_outer, n_inner):
    return a.reshape(n_outer, n_inner, a.shape[1]).transpose(1, 0, 2).reshape(a.shape)


def _pos_dft1_kernel(x_ref, m_ref, zr_ref, zi_ref):
    for k in range(x_ref.shape[0] // FFT_L1):
        sl = slice(k * FFT_L1, (k + 1) * FFT_L1)
        z = jnp.dot(m_ref[...], x_ref[sl, :], preferred_element_type=F32)
        zr_ref[sl, :] = z[:FFT_L1].astype(BF16)
        zi_ref[sl, :] = z[FFT_L1:].astype(BF16)


def _pos_dft1(xt, m1, slabs):
    rows = slabs * FFT_L1
    blk = pl.BlockSpec((rows, xt.shape[1]), lambda j: (j, 0))
    return pl.pallas_call(
        _pos_dft1_kernel,
        grid=(SEQ // rows,),
        in_specs=[blk, pl.BlockSpec((2 * FFT_L1, FFT_L1), lambda j: (0, 0))],
        out_specs=[blk, blk],
        out_shape=[jax.ShapeDtypeStruct(xt.shape, BF16)] * 2,
        compiler_params=_params(("parallel",), 32),
        name="pos_dft1",
    )(xt, m1)


def _pos_dft2_kernel(zr_ref, zi_ref, c_ref, s_ref, m_ref, wc_ref, ws_ref, y_ref, yr_buf, yi_buf):
    for k in range(zr_ref.shape[0] // FFT_L2):
        sl = slice(k * FFT_L2, (k + 1) * FFT_L2)
        zr = zr_ref[sl, :].astype(F32)
        zi = zi_ref[sl, :].astype(F32)
        c = c_ref[sl, :]
        s = s_ref[sl, :]
        t = jnp.concatenate([(zr * c + zi * s).astype(BF16), (zi * c - zr * s).astype(BF16)], axis=0)
        y = jnp.dot(m_ref[...], t, preferred_element_type=F32)
        yr_buf[sl, :] = y[:FFT_L2].astype(BF16)
        yi_buf[sl, :] = y[FFT_L2:].astype(BF16)
    for g in range(zr_ref.shape[1] // FT_CH):
        cs = slice(g * FT_CH, (g + 1) * FT_CH)
        y_ref[:, cs] = (jnp.dot(yr_buf[:, cs], wc_ref[...], preferred_element_type=F32)
                        + jnp.dot(yi_buf[:, cs], ws_ref[...], preferred_element_type=F32)).astype(BF16)


def _pos_dft2(zr, zi, tw_c, tw_s, m2, wc, ws, slabs):
    rows = slabs * FFT_L2
    width = zr.shape[1]
    blk = pl.BlockSpec((rows, width), lambda j: (j, 0))
    tw = pl.BlockSpec((rows, 1), lambda j: (j, 0))
    const = lambda a: pl.BlockSpec(a.shape, lambda j: (0, 0))
    return pl.pallas_call(
        _pos_dft2_kernel,
        grid=(SEQ // rows,),
        in_specs=[blk, blk, tw, tw, const(m2), const(wc), const(ws)],
        out_specs=blk,
        out_shape=jax.ShapeDtypeStruct(zr.shape, BF16),
        scratch_shapes=[pltpu.VMEM((rows, width), BF16), pltpu.VMEM((rows, width), BF16)],
        compiler_params=_params(("parallel",), 40),
        name="pos_dft2",
    )(zr, zi, tw_c, tw_s, m2, wc, ws)


def _fourier_mix(h_parts):
    cc, sc, m1, m2, tw_c, tw_s = _dft_constants()
    m1, m2, cc, sc = m1.astype(BF16), m2.astype(BF16), cc.astype(BF16), sc.astype(BF16)
    hts = [_swap_row_factors(h, FFT_L1, FFT_L2) for h in h_parts]
    zs = [_pos_dft1(ht, m1, DFT_ROWS // FFT_L1) for ht in hts]
    zs = [[_swap_row_factors(z, FFT_L2, FFT_L1) for z in zri] for zri in zs]
    yps = [_pos_dft2(zr, zi, tw_c, tw_s, m2, cc, sc, DFT_ROWS // FFT_L2) for zr, zi in zs]
    return [_swap_row_factors(yp, FFT_L1, FFT_L2) for yp in yps]


def _pack_rows(*rows):
    rows = [r.reshape(1, D_MODEL).astype(F32) for r in rows]
    rows += [jnp.zeros((1, D_MODEL), F32)] * (8 - len(rows))
    return jnp.concatenate(rows, axis=0)


def kernel(x, c, ctx, c_ctx, w_ada, b_ada, norm_gain, w_in, w_out, lb_raw, hg_norm_gain,
           sg_w, sg_b, sg_ln_gain, sg_ln_bias, w_fourier, w_mlp_in, w_mlp_out):
    assert x.shape == (1, SEQ, D_MODEL) and ctx.shape == (1, CTX_LEN, D_MODEL)
    xs = x.reshape(SEQ, D_MODEL)
    cs = ctx.reshape(CTX_LEN, D_MODEL)

    mods = _ada_mods(jnp.concatenate([c.reshape(1, D_MODEL), c_ctx.reshape(1, D_MODEL)], axis=0), w_ada, b_ada)
    mod = lambda l, r, j: mods[l, r, j * D_MODEL:(j + 1) * D_MODEL]

    lb = jnp.cumsum(jax.nn.softmax(lb_raw.astype(F32), axis=1), axis=1)[:, 0]
    lb4 = lb.reshape(2, HG_HEADS, 1, HG_DK)

    pre = lambda gain, l, r, j: _pack_rows(gain * (1.0 + mod(l, r, j + 1)), mod(l, r, j))
    post = lambda gain, l, j, *nxt: _pack_rows(mod(l, 0, j) * gain, *nxt)

    z_a, h0, w_gu, w_v = _norm_mm_rows(xs, pre(norm_gain[0, 0], 0, 0, 0), w_in, 4 * HG_WIDTH, MM_ROWS, MM_COLS,
                                       side=[(w_in, 0, 2 * HG_WIDTH, 2), (w_in, 0, HG_WIDTH, 6)])
    z_ctx = _norm_mm(cs, pre(norm_gain[0, 0], 0, 1, 0), w_in, 3 * HG_WIDTH, CTX_LEN, MM_COLS)
    s_ctx = _ctx_states(z_ctx, lb4)
    z_b, o_intra, q_dec, s_inc, s_dec, w_out0, w1_0, w2_0 = _gla_local(
        h0, w_gu, w_v, z_a, lb4, GLA_LOCAL_ROWS, side=[(w_out, 0), (w_mlp_in, 0), (w_mlp_out, 0)])
    o, w_fou = _gla_finish(o_intra, q_dec, z_b, hg_norm_gain[0], s_inc, s_dec, s_ctx, GLA_FINISH_ROWS,
                           side=[(w_fourier, 0)])
    bias_full = jnp.repeat(sg_b[0].T.astype(F32), SG_CH, axis=1)
    vec = post(norm_gain[0, 1], 0, 2, *pre(norm_gain[0, 2], 0, 0, 3)[:2])
    x1, h = _mm_res([o], w_out0, xs, vec, MM_ROWS,
                    spatial=(z_b, sg_w[0].astype(BF16), bias_full, sg_ln_gain[0].reshape(1, SG_WIDTH),
                             sg_ln_bias[0].reshape(1, SG_WIDTH)))
    vec = post(norm_gain[0, 3], 0, 5, *pre(norm_gain[1, 0], 1, 0, 0)[:2])
    x2, h_a, h_b, w1_1, w2_1 = _mlp(h, w1_0, w2_0, x1, vec, MM_ROWS, MLP_FF_COLS, 2,
                                    side=[(w_mlp_in, 1), (w_mlp_out, 1)])

    y_parts = _fourier_mix([h_a, h_b])
    vec = post(norm_gain[1, 1], 1, 2, *pre(norm_gain[1, 2], 1, 0, 3)[:2])
    x3, h = _mm_res(y_parts, w_fou, x2, vec, MM_ROWS)
    (x4,) = _mlp(h, w1_1, w2_1, x3, post(norm_gain[1, 3], 1, 5), MM_ROWS, MLP_FF_COLS, 0)
    return x4.reshape(1, SEQ, D_MODEL)
```
